```python
import jax, jax.numpy as jnp
from jax import lax
import numpy as np

D_MODEL = 2048
BATCH = 2
SEQ = 4096
DEPTH = 2

CHUNK = 64
NORM_EPS = 1e-6

A_HEADS = 12
A_HEAD_DIM = 64
A_WIDTH = A_HEADS * A_HEAD_DIM
DECAY_LORA = 96
ICLR_LORA = 96
VRES_LORA = 64
GATE_LORA = 256
A_GN_EPS = 64e-5
A_COLS = 3 * A_WIDTH + DECAY_LORA + ICLR_LORA + GATE_LORA
A_SPLITS = (A_WIDTH, 2 * A_WIDTH, 3 * A_WIDTH, 3 * A_WIDTH + DECAY_LORA, 3 * A_WIDTH + DECAY_LORA + ICLR_LORA)

B_WIDTH = 512
B_BLOCKS = 8
B_BLOCK_DIM = B_WIDTH // B_BLOCKS
CONV_WIDTH = 4
LRU_C = 8.0
B_COLS = 2 * B_WIDTH

C_HEADS = 6
C_QK_DIM = 64
C_V_DIM = 128
C_QK_WIDTH = C_HEADS * C_QK_DIM
C_WIDTH = C_HEADS * C_V_DIM
C_GN_EPS = 1e-5
ROPE_THETA = 10000.0
C_COLS = 2 * C_QK_WIDTH + 2 * C_WIDTH
C_SPLITS = (C_QK_WIDTH, 2 * C_QK_WIDTH, 2 * C_QK_WIDTH + C_WIDTH)

MIX_WIDTH = A_WIDTH + B_WIDTH + C_WIDTH
IN_COLS = A_COLS + B_COLS + C_COLS
D_FF = ((8 * D_MODEL // 3 + 255) // 256) * 256

kernel_name = "hybrid_rwkv7_rglru_retention_block"


def rms_norm(x, g):
    xf = x.astype(jnp.float32)
    y = xf * lax.rsqrt(jnp.mean(xf * xf, axis=-1, keepdims=True) + NORM_EPS)
    return (y * g.astype(jnp.float32)).astype(x.dtype)


def head_group_norm(x, n_heads, eps):
    b, s, w = x.shape
    xf = x.astype(jnp.float32).reshape(b, s, n_heads, w // n_heads)
    mu = jnp.mean(xf, axis=-1, keepdims=True)
    var = jnp.mean(jnp.square(xf - mu), axis=-1, keepdims=True)
    return ((xf - mu) * lax.rsqrt(var + eps)).reshape(b, s, w)


def token_shift(z):
    return jnp.pad(z[:, :-1], ((0, 0), (1, 0), (0, 0)))


def rwkv7_scan(r, w, k, v, kk, b):
    def step(state, inp):
        r_t, w_t, k_t, v_t, kk_t, b_t = inp
        sa = -jnp.einsum('bhij,bhj->bhi', state, kk_t)
        state = (state * w_t[:, :, None, :] + sa[..., None] * b_t[:, :, None, :]
                 + v_t[..., None] * k_t[:, :, None, :])
        return state, jnp.einsum('bhij,bhj->bhi', state, r_t)
    bsz, _, h, n = r.shape
    xs = tuple(jnp.moveaxis(t, 1, 0) for t in (r, w, k, v, kk, b))
    _, y = lax.scan(step, jnp.zeros((bsz, h, n, n), jnp.float32), xs)
    return jnp.moveaxis(y, 0, 1)


def rwkv7_mixer(r, k, v, wd, ad, gd, w0, w2, a0, a2, g2, k_k, k_a, r_k, ln_g, ln_b, v_first, vres):
    f32 = jnp.float32
    bsz, s, _ = r.shape
    hd = lambda t: t.reshape(bsz, s, A_HEADS, A_HEAD_DIM)
    r, k, v, wd, ad, gd = (t.astype(f32) for t in (r, k, v, wd, ad, gd))
    log_w = -jax.nn.softplus(-(w0 + jnp.tanh(wd) @ w2)) - 0.5
    decay = jnp.exp(-jnp.exp(log_w))
    a = jax.nn.sigmoid(a0 + ad @ a2)
    g = jax.nn.sigmoid(gd) @ g2
    if vres is not None:
        vd, v0, v2 = vres
        v = v + (v_first - v) * jax.nn.sigmoid(v0 + vd.astype(f32) @ v2)
    kk = hd(k * k_k)
    kk = kk / jnp.maximum(jnp.linalg.norm(kk, axis=-1, keepdims=True), 1e-12)
    k = k * (1.0 + (a - 1.0) * k_a)
    y = rwkv7_scan(hd(r), hd(decay), hd(k), hd(v), kk, kk * hd(a))
    y = head_group_norm(y.reshape(bsz, s, A_WIDTH), A_HEADS, A_GN_EPS) * ln_g + ln_b
    bonus = jnp.sum(hd(r) * hd(k) * r_k, axis=-1, keepdims=True) * hd(v)
    return (y + bonus.reshape(bsz, s, A_WIDTH)) * g, v


def rglru_mixer(gate_in, x_in, conv_w, conv_b, wa, ba, wx, bx, lam):
    f32 = jnp.float32
    bsz, s, _ = x_in.shape
    xc = lax.conv_general_dilated(x_in, conv_w[:, None, :], window_strides=(1,),
                                  padding=[(CONV_WIDTH - 1, 0)],
                                  dimension_numbers=('NWC', 'WIO', 'NWC'),
                                  feature_group_count=B_WIDTH) + conv_b
    xb = xc.reshape(bsz, s, B_BLOCKS, B_BLOCK_DIM)
    r = jax.nn.sigmoid(jnp.einsum('bsgi,gij->bsgj', xb, wa).reshape(bsz, s, B_WIDTH) + ba).astype(f32)
    i = jax.nn.sigmoid(jnp.einsum('bsgi,gij->bsgj', xb, wx).reshape(bsz, s, B_WIDTH) + bx).astype(f32)
    log_a = -LRU_C * r * jax.nn.softplus(-lam.astype(f32))
    a = jnp.exp(log_a)
    u = jnp.sqrt(-jnp.expm1(2.0 * log_a)) * (i * xc.astype(f32))

    def combine(e1, e2):
        a1, b1 = e1
        a2, b2 = e2
        return a1 * a2, a2 * b1 + b2

    _, h = lax.associative_scan(combine, (a, u), axis=1)
    return jax.nn.gelu(gate_in.astype(f32)) * h


def rope_tables(positions):
    inv_freq = ROPE_THETA ** (-jnp.arange(0, C_QK_DIM, 2, dtype=jnp.float32) / C_QK_DIM)
    ang = positions.astype(jnp.float32)[..., None] * inv_freq
    return jnp.cos(ang)[:, :, None, :], jnp.sin(ang)[:, :, None, :]


def apply_rope(t, cos, sin):
    t1, t2 = jnp.split(t, 2, axis=-1)
    return jnp.concatenate([t1 * cos - t2 * sin, t2 * cos + t1 * sin], axis=-1)


def retention_mixer(q, k, v, g, cos, sin, log_gamma, gn_g):
    f32 = jnp.float32
    bsz, s, _ = q.shape
    nc = s // CHUNK
    q = apply_rope(q.astype(f32).reshape(bsz, s, C_HEADS, C_QK_DIM), cos, sin)
    k = apply_rope(k.astype(f32).reshape(bsz, s, C_HEADS, C_QK_DIM), cos, sin) * (C_QK_DIM ** -0.5)
    v = v.astype(f32).reshape(bsz, s, C_HEADS, C_V_DIM)
    qc = q.reshape(bsz, nc, CHUNK, C_HEADS, C_QK_DIM)
    kc = k.reshape(bsz, nc, CHUNK, C_HEADS, C_QK_DIM)
    vc = v.reshape(bsz, nc, CHUNK, C_HEADS, C_V_DIM)
    idx = jnp.arange(CHUNK, dtype=f32)
    dist = jnp.abs(idx[:, None] - idx[None, :])
    intra_decay = jnp.exp(log_gamma[:, None, None] * dist)
    scores = jnp.einsum('bnchd,bnmhd->bnhcm', qc, kc) * intra_decay
    intra = jnp.einsum('bnhcm,bnmhe->bnche', scores, vc)
    k_decay = jnp.exp(log_gamma[None, :] * (CHUNK - 1 - idx)[:, None])
    kv = jnp.einsum('bnmhd,bnmhe->bnhde', kc * k_decay[:, :, None], vc)
    chunk_decay = jnp.exp(log_gamma * CHUNK)[None, :, None, None]

    def step(state, kv_n):
        return state * chunk_decay + kv_n, state

    _, states = lax.scan(step, jnp.zeros((bsz, C_HEADS, C_QK_DIM, C_V_DIM), f32),
                         jnp.moveaxis(kv, 1, 0))
    states = jnp.moveaxis(states, 0, 1)
    q_decay = jnp.exp(log_gamma[None, :] * (idx + 1.0)[:, None])
    cross = jnp.einsum('bnchd,bnhde->bnche', qc * q_decay[:, :, None], states)
    o = (intra + cross).reshape(bsz, s, C_WIDTH)
    o = head_group_norm(o, C_HEADS, C_GN_EPS) * gn_g
    return jax.nn.silu(g.astype(f32)) * o


def setup_inputs(seed: int = 0) -> dict:
    key = jax.random.key(seed)
    ks = iter(jax.random.split(key, 48))
    nrm = lambda shape, scale: jax.random.normal(next(ks), shape, jnp.float32) * scale
    L = DEPTH
    L1 = DEPTH - 1
    x = nrm((BATCH, SEQ, D_MODEL), 1.0)
    start = jax.random.randint(next(ks), (BATCH, 1), 0, 4096, dtype=jnp.int32)
    positions = start + jnp.arange(SEQ, dtype=jnp.int32)[None, :]
    ratio = jnp.linspace(0.0, 1.0, A_WIDTH, dtype=jnp.float32)
    decay_speed = -7.0 + 5.0 * ratio ** 0.85
    u = jax.random.uniform(next(ks), (L, B_WIDTH), jnp.float32, minval=0.9, maxval=0.999)
    base = u ** (1.0 / LRU_C)
    return {
        "x": x,
        "positions": positions,
        "norm1_g": 1.0 + nrm((L, D_MODEL), 0.02),
        "w_in": nrm((L, D_MODEL, IN_COLS), D_MODEL ** -0.5),
        "tshift_mu": jax.random.uniform(next(ks), (L, A_COLS), jnp.float32),
        "rwkv_w0": decay_speed + 0.5 + nrm((L, A_WIDTH), 0.1),
        "rwkv_w2": nrm((L, DECAY_LORA, A_WIDTH), 0.1),
        "rwkv_a0": nrm((L, A_WIDTH), 0.1),
        "rwkv_a2": nrm((L, ICLR_LORA, A_WIDTH), ICLR_LORA ** -0.5),
        "rwkv_g2": nrm((L, GATE_LORA, A_WIDTH), GATE_LORA ** -0.5),
        "rwkv_k_k": 0.85 + nrm((L, A_WIDTH), 0.02),
        "rwkv_k_a": 1.0 + nrm((L, A_WIDTH), 0.02),
        "rwkv_r_k": nrm((L, A_HEADS, A_HEAD_DIM), 0.1),
        "rwkv_ln_g": 1.0 + nrm((L, A_WIDTH), 0.02),
        "rwkv_ln_b": nrm((L, A_WIDTH), 0.02),
        "w_in_vres": nrm((L1, D_MODEL, VRES_LORA), D_MODEL ** -0.5),
        "tshift_mu_vres": jax.random.uniform(next(ks), (L1, VRES_LORA), jnp.float32),
        "rwkv_v0": 1.0 + nrm((L1, A_WIDTH), 0.1),
        "rwkv_v2": nrm((L1, VRES_LORA, A_WIDTH), VRES_LORA ** -0.5),
        "lru_conv_w": nrm((L, CONV_WIDTH, B_WIDTH), CONV_WIDTH ** -0.5),
        "lru_conv_b": nrm((L, B_WIDTH), 0.02),
        "lru_wa": nrm((L, B_BLOCKS, B_BLOCK_DIM, B_BLOCK_DIM), B_BLOCK_DIM ** -0.5),
        "lru_ba": nrm((L, B_WIDTH), 0.02),
        "lru_wx": nrm((L, B_BLOCKS, B_BLOCK_DIM, B_BLOCK_DIM), B_BLOCK_DIM ** -0.5),
        "lru_bx": nrm((L, B_WIDTH), 0.02),
        "lru_lambda": jnp.log(base) - jnp.log1p(-base),
        "ret_gn_g": 1.0 + nrm((L, C_WIDTH), 0.02),
        "w_out": nrm((L, MIX_WIDTH, D_MODEL), MIX_WIDTH ** -0.5),
        "norm2_g": 1.0 + nrm((L, D_MODEL), 0.02),
        "ffn_w_gate": nrm((L, D_MODEL, D_FF), D_MODEL ** -0.5),
        "ffn_w_up": nrm((L, D_MODEL, D_FF), D_MODEL ** -0.5),
        "ffn_w_down": nrm((L, D_FF, D_MODEL), D_FF ** -0.5),
        "final_norm_g": 1.0 + nrm((D_MODEL,), 0.02),
    }


def reference(x, positions, norm1_g, w_in, tshift_mu, rwkv_w0, rwkv_w2, rwkv_a0, rwkv_a2, rwkv_g2,
              rwkv_k_k, rwkv_k_a, rwkv_r_k, rwkv_ln_g, rwkv_ln_b, w_in_vres, tshift_mu_vres, rwkv_v0,
              rwkv_v2, lru_conv_w, lru_conv_b, lru_wa, lru_ba, lru_wx, lru_bx, lru_lambda, ret_gn_g,
              w_out, norm2_g, ffn_w_gate, ffn_w_up, ffn_w_down, final_norm_g):
    cos, sin = rope_tables(positions)
    log_gamma = jnp.log1p(-jnp.exp2(-5.0 - jnp.arange(C_HEADS, dtype=jnp.float32)))
    v_first = None
    for l in range(DEPTH):
        h = rms_norm(x, norm1_g[l])
        w = w_in[l] if l == 0 else jnp.concatenate([w_in[l], w_in_vres[l - 1]], axis=1)
        proj = h @ w
        a_cols = proj[..., :A_COLS]
        b_cols = proj[..., A_COLS:A_COLS + B_COLS]
        c_cols = proj[..., A_COLS + B_COLS:IN_COLS]
        a_cols = a_cols + (token_shift(a_cols) - a_cols) * tshift_mu[l]
        r_a, k_a, v_a, wd, ad, gd = jnp.split(a_cols, A_SPLITS, axis=-1)
        if l == 0:
            vres = None
        else:
            vd = proj[..., IN_COLS:]
            vd = vd + (token_shift(vd) - vd) * tshift_mu_vres[l - 1]
            vres = (vd, rwkv_v0[l - 1], rwkv_v2[l - 1])
        y_a, v_cur = rwkv7_mixer(r_a, k_a, v_a, wd, ad, gd, rwkv_w0[l], rwkv_w2[l], rwkv_a0[l],
                                 rwkv_a2[l], rwkv_g2[l], rwkv_k_k[l], rwkv_k_a[l], rwkv_r_k[l],
                                 rwkv_ln_g[l], rwkv_ln_b[l], v_first, vres)
        if l == 0:
            v_first = v_cur
        gate_b, x_b = jnp.split(b_cols, 2, axis=-1)
        y_b = rglru_mixer(gate_b, x_b, lru_conv_w[l], lru_conv_b[l], lru_wa[l], lru_ba[l],
                          lru_wx[l], lru_bx[l], lru_lambda[l])
        q_c, k_c, v_c, g_c = jnp.split(c_cols, C_SPLITS, axis=-1)
        y_c = retention_mixer(q_c, k_c, v_c, g_c, cos, sin, log_gamma, ret_gn_g[l])
        y = jnp.concatenate([y_a, y_b, y_c], axis=-1).astype(x.dtype)
        x = x + y @ w_out[l]
        h = rms_norm(x, norm2_g[l])
        x = x + (jax.nn.silu(h @ ffn_w_gate[l]) * (h @ ffn_w_up[l])) @ ffn_w_down[l]
    return rms_norm(x, final_norm_g)
```

```python
import functools

import numpy as np
import jax
import jax.numpy as jnp
from jax import lax
from jax.experimental import pallas as pl
from jax.experimental.pallas import tpu as pltpu

F32 = jnp.float32
BF16 = jnp.bfloat16
HI = lax.Precision.HIGHEST

NORM_EPS = 1e-6
CHUNK = 64

A_HEADS = 12
A_HEAD_DIM = 64
A_WIDTH = A_HEADS * A_HEAD_DIM
DECAY_LORA = 96
ICLR_LORA = 96
VRES_LORA = 64
GATE_LORA = 256
A_GN_EPS = 64e-5
A_COLS = 3 * A_WIDTH + DECAY_LORA + ICLR_LORA + GATE_LORA
B_WIDTH = 512
B_BLOCKS = 8
B_BLOCK_DIM = B_WIDTH // B_BLOCKS
CONV_WIDTH = 4
LRU_C = 8.0
B_COLS = 2 * B_WIDTH
C_HEADS = 6
C_QK_DIM = 64
C_V_DIM = 128
C_QK_WIDTH = C_HEADS * C_QK_DIM
C_WIDTH = C_HEADS * C_V_DIM
C_GN_EPS = 1e-5
ROPE_THETA = 10000.0
C_COLS = 2 * C_QK_WIDTH + 2 * C_WIDTH

LANES = 128
SUBLANES = 8
LORA_SLOT = 128
VMEM_LIMIT = 56 * 1024 * 1024

P_COLS = C_COLS + 3 * A_WIDTH + B_COLS + 2 * LORA_SLOT + GATE_LORA + LORA_SLOT


def _cparams(sem):
    return pltpu.CompilerParams(dimension_semantics=sem, vmem_limit_bytes=VMEM_LIMIT)


def _dot(a, b):
    return jnp.dot(a.astype(BF16), b.astype(BF16), preferred_element_type=F32)


def _dot_nt(a, b):
    return lax.dot_general(a.astype(BF16), b.astype(BF16), (((1,), (1,)), ((), ())),
                           preferred_element_type=F32)


def _dot_tn(a, b):
    return lax.dot_general(a.astype(BF16), b.astype(BF16), (((0,), (0,)), ((), ())),
                           preferred_element_type=F32)


def _dot_hi(a, b):
    return jnp.dot(a, b, precision=HI, preferred_element_type=F32)


def _iota2(shape, axis):
    return lax.broadcasted_iota(jnp.int32, shape, axis)


def _blk(idx, size):
    return idx >> (size.bit_length() - 1)


def _rms(x, g):
    ms = jnp.mean(x * x, axis=-1, keepdims=True)
    return x * lax.rsqrt(ms + NORM_EPS) * g


def _inproj_kernel(x_ref, g_ref, w_ref, o_ref, h_ref):
    @pl.when(pl.program_id(1) == 0)
    def _():
        h_ref[...] = _rms(x_ref[...], g_ref[...]).astype(BF16)

    o_ref[...] = jnp.dot(h_ref[...], w_ref[...], preferred_element_type=F32)


def _inproj(x, g, w, tm, tn):
    t, d = x.shape
    n = w.shape[1]
    return pl.pallas_call(
        _inproj_kernel,
        grid=(t // tm, n // tn),
        in_specs=[pl.BlockSpec((tm, d), lambda i, j: (i, 0)),
                  pl.BlockSpec((1, d), lambda i, j: (0, 0)),
                  pl.BlockSpec((d, tn), lambda i, j: (0, j))],
        out_specs=pl.BlockSpec((tm, tn), lambda i, j: (i, j)),
        out_shape=jax.ShapeDtypeStruct((t, n), F32),
        scratch_shapes=[pltpu.VMEM((tm, d), BF16)],
        compiler_params=_cparams(("parallel", "arbitrary")),
        name="inproj",
    )(x, g, w)


def _shift_rows(cur, prev_row):
    rolled = pltpu.roll(cur, 1, 0)
    row = _iota2(cur.shape, 0)
    return jnp.where(row == 0, prev_row, rolled)


def _head_sum(x, e):
    parts = [_dot_hi(x[:, i * LANES:(i + 1) * LANES], e) for i in range(x.shape[1] // LANES)]
    return jnp.concatenate(parts, axis=-1)


def _head_ones():
    r = _blk(_iota2((LANES, LANES), 0), A_HEAD_DIM)
    c = _blk(_iota2((LANES, LANES), 1), A_HEAD_DIM)
    return (r == c).astype(F32)


def _prep_kernel(has_vres, seq_tiles, *refs):
    if has_vres:
        (rkv_ref, lora_ref, vd_ref, prkv_ref, plora_ref, pvd_ref, mu_rkv, mu_lora, mu_vd,
         w0, a0, k_k, k_a, r_k, w2, a2, g2, v0, v2, vfirst_ref,
         at_o, rt_o, kt_o, bt_o, kd_o, bd_o, pc_o, v_o, g_o, bonus_o) = refs
    else:
        (rkv_ref, lora_ref, prkv_ref, plora_ref, mu_rkv, mu_lora,
         w0, a0, k_k, k_a, r_k, w2, a2, g2,
         at_o, rt_o, kt_o, bt_o, kd_o, bd_o, pc_o, v_o, g_o, bonus_o) = refs
    first = (pl.program_id(0) % seq_tiles) == 0
    last_row = slice(SUBLANES - 1, SUBLANES)

    def mixed(cur_ref, prev_ref, mu_ref):
        cur = cur_ref[...]
        prev_row = jnp.where(first, 0.0, prev_ref[last_row, :])
        return cur + (_shift_rows(cur, prev_row) - cur) * mu_ref[...]

    rkv = mixed(rkv_ref, prkv_ref, mu_rkv)
    lora = mixed(lora_ref, plora_ref, mu_lora)
    r = rkv[:, 0:A_WIDTH]
    k = rkv[:, A_WIDTH:2 * A_WIDTH]
    v = rkv[:, 2 * A_WIDTH:3 * A_WIDTH]
    wd = lora[:, 0:LORA_SLOT]
    ad = lora[:, LORA_SLOT:2 * LORA_SLOT]
    gd = lora[:, 2 * LORA_SLOT:2 * LORA_SLOT + GATE_LORA]

    log_w = -jax.nn.softplus(-(w0[...] + _dot_hi(jnp.tanh(wd), w2[...]))) - 0.5
    lw = -jnp.exp(log_w)
    a = jax.nn.sigmoid(a0[...] + _dot_hi(ad, a2[...]))
    g = _dot_hi(jax.nn.sigmoid(gd), g2[...])
    if has_vres:
        vd = mixed(vd_ref, pvd_ref, mu_vd)
        v = v + (vfirst_ref[...] - v) * jax.nn.sigmoid(v0[...] + _dot_hi(vd, v2[...]))

    e = _head_ones()
    kk = k * k_k[...]
    kk = kk / jnp.maximum(jnp.sqrt(_head_sum(kk * kk, e)), 1e-12)
    k = k * (1.0 + (a - 1.0) * k_a[...])
    b = kk * a
    bonus = _head_sum(r * k * r_k[...], e) * v

    tm = lw.shape[0]
    row = _iota2((tm, tm), 0)
    col = _iota2((tm, tm), 1)
    same = _blk(row, CHUNK) == _blk(col, CHUNK)
    cum = _dot_hi((same & (row >= col)).astype(F32), lw)
    tot = _dot_hi(same.astype(F32), lw)
    e_in = jnp.exp(cum)
    e_out = jnp.exp(-cum)
    e_end = jnp.exp(tot - cum)
    at_o[...] = -kk * jnp.exp(cum - lw)
    rt_o[...] = r * e_in
    kt_o[...] = k * e_out
    bt_o[...] = b * e_out
    kd_o[...] = k * e_end
    bd_o[...] = b * e_end
    pc_o[...] = jnp.exp(tot)
    v_o[...] = v
    g_o[...] = g
    bonus_o[...] = bonus


def _rwkv_prep(proj, params, v_first, seq_len, tm):
    t = proj.shape[0]
    has_vres = v_first is not None
    seq_tiles = seq_len // tm
    sub = tm // SUBLANES
    rkv_w, lora_w = 3 * A_WIDTH, 2 * LORA_SLOT + GATE_LORA
    rkv_blk = (C_COLS) // rkv_w
    lora_blk = (C_COLS + rkv_w + B_COLS) // lora_w
    vd_blk = (P_COLS - LORA_SLOT) // LORA_SLOT

    def cur(width, blk):
        return pl.BlockSpec((tm, width), lambda i: (i, blk))

    def prev(width, blk):
        return pl.BlockSpec((SUBLANES, width), lambda i: (jnp.maximum(i * sub - 1, 0), blk))

    def vec(width):
        return pl.BlockSpec((1, width), lambda i: (0, 0))

    def mat(rows):
        return pl.BlockSpec((rows, A_WIDTH), lambda i: (0, 0))

    tok = pl.BlockSpec((tm, A_WIDTH), lambda i: (i, 0))
    if has_vres:
        ins = [proj, proj, proj, proj, proj, proj,
               params["mu_rkv"], params["mu_lora"], params["mu_vd"],
               params["w0"], params["a0"], params["k_k"], params["k_a"], params["r_k"],
               params["w2"], params["a2"], params["g2"], params["v0"], params["v2"], v_first]
        specs = [cur(rkv_w, rkv_blk), cur(lora_w, lora_blk), cur(LORA_SLOT, vd_blk),
                 prev(rkv_w, rkv_blk), prev(lora_w, lora_blk), prev(LORA_SLOT, vd_blk),
                 vec(rkv_w), vec(lora_w), vec(LORA_SLOT),
                 vec(A_WIDTH), vec(A_WIDTH), vec(A_WIDTH), vec(A_WIDTH), vec(A_WIDTH),
                 mat(LORA_SLOT), mat(LORA_SLOT), mat(GATE_LORA), vec(A_WIDTH), mat(LORA_SLOT), tok]
    else:
        ins = [proj, proj, proj, proj,
               params["mu_rkv"], params["mu_lora"],
               params["w0"], params["a0"], params["k_k"], params["k_a"], params["r_k"],
               params["w2"], params["a2"], params["g2"]]
        specs = [cur(rkv_w, rkv_blk), cur(lora_w, lora_blk),
                 prev(rkv_w, rkv_blk), prev(lora_w, lora_blk),
                 vec(rkv_w), vec(lora_w),
                 vec(A_WIDTH), vec(A_WIDTH), vec(A_WIDTH), vec(A_WIDTH), vec(A_WIDTH),
                 mat(LORA_SLOT), mat(LORA_SLOT), mat(GATE_LORA)]
    n_out = 10
    return pl.pallas_call(
        functools.partial(_prep_kernel, has_vres, seq_tiles),
        grid=(t // tm,),
        in_specs=specs,
        out_specs=[tok] * n_out,
        out_shape=[jax.ShapeDtypeStruct((t, A_WIDTH), F32)] * n_out,
        compiler_params=_cparams(("parallel",)),
        name="rwkv_prep_vres" if has_vres else "rwkv_prep",
    )(*ins)


def _unit_lower_inverse(a_strict, row, col, eye):
    base = 8
    n = jnp.where(_blk(row, base) == _blk(col, base), a_strict, 0.0)
    t = eye + n
    n2 = _dot(n, n)
    t = t + _dot(t, n2)
    n4 = _dot(n2, n2)
    t = t + _dot(t, n4)
    m = base
    while m < CHUNK:
        sel = (_blk(row, 2 * m) == _blk(col, 2 * m)) & (_blk(row, m) != _blk(col, m))
        e = jnp.where(sel, a_strict, 0.0)
        t = t + _dot(_dot(t, e), t)
        m *= 2
    return t


def _scan_kernel(cps, at_ref, rt_ref, kt_ref, bt_ref, kd_ref, bd_ref, v_ref, pc_ref, y_ref, s_ref):
    @pl.when(pl.program_id(1) == 0)
    def _():
        s_ref[...] = jnp.zeros_like(s_ref)

    n = 2 * A_HEAD_DIM
    row = _iota2((n, n), 0)
    col = _iota2((n, n), 1)
    strict = row > col
    incl = row >= col
    eye = (row == col).astype(F32)
    lo = _iota2((CHUNK, n), 1) < A_HEAD_DIM

    def blockdiag(x):
        return jnp.concatenate([jnp.where(lo, x, 0.0), jnp.where(lo, 0.0, x)], axis=0)

    for c in range(cps):
        rows = slice(c * CHUNK, (c + 1) * CHUNK)
        for p in range(A_HEADS // 2):
            cols = slice(p * n, (p + 1) * n)
            at = blockdiag(at_ref[0, rows, cols])
            rt = blockdiag(rt_ref[0, rows, cols])
            kt = blockdiag(kt_ref[0, rows, cols])
            bt = blockdiag(bt_ref[0, rows, cols])
            kd = blockdiag(kd_ref[0, rows, cols])
            bd = blockdiag(bd_ref[0, rows, cols])
            v = blockdiag(v_ref[0, rows, cols])
            pc = pc_ref[0, c * CHUNK:c * CHUNK + 1, cols]
            s = s_ref[p]

            lhs = jnp.concatenate([at, rt], axis=0).astype(BF16)
            lk = _dot_nt(lhs, kt)
            lb = _dot_nt(lhs, bt)
            a_ak = jnp.where(strict, lk[:n], 0.0)
            a_rk = jnp.where(incl, lk[n:], 0.0)
            a_ab = jnp.where(strict, lb[:n], 0.0)
            a_rb = jnp.where(incl, lb[n:], 0.0)
            t_inv = _unit_lower_inverse(a_ab, row, col, eye)

            lh = _dot_nt(lhs, s)
            lv = _dot(jnp.concatenate([a_ak, a_rk], axis=0), v)
            u = _dot(t_inv, lh[:n] + lv[:n])
            y = lh[n:] + lv[n:] + _dot(a_rb, u)
            s_ref[p] = s * pc + _dot_tn(jnp.concatenate([v, u], axis=0),
                                        jnp.concatenate([kd, bd], axis=0))
            y_ref[0, rows, cols] = y[:CHUNK] + y[CHUNK:]


def _rwkv_scan(at, rt, kt, bt, kd, bd, v, pc, batch, seq_len, cps):
    ts = cps * CHUNK
    shp = (batch, seq_len, A_WIDTH)
    blk = pl.BlockSpec((1, ts, A_WIDTH), lambda b, s: (b, s, 0))
    args = [z.reshape(shp) for z in (at, rt, kt, bt, kd, bd, v, pc)]
    y = pl.pallas_call(
        functools.partial(_scan_kernel, cps),
        grid=(batch, seq_len // ts),
        in_specs=[blk] * 8,
        out_specs=blk,
        out_shape=jax.ShapeDtypeStruct(shp, F32),
        scratch_shapes=[pltpu.VMEM((A_HEADS // 2, 2 * A_HEAD_DIM, 2 * A_HEAD_DIM), F32)],
        compiler_params=_cparams(("parallel", "arbitrary")),
        name="rwkv_scan",
    )(*args)
    return y.reshape(batch * seq_len, A_WIDTH)


def _post_kernel(y_ref, g_ref, bonus_ref, lng_ref, lnb_ref, o_ref):
    e = _head_ones()
    y = y_ref[...]
    inv_n = 1.0 / A_HEAD_DIM
    mu = _head_sum(y, e) * inv_n
    d = y - mu
    var = _head_sum(d * d, e) * inv_n
    yn = d * lax.rsqrt(var + A_GN_EPS) * lng_ref[...] + lnb_ref[...]
    o_ref[...] = ((yn + bonus_ref[...]) * g_ref[...]).astype(o_ref.dtype)


def _rwkv_post(y, g, bonus, ln_g, ln_b, tm):
    t = y.shape[0]
    tok = pl.BlockSpec((tm, A_WIDTH), lambda i: (i, 0))
    vec = pl.BlockSpec((1, A_WIDTH), lambda i: (0, 0))
    return pl.pallas_call(
        _post_kernel,
        grid=(t // tm,),
        in_specs=[tok, tok, tok, vec, vec],
        out_specs=tok,
        out_shape=jax.ShapeDtypeStruct((t, A_WIDTH), BF16),
        compiler_params=_cparams(("parallel",)),
        name="rwkv_post",
    )(y, g, bonus, ln_g, ln_b)


def _lru_kernel(gate_ref, x_ref, px_ref, cw_ref, cb_ref, wa_ref, ba_ref, wx_ref, bx_ref, lam_ref,
                o_ref, h_ref):
    first = pl.program_id(1) == 0

    @pl.when(first)
    def _():
        h_ref[...] = jnp.zeros_like(h_ref)

    x = x_ref[...]
    ts = x.shape[0]
    prev = jnp.where(first, 0.0, px_ref[...])
    row = _iota2(x.shape, 0)
    row8 = _iota2(prev.shape, 0)

    def delayed(d):
        rolled = pltpu.roll(x, d, 0)
        top = jnp.where(row8 < d, pltpu.roll(prev, d, 0), rolled[:SUBLANES])
        return jnp.concatenate([top, rolled[SUBLANES:]], axis=0)

    xc = cw_ref[CONV_WIDTH - 1:CONV_WIDTH, :] * x + cb_ref[...]
    for d in range(1, CONV_WIDTH):
        xc = xc + cw_ref[CONV_WIDTH - 1 - d:CONV_WIDTH - d, :] * delayed(d)

    r = jax.nn.sigmoid(_dot(xc, wa_ref[...]) + ba_ref[...])
    i = jax.nn.sigmoid(_dot(xc, wx_ref[...]) + bx_ref[...])
    log_a = -LRU_C * r * jax.nn.softplus(-lam_ref[...])
    a = jnp.exp(log_a)
    th = jnp.tanh(log_a)
    u = jnp.sqrt(-2.0 * th / (1.0 - th)) * (i * xc)

    d = 1
    while d < ts:
        keep = row >= d
        a_s = jnp.where(keep, pltpu.roll(a, d, 0), 1.0)
        u_s = jnp.where(keep, pltpu.roll(u, d, 0), 0.0)
        u = a * u_s + u
        a = a * a_s
        d *= 2
    h = a * h_ref[...] + u
    h_ref[...] = h[ts - 1:ts, :]
    o_ref[...] = (jax.nn.gelu(gate_ref[...]) * h).astype(o_ref.dtype)


def _rglru(proj, params, batch, seq_len, ts):
    t = proj.shape[0]
    nst = seq_len // ts
    sub = ts // SUBLANES
    gate_blk = (C_COLS + 3 * A_WIDTH) // B_WIDTH
    x_blk = gate_blk + 1
    vec = pl.BlockSpec((1, B_WIDTH), lambda b, s: (0, 0))
    sq = pl.BlockSpec((B_WIDTH, B_WIDTH), lambda b, s: (0, 0))
    return pl.pallas_call(
        _lru_kernel,
        grid=(batch, nst),
        in_specs=[pl.BlockSpec((ts, B_WIDTH), lambda b, s: (b * nst + s, gate_blk)),
                  pl.BlockSpec((ts, B_WIDTH), lambda b, s: (b * nst + s, x_blk)),
                  pl.BlockSpec((SUBLANES, B_WIDTH),
                               lambda b, s: (jnp.maximum((b * nst + s) * sub - 1, 0), x_blk)),
                  pl.BlockSpec((CONV_WIDTH, B_WIDTH), lambda b, s: (0, 0)),
                  vec, sq, vec, sq, vec, vec],
        out_specs=pl.BlockSpec((ts, B_WIDTH), lambda b, s: (b * nst + s, 0)),
        out_shape=jax.ShapeDtypeStruct((t, B_WIDTH), BF16),
        scratch_shapes=[pltpu.VMEM((1, B_WIDTH), F32)],
        compiler_params=_cparams(("parallel", "arbitrary")),
        name="rglru",
    )(proj, proj, proj, params["conv_w"], params["conv_b"], params["wa"], params["ba"],
      params["wx"], params["bx"], params["lam"])


def _log_gammas():
    return np.log1p(-np.exp2(-5.0 - np.arange(C_HEADS, dtype=np.float32))).astype(np.float32)


def _ret_kernel(cpr, q_ref, k_ref, v_ref, g_ref, pos_ref, invf_ref, gng_ref, o_ref, st_ref):
    @pl.when(pl.program_id(1) == 0)
    def _():
        st_ref[...] = jnp.zeros_like(st_ref)

    half = C_QK_DIM // 2
    lane = _iota2((CHUNK, LANES), 1)
    in_first_half = (lane & (C_QK_DIM - 1)) < half
    lo = lane < C_QK_DIM
    idx_r = _iota2((CHUNK, CHUNK), 0).astype(F32)
    idx_c = _iota2((CHUNK, CHUNK), 1).astype(F32)
    dist = jnp.abs(idx_r - idx_c)
    pos_in = _iota2((CHUNK, 1), 0).astype(F32)
    log_gamma = _log_gammas()

    def rope(t, cos, sin_signed):
        swapped = jnp.where(in_first_half, pltpu.roll(t, LANES - half, 1), pltpu.roll(t, half, 1))
        return t * cos + swapped * sin_signed

    for c in range(cpr):
        rows = slice(c * CHUNK, (c + 1) * CHUNK)
        ang = pos_ref[0, rows, :] * invf_ref[...]
        cos = jnp.cos(ang)
        sin = jnp.sin(ang)
        sin_signed = jnp.where(in_first_half, -sin, sin)
        for p in range(C_HEADS // 2):
            qk_cols = slice(p * LANES, (p + 1) * LANES)
            q2 = rope(q_ref[rows, qk_cols], cos, sin_signed)
            k2 = rope(k_ref[rows, qk_cols], cos, sin_signed) * (C_QK_DIM ** -0.5)
            for j in range(2):
                h = 2 * p + j
                lg = float(log_gamma[h])
                mine = lo if j == 0 else jnp.logical_not(lo)
                qh = jnp.where(mine, q2, 0.0)
                kh = jnp.where(mine, k2, 0.0)
                v_cols = slice(h * C_V_DIM, (h + 1) * C_V_DIM)
                vh = v_ref[rows, v_cols]
                scores = _dot_nt(qh, kh) * jnp.exp(lg * dist)
                intra = _dot(scores, vh)
                state = st_ref[h]
                cross = _dot(qh * jnp.exp(lg * (pos_in + 1.0)), state)
                kv = _dot_tn(kh * jnp.exp(lg * (CHUNK - 1.0 - pos_in)), vh)
                st_ref[h] = state * float(np.exp(np.float32(lg) * np.float32(CHUNK))) + kv
                o = intra + cross
                mu = jnp.mean(o, axis=-1, keepdims=True)
                d = o - mu
                var = jnp.mean(d * d, axis=-1, keepdims=True)
                on = d * lax.rsqrt(var + C_GN_EPS) * gng_ref[:, v_cols]
                o_ref[rows, v_cols] = (jax.nn.silu(g_ref[rows, v_cols]) * on).astype(o_ref.dtype)


def _retention(proj, posb, invf, gn_g, batch, seq_len, cpr):
    t = proj.shape[0]
    ts = cpr * CHUNK
    nst = seq_len // ts
    return pl.pallas_call(
        functools.partial(_ret_kernel, cpr),
        grid=(batch, nst),
        in_specs=[pl.BlockSpec((ts, C_QK_WIDTH), lambda b, s: (b * nst + s, 0)),
                  pl.BlockSpec((ts, C_QK_WIDTH), lambda b, s: (b * nst + s, 1)),
                  pl.BlockSpec((ts, C_WIDTH), lambda b, s: (b * nst + s, 1)),
                  pl.BlockSpec((ts, C_WIDTH), lambda b, s: (b * nst + s, 2)),
                  pl.BlockSpec((1, ts, LANES), lambda b, s: (b, s, 0)),
                  pl.BlockSpec((1, LANES), lambda b, s: (0, 0)),
                  pl.BlockSpec((1, C_WIDTH), lambda b, s: (0, 0))],
        out_specs=pl.BlockSpec((ts, C_WIDTH), lambda b, s: (b * nst + s, 0)),
        out_shape=jax.ShapeDtypeStruct((t, C_WIDTH), BF16),
        scratch_shapes=[pltpu.VMEM((C_HEADS, LANES, C_V_DIM), F32)],
        compiler_params=_cparams(("parallel", "arbitrary")),
        name="retention",
    )(proj, proj, proj, proj, posb, invf, gn_g)


def _outproj_kernel(x_ref, ya_ref, yb_ref, yc_ref, wa_ref, wb_ref, wc_ref, o_ref):
    acc = jnp.dot(ya_ref[...], wa_ref[...], preferred_element_type=F32)
    acc += jnp.dot(yb_ref[...], wb_ref[...], preferred_element_type=F32)
    acc += jnp.dot(yc_ref[...], wc_ref[...], preferred_element_type=F32)
    o_ref[...] = x_ref[...] + acc


def _outproj(x, ya, yb, yc, wa, wb, wc, tm):
    t, d = x.shape

    def tok(width):
        return pl.BlockSpec((tm, width), lambda i: (i, 0))

    def wspec(rows):
        return pl.BlockSpec((rows, d), lambda i: (0, 0))

    return pl.pallas_call(
        _outproj_kernel,
        grid=(t // tm,),
        in_specs=[tok(d), tok(A_WIDTH), tok(B_WIDTH), tok(C_WIDTH),
                  wspec(A_WIDTH), wspec(B_WIDTH), wspec(C_WIDTH)],
        out_specs=tok(d),
        out_shape=jax.ShapeDtypeStruct((t, d), F32),
        compiler_params=_cparams(("parallel",)),
        name="outproj",
    )(x, ya, yb, yc, wa, wb, wc)


def _ffn_kernel(final, x_ref, g_ref, wg_ref, wu_ref, wd_ref, fg_ref, o_ref, h_ref, acc_ref):
    j = pl.program_id(1)

    @pl.when(j == 0)
    def _():
        h_ref[...] = _rms(x_ref[...], g_ref[...]).astype(BF16)
        acc_ref[...] = jnp.zeros_like(acc_ref)

    h = h_ref[...]
    gate = jnp.dot(h, wg_ref[...], preferred_element_type=F32)
    up = jnp.dot(h, wu_ref[...], preferred_element_type=F32)
    act = (jax.nn.silu(gate) * up).astype(BF16)
    acc_ref[...] += jnp.dot(act, wd_ref[...], preferred_element_type=F32)

    @pl.when(j == pl.num_programs(1) - 1)
    def _():
        out = x_ref[...] + acc_ref[...]
        if final:
            out = _rms(out, fg_ref[...])
        o_ref[...] = out


def _ffn(x, g, wg, wu, wd, fg, final, tm, tf):
    t, d = x.shape
    f = wg.shape[1]
    return pl.pallas_call(
        functools.partial(_ffn_kernel, final),
        grid=(t // tm, f // tf),
        in_specs=[pl.BlockSpec((tm, d), lambda i, j: (i, 0)),
                  pl.BlockSpec((1, d), lambda i, j: (0, 0)),
                  pl.BlockSpec((d, tf), lambda i, j: (0, j)),
                  pl.BlockSpec((d, tf), lambda i, j: (0, j)),
                  pl.BlockSpec((tf, d), lambda i, j: (j, 0)),
                  pl.BlockSpec((1, d), lambda i, j: (0, 0))],
        out_specs=pl.BlockSpec((tm, d), lambda i, j: (i, 0)),
        out_shape=jax.ShapeDtypeStruct((t, d), F32),
        scratch_shapes=[pltpu.VMEM((tm, d), BF16), pltpu.VMEM((tm, d), F32)],
        compiler_params=_cparams(("parallel", "arbitrary")),
        name="ffn_final" if final else "ffn",
    )(x, g, wg, wu, wd, fg)


def _pad_cols(w, width):
    return jnp.pad(w, ((0, 0), (0, width - w.shape[1])))


def _pad_rows(w, rows):
    return jnp.pad(w, ((0, rows - w.shape[0]), (0, 0)))


def _pad_vec(v, width):
    return jnp.pad(v, (0, width - v.shape[0]))


def _layer_in_weight(w_in_l, w_vres_l):
    d = w_in_l.shape[0]
    aw = A_WIDTH
    o_wd, o_ad, o_gd = 3 * aw, 3 * aw + DECAY_LORA, 3 * aw + DECAY_LORA + ICLR_LORA
    vd = jnp.zeros((d, LORA_SLOT), F32) if w_vres_l is None else _pad_cols(w_vres_l, LORA_SLOT)
    cols = [w_in_l[:, A_COLS + B_COLS:],
            w_in_l[:, :3 * aw],
            w_in_l[:, A_COLS:A_COLS + B_COLS],
            _pad_cols(w_in_l[:, o_wd:o_ad], LORA_SLOT),
            _pad_cols(w_in_l[:, o_ad:o_gd], LORA_SLOT),
            w_in_l[:, o_gd:A_COLS],
            vd]
    return jnp.concatenate(cols, axis=1).astype(BF16)


def _block_diag(w):
    g, n, _ = w.shape
    eye = jnp.eye(g, dtype=w.dtype)
    return (eye[:, None, :, None] * w[:, :, None, :]).reshape(g * n, g * n)


def kernel(x, positions, norm1_g, w_in, tshift_mu, rwkv_w0, rwkv_w2, rwkv_a0, rwkv_a2, rwkv_g2, rwkv_k_k, rwkv_k_a, rwkv_r_k, rwkv_ln_g, rwkv_ln_b, w_in_vres, tshift_mu_vres, rwkv_v0, rwkv_v2, lru_conv_w, lru_conv_b, lru_wa, lru_ba, lru_wx, lru_bx, lru_lambda, ret_gn_g, w_out, norm2_g, ffn_w_gate, ffn_w_up, ffn_w_down, final_norm_g):
    batch, seq_len, d = x.shape
    depth = w_in.shape[0]
    t = batch * seq_len
    aw = A_WIDTH
    row = lambda v: v.reshape(1, -1).astype(F32)

    tm_proj = min(512, seq_len)
    tm_tok = min(256, seq_len)
    cps = 1
    cpr = min(4, seq_len // CHUNK)

    inv_freq = ROPE_THETA ** (-jnp.arange(0, C_QK_DIM, 2, dtype=F32) / C_QK_DIM)
    invf = jnp.tile(inv_freq, LANES // inv_freq.shape[0]).reshape(1, LANES)
    posb = jnp.broadcast_to(positions.astype(F32)[..., None], (batch, seq_len, LANES))

    xf = x.reshape(t, d)
    v_first = None
    for l in range(depth):
        has_vres = l > 0
        w_l = _layer_in_weight(w_in[l], w_in_vres[l - 1] if has_vres else None)
        proj = _inproj(xf, row(norm1_g[l]), w_l, tm_proj, P_COLS // 7)

        mu = tshift_mu[l]
        o_wd, o_ad, o_gd = 3 * aw, 3 * aw + DECAY_LORA, 3 * aw + DECAY_LORA + ICLR_LORA
        params = {
            "mu_rkv": row(mu[:3 * aw]),
            "mu_lora": row(jnp.concatenate([_pad_vec(mu[o_wd:o_ad], LORA_SLOT),
                                            _pad_vec(mu[o_ad:o_gd], LORA_SLOT), mu[o_gd:A_COLS]])),
            "w0": row(rwkv_w0[l]), "a0": row(rwkv_a0[l]), "k_k": row(rwkv_k_k[l]),
            "k_a": row(rwkv_k_a[l]), "r_k": row(rwkv_r_k[l]),
            "w2": _pad_rows(rwkv_w2[l], LORA_SLOT), "a2": _pad_rows(rwkv_a2[l], LORA_SLOT),
            "g2": rwkv_g2[l],
        }
        if has_vres:
            params["mu_vd"] = row(_pad_vec(tshift_mu_vres[l - 1], LORA_SLOT))
            params["v0"] = row(rwkv_v0[l - 1])
            params["v2"] = _pad_rows(rwkv_v2[l - 1], LORA_SLOT)
        at, rt, kt, bt, kd, bd, pc, v_cur, g_a, bonus = _rwkv_prep(
            proj, params, v_first if has_vres else None, seq_len, tm_tok)
        if l == 0:
            v_first = v_cur
        y_raw = _rwkv_scan(at, rt, kt, bt, kd, bd, v_cur, pc, batch, seq_len, cps)
        y_a = _rwkv_post(y_raw, g_a, bonus, row(rwkv_ln_g[l]), row(rwkv_ln_b[l]), tm_tok)

        lru_params = {
            "conv_w": lru_conv_w[l], "conv_b": row(lru_conv_b[l]),
            "wa": _block_diag(lru_wa[l]).astype(BF16), "ba": row(lru_ba[l]),
            "wx": _block_diag(lru_wx[l]).astype(BF16), "bx": row(lru_bx[l]),
            "lam": row(lru_lambda[l]),
        }
        y_b = _rglru(proj, lru_params, batch, seq_len, tm_tok)
        y_c = _retention(proj, posb, invf, row(ret_gn_g[l]), batch, seq_len, cpr)

        wo = w_out[l].astype(BF16)
        xf = _outproj(xf, y_a, y_b, y_c, wo[:aw], wo[aw:aw + B_WIDTH], wo[aw + B_WIDTH:], tm_proj)
        xf = _ffn(xf, row(norm2_g[l]), ffn_w_gate[l].astype(BF16), ffn_w_up[l].astype(BF16),
                  ffn_w_down[l].astype(BF16), row(final_norm_g), l == depth - 1, tm_proj, 512)
    return xf.reshape(batch, seq_len, d)
```

```python
import functools

import numpy as np
import jax
import jax.numpy as jnp
from jax import lax
from jax.experimental import pallas as pl
from jax.experimental.pallas import tpu as pltpu

F32 = jnp.float32
BF16 = jnp.bfloat16
HI = lax.Precision.HIGHEST

NORM_EPS = 1e-6
CHUNK = 64

A_HEADS = 12
A_HEAD_DIM = 64
A_WIDTH = A_HEADS * A_HEAD_DIM
DECAY_LORA = 96
ICLR_LORA = 96
VRES_LORA = 64
GATE_LORA = 256
A_GN_EPS = 64e-5
A_COLS = 3 * A_WIDTH + DECAY_LORA + ICLR_LORA + GATE_LORA
B_WIDTH = 512
B_BLOCKS = 8
B_BLOCK_DIM = B_WIDTH // B_BLOCKS
CONV_WIDTH = 4
LRU_C = 8.0
B_COLS = 2 * B_WIDTH
C_HEADS = 6
C_QK_DIM = 64
C_V_DIM = 128
C_QK_WIDTH = C_HEADS * C_QK_DIM
C_WIDTH = C_HEADS * C_V_DIM
C_GN_EPS = 1e-5
ROPE_THETA = 10000.0
C_COLS = 2 * C_QK_WIDTH + 2 * C_WIDTH

LANES = 128
SUBLANES = 8
LORA_SLOT = 128
VMEM_LIMIT = 56 * 1024 * 1024

P_COLS = C_COLS + 3 * A_WIDTH + B_COLS + 2 * LORA_SLOT + GATE_LORA + LORA_SLOT


def _cparams(sem):
    return pltpu.CompilerParams(dimension_semantics=sem, vmem_limit_bytes=VMEM_LIMIT)


def _dot(a, b):
    return jnp.dot(a.astype(BF16), b.astype(BF16), preferred_element_type=F32)


def _dot_nt(a, b):
    return lax.dot_general(a.astype(BF16), b.astype(BF16), (((1,), (1,)), ((), ())),
                           preferred_element_type=F32)


def _dot_tn(a, b):
    return lax.dot_general(a.astype(BF16), b.astype(BF16), (((0,), (0,)), ((), ())),
                           preferred_element_type=F32)


def _dot_hi(a, b):
    return jnp.dot(a, b, precision=HI, preferred_element_type=F32)


def _iota2(shape, axis):
    return lax.broadcasted_iota(jnp.int32, shape, axis)


def _blk(idx, size):
    return idx >> (size.bit_length() - 1)


def _rms(x, g):
    ms = jnp.mean(x * x, axis=-1, keepdims=True)
    return x * lax.rsqrt(ms + NORM_EPS) * g


def _inproj_kernel(x_ref, g_ref, w_ref, o_ref, h_ref):
    @pl.when(pl.program_id(1) == 0)
    def _():
        h_ref[...] = _rms(x_ref[...], g_ref[...]).astype(BF16)

    o_ref[...] = jnp.dot(h_ref[...], w_ref[...], preferred_element_type=F32)


def _inproj(x, g, w, tm, tn):
    t, d = x.shape
    n = w.shape[1]
    return pl.pallas_call(
        _inproj_kernel,
        grid=(t // tm, n // tn),
        in_specs=[pl.BlockSpec((tm, d), lambda i, j: (i, 0)),
                  pl.BlockSpec((1, d), lambda i, j: (0, 0)),
                  pl.BlockSpec((d, tn), lambda i, j: (0, j))],
        out_specs=pl.BlockSpec((tm, tn), lambda i, j: (i, j)),
        out_shape=jax.ShapeDtypeStruct((t, n), F32),
        scratch_shapes=[pltpu.VMEM((tm, d), BF16)],
        compiler_params=_cparams(("parallel", "arbitrary")),
        name="inproj",
    )(x, g, w)


def _shift_rows(cur, prev_row):
    rolled = pltpu.roll(cur, 1, 0)
    row = _iota2(cur.shape, 0)
    return jnp.where(row == 0, prev_row, rolled)


def _head_sum(x, e):
    parts = [_dot_hi(x[:, i * LANES:(i + 1) * LANES], e) for i in range(x.shape[1] // LANES)]
    return jnp.concatenate(parts, axis=-1)


def _head_ones():
    r = _blk(_iota2((LANES, LANES), 0), A_HEAD_DIM)
    c = _blk(_iota2((LANES, LANES), 1), A_HEAD_DIM)
    return (r == c).astype(F32)


def _prep_kernel(has_vres, seq_tiles, *refs):
    if has_vres:
        (rkv_ref, lora_ref, vd_ref, prkv_ref, plora_ref, pvd_ref, mu_rkv, mu_lora, mu_vd,
         w0, a0, k_k, k_a, r_k, w2, a2, g2, v0, v2, vfirst_ref,
         at_o, rt_o, kt_o, bt_o, kd_o, bd_o, pc_o, v_o, g_o, bonus_o) = refs
    else:
        (rkv_ref, lora_ref, prkv_ref, plora_ref, mu_rkv, mu_lora,
         w0, a0, k_k, k_a, r_k, w2, a2, g2,
         at_o, rt_o, kt_o, bt_o, kd_o, bd_o, pc_o, v_o, g_o, bonus_o) = refs
    first = (pl.program_id(0) % seq_tiles) == 0
    last_row = slice(SUBLANES - 1, SUBLANES)

    def mixed(cur_ref, prev_ref, mu_ref):
        cur = cur_ref[...]
        prev_row = jnp.where(first, 0.0, prev_ref[last_row, :])
        return cur + (_shift_rows(cur, prev_row) - cur) * mu_ref[...]

    rkv = mixed(rkv_ref, prkv_ref, mu_rkv)
    lora = mixed(lora_ref, plora_ref, mu_lora)
    r = rkv[:, 0:A_WIDTH]
    k = rkv[:, A_WIDTH:2 * A_WIDTH]
    v = rkv[:, 2 * A_WIDTH:3 * A_WIDTH]
    wd = lora[:, 0:LORA_SLOT]
    ad = lora[:, LORA_SLOT:2 * LORA_SLOT]
    gd = lora[:, 2 * LORA_SLOT:2 * LORA_SLOT + GATE_LORA]

    log_w = -jax.nn.softplus(-(w0[...] + _dot_hi(jnp.tanh(wd), w2[...]))) - 0.5
    lw = -jnp.exp(log_w)
    a = jax.nn.sigmoid(a0[...] + _dot_hi(ad, a2[...]))
    g = _dot_hi(jax.nn.sigmoid(gd), g2[...])
    if has_vres:
        vd = mixed(vd_ref, pvd_ref, mu_vd)
        v = v + (vfirst_ref[...] - v) * jax.nn.sigmoid(v0[...] + _dot_hi(vd, v2[...]))

    e = _head_ones()
    kk = k * k_k[...]
    kk = kk / jnp.maximum(jnp.sqrt(_head_sum(kk * kk, e)), 1e-12)
    k = k * (1.0 + (a - 1.0) * k_a[...])
    b = kk * a
    bonus = _head_sum(r * k * r_k[...], e) * v

    tm = lw.shape[0]
    row = _iota2((tm, tm), 0)
    col = _iota2((tm, tm), 1)
    same = _blk(row, CHUNK) == _blk(col, CHUNK)
    cum = _dot_hi((same & (row >= col)).astype(F32), lw)
    tot = _dot_hi(same.astype(F32), lw)
    e_in = jnp.exp(cum)
    e_out = jnp.exp(-cum)
    e_end = jnp.exp(tot - cum)
    at_o[...] = -kk * jnp.exp(cum - lw)
    rt_o[...] = r * e_in
    kt_o[...] = k * e_out
    bt_o[...] = b * e_out
    kd_o[...] = k * e_end
    bd_o[...] = b * e_end
    pc_o[...] = jnp.exp(tot)
    v_o[...] = v
    g_o[...] = g
    bonus_o[...] = bonus


def _rwkv_prep(proj, params, v_first, seq_len, tm):
    t = proj.shape[0]
    has_vres = v_first is not None
    seq_tiles = seq_len // tm
    sub = tm // SUBLANES
    rkv_w, lora_w = 3 * A_WIDTH, 2 * LORA_SLOT + GATE_LORA
    rkv_blk = (C_COLS) // rkv_w
    lora_blk = (C_COLS + rkv_w + B_COLS) // lora_w
    vd_blk = (P_COLS - LORA_SLOT) // LORA_SLOT

    def cur(width, blk):
        return pl.BlockSpec((tm, width), lambda i: (i, blk))

    def prev(width, blk):
        return pl.BlockSpec((SUBLANES, width), lambda i: (jnp.maximum(i * sub - 1, 0), blk))

    def vec(width):
        return pl.BlockSpec((1, width), lambda i: (0, 0))

    def mat(rows):
        return pl.BlockSpec((rows, A_WIDTH), lambda i: (0, 0))

    tok = pl.BlockSpec((tm, A_WIDTH), lambda i: (i, 0))
    if has_vres:
        ins = [proj, proj, proj, proj, proj, proj,
               params["mu_rkv"], params["mu_lora"], params["mu_vd"],
               params["w0"], params["a0"], params["k_k"], params["k_a"], params["r_k"],
               params["w2"], params["a2"], params["g2"], params["v0"], params["v2"], v_first]
        specs = [cur(rkv_w, rkv_blk), cur(lora_w, lora_blk), cur(LORA_SLOT, vd_blk),
                 prev(rkv_w, rkv_blk), prev(lora_w, lora_blk), prev(LORA_SLOT, vd_blk),
                 vec(rkv_w), vec(lora_w), vec(LORA_SLOT),
                 vec(A_WIDTH), vec(A_WIDTH), vec(A_WIDTH), vec(A_WIDTH), vec(A_WIDTH),
                 mat(LORA_SLOT), mat(LORA_SLOT), mat(GATE_LORA), vec(A_WIDTH), mat(LORA_SLOT), tok]
    else:
        ins = [proj, proj, proj, proj,
               params["mu_rkv"], params["mu_lora"],
               params["w0"], params["a0"], params["k_k"], params["k_a"], params["r_k"],
               params["w2"], params["a2"], params["g2"]]
        specs = [cur(rkv_w, rkv_blk), cur(lora_w, lora_blk),
                 prev(rkv_w, rkv_blk), prev(lora_w, lora_blk),
                 vec(rkv_w), vec(lora_w),
                 vec(A_WIDTH), vec(A_WIDTH), vec(A_WIDTH), vec(A_WIDTH), vec(A_WIDTH),
                 mat(LORA_SLOT), mat(LORA_SLOT), mat(GATE_LORA)]
    n_out = 10
    return pl.pallas_call(
        functools.partial(_prep_kernel, has_vres, seq_tiles),
        grid=(t // tm,),
        in_specs=specs,
        out_specs=[tok] * n_out,
        out_shape=[jax.ShapeDtypeStruct((t, A_WIDTH), F32)] * n_out,
        compiler_params=_cparams(("parallel",)),
        name="rwkv_prep_vres" if has_vres else "rwkv_prep",
    )(*ins)


def _unit_lower_inverse(a_strict, row, col, eye):
    base = 8
    diag = _blk(row, base) == _blk(col, base)
    n1 = [jnp.where(diag, a, 0.0) for a in a_strict]
    n1b = [x.astype(BF16) for x in n1]
    n2 = [_dot(x, x).astype(BF16) for x in n1b]
    t = [eye + x for x in n1]
    t = [x + _dot(x, m) for x, m in zip(t, n2)]
    n4 = [_dot(m, m) for m in n2]
    t = [x + _dot(x, m) for x, m in zip(t, n4)]
    m = base
    while m < CHUNK:
        sel = (_blk(row, 2 * m) == _blk(col, 2 * m)) & (_blk(row, m) != _blk(col, m))
        e = [jnp.where(sel, a, 0.0).astype(BF16) for a in a_strict]
        tb = [x.astype(BF16) for x in t]
        te = [_dot(x, y) for x, y in zip(tb, e)]
        t = [x + _dot(y, xb) for x, y, xb in zip(t, te, tb)]
        m *= 2
    return t


def _scan_kernel(cps, at_ref, rt_ref, kt_ref, bt_ref, kd_ref, bd_ref, v_ref, pc_ref, y_ref, s_ref):
    @pl.when(pl.program_id(1) == 0)
    def _():
        s_ref[...] = jnp.zeros_like(s_ref)

    n = 2 * A_HEAD_DIM
    pairs = A_HEADS // 2
    row = _iota2((n, n), 0)
    col = _iota2((n, n), 1)
    strict = row > col
    incl = row >= col
    eye = (row == col).astype(F32)
    lo = _iota2((CHUNK, n), 1) < A_HEAD_DIM
    items = [(c, p) for c in range(cps) for p in range(pairs)]

    def load(ref):
        out = []
        for c, p in items:
            x = ref[0, c * CHUNK:(c + 1) * CHUNK, p * n:(p + 1) * n]
            out.append(jnp.concatenate([jnp.where(lo, x, 0.0), jnp.where(lo, 0.0, x)], axis=0))
        return out

    at = load(at_ref)
    rt = load(rt_ref)
    lhs = [jnp.concatenate([a, r], axis=0).astype(BF16) for a, r in zip(at, rt)]
    lk = [_dot_nt(x, k) for x, k in zip(lhs, load(kt_ref))]
    lb = [_dot_nt(x, b) for x, b in zip(lhs, load(bt_ref))]
    a_k = [jnp.concatenate([jnp.where(strict, x[:n], 0.0), jnp.where(incl, x[n:], 0.0)],
                           axis=0).astype(BF16) for x in lk]
    a_ab = [jnp.where(strict, x[:n], 0.0) for x in lb]
    a_rb = [jnp.where(incl, x[n:], 0.0).astype(BF16) for x in lb]
    t_inv = _unit_lower_inverse(a_ab, row, col, eye)
    v = [x.astype(BF16) for x in load(v_ref)]
    lv = [_dot(a, x) for a, x in zip(a_k, v)]
    wu = [_dot(t, jnp.concatenate([a, x[:n]], axis=1))
          for t, a, x in zip(t_inv, at, lv)]
    wub = [x.astype(BF16) for x in wu]
    z = [_dot(a, x) for a, x in zip(a_rb, wub)]
    rw = [(r + x[:, :n]).astype(BF16) for r, x in zip(rt, z)]
    y0 = [x[n:] + zz[:, n:] for x, zz in zip(lv, z)]
    q = [_dot_tn(x, b) for x, b in zip(wub, load(bd_ref))]
    vk = [_dot_tn(x, k) for x, k in zip(v, load(kd_ref))]
    g = [x[:n].astype(BF16) for x in q]
    s_add = [x[n:] + y for x, y in zip(q, vk)]

    s = [s_ref[p] for p in range(pairs)]
    for c in range(cps):
        rows = slice(c * CHUNK, (c + 1) * CHUNK)
        sb = [x.astype(BF16) for x in s]
        for p in range(pairs):
            i = c * pairs + p
            y = _dot_nt(rw[i], sb[p]) + y0[i]
            y_ref[0, rows, p * n:(p + 1) * n] = y[:CHUNK] + y[CHUNK:]
        for p in range(pairs):
            i = c * pairs + p
            pc = pc_ref[0, c * CHUNK:c * CHUNK + 1, p * n:(p + 1) * n]
            s[p] = s[p] * pc + _dot(sb[p], g[i]) + s_add[i]
    for p in range(pairs):
        s_ref[p] = s[p]


def _rwkv_scan(at, rt, kt, bt, kd, bd, v, pc, batch, seq_len, cps):
    ts = cps * CHUNK
    shp = (batch, seq_len, A_WIDTH)
    blk = pl.BlockSpec((1, ts, A_WIDTH), lambda b, s: (b, s, 0))
    args = [z.reshape(shp) for z in (at, rt, kt, bt, kd, bd, v, pc)]
    y = pl.pallas_call(
        functools.partial(_scan_kernel, cps),
        grid=(batch, seq_len // ts),
        in_specs=[blk] * 8,
        out_specs=blk,
        out_shape=jax.ShapeDtypeStruct(shp, F32),
        scratch_shapes=[pltpu.VMEM((A_HEADS // 2, 2 * A_HEAD_DIM, 2 * A_HEAD_DIM), F32)],
        compiler_params=_cparams(("parallel", "arbitrary")),
        name="rwkv_scan",
    )(*args)
    return y.reshape(batch * seq_len, A_WIDTH)


def _post_kernel(y_ref, g_ref, bonus_ref, lng_ref, lnb_ref, o_ref):
    e = _head_ones()
    y = y_ref[...]
    inv_n = 1.0 / A_HEAD_DIM
    mu = _head_sum(y, e) * inv_n
    d = y - mu
    var = _head_sum(d * d, e) * inv_n
    yn = d * lax.rsqrt(var + A_GN_EPS) * lng_ref[...] + lnb_ref[...]
    o_ref[...] = ((yn + bonus_ref[...]) * g_ref[...]).astype(o_ref.dtype)


def _rwkv_post(y, g, bonus, ln_g, ln_b, tm):
    t = y.shape[0]
    tok = pl.BlockSpec((tm, A_WIDTH), lambda i: (i, 0))
    vec = pl.BlockSpec((1, A_WIDTH), lambda i: (0, 0))
    return pl.pallas_call(
        _post_kernel,
        grid=(t // tm,),
        in_specs=[tok, tok, tok, vec, vec],
        out_specs=tok,
        out_shape=jax.ShapeDtypeStruct((t, A_WIDTH), BF16),
        compiler_params=_cparams(("parallel",)),
        name="rwkv_post",
    )(y, g, bonus, ln_g, ln_b)


def _lru_kernel(gate_ref, x_ref, px_ref, cw_ref, cb_ref, wa_ref, ba_ref, wx_ref, bx_ref, lam_ref,
                o_ref, h_ref):
    first = pl.program_id(1) == 0

    @pl.when(first)
    def _():
        h_ref[...] = jnp.zeros_like(h_ref)

    x = x_ref[...]
    ts = x.shape[0]
    prev = jnp.where(first, 0.0, px_ref[...])
    row = _iota2(x.shape, 0)
    row8 = _iota2(prev.shape, 0)

    def delayed(d):
        rolled = pltpu.roll(x, d, 0)
        top = jnp.where(row8 < d, pltpu.roll(prev, d, 0), rolled[:SUBLANES])
        return jnp.concatenate([top, rolled[SUBLANES:]], axis=0)

    xc = cw_ref[CONV_WIDTH - 1:CONV_WIDTH, :] * x + cb_ref[...]
    for d in range(1, CONV_WIDTH):
        xc = xc + cw_ref[CONV_WIDTH - 1 - d:CONV_WIDTH - d, :] * delayed(d)

    r = jax.nn.sigmoid(_dot(xc, wa_ref[...]) + ba_ref[...])
    i = jax.nn.sigmoid(_dot(xc, wx_ref[...]) + bx_ref[...])
    log_a = -LRU_C * r * jax.nn.softplus(-lam_ref[...])
    a = jnp.exp(log_a)
    th = jnp.tanh(log_a)
    u = jnp.sqrt(-2.0 * th / (1.0 - th)) * (i * xc)

    d = 1
    while d < ts:
        keep = row >= d
        a_s = jnp.where(keep, pltpu.roll(a, d, 0), 1.0)
        u_s = jnp.where(keep, pltpu.roll(u, d, 0), 0.0)
        u = a * u_s + u
        a = a * a_s
        d *= 2
    h = a * h_ref[...] + u
    h_ref[...] = h[ts - 1:ts, :]
    o_ref[...] = (jax.nn.gelu(gate_ref[...]) * h).astype(o_ref.dtype)


def _rglru(proj, params, batch, seq_len, ts):
    t = proj.shape[0]
    nst = seq_len // ts
    sub = ts // SUBLANES
    gate_blk = (C_COLS + 3 * A_WIDTH) // B_WIDTH
    x_blk = gate_blk + 1
    vec = pl.BlockSpec((1, B_WIDTH), lambda b, s: (0, 0))
    sq = pl.BlockSpec((B_WIDTH, B_WIDTH), lambda b, s: (0, 0))
    return pl.pallas_call(
        _lru_kernel,
        grid=(batch, nst),
        in_specs=[pl.BlockSpec((ts, B_WIDTH), lambda b, s: (b * nst + s, gate_blk)),
                  pl.BlockSpec((ts, B_WIDTH), lambda b, s: (b * nst + s, x_blk)),
                  pl.BlockSpec((SUBLANES, B_WIDTH),
                               lambda b, s: (jnp.maximum((b * nst + s) * sub - 1, 0), x_blk)),
                  pl.BlockSpec((CONV_WIDTH, B_WIDTH), lambda b, s: (0, 0)),
                  vec, sq, vec, sq, vec, vec],
        out_specs=pl.BlockSpec((ts, B_WIDTH), lambda b, s: (b * nst + s, 0)),
        out_shape=jax.ShapeDtypeStruct((t, B_WIDTH), BF16),
        scratch_shapes=[pltpu.VMEM((1, B_WIDTH), F32)],
        compiler_params=_cparams(("parallel", "arbitrary")),
        name="rglru",
    )(proj, proj, proj, params["conv_w"], params["conv_b"], params["wa"], params["ba"],
      params["wx"], params["bx"], params["lam"])


def _log_gammas():
    return np.log1p(-np.exp2(-5.0 - np.arange(C_HEADS, dtype=np.float32))).astype(np.float32)


def _ret_kernel(cpr, q_ref, k_ref, v_ref, g_ref, pos_ref, invf_ref, gng_ref, o_ref, st_ref):
    @pl.when(pl.program_id(1) == 0)
    def _():
        st_ref[...] = jnp.zeros_like(st_ref)

    half = C_QK_DIM // 2
    lane = _iota2((CHUNK, LANES), 1)
    in_first_half = (lane & (C_QK_DIM - 1)) < half
    lo = lane < C_QK_DIM
    idx_r = _iota2((CHUNK, CHUNK), 0).astype(F32)
    idx_c = _iota2((CHUNK, CHUNK), 1).astype(F32)
    dist = jnp.abs(idx_r - idx_c)
    pos_in = _iota2((CHUNK, 1), 0).astype(F32)
    log_gamma = _log_gammas()

    def rope(t, cos, sin_signed):
        swapped = jnp.where(in_first_half, pltpu.roll(t, LANES - half, 1), pltpu.roll(t, half, 1))
        return t * cos + swapped * sin_signed

    for c in range(cpr):
        rows = slice(c * CHUNK, (c + 1) * CHUNK)
        ang = pos_ref[0, rows, :] * invf_ref[...]
        cos = jnp.cos(ang)
        sin = jnp.sin(ang)
        sin_signed = jnp.where(in_first_half, -sin, sin)
        for p in range(C_HEADS // 2):
            qk_cols = slice(p * LANES, (p + 1) * LANES)
            q2 = rope(q_ref[rows, qk_cols], cos, sin_signed)
            k2 = rope(k_ref[rows, qk_cols], cos, sin_signed) * (C_QK_DIM ** -0.5)
            for j in range(2):
                h = 2 * p + j
                lg = float(log_gamma[h])
                mine = lo if j == 0 else jnp.logical_not(lo)
                qh = jnp.where(mine, q2, 0.0)
                kh = jnp.where(mine, k2, 0.0)
                v_cols = slice(h * C_V_DIM, (h + 1) * C_V_DIM)
                vh = v_ref[rows, v_cols]
                scores = _dot_nt(qh, kh) * jnp.exp(lg * dist)
                intra = _dot(scores, vh)
                state = st_ref[h]
                cross = _dot(qh * jnp.exp(lg * (pos_in + 1.0)), state)
                kv = _dot_tn(kh * jnp.exp(lg * (CHUNK - 1.0 - pos_in)), vh)
                st_ref[h] = state * float(np.exp(np.float32(lg) * np.float32(CHUNK))) + kv
                o = intra + cross
                mu = jnp.mean(o, axis=-1, keepdims=True)
                d = o - mu
                var = jnp.mean(d * d, axis=-1, keepdims=True)
                on = d * lax.rsqrt(var + C_GN_EPS) * gng_ref[:, v_cols]
                o_ref[rows, v_cols] = (jax.nn.silu(g_ref[rows, v_cols]) * on).astype(o_ref.dtype)


def _retention(proj, posb, invf, gn_g, batch, seq_len, cpr):
    t = proj.shape[0]
    ts = cpr * CHUNK
    nst = seq_len // ts
    return pl.pallas_call(
        functools.partial(_ret_kernel, cpr),
        grid=(batch, nst),
        in_specs=[pl.BlockSpec((ts, C_QK_WIDTH), lambda b, s: (b * nst + s, 0)),
                  pl.BlockSpec((ts, C_QK_WIDTH), lambda b, s: (b * nst + s, 1)),
                  pl.BlockSpec((ts, C_WIDTH), lambda b, s: (b * nst + s, 1)),
                  pl.BlockSpec((ts, C_WIDTH), lambda b, s: (b * nst + s, 2)),
                  pl.BlockSpec((1, ts, LANES), lambda b, s: (b, s, 0)),
                  pl.BlockSpec((1, LANES), lambda b, s: (0, 0)),
                  pl.BlockSpec((1, C_WIDTH), lambda b, s: (0, 0))],
        out_specs=pl.BlockSpec((ts, C_WIDTH), lambda b, s: (b * nst + s, 0)),
        out_shape=jax.ShapeDtypeStruct((t, C_WIDTH), BF16),
        scratch_shapes=[pltpu.VMEM((C_HEADS, LANES, C_V_DIM), F32)],
        compiler_params=_cparams(("parallel", "arbitrary")),
        name="retention",
    )(proj, proj, proj, proj, posb, invf, gn_g)


def _outproj_kernel(x_ref, ya_ref, yb_ref, yc_ref, wa_ref, wb_ref, wc_ref, o_ref):
    acc = jnp.dot(ya_ref[...], wa_ref[...], preferred_element_type=F32)
    acc += jnp.dot(yb_ref[...], wb_ref[...], preferred_element_type=F32)
    acc += jnp.dot(yc_ref[...], wc_ref[...], preferred_element_type=F32)
    o_ref[...] = x_ref[...] + acc


def _outproj(x, ya, yb, yc, wa, wb, wc, tm):
    t, d = x.shape

    def tok(width):
        return pl.BlockSpec((tm, width), lambda i: (i, 0))

    def wspec(rows):
        return pl.BlockSpec((rows, d), lambda i: (0, 0))

    return pl.pallas_call(
        _outproj_kernel,
        grid=(t // tm,),
        in_specs=[tok(d), tok(A_WIDTH), tok(B_WIDTH), tok(C_WIDTH),
                  wspec(A_WIDTH), wspec(B_WIDTH), wspec(C_WIDTH)],
        out_specs=tok(d),
        out_shape=jax.ShapeDtypeStruct((t, d), F32),
        compiler_params=_cparams(("parallel",)),
        name="outproj",
    )(x, ya, yb, yc, wa, wb, wc)


def _ffn_kernel(final, x_ref, g_ref, wg_ref, wu_ref, wd_ref, fg_ref, o_ref, h_ref, acc_ref):
    j = pl.program_id(1)

    @pl.when(j == 0)
    def _():
        h_ref[...] = _rms(x_ref[...], g_ref[...]).astype(BF16)
        acc_ref[...] = jnp.zeros_like(acc_ref)

    h = h_ref[...]
    gate = jnp.dot(h, wg_ref[...], preferred_element_type=F32)
    up = jnp.dot(h, wu_ref[...], preferred_element_type=F32)
    act = (jax.nn.silu(gate) * up).astype(BF16)
    acc_ref[...] += jnp.dot(act, wd_ref[...], preferred_element_type=F32)

    @pl.when(j == pl.num_programs(1) - 1)
    def _():
        out = x_ref[...] + acc_ref[...]
        if final:
            out = _rms(out, fg_ref[...])
        o_ref[...] = out


def _ffn(x, g, wg, wu, wd, fg, final, tm, tf):
    t, d = x.shape
    f = wg.shape[1]
    return pl.pallas_call(
        functools.partial(_ffn_kernel, final),
        grid=(t // tm, f // tf),
        in_specs=[pl.BlockSpec((tm, d), lambda i, j: (i, 0)),
                  pl.BlockSpec((1, d), lambda i, j: (0, 0)),
                  pl.BlockSpec((d, tf), lambda i, j: (0, j)),
                  pl.BlockSpec((d, tf), lambda i, j: (0, j)),
                  pl.BlockSpec((tf, d), lambda i, j: (j, 0)),
                  pl.BlockSpec((1, d), lambda i, j: (0, 0))],
        out_specs=pl.BlockSpec((tm, d), lambda i, j: (i, 0)),
        out_shape=jax.ShapeDtypeStruct((t, d), F32),
        scratch_shapes=[pltpu.VMEM((tm, d), BF16), pltpu.VMEM((tm, d), F32)],
        compiler_params=_cparams(("parallel", "arbitrary")),
        name="ffn_final" if final else "ffn",
    )(x, g, wg, wu, wd, fg)


def _pad_cols(w, width):
    return jnp.pad(w, ((0, 0), (0, width - w.shape[1])))


def _pad_rows(w, rows):
    return jnp.pad(w, ((0, rows - w.shape[0]), (0, 0)))


def _pad_vec(v, width):
    return jnp.pad(v, (0, width - v.shape[0]))


def _layer_in_weight(w_in_l, w_vres_l):
    d = w_in_l.shape[0]
    aw = A_WIDTH
    o_wd, o_ad, o_gd = 3 * aw, 3 * aw + DECAY_LORA, 3 * aw + DECAY_LORA + ICLR_LORA
    vd = jnp.zeros((d, LORA_SLOT), F32) if w_vres_l is None else _pad_cols(w_vres_l, LORA_SLOT)
    cols = [w_in_l[:, A_COLS + B_COLS:],
            w_in_l[:, :3 * aw],
            w_in_l[:, A_COLS:A_COLS + B_COLS],
            _pad_cols(w_in_l[:, o_wd:o_ad], LORA_SLOT),
            _pad_cols(w_in_l[:, o_ad:o_gd], LORA_SLOT),
            w_in_l[:, o_gd:A_COLS],
            vd]
    return jnp.concatenate(cols, axis=1).astype(BF16)


def _block_diag(w):
    g, n, _ = w.shape
    eye = jnp.eye(g, dtype=w.dtype)
    return (eye[:, None, :, None] * w[:, :, None, :]).reshape(g * n, g * n)


def kernel(x, positions, norm1_g, w_in, tshift_mu, rwkv_w0, rwkv_w2, rwkv_a0, rwkv_a2, rwkv_g2, rwkv_k_k, rwkv_k_a, rwkv_r_k, rwkv_ln_g, rwkv_ln_b, w_in_vres, tshift_mu_vres, rwkv_v0, rwkv_v2, lru_conv_w, lru_conv_b, lru_wa, lru_ba, lru_wx, lru_bx, lru_lambda, ret_gn_g, w_out, norm2_g, ffn_w_gate, ffn_w_up, ffn_w_down, final_norm_g):
    batch, seq_len, d = x.shape
    depth = w_in.shape[0]
    t = batch * seq_len
    aw = A_WIDTH
    row = lambda v: v.reshape(1, -1).astype(F32)

    tm_proj = min(512, seq_len)
    tm_tok = min(256, seq_len)
    cps = min(4, seq_len // CHUNK)
    cpr = min(4, seq_len // CHUNK)

    inv_freq = ROPE_THETA ** (-jnp.arange(0, C_QK_DIM, 2, dtype=F32) / C_QK_DIM)
    invf = jnp.tile(inv_freq, LANES // inv_freq.shape[0]).reshape(1, LANES)
    posb = jnp.broadcast_to(positions.astype(F32)[..., None], (batch, seq_len, LANES))

    xf = x.reshape(t, d)
    v_first = None
    for l in range(depth):
        has_vres = l > 0
        w_l = _layer_in_weight(w_in[l], w_in_vres[l - 1] if has_vres else None)
        proj = _inproj(xf, row(norm1_g[l]), w_l, tm_proj, P_COLS // 7)

        mu = tshift_mu[l]
        o_wd, o_ad, o_gd = 3 * aw, 3 * aw + DECAY_LORA, 3 * aw + DECAY_LORA + ICLR_LORA
        params = {
            "mu_rkv": row(mu[:3 * aw]),
            "mu_lora": row(jnp.concatenate([_pad_vec(mu[o_wd:o_ad], LORA_SLOT),
                                            _pad_vec(mu[o_ad:o_gd], LORA_SLOT), mu[o_gd:A_COLS]])),
            "w0": row(rwkv_w0[l]), "a0": row(rwkv_a0[l]), "k_k": row(rwkv_k_k[l]),
            "k_a": row(rwkv_k_a[l]), "r_k": row(rwkv_r_k[l]),
            "w2": _pad_rows(rwkv_w2[l], LORA_SLOT), "a2": _pad_rows(rwkv_a2[l], LORA_SLOT),
            "g2": rwkv_g2[l],
        }
        if has_vres:
            params["mu_vd"] = row(_pad_vec(tshift_mu_vres[l - 1], LORA_SLOT))
            params["v0"] = row(rwkv_v0[l - 1])
            params["v2"] = _pad_rows(rwkv_v2[l - 1], LORA_SLOT)
        at, rt, kt, bt, kd, bd, pc, v_cur, g_a, bonus = _rwkv_prep(
            proj, params, v_first if has_vres else None, seq_len, tm_tok)
        if l == 0:
            v_first = v_cur
        y_raw = _rwkv_scan(at, rt, kt, bt, kd, bd, v_cur, pc, batch, seq_len, cps)
        y_a = _rwkv_post(y_raw, g_a, bonus, row(rwkv_ln_g[l]), row(rwkv_ln_b[l]), tm_tok)

        lru_params = {
            "conv_w": lru_conv_w[l], "conv_b": row(lru_conv_b[l]),
            "wa": _block_diag(lru_wa[l]).astype(BF16), "ba": row(lru_ba[l]),
            "wx": _block_diag(lru_wx[l]).astype(BF16), "bx": row(lru_bx[l]),
            "lam": row(lru_lambda[l]),
        }
        y_b = _rglru(proj, lru_params, batch, seq_len, tm_tok)
        y_c = _retention(proj, posb, invf, row(ret_gn_g[l]), batch, seq_len, cpr)

        wo = w_out[l].astype(BF16)
        xf = _outproj(xf, y_a, y_b, y_c, wo[:aw], wo[aw:aw + B_WIDTH], wo[aw + B_WIDTH:], tm_proj)
        xf = _ffn(xf, row(norm2_g[l]), ffn_w_gate[l].astype(BF16), ffn_w_up[l].astype(BF16),
                  ffn_w_down[l].astype(BF16), row(final_norm_g), l == depth - 1, tm_proj, 512)
    return xf.reshape(batch, seq_len, d)
```

```python
import functools

import numpy as np
import jax
import jax.numpy as jnp
from jax import lax
from jax.experimental import pallas as pl
from jax.experimental.pallas import tpu as pltpu

F32 = jnp.float32
BF16 = jnp.bfloat16

NORM_EPS = 1e-6
CHUNK = 64

A_HEADS = 12
A_HEAD_DIM = 64
A_WIDTH = A_HEADS * A_HEAD_DIM
DECAY_LORA = 96
ICLR_LORA = 96
VRES_LORA = 64
GATE_LORA = 256
A_GN_EPS = 64e-5
A_COLS = 3 * A_WIDTH + DECAY_LORA + ICLR_LORA + GATE_LORA
B_WIDTH = 512
B_BLOCKS = 8
B_BLOCK_DIM = B_WIDTH // B_BLOCKS
CONV_WIDTH = 4
LRU_C = 8.0
B_COLS = 2 * B_WIDTH
C_HEADS = 6
C_QK_DIM = 64
C_V_DIM = 128
C_QK_WIDTH = C_HEADS * C_QK_DIM
C_WIDTH = C_HEADS * C_V_DIM
C_GN_EPS = 1e-5
ROPE_THETA = 10000.0
C_COLS = 2 * C_QK_WIDTH + 2 * C_WIDTH

LANES = 128
SUBLANES = 8
LORA_PACK = DECAY_LORA + ICLR_LORA + VRES_LORA
VMEM_LIMIT = 56 * 1024 * 1024

P_COLS = C_COLS + 3 * A_WIDTH + B_COLS + LORA_PACK + GATE_LORA


def _cparams(sem):
    return pltpu.CompilerParams(dimension_semantics=sem, vmem_limit_bytes=VMEM_LIMIT)


def _dot(a, b):
    return jnp.dot(a.astype(BF16), b.astype(BF16), preferred_element_type=F32)


def _dot_nt(a, b):
    return lax.dot_general(a.astype(BF16), b.astype(BF16), (((1,), (1,)), ((), ())),
                           preferred_element_type=F32)


def _dot_tn(a, b):
    return lax.dot_general(a.astype(BF16), b.astype(BF16), (((0,), (0,)), ((), ())),
                           preferred_element_type=F32)


def _iota2(shape, axis):
    return lax.broadcasted_iota(jnp.int32, shape, axis)


def _blk(idx, size):
    return idx >> (size.bit_length() - 1)


def _rms(x, g):
    ms = jnp.mean(x * x, axis=-1, keepdims=True)
    return x * lax.rsqrt(ms + NORM_EPS) * g


def _inproj_kernel(x_ref, g_ref, w_ref, o_ref, h_ref):
    @pl.when(pl.program_id(1) == 0)
    def _():
        h_ref[...] = _rms(x_ref[...], g_ref[...]).astype(BF16)

    o_ref[...] = jnp.dot(h_ref[...], w_ref[...], preferred_element_type=F32)


def _inproj(x, g, w, tm, tn):
    t, d = x.shape
    n = w.shape[1]
    return pl.pallas_call(
        _inproj_kernel,
        grid=(t // tm, n // tn),
        in_specs=[pl.BlockSpec((tm, d), lambda i, j: (i, 0)),
                  pl.BlockSpec((1, d), lambda i, j: (0, 0)),
                  pl.BlockSpec((d, tn), lambda i, j: (0, j))],
        out_specs=pl.BlockSpec((tm, tn), lambda i, j: (i, j)),
        out_shape=jax.ShapeDtypeStruct((t, n), F32),
        scratch_shapes=[pltpu.VMEM((tm, d), BF16)],
        compiler_params=_cparams(("parallel", "arbitrary")),
        name="inproj",
    )(x, g, w)


def _shift_rows(cur, prev_row):
    rolled = pltpu.roll(cur, 1, 0)
    row = _iota2(cur.shape, 0)
    return jnp.where(row == 0, prev_row, rolled)


def _split3(x):
    hi = x.astype(BF16)
    r1 = x - hi.astype(F32)
    mid = r1.astype(BF16)
    lo = (r1 - mid.astype(F32)).astype(BF16)
    return hi, mid, lo


def _sel_dot(sel, x):
    hi, mid, lo = _split3(x)
    d = functools.partial(jnp.dot, sel, preferred_element_type=F32)
    return (d(lo) + d(mid)) + d(hi)


def _dot_sel(x, sel):
    hi, mid, lo = _split3(x)
    d = lambda p: jnp.dot(p, sel, preferred_element_type=F32)
    return (d(lo) + d(mid)) + d(hi)


def _head_sum(x, e):
    parts = [_dot_sel(x[:, i * LANES:(i + 1) * LANES], e) for i in range(x.shape[1] // LANES)]
    return jnp.concatenate(parts, axis=-1)


def _head_ones():
    r = _blk(_iota2((LANES, LANES), 0), A_HEAD_DIM)
    c = _blk(_iota2((LANES, LANES), 1), A_HEAD_DIM)
    return (r == c).astype(BF16)


def _prep_kernel(has_vres, seq_tiles, *refs):
    if has_vres:
        (rkv_ref, lora_ref, prkv_ref, plora_ref, mu_rkv, mu_lora,
         w0, a0, k_k, k_a, r_k, w2, a2, g2, v0, v2, vfirst_ref,
         at_o, rt_o, kt_o, bt_o, kd_o, bd_o, pc_o, v_o, g_o, bonus_o) = refs
    else:
        (rkv_ref, lora_ref, prkv_ref, plora_ref, mu_rkv, mu_lora,
         w0, a0, k_k, k_a, r_k, w2, a2, g2,
         at_o, rt_o, kt_o, bt_o, kd_o, bd_o, pc_o, v_o, g_o, bonus_o) = refs
    first = (pl.program_id(0) % seq_tiles) == 0
    last_row = slice(SUBLANES - 1, SUBLANES)

    def mixed(cur_ref, prev_ref, mu_ref):
        cur = cur_ref[...]
        prev_row = jnp.where(first, 0.0, prev_ref[last_row, :])
        return cur + (_shift_rows(cur, prev_row) - cur) * mu_ref[...]

    rkv = mixed(rkv_ref, prkv_ref, mu_rkv)
    lora = mixed(lora_ref, plora_ref, mu_lora)
    r = rkv[:, 0:A_WIDTH]
    k = rkv[:, A_WIDTH:2 * A_WIDTH]
    v = rkv[:, 2 * A_WIDTH:3 * A_WIDTH]
    small = lora[:, 0:LORA_PACK]
    gd = lora[:, LORA_PACK:LORA_PACK + GATE_LORA]

    log_w = -jax.nn.softplus(-(w0[...] + _dot(jnp.tanh(small), w2[...]))) - 0.5
    lw = -jnp.exp(log_w)
    a = jax.nn.sigmoid(a0[...] + _dot(small, a2[...]))
    g = _dot(jax.nn.sigmoid(gd), g2[...])
    if has_vres:
        v = v + (vfirst_ref[...] - v) * jax.nn.sigmoid(v0[...] + _dot(small, v2[...]))

    e = _head_ones()
    kk = k * k_k[...]
    kk = kk / jnp.maximum(jnp.sqrt(_head_sum(kk * kk, e)), 1e-12)
    k = k * (1.0 + (a - 1.0) * k_a[...])
    b = kk * a
    bonus = _head_sum(r * k * r_k[...], e) * v

    tm = lw.shape[0]
    row = _iota2((tm, tm), 0)
    col = _iota2((tm, tm), 1)
    tri = (_blk(row, CHUNK) == _blk(col, CHUNK)) & (row >= col)
    cum = _sel_dot(tri.astype(BF16), lw)
    cum3 = cum.reshape(tm // CHUNK, CHUNK, A_WIDTH)
    tot = jnp.broadcast_to(cum3[:, CHUNK - 1:CHUNK, :], cum3.shape).reshape(tm, A_WIDTH)
    e_in = jnp.exp(cum)
    e_out = jnp.exp(-cum)
    e_end = jnp.exp(tot - cum)
    at_o[...] = -kk * jnp.exp(cum - lw)
    rt_o[...] = r * e_in
    kt_o[...] = k * e_out
    bt_o[...] = b * e_out
    kd_o[...] = k * e_end
    bd_o[...] = b * e_end
    pc_o[...] = jnp.exp(tot)
    v_o[...] = v
    g_o[...] = g
    bonus_o[...] = bonus


def _rwkv_prep(proj, params, v_first, seq_len, tm):
    t = proj.shape[0]
    has_vres = v_first is not None
    seq_tiles = seq_len // tm
    sub = tm // SUBLANES
    rkv_w, lora_w = 3 * A_WIDTH, LORA_PACK + GATE_LORA
    rkv_blk = (C_COLS) // rkv_w
    lora_blk = (C_COLS + rkv_w + B_COLS) // lora_w

    def cur(width, blk):
        return pl.BlockSpec((tm, width), lambda i: (i, blk))

    def prev(width, blk):
        return pl.BlockSpec((SUBLANES, width), lambda i: (jnp.maximum(i * sub - 1, 0), blk))

    def vec(width):
        return pl.BlockSpec((1, width), lambda i: (0, 0))

    def mat(rows):
        return pl.BlockSpec((rows, A_WIDTH), lambda i: (0, 0))

    tok = pl.BlockSpec((tm, A_WIDTH), lambda i: (i, 0))
    ins = [proj, proj, proj, proj,
           params["mu_rkv"], params["mu_lora"],
           params["w0"], params["a0"], params["k_k"], params["k_a"], params["r_k"],
           params["w2"], params["a2"], params["g2"]]
    specs = [cur(rkv_w, rkv_blk), cur(lora_w, lora_blk),
             prev(rkv_w, rkv_blk), prev(lora_w, lora_blk),
             vec(rkv_w), vec(lora_w),
             vec(A_WIDTH), vec(A_WIDTH), vec(A_WIDTH), vec(A_WIDTH), vec(A_WIDTH),
             mat(LORA_PACK), mat(LORA_PACK), mat(GATE_LORA)]
    if has_vres:
        ins += [params["v0"], params["v2"], v_first]
        specs += [vec(A_WIDTH), mat(LORA_PACK), tok]
    n_out = 10
    return pl.pallas_call(
        functools.partial(_prep_kernel, has_vres, seq_tiles),
        grid=(t // tm,),
        in_specs=specs,
        out_specs=[tok] * n_out,
        out_shape=[jax.ShapeDtypeStruct((t, A_WIDTH), F32)] * n_out,
        compiler_params=_cparams(("parallel",)),
        name="rwkv_prep_vres" if has_vres else "rwkv_prep",
    )(*ins)


def _unit_lower_inverse(a_strict, row, col, eye):
    base = 8
    diag = _blk(row, base) == _blk(col, base)
    n1 = [jnp.where(diag, a, 0.0) for a in a_strict]
    n1b = [x.astype(BF16) for x in n1]
    n2 = [_dot(x, x).astype(BF16) for x in n1b]
    t = [eye + x for x in n1]
    t = [x + _dot(x, m) for x, m in zip(t, n2)]
    n4 = [_dot(m, m) for m in n2]
    t = [x + _dot(x, m) for x, m in zip(t, n4)]
    m = base
    while m < CHUNK:
        sel = (_blk(row, 2 * m) == _blk(col, 2 * m)) & (_blk(row, m) != _blk(col, m))
        e = [jnp.where(sel, a, 0.0).astype(BF16) for a in a_strict]
        tb = [x.astype(BF16) for x in t]
        te = [_dot(x, y) for x, y in zip(tb, e)]
        t = [x + _dot(y, xb) for x, y, xb in zip(t, te, tb)]
        m *= 2
    return t


def _scan_kernel(cps, at_ref, rt_ref, kt_ref, bt_ref, kd_ref, bd_ref, v_ref, pc_ref, y_ref, s_ref):
    @pl.when(pl.program_id(1) == 0)
    def _():
        s_ref[...] = jnp.zeros_like(s_ref)

    n = 2 * A_HEAD_DIM
    pairs = A_HEADS // 2
    row = _iota2((n, n), 0)
    col = _iota2((n, n), 1)
    strict = row > col
    incl = row >= col
    eye = (row == col).astype(F32)
    lo = _iota2((CHUNK, n), 1) < A_HEAD_DIM
    items = [(c, p) for c in range(cps) for p in range(pairs)]

    def load(ref):
        out = []
        for c, p in items:
            x = ref[0, c * CHUNK:(c + 1) * CHUNK, p * n:(p + 1) * n]
            out.append(jnp.concatenate([jnp.where(lo, x, 0.0), jnp.where(lo, 0.0, x)], axis=0))
        return out

    at = load(at_ref)
    rt = load(rt_ref)
    lhs = [jnp.concatenate([a, r], axis=0).astype(BF16) for a, r in zip(at, rt)]
    lk = [_dot_nt(x, k) for x, k in zip(lhs, load(kt_ref))]
    lb = [_dot_nt(x, b) for x, b in zip(lhs, load(bt_ref))]
    a_k = [jnp.concatenate([jnp.where(strict, x[:n], 0.0), jnp.where(incl, x[n:], 0.0)],
                           axis=0).astype(BF16) for x in lk]
    a_ab = [jnp.where(strict, x[:n], 0.0) for x in lb]
    a_rb = [jnp.where(incl, x[n:], 0.0).astype(BF16) for x in lb]
    t_inv = _unit_lower_inverse(a_ab, row, col, eye)
    v = [x.astype(BF16) for x in load(v_ref)]
    lv = [_dot(a, x) for a, x in zip(a_k, v)]
    wu = [_dot(t, jnp.concatenate([a, x[:n]], axis=1))
          for t, a, x in zip(t_inv, at, lv)]
    wub = [x.astype(BF16) for x in wu]
    z = [_dot(a, x) for a, x in zip(a_rb, wub)]
    rw = [(r + x[:, :n]).astype(BF16) for r, x in zip(rt, z)]
    y0 = [x[n:] + zz[:, n:] for x, zz in zip(lv, z)]
    q = [_dot_tn(x, b) for x, b in zip(wub, load(bd_ref))]
    vk = [_dot_tn(x, k) for x, k in zip(v, load(kd_ref))]
    g = [x[:n].astype(BF16) for x in q]
    s_add = [x[n:] + y for x, y in zip(q, vk)]

    s = [s_ref[p] for p in range(pairs)]
    for c in range(cps):
        rows = slice(c * CHUNK, (c + 1) * CHUNK)
        sb = [x.astype(BF16) for x in s]
        for p in range(pairs):
            i = c * pairs + p
            y = _dot_nt(rw[i], sb[p]) + y0[i]
            y_ref[0, rows, p * n:(p + 1) * n] = y[:CHUNK] + y[CHUNK:]
        for p in range(pairs):
            i = c * pairs + p
            pc = pc_ref[0, c * CHUNK:c * CHUNK + 1, p * n:(p + 1) * n]
            s[p] = s[p] * pc + _dot(sb[p], g[i]) + s_add[i]
    for p in range(pairs):
        s_ref[p] = s[p]


def _rwkv_scan(at, rt, kt, bt, kd, bd, v, pc, batch, seq_len, cps):
    ts = cps * CHUNK
    shp = (batch, seq_len, A_WIDTH)
    blk = pl.BlockSpec((1, ts, A_WIDTH), lambda b, s: (b, s, 0))
    args = [z.reshape(shp) for z in (at, rt, kt, bt, kd, bd, v, pc)]
    y = pl.pallas_call(
        functools.partial(_scan_kernel, cps),
        grid=(batch, seq_len // ts),
        in_specs=[blk] * 8,
        out_specs=blk,
        out_shape=jax.ShapeDtypeStruct(shp, F32),
        scratch_shapes=[pltpu.VMEM((A_HEADS // 2, 2 * A_HEAD_DIM, 2 * A_HEAD_DIM), F32)],
        compiler_params=_cparams(("parallel", "arbitrary")),
        name="rwkv_scan",
    )(*args)
    return y.reshape(batch * seq_len, A_WIDTH)


def _post_kernel(y_ref, g_ref, bonus_ref, lng_ref, lnb_ref, o_ref):
    e = _head_ones()
    y = y_ref[...]
    inv_n = 1.0 / A_HEAD_DIM
    mu = _head_sum(y, e) * inv_n
    d = y - mu
    var = _head_sum(d * d, e) * inv_n
    yn = d * lax.rsqrt(var + A_GN_EPS) * lng_ref[...] + lnb_ref[...]
    o_ref[...] = ((yn + bonus_ref[...]) * g_ref[...]).astype(o_ref.dtype)


def _rwkv_post(y, g, bonus, ln_g, ln_b, tm):
    t = y.shape[0]
    tok = pl.BlockSpec((tm, A_WIDTH), lambda i: (i, 0))
    vec = pl.BlockSpec((1, A_WIDTH), lambda i: (0, 0))
    return pl.pallas_call(
        _post_kernel,
        grid=(t // tm,),
        in_specs=[tok, tok, tok, vec, vec],
        out_specs=tok,
        out_shape=jax.ShapeDtypeStruct((t, A_WIDTH), BF16),
        compiler_params=_cparams(("parallel",)),
        name="rwkv_post",
    )(y, g, bonus, ln_g, ln_b)


def _lru_kernel(gate_ref, x_ref, px_ref, cw_ref, cb_ref, wa_ref, ba_ref, wx_ref, bx_ref, lam_ref,
                o_ref, h_ref):
    first = pl.program_id(1) == 0

    @pl.when(first)
    def _():
        h_ref[...] = jnp.zeros_like(h_ref)

    x = x_ref[...]
    ts = x.shape[0]
    prev = jnp.where(first, 0.0, px_ref[...])
    row = _iota2(x.shape, 0)
    row8 = _iota2(prev.shape, 0)

    def delayed(d):
        rolled = pltpu.roll(x, d, 0)
        top = jnp.where(row8 < d, pltpu.roll(prev, d, 0), rolled[:SUBLANES])
        return jnp.concatenate([top, rolled[SUBLANES:]], axis=0)

    xc = cw_ref[CONV_WIDTH - 1:CONV_WIDTH, :] * x + cb_ref[...]
    for d in range(1, CONV_WIDTH):
        xc = xc + cw_ref[CONV_WIDTH - 1 - d:CONV_WIDTH - d, :] * delayed(d)

    r = jax.nn.sigmoid(_dot(xc, wa_ref[...]) + ba_ref[...])
    i = jax.nn.sigmoid(_dot(xc, wx_ref[...]) + bx_ref[...])
    log_a = -LRU_C * r * jax.nn.softplus(-lam_ref[...])
    a = jnp.exp(log_a)
    th = jnp.tanh(log_a)
    u = jnp.sqrt(-2.0 * th / (1.0 - th)) * (i * xc)

    d = 1
    while d < ts:
        keep = row >= d
        a_s = jnp.where(keep, pltpu.roll(a, d, 0), 1.0)
        u_s = jnp.where(keep, pltpu.roll(u, d, 0), 0.0)
        u = a * u_s + u
        a = a * a_s
        d *= 2
    h = a * h_ref[...] + u
    h_ref[...] = h[ts - 1:ts, :]
    o_ref[...] = (jax.nn.gelu(gate_ref[...]) * h).astype(o_ref.dtype)


def _rglru(proj, params, batch, seq_len, ts):
    t = proj.shape[0]
    nst = seq_len // ts
    sub = ts // SUBLANES
    gate_blk = (C_COLS + 3 * A_WIDTH) // B_WIDTH
    x_blk = gate_blk + 1
    vec = pl.BlockSpec((1, B_WIDTH), lambda b, s: (0, 0))
    sq = pl.BlockSpec((B_WIDTH, B_WIDTH), lambda b, s: (0, 0))
    return pl.pallas_call(
        _lru_kernel,
        grid=(batch, nst),
        in_specs=[pl.BlockSpec((ts, B_WIDTH), lambda b, s: (b * nst + s, gate_blk)),
                  pl.BlockSpec((ts, B_WIDTH), lambda b, s: (b * nst + s, x_blk)),
                  pl.BlockSpec((SUBLANES, B_WIDTH),
                               lambda b, s: (jnp.maximum((b * nst + s) * sub - 1, 0), x_blk)),
                  pl.BlockSpec((CONV_WIDTH, B_WIDTH), lambda b, s: (0, 0)),
                  vec, sq, vec, sq, vec, vec],
        out_specs=pl.BlockSpec((ts, B_WIDTH), lambda b, s: (b * nst + s, 0)),
        out_shape=jax.ShapeDtypeStruct((t, B_WIDTH), BF16),
        scratch_shapes=[pltpu.VMEM((1, B_WIDTH), F32)],
        compiler_params=_cparams(("parallel", "arbitrary")),
        name="rglru",
    )(proj, proj, proj, params["conv_w"], params["conv_b"], params["wa"], params["ba"],
      params["wx"], params["bx"], params["lam"])


def _log_gammas():
    return np.log1p(-np.exp2(-5.0 - np.arange(C_HEADS, dtype=np.float32))).astype(np.float32)


def _ret_kernel(cpr, q_ref, k_ref, v_ref, g_ref, pos_ref, invf_ref, gng_ref, o_ref, st_ref):
    @pl.when(pl.program_id(1) == 0)
    def _():
        st_ref[...] = jnp.zeros_like(st_ref)

    half = C_QK_DIM // 2
    lane = _iota2((CHUNK, LANES), 1)
    in_first_half = (lane & (C_QK_DIM - 1)) < half
    lo = lane < C_QK_DIM
    idx_r = _iota2((CHUNK, CHUNK), 0).astype(F32)
    idx_c = _iota2((CHUNK, CHUNK), 1).astype(F32)
    dist = jnp.abs(idx_r - idx_c)
    pos_in = _iota2((CHUNK, 1), 0).astype(F32)
    log_gamma = _log_gammas()

    def rope(t, cos, sin_signed):
        swapped = jnp.where(in_first_half, pltpu.roll(t, LANES - half, 1), pltpu.roll(t, half, 1))
        return t * cos + swapped * sin_signed

    intra_decay = [jnp.exp(float(lg) * dist) for lg in log_gamma]
    q_decay = [jnp.exp(float(lg) * (pos_in + 1.0)) for lg in log_gamma]
    k_decay = [jnp.exp(float(lg) * (CHUNK - 1.0 - pos_in)) for lg in log_gamma]
    chunk_decay = [float(np.exp(lg * np.float32(CHUNK))) for lg in log_gamma]
    items = [(c, h) for c in range(cpr) for h in range(C_HEADS)]

    q2, k2 = {}, {}
    for c in range(cpr):
        rows = slice(c * CHUNK, (c + 1) * CHUNK)
        ang = pos_ref[0, rows, :] * invf_ref[...]
        cos = jnp.cos(ang)
        sin = jnp.sin(ang)
        sin_signed = jnp.where(in_first_half, -sin, sin)
        for p in range(C_HEADS // 2):
            qk_cols = slice(p * LANES, (p + 1) * LANES)
            q2[c, p] = rope(q_ref[rows, qk_cols], cos, sin_signed)
            k2[c, p] = rope(k_ref[rows, qk_cols], cos, sin_signed) * (C_QK_DIM ** -0.5)

    def mine(x, h):
        return jnp.where(lo, x, 0.0) if h % 2 == 0 else jnp.where(lo, 0.0, x)

    qh = [mine(q2[c, h // 2], h) for c, h in items]
    kh = [mine(k2[c, h // 2], h) for c, h in items]
    vh = [v_ref[c * CHUNK:(c + 1) * CHUNK, h * C_V_DIM:(h + 1) * C_V_DIM].astype(BF16)
          for c, h in items]
    scores = [_dot_nt(q, k) * intra_decay[h] for q, k, (c, h) in zip(qh, kh, items)]
    kv = [_dot_tn(k * k_decay[h], v) for k, v, (c, h) in zip(kh, vh, items)]
    intra = [_dot(s, v) for s, v in zip(scores, vh)]

    state = [st_ref[h] for h in range(C_HEADS)]
    starts = []
    for i, (c, h) in enumerate(items):
        starts.append(state[h])
        state[h] = state[h] * chunk_decay[h] + kv[i]
    for h in range(C_HEADS):
        st_ref[h] = state[h]
    cross = [_dot(q * q_decay[h], s) for q, s, (c, h) in zip(qh, starts, items)]

    for i, (c, h) in enumerate(items):
        rows = slice(c * CHUNK, (c + 1) * CHUNK)
        v_cols = slice(h * C_V_DIM, (h + 1) * C_V_DIM)
        o = intra[i] + cross[i]
        mu = jnp.mean(o, axis=-1, keepdims=True)
        d = o - mu
        var = jnp.mean(d * d, axis=-1, keepdims=True)
        on = d * lax.rsqrt(var + C_GN_EPS) * gng_ref[:, v_cols]
        o_ref[rows, v_cols] = (jax.nn.silu(g_ref[rows, v_cols]) * on).astype(o_ref.dtype)


def _retention(proj, posb, invf, gn_g, batch, seq_len, cpr):
    t = proj.shape[0]
    ts = cpr * CHUNK
    nst = seq_len // ts
    return pl.pallas_call(
        functools.partial(_ret_kernel, cpr),
        grid=(batch, nst),
        in_specs=[pl.BlockSpec((ts, C_QK_WIDTH), lambda b, s: (b * nst + s, 0)),
                  pl.BlockSpec((ts, C_QK_WIDTH), lambda b, s: (b * nst + s, 1)),
                  pl.BlockSpec((ts, C_WIDTH), lambda b, s: (b * nst + s, 1)),
                  pl.BlockSpec((ts, C_WIDTH), lambda b, s: (b * nst + s, 2)),
                  pl.BlockSpec((1, ts, LANES), lambda b, s: (b, s, 0)),
                  pl.BlockSpec((1, LANES), lambda b, s: (0, 0)),
                  pl.BlockSpec((1, C_WIDTH), lambda b, s: (0, 0))],
        out_specs=pl.BlockSpec((ts, C_WIDTH), lambda b, s: (b * nst + s, 0)),
        out_shape=jax.ShapeDtypeStruct((t, C_WIDTH), BF16),
        scratch_shapes=[pltpu.VMEM((C_HEADS, LANES, C_V_DIM), F32)],
        compiler_params=_cparams(("parallel", "arbitrary")),
        name="retention",
    )(proj, proj, proj, proj, posb, invf, gn_g)


def _outproj_kernel(x_ref, ya_ref, yb_ref, yc_ref, w_ref, o_ref):
    b0, c0 = A_WIDTH, A_WIDTH + B_WIDTH
    acc = jnp.dot(ya_ref[...], w_ref[0:b0, :], preferred_element_type=F32)
    acc += jnp.dot(yb_ref[...], w_ref[b0:c0, :], preferred_element_type=F32)
    acc += jnp.dot(yc_ref[...], w_ref[c0:, :], preferred_element_type=F32)
    o_ref[...] = x_ref[...] + acc


def _outproj(x, ya, yb, yc, w_stack, layer, tm):
    t, d = x.shape

    def tok(width):
        return pl.BlockSpec((tm, width), lambda i: (i, 0))

    return pl.pallas_call(
        _outproj_kernel,
        grid=(t // tm,),
        in_specs=[tok(d), tok(A_WIDTH), tok(B_WIDTH), tok(C_WIDTH),
                  pl.BlockSpec((None, d, d), lambda i: (layer, 0, 0))],
        out_specs=tok(d),
        out_shape=jax.ShapeDtypeStruct((t, d), F32),
        compiler_params=_cparams(("parallel",)),
        name="outproj",
    )(x, ya, yb, yc, w_stack)


def _ffn_kernel(final, x_ref, g_ref, wg_ref, wu_ref, wd_ref, fg_ref, o_ref, h_ref, acc_ref):
    j = pl.program_id(1)

    @pl.when(j == 0)
    def _():
        h_ref[...] = _rms(x_ref[...], g_ref[...]).astype(BF16)
        acc_ref[...] = jnp.zeros_like(acc_ref)

    h = h_ref[...]
    gate = jnp.dot(h, wg_ref[...], preferred_element_type=F32)
    up = jnp.dot(h, wu_ref[...], preferred_element_type=F32)
    act = (jax.nn.silu(gate) * up).astype(BF16)
    acc_ref[...] += jnp.dot(act, wd_ref[...], preferred_element_type=F32)

    @pl.when(j == pl.num_programs(1) - 1)
    def _():
        out = x_ref[...] + acc_ref[...]
        if final:
            out = _rms(out, fg_ref[...])
        o_ref[...] = out


def _ffn(x, g, wg, wu, wd, fg, layer, final, tm, tf):
    t, d = x.shape
    f = wg.shape[2]
    return pl.pallas_call(
        functools.partial(_ffn_kernel, final),
        grid=(t // tm, f // tf),
        in_specs=[pl.BlockSpec((tm, d), lambda i, j: (i, 0)),
                  pl.BlockSpec((1, d), lambda i, j: (0, 0)),
                  pl.BlockSpec((None, d, tf), lambda i, j: (layer, 0, j)),
                  pl.BlockSpec((None, d, tf), lambda i, j: (layer, 0, j)),
                  pl.BlockSpec((None, tf, d), lambda i, j: (layer, j, 0)),
                  pl.BlockSpec((1, d), lambda i, j: (0, 0))],
        out_specs=pl.BlockSpec((tm, d), lambda i, j: (i, 0)),
        out_shape=jax.ShapeDtypeStruct((t, d), F32),
        scratch_shapes=[pltpu.VMEM((tm, d), BF16), pltpu.VMEM((tm, d), F32)],
        compiler_params=_cparams(("parallel", "arbitrary")),
        name="ffn_final" if final else "ffn",
    )(x, g, wg, wu, wd, fg)


def _rows_at(w, offset):
    return jnp.pad(w, ((offset, LORA_PACK - offset - w.shape[0]), (0, 0)))


def _layer_in_weight(w_in_l, w_vres_l):
    d = w_in_l.shape[0]
    o_gd = 3 * A_WIDTH + DECAY_LORA + ICLR_LORA
    vd = jnp.zeros((d, VRES_LORA), F32) if w_vres_l is None else w_vres_l
    cols = [w_in_l[:, A_COLS + B_COLS:],
            w_in_l[:, :3 * A_WIDTH],
            w_in_l[:, A_COLS:A_COLS + B_COLS],
            w_in_l[:, 3 * A_WIDTH:o_gd],
            vd,
            w_in_l[:, o_gd:A_COLS]]
    return jnp.concatenate([c.astype(BF16) for c in cols], axis=1)


def _block_diag(w):
    g, n, _ = w.shape
    eye = jnp.eye(g, dtype=w.dtype)
    return (eye[:, None, :, None] * w[:, :, None, :]).reshape(g * n, g * n)


def kernel(x, positions, norm1_g, w_in, tshift_mu, rwkv_w0, rwkv_w2, rwkv_a0, rwkv_a2, rwkv_g2, rwkv_k_k, rwkv_k_a, rwkv_r_k, rwkv_ln_g, rwkv_ln_b, w_in_vres, tshift_mu_vres, rwkv_v0, rwkv_v2, lru_conv_w, lru_conv_b, lru_wa, lru_ba, lru_wx, lru_bx, lru_lambda, ret_gn_g, w_out, norm2_g, ffn_w_gate, ffn_w_up, ffn_w_down, final_norm_g):
    batch, seq_len, d = x.shape
    depth = w_in.shape[0]
    t = batch * seq_len
    aw = A_WIDTH
    row = lambda v: v.reshape(1, -1).astype(F32)

    tm_proj = min(512, seq_len)
    tm_in = min(1024, seq_len)
    tm_tok = min(256, seq_len)
    cps = min(4, seq_len // CHUNK)
    cpr = min(4, seq_len // CHUNK)

    inv_freq = ROPE_THETA ** (-jnp.arange(0, C_QK_DIM, 2, dtype=F32) / C_QK_DIM)
    invf = jnp.tile(inv_freq, LANES // inv_freq.shape[0]).reshape(1, LANES)
    posb = jnp.broadcast_to(positions.astype(F32)[..., None], (batch, seq_len, LANES))

    w_out_b = w_out.astype(BF16)
    ffn_gate_b = ffn_w_gate.astype(BF16)
    ffn_up_b = ffn_w_up.astype(BF16)
    ffn_down_b = ffn_w_down.astype(BF16)

    xf = x.reshape(t, d)
    v_first = None
    for l in range(depth):
        has_vres = l > 0
        w_l = _layer_in_weight(w_in[l], w_in_vres[l - 1] if has_vres else None)
        proj = _inproj(xf, row(norm1_g[l]), w_l, tm_in, P_COLS // 6)

        mu = tshift_mu[l]
        o_gd = 3 * aw + DECAY_LORA + ICLR_LORA
        mu_vd = tshift_mu_vres[l - 1] if has_vres else jnp.zeros((VRES_LORA,), F32)
        params = {
            "mu_rkv": row(mu[:3 * aw]),
            "mu_lora": row(jnp.concatenate([mu[3 * aw:o_gd], mu_vd, mu[o_gd:A_COLS]])),
            "w0": row(rwkv_w0[l]), "a0": row(rwkv_a0[l]), "k_k": row(rwkv_k_k[l]),
            "k_a": row(rwkv_k_a[l]), "r_k": row(rwkv_r_k[l]),
            "w2": _rows_at(rwkv_w2[l], 0), "a2": _rows_at(rwkv_a2[l], DECAY_LORA),
            "g2": rwkv_g2[l],
        }
        if has_vres:
            params["v0"] = row(rwkv_v0[l - 1])
            params["v2"] = _rows_at(rwkv_v2[l - 1], DECAY_LORA + ICLR_LORA)
        at, rt, kt, bt, kd, bd, pc, v_cur, g_a, bonus = _rwkv_prep(
            proj, params, v_first if has_vres else None, seq_len, tm_tok)
        if l == 0:
            v_first = v_cur
        y_raw = _rwkv_scan(at, rt, kt, bt, kd, bd, v_cur, pc, batch, seq_len, cps)
        y_a = _rwkv_post(y_raw, g_a, bonus, row(rwkv_ln_g[l]), row(rwkv_ln_b[l]), tm_tok)

        lru_params = {
            "conv_w": lru_conv_w[l], "conv_b": row(lru_conv_b[l]),
            "wa": _block_diag(lru_wa[l]).astype(BF16), "ba": row(lru_ba[l]),
            "wx": _block_diag(lru_wx[l]).astype(BF16), "bx": row(lru_bx[l]),
            "lam": row(lru_lambda[l]),
        }
        y_b = _rglru(proj, lru_params, batch, seq_len, tm_tok)
        y_c = _retention(proj, posb, invf, row(ret_gn_g[l]), batch, seq_len, cpr)

        xf = _outproj(xf, y_a, y_b, y_c, w_out_b, l, tm_proj)
        xf = _ffn(xf, row(norm2_g[l]), ffn_gate_b, ffn_up_b, ffn_down_b, row(final_norm_g),
                  l, l == depth - 1, tm_proj, 512)
    return xf.reshape(batch, seq_len, d)
```

```python
import functools

import numpy as np
import jax
import jax.numpy as jnp
from jax import lax
from jax.experimental import pallas as pl
from jax.experimental.pallas import tpu as pltpu

F32 = jnp.float32
BF16 = jnp.bfloat16

NORM_EPS = 1e-6
CHUNK = 64

A_HEADS = 12
A_HEAD_DIM = 64
A_WIDTH = A_HEADS * A_HEAD_DIM
DECAY_LORA = 96
ICLR_LORA = 96
VRES_LORA = 64
GATE_LORA = 256
A_GN_EPS = 64e-5
A_COLS = 3 * A_WIDTH + DECAY_LORA + ICLR_LORA + GATE_LORA
B_WIDTH = 512
B_BLOCKS = 8
B_BLOCK_DIM = B_WIDTH // B_BLOCKS
CONV_WIDTH = 4
LRU_C = 8.0
B_COLS = 2 * B_WIDTH
C_HEADS = 6
C_QK_DIM = 64
C_V_DIM = 128
C_QK_WIDTH = C_HEADS * C_QK_DIM
C_WIDTH = C_HEADS * C_V_DIM
C_GN_EPS = 1e-5
ROPE_THETA = 10000.0
C_COLS = 2 * C_QK_WIDTH + 2 * C_WIDTH

LANES = 128
SUBLANES = 8
LORA_PACK = DECAY_LORA + ICLR_LORA + VRES_LORA
VMEM_LIMIT = 56 * 1024 * 1024

P_COLS = C_COLS + 3 * A_WIDTH + B_COLS + LORA_PACK + GATE_LORA


def _cparams(sem):
    return pltpu.CompilerParams(dimension_semantics=sem, vmem_limit_bytes=VMEM_LIMIT)


def _dot(a, b):
    return jnp.dot(a.astype(BF16), b.astype(BF16), preferred_element_type=F32)


def _dot_nt(a, b):
    return lax.dot_general(a.astype(BF16), b.astype(BF16), (((1,), (1,)), ((), ())),
                           preferred_element_type=F32)


def _dot_tn(a, b):
    return lax.dot_general(a.astype(BF16), b.astype(BF16), (((0,), (0,)), ((), ())),
                           preferred_element_type=F32)


def _iota2(shape, axis):
    return lax.broadcasted_iota(jnp.int32, shape, axis)


def _blk(idx, size):
    return idx >> (size.bit_length() - 1)


def _rms(x, g):
    ms = jnp.mean(x * x, axis=-1, keepdims=True)
    return x * lax.rsqrt(ms + NORM_EPS) * g


def _inproj_kernel(x_ref, g_ref, w_ref, o_ref, h_ref):
    @pl.when(pl.program_id(1) == 0)
    def _():
        h_ref[...] = _rms(x_ref[...], g_ref[...]).astype(BF16)

    o_ref[...] = jnp.dot(h_ref[...], w_ref[...], preferred_element_type=F32)


def _inproj(x, g, w, tm, tn):
    t, d = x.shape
    n = w.shape[1]
    return pl.pallas_call(
        _inproj_kernel,
        grid=(t // tm, n // tn),
        in_specs=[pl.BlockSpec((tm, d), lambda i, j: (i, 0)),
                  pl.BlockSpec((1, d), lambda i, j: (0, 0)),
                  pl.BlockSpec((d, tn), lambda i, j: (0, j))],
        out_specs=pl.BlockSpec((tm, tn), lambda i, j: (i, j)),
        out_shape=jax.ShapeDtypeStruct((t, n), F32),
        scratch_shapes=[pltpu.VMEM((tm, d), BF16)],
        compiler_params=_cparams(("parallel", "arbitrary")),
        name="inproj",
    )(x, g, w)


def _shift_rows(cur, prev_row):
    rolled = pltpu.roll(cur, 1, 0)
    row = _iota2(cur.shape, 0)
    return jnp.where(row == 0, prev_row, rolled)


def _split3(x):
    hi = x.astype(BF16)
    r1 = x - hi.astype(F32)
    mid = r1.astype(BF16)
    lo = (r1 - mid.astype(F32)).astype(BF16)
    return hi, mid, lo


def _sel_dot(sel, x):
    hi, mid, lo = _split3(x)
    d = functools.partial(jnp.dot, sel, preferred_element_type=F32)
    return (d(lo) + d(mid)) + d(hi)


def _dot_sel(x, sel):
    hi, mid, lo = _split3(x)
    d = lambda p: jnp.dot(p, sel, preferred_element_type=F32)
    return (d(lo) + d(mid)) + d(hi)


def _head_sum(x, e):
    parts = [_dot_sel(x[:, i * LANES:(i + 1) * LANES], e) for i in range(x.shape[1] // LANES)]
    return jnp.concatenate(parts, axis=-1)


def _head_ones():
    r = _blk(_iota2((LANES, LANES), 0), A_HEAD_DIM)
    c = _blk(_iota2((LANES, LANES), 1), A_HEAD_DIM)
    return (r == c).astype(BF16)


def _prep_kernel(has_vres, seq_tiles, *refs):
    if has_vres:
        (rkv_ref, lora_ref, prkv_ref, plora_ref, mu_rkv, mu_lora,
         w0, a0, k_k, k_a, r_k, w2, a2, g2, v0, v2, vfirst_ref,
         at_o, rt_o, kt_o, bt_o, kd_o, bd_o, pc_o, v_o, g_o, bonus_o) = refs
    else:
        (rkv_ref, lora_ref, prkv_ref, plora_ref, mu_rkv, mu_lora,
         w0, a0, k_k, k_a, r_k, w2, a2, g2,
         at_o, rt_o, kt_o, bt_o, kd_o, bd_o, pc_o, v_o, g_o, bonus_o) = refs
    first = (pl.program_id(0) % seq_tiles) == 0
    last_row = slice(SUBLANES - 1, SUBLANES)

    def mixed(cur_ref, prev_ref, mu_ref):
        cur = cur_ref[...]
        prev_row = jnp.where(first, 0.0, prev_ref[last_row, :])
        return cur + (_shift_rows(cur, prev_row) - cur) * mu_ref[...]

    rkv = mixed(rkv_ref, prkv_ref, mu_rkv)
    lora = mixed(lora_ref, plora_ref, mu_lora)
    r = rkv[:, 0:A_WIDTH]
    k = rkv[:, A_WIDTH:2 * A_WIDTH]
    v = rkv[:, 2 * A_WIDTH:3 * A_WIDTH]
    small = lora[:, 0:LORA_PACK]
    gd = lora[:, LORA_PACK:LORA_PACK + GATE_LORA]

    log_w = -jax.nn.softplus(-(w0[...] + _dot(jnp.tanh(small), w2[...]))) - 0.5
    lw = -jnp.exp(log_w)
    a = jax.nn.sigmoid(a0[...] + _dot(small, a2[...]))
    g = _dot(jax.nn.sigmoid(gd), g2[...])
    if has_vres:
        v = v + (vfirst_ref[...] - v) * jax.nn.sigmoid(v0[...] + _dot(small, v2[...]))

    e = _head_ones()
    kk = k * k_k[...]
    kk = kk / jnp.maximum(jnp.sqrt(_head_sum(kk * kk, e)), 1e-12)
    k = k * (1.0 + (a - 1.0) * k_a[...])
    b = kk * a
    bonus = _head_sum(r * k * r_k[...], e) * v

    tm = lw.shape[0]
    row = _iota2((tm, tm), 0)
    col = _iota2((tm, tm), 1)
    tri = (_blk(row, CHUNK) == _blk(col, CHUNK)) & (row >= col)
    cum = _sel_dot(tri.astype(BF16), lw)
    cum3 = cum.reshape(tm // CHUNK, CHUNK, A_WIDTH)
    tot = jnp.broadcast_to(cum3[:, CHUNK - 1:CHUNK, :], cum3.shape).reshape(tm, A_WIDTH)
    e_in = jnp.exp(cum)
    e_out = jnp.exp(-cum)
    e_end = jnp.exp(tot - cum)
    at_o[...] = -kk * jnp.exp(cum - lw)
    rt_o[...] = r * e_in
    kt_o[...] = k * e_out
    bt_o[...] = b * e_out
    kd_o[...] = k * e_end
    bd_o[...] = b * e_end
    pc_o[...] = jnp.exp(tot)
    v_o[...] = v
    g_o[...] = g
    bonus_o[...] = bonus


def _rwkv_prep(proj, params, v_first, seq_len, tm):
    t = proj.shape[0]
    has_vres = v_first is not None
    seq_tiles = seq_len // tm
    sub = tm // SUBLANES
    rkv_w, lora_w = 3 * A_WIDTH, LORA_PACK + GATE_LORA
    rkv_blk = (C_COLS) // rkv_w
    lora_blk = (C_COLS + rkv_w + B_COLS) // lora_w

    def cur(width, blk):
        return pl.BlockSpec((tm, width), lambda i: (i, blk))

    def prev(width, blk):
        return pl.BlockSpec((SUBLANES, width), lambda i: (jnp.maximum(i * sub - 1, 0), blk))

    def vec(width):
        return pl.BlockSpec((1, width), lambda i: (0, 0))

    def mat(rows):
        return pl.BlockSpec((rows, A_WIDTH), lambda i: (0, 0))

    tok = pl.BlockSpec((tm, A_WIDTH), lambda i: (i, 0))
    ins = [proj, proj, proj, proj,
           params["mu_rkv"], params["mu_lora"],
           params["w0"], params["a0"], params["k_k"], params["k_a"], params["r_k"],
           params["w2"], params["a2"], params["g2"]]
    specs = [cur(rkv_w, rkv_blk), cur(lora_w, lora_blk),
             prev(rkv_w, rkv_blk), prev(lora_w, lora_blk),
             vec(rkv_w), vec(lora_w),
             vec(A_WIDTH), vec(A_WIDTH), vec(A_WIDTH), vec(A_WIDTH), vec(A_WIDTH),
             mat(LORA_PACK), mat(LORA_PACK), mat(GATE_LORA)]
    if has_vres:
        ins += [params["v0"], params["v2"], v_first]
        specs += [vec(A_WIDTH), mat(LORA_PACK), tok]
    n_out = 10
    return pl.pallas_call(
        functools.partial(_prep_kernel, has_vres, seq_tiles),
        grid=(t // tm,),
        in_specs=specs,
        out_specs=[tok] * n_out,
        out_shape=[jax.ShapeDtypeStruct((t, A_WIDTH), F32)] * n_out,
        compiler_params=_cparams(("parallel",)),
        name="rwkv_prep_vres" if has_vres else "rwkv_prep",
    )(*ins)


def _unit_lower_inverse(a_strict, row, col, eye, pair_diag):
    base = 8
    diag = _blk(row, base) == _blk(col, base)
    n1 = [jnp.where(diag, a, 0.0) for a in a_strict]
    n1d = [pair_diag(x) for x in n1]
    n2 = [_dot(x, d) for x, d in zip(n1, n1d)]
    n2d = [pair_diag(x) for x in n2]
    t = [eye + x for x in n1]
    t = [x + _dot(x, d) for x, d in zip(t, n2d)]
    n4d = [pair_diag(_dot(x, d)) for x, d in zip(n2, n2d)]
    t = [x + _dot(x, d) for x, d in zip(t, n4d)]
    m = base
    while m < CHUNK:
        sel = (_blk(row, 2 * m) == _blk(col, 2 * m)) & (_blk(row, m) != _blk(col, m))
        ed = [pair_diag(jnp.where(sel, a, 0.0)) for a in a_strict]
        te = [_dot(x, d) for x, d in zip(t, ed)]
        t = [x + _dot(y, pair_diag(x)) for x, y in zip(t, te)]
        m *= 2
    return t


def _scan_kernel(cps, at_ref, rt_ref, kt_ref, bt_ref, kd_ref, bd_ref, v_ref, pc_ref,
                 gate_ref, bonus_ref, lng_ref, lnb_ref, o_ref, s_ref):
    @pl.when(pl.program_id(1) == 0)
    def _():
        s_ref[...] = jnp.zeros_like(s_ref)

    n = 2 * A_HEAD_DIM
    pairs = A_HEADS // 2
    row = _iota2((CHUNK, n), 0)
    lane = _iota2((CHUNK, n), 1)
    col = lane & (A_HEAD_DIM - 1)
    lo = lane < A_HEAD_DIM
    strict = row > col
    incl = row >= col
    eye = (row == col).astype(F32)
    same_head = _blk(_iota2((n, n), 0), A_HEAD_DIM) == _blk(_iota2((n, n), 1), A_HEAD_DIM)
    items = [(c, p) for c in range(cps) for p in range(pairs)]

    def load(ref):
        return [ref[0, c * CHUNK:(c + 1) * CHUNK, p * n:(p + 1) * n] for c, p in items]

    def pair_diag(x):
        return jnp.concatenate([jnp.where(lo, x, 0.0), jnp.where(lo, 0.0, x)],
                               axis=0).astype(BF16)

    def fold(m):
        return jnp.where(lo, m[:CHUNK], m[CHUNK:])

    at = load(at_ref)
    rt = load(rt_ref)
    lhs = [jnp.concatenate([a, r], axis=0) for a, r in zip(at, rt)]
    lk = [_dot_nt(x, pair_diag(k)) for x, k in zip(lhs, load(kt_ref))]
    lb = [_dot_nt(x, pair_diag(b)) for x, b in zip(lhs, load(bt_ref))]
    a_k = [jnp.concatenate([jnp.where(strict, x[:CHUNK], 0.0), jnp.where(incl, x[CHUNK:], 0.0)],
                           axis=0) for x in lk]
    a_ab = [jnp.where(strict, x[:CHUNK], 0.0) for x in lb]
    a_rb = [jnp.where(incl, x[CHUNK:], 0.0) for x in lb]
    t_inv = _unit_lower_inverse(a_ab, row, col, eye, pair_diag)
    v = load(v_ref)
    lv = [_dot(a, pair_diag(x)) for a, x in zip(a_k, v)]
    wu = [_dot(t, jnp.concatenate([pair_diag(a), pair_diag(x[:CHUNK])], axis=1))
          for t, a, x in zip(t_inv, at, lv)]
    z = [_dot(a, jnp.concatenate([pair_diag(x[:, :n]), pair_diag(x[:, n:])], axis=1))
         for a, x in zip(a_rb, wu)]
    rw = [r + x[:, :n] for r, x in zip(rt, z)]
    y0 = [x[CHUNK:] + zz[:, n:] for x, zz in zip(lv, z)]
    bd = load(bd_ref)
    g = [jnp.where(same_head, _dot_tn(x[:, :n], b), 0.0).astype(BF16)
         for x, b in zip(wu, bd)]
    s_add = [fold(_dot_tn(jnp.concatenate([x[:, n:], vv], axis=0),
                          jnp.concatenate([b, k], axis=0)))
             for x, vv, b, k in zip(wu, v, bd, load(kd_ref))]

    inv_n = 1.0 / A_HEAD_DIM

    def head_mean(x):
        m_lo = jnp.sum(jnp.where(lo, x, 0.0), axis=-1, keepdims=True)
        m_hi = jnp.sum(jnp.where(lo, 0.0, x), axis=-1, keepdims=True)
        return jnp.where(lo, m_lo, m_hi) * inv_n

    s = [s_ref[p] for p in range(pairs)]
    for c in range(cps):
        rows = slice(c * CHUNK, (c + 1) * CHUNK)
        sd = [pair_diag(x) for x in s]
        ys = [_dot_nt(rw[c * pairs + p], sd[p]) + y0[c * pairs + p] for p in range(pairs)]
        for p in range(pairs):
            i = c * pairs + p
            cols = slice(p * n, (p + 1) * n)
            pc = pc_ref[0, c * CHUNK:c * CHUNK + 1, cols]
            s[p] = s[p] * pc + _dot(s[p], g[i]) + s_add[i]
        for p in range(pairs):
            cols = slice(p * n, (p + 1) * n)
            d = ys[p] - head_mean(ys[p])
            var = head_mean(d * d)
            yn = d * lax.rsqrt(var + A_GN_EPS) * lng_ref[:, cols] + lnb_ref[:, cols]
            o_ref[0, rows, cols] = ((yn + bonus_ref[0, rows, cols])
                                    * gate_ref[0, rows, cols]).astype(o_ref.dtype)
    for p in range(pairs):
        s_ref[p] = s[p]


def _rwkv_scan(at, rt, kt, bt, kd, bd, v, pc, gate, bonus, ln_g, ln_b, batch, seq_len, cps):
    ts = cps * CHUNK
    shp = (batch, seq_len, A_WIDTH)
    blk = pl.BlockSpec((1, ts, A_WIDTH), lambda b, s: (b, s, 0))
    vec = pl.BlockSpec((1, A_WIDTH), lambda b, s: (0, 0))
    args = [z.reshape(shp) for z in (at, rt, kt, bt, kd, bd, v, pc, gate, bonus)]
    y = pl.pallas_call(
        functools.partial(_scan_kernel, cps),
        grid=(batch, seq_len // ts),
        in_specs=[blk] * 10 + [vec, vec],
        out_specs=blk,
        out_shape=jax.ShapeDtypeStruct(shp, BF16),
        scratch_shapes=[pltpu.VMEM((A_HEADS // 2, A_HEAD_DIM, 2 * A_HEAD_DIM), F32)],
        compiler_params=_cparams(("parallel", "arbitrary")),
        name="rwkv_scan",
    )(*args, ln_g, ln_b)
    return y.reshape(batch * seq_len, A_WIDTH)


def _lru_kernel(gate_ref, x_ref, px_ref, cw_ref, cb_ref, wa_ref, ba_ref, wx_ref, bx_ref, lam_ref,
                o_ref, h_ref):
    first = pl.program_id(1) == 0

    @pl.when(first)
    def _():
        h_ref[...] = jnp.zeros_like(h_ref)

    x = x_ref[...]
    ts = x.shape[0]
    prev = jnp.where(first, 0.0, px_ref[...])
    row = _iota2(x.shape, 0)
    row8 = _iota2(prev.shape, 0)

    def delayed(d):
        rolled = pltpu.roll(x, d, 0)
        top = jnp.where(row8 < d, pltpu.roll(prev, d, 0), rolled[:SUBLANES])
        return jnp.concatenate([top, rolled[SUBLANES:]], axis=0)

    xc = cw_ref[CONV_WIDTH - 1:CONV_WIDTH, :] * x + cb_ref[...]
    for d in range(1, CONV_WIDTH):
        xc = xc + cw_ref[CONV_WIDTH - 1 - d:CONV_WIDTH - d, :] * delayed(d)

    r = jax.nn.sigmoid(_dot(xc, wa_ref[...]) + ba_ref[...])
    i = jax.nn.sigmoid(_dot(xc, wx_ref[...]) + bx_ref[...])
    log_a = -LRU_C * r * jax.nn.softplus(-lam_ref[...])
    a = jnp.exp(log_a)
    th = jnp.tanh(log_a)
    u = jnp.sqrt(-2.0 * th / (1.0 - th)) * (i * xc)

    d = 1
    while d < ts:
        keep = row >= d
        a_s = jnp.where(keep, pltpu.roll(a, d, 0), 1.0)
        u_s = jnp.where(keep, pltpu.roll(u, d, 0), 0.0)
        u = a * u_s + u
        a = a * a_s
        d *= 2
    h = a * h_ref[...] + u
    h_ref[...] = h[ts - 1:ts, :]
    o_ref[...] = (jax.nn.gelu(gate_ref[...]) * h).astype(o_ref.dtype)


def _rglru(proj, params, batch, seq_len, ts):
    t = proj.shape[0]
    nst = seq_len // ts
    sub = ts // SUBLANES
    gate_blk = (C_COLS + 3 * A_WIDTH) // B_WIDTH
    x_blk = gate_blk + 1
    vec = pl.BlockSpec((1, B_WIDTH), lambda b, s: (0, 0))
    sq = pl.BlockSpec((B_WIDTH, B_WIDTH), lambda b, s: (0, 0))
    return pl.pallas_call(
        _lru_kernel,
        grid=(batch, nst),
        in_specs=[pl.BlockSpec((ts, B_WIDTH), lambda b, s: (b * nst + s, gate_blk)),
                  pl.BlockSpec((ts, B_WIDTH), lambda b, s: (b * nst + s, x_blk)),
                  pl.BlockSpec((SUBLANES, B_WIDTH),
                               lambda b, s: (jnp.maximum((b * nst + s) * sub - 1, 0), x_blk)),
                  pl.BlockSpec((CONV_WIDTH, B_WIDTH), lambda b, s: (0, 0)),
                  vec, sq, vec, sq, vec, vec],
        out_specs=pl.BlockSpec((ts, B_WIDTH), lambda b, s: (b * nst + s, 0)),
        out_shape=jax.ShapeDtypeStruct((t, B_WIDTH), BF16),
        scratch_shapes=[pltpu.VMEM((1, B_WIDTH), F32)],
        compiler_params=_cparams(("parallel", "arbitrary")),
        name="rglru",
    )(proj, proj, proj, params["conv_w"], params["conv_b"], params["wa"], params["ba"],
      params["wx"], params["bx"], params["lam"])


def _log_gammas():
    return np.log1p(-np.exp2(-5.0 - np.arange(C_HEADS, dtype=np.float32))).astype(np.float32)


def _ret_kernel(cpr, q_ref, k_ref, v_ref, g_ref, pos_ref, invf_ref, gng_ref, o_ref, st_ref):
    @pl.when(pl.program_id(1) == 0)
    def _():
        st_ref[...] = jnp.zeros_like(st_ref)

    half = C_QK_DIM // 2
    lane = _iota2((CHUNK, LANES), 1)
    in_first_half = (lane & (C_QK_DIM - 1)) < half
    lo = lane < C_QK_DIM
    idx_r = _iota2((CHUNK, CHUNK), 0).astype(F32)
    idx_c = _iota2((CHUNK, CHUNK), 1).astype(F32)
    dist = jnp.abs(idx_r - idx_c)
    pos_in = _iota2((CHUNK, 1), 0).astype(F32)
    log_gamma = _log_gammas()

    def rope(t, cos, sin_signed):
        swapped = jnp.where(in_first_half, pltpu.roll(t, LANES - half, 1), pltpu.roll(t, half, 1))
        return t * cos + swapped * sin_signed

    intra_decay = [jnp.exp(float(lg) * dist) for lg in log_gamma]
    q_decay = [jnp.exp(float(lg) * (pos_in + 1.0)) for lg in log_gamma]
    k_decay = [jnp.exp(float(lg) * (CHUNK - 1.0 - pos_in)) for lg in log_gamma]
    chunk_decay = [float(np.exp(lg * np.float32(CHUNK))) for lg in log_gamma]
    items = [(c, h) for c in range(cpr) for h in range(C_HEADS)]

    q2, k2 = {}, {}
    for c in range(cpr):
        rows = slice(c * CHUNK, (c + 1) * CHUNK)
        ang = pos_ref[0, rows, :] * invf_ref[...]
        cos = jnp.cos(ang)
        sin = jnp.sin(ang)
        sin_signed = jnp.where(in_first_half, -sin, sin)
        for p in range(C_HEADS // 2):
            qk_cols = slice(p * LANES, (p + 1) * LANES)
            q2[c, p] = rope(q_ref[rows, qk_cols], cos, sin_signed)
            k2[c, p] = rope(k_ref[rows, qk_cols], cos, sin_signed) * (C_QK_DIM ** -0.5)

    def mine(x, h):
        return jnp.where(lo, x, 0.0) if h % 2 == 0 else jnp.where(lo, 0.0, x)

    qh = [mine(q2[c, h // 2], h) for c, h in items]
    kh = [mine(k2[c, h // 2], h) for c, h in items]
    vh = [v_ref[c * CHUNK:(c + 1) * CHUNK, h * C_V_DIM:(h + 1) * C_V_DIM].astype(BF16)
          for c, h in items]
    scores = [_dot_nt(q, k) * intra_decay[h] for q, k, (c, h) in zip(qh, kh, items)]
    kv = [_dot_tn(k * k_decay[h], v) for k, v, (c, h) in zip(kh, vh, items)]
    intra = [_dot(s, v) for s, v in zip(scores, vh)]

    state = [st_ref[h] for h in range(C_HEADS)]
    starts = []
    for i, (c, h) in enumerate(items):
        starts.append(state[h])
        state[h] = state[h] * chunk_decay[h] + kv[i]
    for h in range(C_HEADS):
        st_ref[h] = state[h]
    cross = [_dot(q * q_decay[h], s) for q, s, (c, h) in zip(qh, starts, items)]

    for i, (c, h) in enumerate(items):
        rows = slice(c * CHUNK, (c + 1) * CHUNK)
        v_cols = slice(h * C_V_DIM, (h + 1) * C_V_DIM)
        o = intra[i] + cross[i]
        mu = jnp.mean(o, axis=-1, keepdims=True)
        d = o - mu
        var = jnp.mean(d * d, axis=-1, keepdims=True)
        on = d * lax.rsqrt(var + C_GN_EPS) * gng_ref[:, v_cols]
        o_ref[rows, v_cols] = (jax.nn.silu(g_ref[rows, v_cols]) * on).astype(o_ref.dtype)


def _retention(proj, posb, invf, gn_g, batch, seq_len, cpr):
    t = proj.shape[0]
    ts = cpr * CHUNK
    nst = seq_len // ts
    return pl.pallas_call(
        functools.partial(_ret_kernel, cpr),
        grid=(batch, nst),
        in_specs=[pl.BlockSpec((ts, C_QK_WIDTH), lambda b, s: (b * nst + s, 0)),
                  pl.BlockSpec((ts, C_QK_WIDTH), lambda b, s: (b * nst + s, 1)),
                  pl.BlockSpec((ts, C_WIDTH), lambda b, s: (b * nst + s, 1)),
                  pl.BlockSpec((ts, C_WIDTH), lambda b, s: (b * nst + s, 2)),
                  pl.BlockSpec((1, ts, LANES), lambda b, s: (b, s, 0)),
                  pl.BlockSpec((1, LANES), lambda b, s: (0, 0)),
                  pl.BlockSpec((1, C_WIDTH), lambda b, s: (0, 0))],
        out_specs=pl.BlockSpec((ts, C_WIDTH), lambda b, s: (b * nst + s, 0)),
        out_shape=jax.ShapeDtypeStruct((t, C_WIDTH), BF16),
        scratch_shapes=[pltpu.VMEM((C_HEADS, LANES, C_V_DIM), F32)],
        compiler_params=_cparams(("parallel", "arbitrary")),
        name="retention",
    )(proj, proj, proj, proj, posb, invf, gn_g)


def _outproj_kernel(x_ref, ya_ref, yb_ref, yc_ref, w_ref, o_ref):
    b0, c0 = A_WIDTH, A_WIDTH + B_WIDTH
    acc = jnp.dot(ya_ref[...], w_ref[0:b0, :], preferred_element_type=F32)
    acc += jnp.dot(yb_ref[...], w_ref[b0:c0, :], preferred_element_type=F32)
    acc += jnp.dot(yc_ref[...], w_ref[c0:, :], preferred_element_type=F32)
    o_ref[...] = x_ref[...] + acc


def _outproj(x, ya, yb, yc, w_stack, layer, tm):
    t, d = x.shape

    def tok(width):
        return pl.BlockSpec((tm, width), lambda i: (i, 0))

    return pl.pallas_call(
        _outproj_kernel,
        grid=(t // tm,),
        in_specs=[tok(d), tok(A_WIDTH), tok(B_WIDTH), tok(C_WIDTH),
                  pl.BlockSpec((None, d, d), lambda i: (layer, 0, 0))],
        out_specs=tok(d),
        out_shape=jax.ShapeDtypeStruct((t, d), F32),
        compiler_params=_cparams(("parallel",)),
        name="outproj",
    )(x, ya, yb, yc, w_stack)


def _ffn_kernel(final, x_ref, g_ref, wg_ref, wu_ref, wd_ref, fg_ref, o_ref, h_ref, acc_ref):
    j = pl.program_id(1)

    @pl.when(j == 0)
    def _():
        h_ref[...] = _rms(x_ref[...], g_ref[...]).astype(BF16)
        acc_ref[...] = jnp.zeros_like(acc_ref)

    h = h_ref[...]
    gate = jnp.dot(h, wg_ref[...], preferred_element_type=F32)
    up = jnp.dot(h, wu_ref[...], preferred_element_type=F32)
    act = (jax.nn.silu(gate) * up).astype(BF16)
    acc_ref[...] += jnp.dot(act, wd_ref[...], preferred_element_type=F32)

    @pl.when(j == pl.num_programs(1) - 1)
    def _():
        out = x_ref[...] + acc_ref[...]
        if final:
            out = _rms(out, fg_ref[...])
        o_ref[...] = out


def _ffn(x, g, wg, wu, wd, fg, layer, final, tm, tf):
    t, d = x.shape
    f = wg.shape[2]
    return pl.pallas_call(
        functools.partial(_ffn_kernel, final),
        grid=(t // tm, f // tf),
        in_specs=[pl.BlockSpec((tm, d), lambda i, j: (i, 0)),
                  pl.BlockSpec((1, d), lambda i, j: (0, 0)),
                  pl.BlockSpec((None, d, tf), lambda i, j: (layer, 0, j)),
                  pl.BlockSpec((None, d, tf), lambda i, j: (layer, 0, j)),
                  pl.BlockSpec((None, tf, d), lambda i, j: (layer, j, 0)),
                  pl.BlockSpec((1, d), lambda i, j: (0, 0))],
        out_specs=pl.BlockSpec((tm, d), lambda i, j: (i, 0)),
        out_shape=jax.ShapeDtypeStruct((t, d), F32),
        scratch_shapes=[pltpu.VMEM((tm, d), BF16), pltpu.VMEM((tm, d), F32)],
        compiler_params=_cparams(("parallel", "arbitrary")),
        name="ffn_final" if final else "ffn",
    )(x, g, wg, wu, wd, fg)


def _rows_at(w, offset):
    return jnp.pad(w, ((offset, LORA_PACK - offset - w.shape[0]), (0, 0)))


def _layer_in_weight(w_in_l, w_vres_l):
    d = w_in_l.shape[0]
    o_gd = 3 * A_WIDTH + DECAY_LORA + ICLR_LORA
    vd = jnp.zeros((d, VRES_LORA), F32) if w_vres_l is None else w_vres_l
    cols = [w_in_l[:, A_COLS + B_COLS:],
            w_in_l[:, :3 * A_WIDTH],
            w_in_l[:, A_COLS:A_COLS + B_COLS],
            w_in_l[:, 3 * A_WIDTH:o_gd],
            vd,
            w_in_l[:, o_gd:A_COLS]]
    return jnp.concatenate([c.astype(BF16) for c in cols], axis=1)


def _block_diag(w):
    g, n, _ = w.shape
    eye = jnp.eye(g, dtype=w.dtype)
    return (eye[:, None, :, None] * w[:, :, None, :]).reshape(g * n, g * n)


def kernel(x, positions, norm1_g, w_in, tshift_mu, rwkv_w0, rwkv_w2, rwkv_a0, rwkv_a2, rwkv_g2, rwkv_k_k, rwkv_k_a, rwkv_r_k, rwkv_ln_g, rwkv_ln_b, w_in_vres, tshift_mu_vres, rwkv_v0, rwkv_v2, lru_conv_w, lru_conv_b, lru_wa, lru_ba, lru_wx, lru_bx, lru_lambda, ret_gn_g, w_out, norm2_g, ffn_w_gate, ffn_w_up, ffn_w_down, final_norm_g):
    batch, seq_len, d = x.shape
    depth = w_in.shape[0]
    t = batch * seq_len
    aw = A_WIDTH
    row = lambda v: v.reshape(1, -1).astype(F32)

    tm_proj = min(512, seq_len)
    tm_in = min(1024, seq_len)
    tm_tok = min(256, seq_len)
    cps = min(4, seq_len // CHUNK)
    cpr = min(4, seq_len // CHUNK)

    inv_freq = ROPE_THETA ** (-jnp.arange(0, C_QK_DIM, 2, dtype=F32) / C_QK_DIM)
    invf = jnp.tile(inv_freq, LANES // inv_freq.shape[0]).reshape(1, LANES)
    posb = jnp.broadcast_to(positions.astype(F32)[..., None], (batch, seq_len, LANES))

    w_out_b = w_out.astype(BF16)
    ffn_gate_b = ffn_w_gate.astype(BF16)
    ffn_up_b = ffn_w_up.astype(BF16)
    ffn_down_b = ffn_w_down.astype(BF16)

    xf = x.reshape(t, d)
    v_first = None
    for l in range(depth):
        has_vres = l > 0
        w_l = _layer_in_weight(w_in[l], w_in_vres[l - 1] if has_vres else None)
        proj = _inproj(xf, row(norm1_g[l]), w_l, tm_in, P_COLS // 6)

        mu = tshift_mu[l]
        o_gd = 3 * aw + DECAY_LORA + ICLR_LORA
        mu_vd = tshift_mu_vres[l - 1] if has_vres else jnp.zeros((VRES_LORA,), F32)
        params = {
            "mu_rkv": row(mu[:3 * aw]),
            "mu_lora": row(jnp.concatenate([mu[3 * aw:o_gd], mu_vd, mu[o_gd:A_COLS]])),
            "w0": row(rwkv_w0[l]), "a0": row(rwkv_a0[l]), "k_k": row(rwkv_k_k[l]),
            "k_a": row(rwkv_k_a[l]), "r_k": row(rwkv_r_k[l]),
            "w2": _rows_at(rwkv_w2[l], 0), "a2": _rows_at(rwkv_a2[l], DECAY_LORA),
            "g2": rwkv_g2[l],
        }
        if has_vres:
            params["v0"] = row(rwkv_v0[l - 1])
            params["v2"] = _rows_at(rwkv_v2[l - 1], DECAY_LORA + ICLR_LORA)
        at, rt, kt, bt, kd, bd, pc, v_cur, g_a, bonus = _rwkv_prep(
            proj, params, v_first if has_vres else None, seq_len, tm_tok)
        if l == 0:
            v_first = v_cur
        y_a = _rwkv_scan(at, rt, kt, bt, kd, bd, v_cur, pc, g_a, bonus,
                         row(rwkv_ln_g[l]), row(rwkv_ln_b[l]), batch, seq_len, cps)

        lru_params = {
            "conv_w": lru_conv_w[l], "conv_b": row(lru_conv_b[l]),
            "wa": _block_diag(lru_wa[l]).astype(BF16), "ba": row(lru_ba[l]),
            "wx": _block_diag(lru_wx[l]).astype(BF16), "bx": row(lru_bx[l]),
            "lam": row(lru_lambda[l]),
        }
        y_b = _rglru(proj, lru_params, batch, seq_len, tm_tok)
        y_c = _retention(proj, posb, invf, row(ret_gn_g[l]), batch, seq_len, cpr)

        xf = _outproj(xf, y_a, y_b, y_c, w_out_b, l, tm_proj)
        xf = _ffn(xf, row(norm2_g[l]), ffn_gate_b, ffn_up_b, ffn_down_b, row(final_norm_g),
                  l, l == depth - 1, tm_proj, 512)
    return xf.reshape(batch, seq_len, d)
```

```python
import functools

import numpy as np
import jax
import jax.numpy as jnp
from jax import lax
from jax.experimental import pallas as pl
from jax.experimental.pallas import tpu as pltpu

F32 = jnp.float32
BF16 = jnp.bfloat16

NORM_EPS = 1e-6
CHUNK = 64

A_HEADS = 12
A_HEAD_DIM = 64
A_WIDTH = A_HEADS * A_HEAD_DIM
DECAY_LORA = 96
ICLR_LORA = 96
VRES_LORA = 64
GATE_LORA = 256
A_GN_EPS = 64e-5
A_COLS = 3 * A_WIDTH + DECAY_LORA + ICLR_LORA + GATE_LORA
B_WIDTH = 512
B_BLOCKS = 8
B_BLOCK_DIM = B_WIDTH // B_BLOCKS
CONV_WIDTH = 4
LRU_C = 8.0
B_COLS = 2 * B_WIDTH
C_HEADS = 6
C_QK_DIM = 64
C_V_DIM = 128
C_QK_WIDTH = C_HEADS * C_QK_DIM
C_WIDTH = C_HEADS * C_V_DIM
C_GN_EPS = 1e-5
ROPE_THETA = 10000.0
C_COLS = 2 * C_QK_WIDTH + 2 * C_WIDTH

LANES = 128
SUBLANES = 8
LORA_PACK = DECAY_LORA + ICLR_LORA + VRES_LORA
VMEM_LIMIT = 56 * 1024 * 1024

P_COLS = C_COLS + 3 * A_WIDTH + B_COLS + LORA_PACK + GATE_LORA


def _cparams(sem):
    return pltpu.CompilerParams(dimension_semantics=sem, vmem_limit_bytes=VMEM_LIMIT)


def _dot(a, b):
    return jnp.dot(a.astype(BF16), b.astype(BF16), preferred_element_type=F32)


def _dot_nt(a, b):
    return lax.dot_general(a.astype(BF16), b.astype(BF16), (((1,), (1,)), ((), ())),
                           preferred_element_type=F32)


def _dot_tn(a, b):
    return lax.dot_general(a.astype(BF16), b.astype(BF16), (((0,), (0,)), ((), ())),
                           preferred_element_type=F32)


def _iota2(shape, axis):
    return lax.broadcasted_iota(jnp.int32, shape, axis)


def _blk(idx, size):
    return idx >> (size.bit_length() - 1)


def _rms(x, g):
    ms = jnp.mean(x * x, axis=-1, keepdims=True)
    return x * lax.rsqrt(ms + NORM_EPS) * g


def _inproj_kernel(x_ref, g_ref, w_ref, o_ref, h_ref):
    @pl.when(pl.program_id(1) == 0)
    def _():
        h_ref[...] = _rms(x_ref[...], g_ref[...]).astype(BF16)

    o_ref[...] = jnp.dot(h_ref[...], w_ref[...], preferred_element_type=F32)


def _inproj(x, g, w, tm, tn):
    t, d = x.shape
    n = w.shape[1]
    return pl.pallas_call(
        _inproj_kernel,
        grid=(t // tm, n // tn),
        in_specs=[pl.BlockSpec((tm, d), lambda i, j: (i, 0)),
                  pl.BlockSpec((1, d), lambda i, j: (0, 0)),
                  pl.BlockSpec((d, tn), lambda i, j: (0, j))],
        out_specs=pl.BlockSpec((tm, tn), lambda i, j: (i, j)),
        out_shape=jax.ShapeDtypeStruct((t, n), F32),
        scratch_shapes=[pltpu.VMEM((tm, d), BF16)],
        compiler_params=_cparams(("parallel", "arbitrary")),
        name="inproj",
    )(x, g, w)


def _shift_rows(cur, prev_row):
    rolled = pltpu.roll(cur, 1, 0)
    row = _iota2(cur.shape, 0)
    return jnp.where(row == 0, prev_row, rolled)


def _split3(x):
    hi = x.astype(BF16)
    r1 = x - hi.astype(F32)
    mid = r1.astype(BF16)
    lo = (r1 - mid.astype(F32)).astype(BF16)
    return hi, mid, lo


def _sel_dot(sel, x):
    hi, mid, lo = _split3(x)
    d = functools.partial(jnp.dot, sel, preferred_element_type=F32)
    return (d(lo) + d(mid)) + d(hi)


def _dot_sel(x, sel):
    hi = x.astype(BF16)
    mid = (x - hi.astype(F32)).astype(BF16)
    d = functools.partial(jnp.dot, preferred_element_type=F32)
    return d(mid, sel) + d(hi, sel)


def _head_sum(x, e):
    parts = [_dot_sel(x[:, i * LANES:(i + 1) * LANES], e) for i in range(x.shape[1] // LANES)]
    return jnp.concatenate(parts, axis=-1)


def _head_ones():
    r = _blk(_iota2((LANES, LANES), 0), A_HEAD_DIM)
    c = _blk(_iota2((LANES, LANES), 1), A_HEAD_DIM)
    return (r == c).astype(BF16)


def _prep_kernel(has_vres, seq_tiles, *refs):
    if has_vres:
        (rkv_ref, lora_ref, prkv_ref, plora_ref, mu_rkv, mu_lora,
         w0, a0, k_k, k_a, r_k, w2, a2, g2, v0, v2, vfirst_ref,
         at_o, rt_o, kt_o, bt_o, kd_o, bd_o, pc_o, v_o, g_o, bonus_o) = refs
    else:
        (rkv_ref, lora_ref, prkv_ref, plora_ref, mu_rkv, mu_lora,
         w0, a0, k_k, k_a, r_k, w2, a2, g2,
         at_o, rt_o, kt_o, bt_o, kd_o, bd_o, pc_o, v_o, g_o, bonus_o) = refs
    first = (pl.program_id(0) % seq_tiles) == 0
    last_row = slice(SUBLANES - 1, SUBLANES)

    def mixed(cur_ref, prev_ref, mu_ref):
        cur = cur_ref[...]
        prev_row = jnp.where(first, 0.0, prev_ref[last_row, :])
        return cur + (_shift_rows(cur, prev_row) - cur) * mu_ref[...]

    rkv = mixed(rkv_ref, prkv_ref, mu_rkv)
    lora = mixed(lora_ref, plora_ref, mu_lora)
    r = rkv[:, 0:A_WIDTH]
    k = rkv[:, A_WIDTH:2 * A_WIDTH]
    v = rkv[:, 2 * A_WIDTH:3 * A_WIDTH]
    small = lora[:, 0:LORA_PACK]
    gd = lora[:, LORA_PACK:LORA_PACK + GATE_LORA]

    zw = -(w0[...] + _dot(jnp.tanh(small), w2[...]))
    softplus = jnp.maximum(zw, 0.0) + jnp.log1p(jnp.exp(-jnp.abs(zw)))
    lw = -jnp.exp(-softplus - 0.5)
    a = jax.nn.sigmoid(a0[...] + _dot(small, a2[...]))
    g = _dot(jax.nn.sigmoid(gd), g2[...])
    if has_vres:
        v = v + (vfirst_ref[...] - v) * jax.nn.sigmoid(v0[...] + _dot(small, v2[...]))

    e = _head_ones()
    kk = k * k_k[...]
    kk = kk * lax.rsqrt(jnp.maximum(_head_sum(kk * kk, e), 1e-24))
    k = k * (1.0 + (a - 1.0) * k_a[...])
    b = kk * a
    bonus = _head_sum(r * k * r_k[...], e) * v

    tm = lw.shape[0]
    row = _iota2((tm, tm), 0)
    col = _iota2((tm, tm), 1)
    tri = (_blk(row, CHUNK) == _blk(col, CHUNK)) & (row >= col)
    cum = _sel_dot(tri.astype(BF16), lw)
    cum3 = cum.reshape(tm // CHUNK, CHUNK, A_WIDTH)
    tot = jnp.broadcast_to(cum3[:, CHUNK - 1:CHUNK, :], cum3.shape).reshape(tm, A_WIDTH)
    e_in = jnp.exp(cum)
    e_out = jnp.exp(-cum)
    e_end = jnp.exp(tot - cum)
    at_o[...] = -kk * jnp.exp(cum - lw)
    rt_o[...] = r * e_in
    kt_o[...] = k * e_out
    bt_o[...] = b * e_out
    kd_o[...] = k * e_end
    bd_o[...] = b * e_end
    pc_o[...] = jnp.exp(tot)
    v_o[...] = v
    g_o[...] = g
    bonus_o[...] = bonus


def _rwkv_prep(proj, params, v_first, seq_len, tm):
    t = proj.shape[0]
    has_vres = v_first is not None
    seq_tiles = seq_len // tm
    sub = tm // SUBLANES
    rkv_w, lora_w = 3 * A_WIDTH, LORA_PACK + GATE_LORA
    rkv_blk = (C_COLS) // rkv_w
    lora_blk = (C_COLS + rkv_w + B_COLS) // lora_w

    def cur(width, blk):
        return pl.BlockSpec((tm, width), lambda i: (i, blk))

    def prev(width, blk):
        return pl.BlockSpec((SUBLANES, width), lambda i: (jnp.maximum(i * sub - 1, 0), blk))

    def vec(width):
        return pl.BlockSpec((1, width), lambda i: (0, 0))

    def mat(rows):
        return pl.BlockSpec((rows, A_WIDTH), lambda i: (0, 0))

    tok = pl.BlockSpec((tm, A_WIDTH), lambda i: (i, 0))
    ins = [proj, proj, proj, proj,
           params["mu_rkv"], params["mu_lora"],
           params["w0"], params["a0"], params["k_k"], params["k_a"], params["r_k"],
           params["w2"], params["a2"], params["g2"]]
    specs = [cur(rkv_w, rkv_blk), cur(lora_w, lora_blk),
             prev(rkv_w, rkv_blk), prev(lora_w, lora_blk),
             vec(rkv_w), vec(lora_w),
             vec(A_WIDTH), vec(A_WIDTH), vec(A_WIDTH), vec(A_WIDTH), vec(A_WIDTH),
             mat(LORA_PACK), mat(LORA_PACK), mat(GATE_LORA)]
    if has_vres:
        ins += [params["v0"], params["v2"], v_first]
        specs += [vec(A_WIDTH), mat(LORA_PACK), tok]
    n_out = 10
    return pl.pallas_call(
        functools.partial(_prep_kernel, has_vres, seq_tiles),
        grid=(t // tm,),
        in_specs=specs,
        out_specs=[tok] * n_out,
        out_shape=[jax.ShapeDtypeStruct((t, A_WIDTH), F32)] * n_out,
        compiler_params=_cparams(("parallel",)),
        name="rwkv_prep_vres" if has_vres else "rwkv_prep",
    )(*ins)


def _unit_lower_inverse(a_strict, row, col, eye, pair_diag):
    base = 8
    diag = _blk(row, base) == _blk(col, base)
    n1 = [jnp.where(diag, a, 0.0) for a in a_strict]
    n1d = [pair_diag(x) for x in n1]
    n2 = [_dot(x, d) for x, d in zip(n1, n1d)]
    n2d = [pair_diag(x) for x in n2]
    t = [eye + x for x in n1]
    t = [x + _dot(x, d) for x, d in zip(t, n2d)]
    n4d = [pair_diag(_dot(x, d)) for x, d in zip(n2, n2d)]
    t = [x + _dot(x, d) for x, d in zip(t, n4d)]
    m = base
    while m < CHUNK:
        sel = (_blk(row, 2 * m) == _blk(col, 2 * m)) & (_blk(row, m) != _blk(col, m))
        ed = [pair_diag(jnp.where(sel, a, 0.0)) for a in a_strict]
        te = [_dot(x, d) for x, d in zip(t, ed)]
        t = [x + _dot(y, pair_diag(x)) for x, y in zip(t, te)]
        m *= 2
    return t


def _scan_kernel(cps, at_ref, rt_ref, kt_ref, bt_ref, kd_ref, bd_ref, v_ref, pc_ref,
                 gate_ref, bonus_ref, lng_ref, lnb_ref, o_ref, s_ref):
    @pl.when(pl.program_id(1) == 0)
    def _():
        s_ref[...] = jnp.zeros_like(s_ref)

    n = 2 * A_HEAD_DIM
    pairs = A_HEADS // 2
    row = _iota2((CHUNK, n), 0)
    lane = _iota2((CHUNK, n), 1)
    col = lane & (A_HEAD_DIM - 1)
    lo = lane < A_HEAD_DIM
    strict = row > col
    incl = row >= col
    eye = (row == col).astype(F32)
    same_head = _blk(_iota2((n, n), 0), A_HEAD_DIM) == _blk(_iota2((n, n), 1), A_HEAD_DIM)
    items = [(c, p) for c in range(cps) for p in range(pairs)]

    def load(ref):
        return [ref[0, c * CHUNK:(c + 1) * CHUNK, p * n:(p + 1) * n] for c, p in items]

    def pair_diag(x):
        return jnp.concatenate([jnp.where(lo, x, 0.0), jnp.where(lo, 0.0, x)],
                               axis=0).astype(BF16)

    def fold(m):
        return jnp.where(lo, m[:CHUNK], m[CHUNK:])

    at = load(at_ref)
    rt = load(rt_ref)
    lhs = [jnp.concatenate([a, r], axis=0) for a, r in zip(at, rt)]
    lk = [_dot_nt(x, pair_diag(k)) for x, k in zip(lhs, load(kt_ref))]
    lb = [_dot_nt(x, pair_diag(b)) for x, b in zip(lhs, load(bt_ref))]
    a_k = [jnp.concatenate([jnp.where(strict, x[:CHUNK], 0.0), jnp.where(incl, x[CHUNK:], 0.0)],
                           axis=0) for x in lk]
    a_ab = [jnp.where(strict, x[:CHUNK], 0.0) for x in lb]
    a_rb = [jnp.where(incl, x[CHUNK:], 0.0) for x in lb]
    t_inv = _unit_lower_inverse(a_ab, row, col, eye, pair_diag)
    v = load(v_ref)
    lv = [_dot(a, pair_diag(x)) for a, x in zip(a_k, v)]
    wu = [_dot(t, jnp.concatenate([pair_diag(a), pair_diag(x[:CHUNK])], axis=1))
          for t, a, x in zip(t_inv, at, lv)]
    z = [_dot(a, jnp.concatenate([pair_diag(x[:, :n]), pair_diag(x[:, n:])], axis=1))
         for a, x in zip(a_rb, wu)]
    rw = [r + x[:, :n] for r, x in zip(rt, z)]
    y0 = [x[CHUNK:] + zz[:, n:] for x, zz in zip(lv, z)]
    bd = load(bd_ref)
    g = [jnp.where(same_head, _dot_tn(x[:, :n], b), 0.0).astype(BF16)
         for x, b in zip(wu, bd)]
    s_add = [fold(_dot_tn(jnp.concatenate([x[:, n:], vv], axis=0),
                          jnp.concatenate([b, k], axis=0)))
             for x, vv, b, k in zip(wu, v, bd, load(kd_ref))]

    inv_n = 1.0 / A_HEAD_DIM

    def head_mean(x):
        m_lo = jnp.sum(jnp.where(lo, x, 0.0), axis=-1, keepdims=True)
        m_hi = jnp.sum(jnp.where(lo, 0.0, x), axis=-1, keepdims=True)
        return jnp.where(lo, m_lo, m_hi) * inv_n

    s = [s_ref[p] for p in range(pairs)]
    for c in range(cps):
        rows = slice(c * CHUNK, (c + 1) * CHUNK)
        sd = [pair_diag(x) for x in s]
        ys = [_dot_nt(rw[c * pairs + p], sd[p]) + y0[c * pairs + p] for p in range(pairs)]
        for p in range(pairs):
            i = c * pairs + p
            cols = slice(p * n, (p + 1) * n)
            pc = pc_ref[0, c * CHUNK:c * CHUNK + 1, cols]
            s[p] = s[p] * pc + _dot(s[p], g[i]) + s_add[i]
        for p in range(pairs):
            cols = slice(p * n, (p + 1) * n)
            d = ys[p] - head_mean(ys[p])
            var = head_mean(d * d)
            yn = d * lax.rsqrt(var + A_GN_EPS) * lng_ref[:, cols] + lnb_ref[:, cols]
            o_ref[0, rows, cols] = ((yn + bonus_ref[0, rows, cols])
                                    * gate_ref[0, rows, cols]).astype(o_ref.dtype)
    for p in range(pairs):
        s_ref[p] = s[p]


def _rwkv_scan(at, rt, kt, bt, kd, bd, v, pc, gate, bonus, ln_g, ln_b, batch, seq_len, cps):
    ts = cps * CHUNK
    shp = (batch, seq_len, A_WIDTH)
    blk = pl.BlockSpec((1, ts, A_WIDTH), lambda b, s: (b, s, 0))
    vec = pl.BlockSpec((1, A_WIDTH), lambda b, s: (0, 0))
    args = [z.reshape(shp) for z in (at, rt, kt, bt, kd, bd, v, pc, gate, bonus)]
    y = pl.pallas_call(
        functools.partial(_scan_kernel, cps),
        grid=(batch, seq_len // ts),
        in_specs=[blk] * 10 + [vec, vec],
        out_specs=blk,
        out_shape=jax.ShapeDtypeStruct(shp, BF16),
        scratch_shapes=[pltpu.VMEM((A_HEADS // 2, A_HEAD_DIM, 2 * A_HEAD_DIM), F32)],
        compiler_params=_cparams(("parallel", "arbitrary")),
        name="rwkv_scan",
    )(*args, ln_g, ln_b)
    return y.reshape(batch * seq_len, A_WIDTH)


def _lru_kernel(gate_ref, x_ref, px_ref, cw_ref, cb_ref, wa_ref, ba_ref, wx_ref, bx_ref, lam_ref,
                o_ref, h_ref):
    first = pl.program_id(1) == 0

    @pl.when(first)
    def _():
        h_ref[...] = jnp.zeros_like(h_ref)

    x = x_ref[...]
    ts = x.shape[0]
    prev = jnp.where(first, 0.0, px_ref[...])
    row = _iota2(x.shape, 0)
    row8 = _iota2(prev.shape, 0)

    def delayed(d):
        rolled = pltpu.roll(x, d, 0)
        top = jnp.where(row8 < d, pltpu.roll(prev, d, 0), rolled[:SUBLANES])
        return jnp.concatenate([top, rolled[SUBLANES:]], axis=0)

    xc = cw_ref[CONV_WIDTH - 1:CONV_WIDTH, :] * x + cb_ref[...]
    for d in range(1, CONV_WIDTH):
        xc = xc + cw_ref[CONV_WIDTH - 1 - d:CONV_WIDTH - d, :] * delayed(d)

    r = jax.nn.sigmoid(_dot(xc, wa_ref[...]) + ba_ref[...])
    i = jax.nn.sigmoid(_dot(xc, wx_ref[...]) + bx_ref[...])
    log_a = -LRU_C * r * jax.nn.softplus(-lam_ref[...])
    a = jnp.exp(log_a)
    th = jnp.tanh(log_a)
    u = jnp.sqrt(-2.0 * th / (1.0 - th)) * (i * xc)

    d = 1
    while d < ts:
        keep = row >= d
        a_s = jnp.where(keep, pltpu.roll(a, d, 0), 1.0)
        u_s = jnp.where(keep, pltpu.roll(u, d, 0), 0.0)
        u = a * u_s + u
        a = a * a_s
        d *= 2
    h = a * h_ref[...] + u
    h_ref[...] = h[ts - 1:ts, :]
    o_ref[...] = (jax.nn.gelu(gate_ref[...]) * h).astype(o_ref.dtype)


def _rglru(proj, params, batch, seq_len, ts):
    t = proj.shape[0]
    nst = seq_len // ts
    sub = ts // SUBLANES
    gate_blk = (C_COLS + 3 * A_WIDTH) // B_WIDTH
    x_blk = gate_blk + 1
    vec = pl.BlockSpec((1, B_WIDTH), lambda b, s: (0, 0))
    sq = pl.BlockSpec((B_WIDTH, B_WIDTH), lambda b, s: (0, 0))
    return pl.pallas_call(
        _lru_kernel,
        grid=(batch, nst),
        in_specs=[pl.BlockSpec((ts, B_WIDTH), lambda b, s: (b * nst + s, gate_blk)),
                  pl.BlockSpec((ts, B_WIDTH), lambda b, s: (b * nst + s, x_blk)),
                  pl.BlockSpec((SUBLANES, B_WIDTH),
                               lambda b, s: (jnp.maximum((b * nst + s) * sub - 1, 0), x_blk)),
                  pl.BlockSpec((CONV_WIDTH, B_WIDTH), lambda b, s: (0, 0)),
                  vec, sq, vec, sq, vec, vec],
        out_specs=pl.BlockSpec((ts, B_WIDTH), lambda b, s: (b * nst + s, 0)),
        out_shape=jax.ShapeDtypeStruct((t, B_WIDTH), BF16),
        scratch_shapes=[pltpu.VMEM((1, B_WIDTH), F32)],
        compiler_params=_cparams(("parallel", "arbitrary")),
        name="rglru",
    )(proj, proj, proj, params["conv_w"], params["conv_b"], params["wa"], params["ba"],
      params["wx"], params["bx"], params["lam"])


def _log_gammas():
    return np.log1p(-np.exp2(-5.0 - np.arange(C_HEADS, dtype=np.float32))).astype(np.float32)


def _ret_kernel(cpr, q_ref, k_ref, v_ref, g_ref, pos_ref, invf_ref, gng_ref, o_ref, st_ref):
    @pl.when(pl.program_id(1) == 0)
    def _():
        st_ref[...] = jnp.zeros_like(st_ref)

    half = C_QK_DIM // 2
    lane = _iota2((CHUNK, LANES), 1)
    in_first_half = (lane & (C_QK_DIM - 1)) < half
    lo = lane < C_QK_DIM
    idx_r = _iota2((CHUNK, CHUNK), 0).astype(F32)
    idx_c = _iota2((CHUNK, CHUNK), 1).astype(F32)
    dist = jnp.abs(idx_r - idx_c)
    pos_in = _iota2((CHUNK, 1), 0).astype(F32)
    log_gamma = _log_gammas()

    def rope(t, cos, sin_signed):
        swapped = jnp.where(in_first_half, pltpu.roll(t, LANES - half, 1), pltpu.roll(t, half, 1))
        return t * cos + swapped * sin_signed

    intra_decay = [jnp.exp(float(lg) * dist) for lg in log_gamma]
    q_decay = [jnp.exp(float(lg) * (pos_in + 1.0)) for lg in log_gamma]
    k_decay = [jnp.exp(float(lg) * (CHUNK - 1.0 - pos_in)) for lg in log_gamma]
    chunk_decay = [float(np.exp(lg * np.float32(CHUNK))) for lg in log_gamma]
    items = [(c, h) for c in range(cpr) for h in range(C_HEADS)]

    q2, k2 = {}, {}
    for c in range(cpr):
        rows = slice(c * CHUNK, (c + 1) * CHUNK)
        ang = pos_ref[0, rows, :] * invf_ref[...]
        cos = jnp.cos(ang)
        sin = jnp.sin(ang)
        sin_signed = jnp.where(in_first_half, -sin, sin)
        for p in range(C_HEADS // 2):
            qk_cols = slice(p * LANES, (p + 1) * LANES)
            q2[c, p] = rope(q_ref[rows, qk_cols], cos, sin_signed)
            k2[c, p] = rope(k_ref[rows, qk_cols], cos, sin_signed) * (C_QK_DIM ** -0.5)

    def mine(x, h):
        return jnp.where(lo, x, 0.0) if h % 2 == 0 else jnp.where(lo, 0.0, x)

    qh = [mine(q2[c, h // 2], h) for c, h in items]
    kh = [mine(k2[c, h // 2], h) for c, h in items]
    vh = [v_ref[c * CHUNK:(c + 1) * CHUNK, h * C_V_DIM:(h + 1) * C_V_DIM].astype(BF16)
          for c, h in items]
    scores = [_dot_nt(q, k) * intra_decay[h] for q, k, (c, h) in zip(qh, kh, items)]
    kv = [_dot_tn(k * k_decay[h], v) for k, v, (c, h) in zip(kh, vh, items)]
    intra = [_dot(s, v) for s, v in zip(scores, vh)]

    state = [st_ref[h] for h in range(C_HEADS)]
    starts = []
    for i, (c, h) in enumerate(items):
        starts.append(state[h])
        state[h] = state[h] * chunk_decay[h] + kv[i]
    for h in range(C_HEADS):
        st_ref[h] = state[h]
    cross = [_dot(q * q_decay[h], s) for q, s, (c, h) in zip(qh, starts, items)]

    for i, (c, h) in enumerate(items):
        rows = slice(c * CHUNK, (c + 1) * CHUNK)
        v_cols = slice(h * C_V_DIM, (h + 1) * C_V_DIM)
        o = intra[i] + cross[i]
        mu = jnp.mean(o, axis=-1, keepdims=True)
        d = o - mu
        var = jnp.mean(d * d, axis=-1, keepdims=True)
        on = d * lax.rsqrt(var + C_GN_EPS) * gng_ref[:, v_cols]
        o_ref[rows, v_cols] = (jax.nn.silu(g_ref[rows, v_cols]) * on).astype(o_ref.dtype)


def _retention(proj, posb, invf, gn_g, batch, seq_len, cpr):
    t = proj.shape[0]
    ts = cpr * CHUNK
    nst = seq_len // ts
    return pl.pallas_call(
        functools.partial(_ret_kernel, cpr),
        grid=(batch, nst),
        in_specs=[pl.BlockSpec((ts, C_QK_WIDTH), lambda b, s: (b * nst + s, 0)),
                  pl.BlockSpec((ts, C_QK_WIDTH), lambda b, s: (b * nst + s, 1)),
                  pl.BlockSpec((ts, C_WIDTH), lambda b, s: (b * nst + s, 1)),
                  pl.BlockSpec((ts, C_WIDTH), lambda b, s: (b * nst + s, 2)),
                  pl.BlockSpec((1, ts, LANES), lambda b, s: (b, s, 0)),
                  pl.BlockSpec((1, LANES), lambda b, s: (0, 0)),
                  pl.BlockSpec((1, C_WIDTH), lambda b, s: (0, 0))],
        out_specs=pl.BlockSpec((ts, C_WIDTH), lambda b, s: (b * nst + s, 0)),
        out_shape=jax.ShapeDtypeStruct((t, C_WIDTH), BF16),
        scratch_shapes=[pltpu.VMEM((C_HEADS, LANES, C_V_DIM), F32)],
        compiler_params=_cparams(("parallel", "arbitrary")),
        name="retention",
    )(proj, proj, proj, proj, posb, invf, gn_g)


def _outproj_kernel(x_ref, ya_ref, yb_ref, yc_ref, w_ref, o_ref):
    b0, c0 = A_WIDTH, A_WIDTH + B_WIDTH
    acc = jnp.dot(ya_ref[...], w_ref[0:b0, :].astype(BF16), preferred_element_type=F32)
    acc += jnp.dot(yb_ref[...], w_ref[b0:c0, :].astype(BF16), preferred_element_type=F32)
    acc += jnp.dot(yc_ref[...], w_ref[c0:, :].astype(BF16), preferred_element_type=F32)
    o_ref[...] = x_ref[...] + acc


def _outproj(x, ya, yb, yc, w_stack, layer, tm):
    t, d = x.shape

    def tok(width):
        return pl.BlockSpec((tm, width), lambda i: (i, 0))

    return pl.pallas_call(
        _outproj_kernel,
        grid=(t // tm,),
        in_specs=[tok(d), tok(A_WIDTH), tok(B_WIDTH), tok(C_WIDTH),
                  pl.BlockSpec((None, d, d), lambda i: (layer, 0, 0),
                               pipeline_mode=pl.Buffered(1))],
        out_specs=tok(d),
        out_shape=jax.ShapeDtypeStruct((t, d), F32),
        compiler_params=_cparams(("parallel",)),
        name="outproj",
    )(x, ya, yb, yc, w_stack)


def _ffn_kernel(final, x_ref, g_ref, wg_ref, wu_ref, wd_ref, fg_ref, o_ref, h_ref):
    j = pl.program_id(1)

    @pl.when(j == 0)
    def _():
        x = x_ref[...]
        h_ref[...] = _rms(x, g_ref[...]).astype(BF16)
        o_ref[...] = x

    h = h_ref[...]
    gate = jnp.dot(h, wg_ref[...].astype(BF16), preferred_element_type=F32)
    up = jnp.dot(h, wu_ref[...].astype(BF16), preferred_element_type=F32)
    act = (jax.nn.silu(gate) * up).astype(BF16)
    o_ref[...] += jnp.dot(act, wd_ref[...].astype(BF16), preferred_element_type=F32)

    if final:
        @pl.when(j == pl.num_programs(1) - 1)
        def _():
            o_ref[...] = _rms(o_ref[...], fg_ref[...])


def _ffn(x, g, wg, wu, wd, fg, layer, final, tm, tf):
    t, d = x.shape
    f = wg.shape[2]
    return pl.pallas_call(
        functools.partial(_ffn_kernel, final),
        grid=(t // tm, f // tf),
        in_specs=[pl.BlockSpec((tm, d), lambda i, j: (i, 0)),
                  pl.BlockSpec((1, d), lambda i, j: (0, 0)),
                  pl.BlockSpec((None, d, tf), lambda i, j: (layer, 0, j)),
                  pl.BlockSpec((None, d, tf), lambda i, j: (layer, 0, j)),
                  pl.BlockSpec((None, tf, d), lambda i, j: (layer, j, 0)),
                  pl.BlockSpec((1, d), lambda i, j: (0, 0))],
        out_specs=pl.BlockSpec((tm, d), lambda i, j: (i, 0)),
        out_shape=jax.ShapeDtypeStruct((t, d), F32),
        scratch_shapes=[pltpu.VMEM((tm, d), BF16)],
        compiler_params=_cparams(("parallel", "arbitrary")),
        name="ffn_final" if final else "ffn",
    )(x, g, wg, wu, wd, fg)


def _rows_at(w, offset):
    return jnp.pad(w, ((offset, LORA_PACK - offset - w.shape[0]), (0, 0)))


def _layer_in_weight(w_in_l, w_vres_l):
    d = w_in_l.shape[0]
    o_gd = 3 * A_WIDTH + DECAY_LORA + ICLR_LORA
    vd = jnp.zeros((d, VRES_LORA), F32) if w_vres_l is None else w_vres_l
    cols = [w_in_l[:, A_COLS + B_COLS:],
            w_in_l[:, :3 * A_WIDTH],
            w_in_l[:, A_COLS:A_COLS + B_COLS],
            w_in_l[:, 3 * A_WIDTH:o_gd],
            vd,
            w_in_l[:, o_gd:A_COLS]]
    return jnp.concatenate([c.astype(BF16) for c in cols], axis=1)


def _block_diag(w):
    g, n, _ = w.shape
    eye = jnp.eye(g, dtype=w.dtype)
    return (eye[:, None, :, None] * w[:, :, None, :]).reshape(g * n, g * n)


def kernel(x, positions, norm1_g, w_in, tshift_mu, rwkv_w0, rwkv_w2, rwkv_a0, rwkv_a2, rwkv_g2, rwkv_k_k, rwkv_k_a, rwkv_r_k, rwkv_ln_g, rwkv_ln_b, w_in_vres, tshift_mu_vres, rwkv_v0, rwkv_v2, lru_conv_w, lru_conv_b, lru_wa, lru_ba, lru_wx, lru_bx, lru_lambda, ret_gn_g, w_out, norm2_g, ffn_w_gate, ffn_w_up, ffn_w_down, final_norm_g):
    batch, seq_len, d = x.shape
    depth = w_in.shape[0]
    t = batch * seq_len
    aw = A_WIDTH
    row = lambda v: v.reshape(1, -1).astype(F32)

    tm_proj = min(512, seq_len)
    tm_in = min(1024, seq_len)
    tm_tok = min(256, seq_len)
    cps = min(4, seq_len // CHUNK)
    cpr = min(4, seq_len // CHUNK)

    inv_freq = ROPE_THETA ** (-jnp.arange(0, C_QK_DIM, 2, dtype=F32) / C_QK_DIM)
    invf = jnp.tile(inv_freq, LANES // inv_freq.shape[0]).reshape(1, LANES)
    posb = jnp.broadcast_to(positions.astype(F32)[..., None], (batch, seq_len, LANES))

    xf = x.reshape(t, d)
    v_first = None
    for l in range(depth):
        has_vres = l > 0
        w_l = _layer_in_weight(w_in[l], w_in_vres[l - 1] if has_vres else None)
        proj = _inproj(xf, row(norm1_g[l]), w_l, tm_in, P_COLS // 6)

        mu = tshift_mu[l]
        o_gd = 3 * aw + DECAY_LORA + ICLR_LORA
        mu_vd = tshift_mu_vres[l - 1] if has_vres else jnp.zeros((VRES_LORA,), F32)
        params = {
            "mu_rkv": row(mu[:3 * aw]),
            "mu_lora": row(jnp.concatenate([mu[3 * aw:o_gd], mu_vd, mu[o_gd:A_COLS]])),
            "w0": row(rwkv_w0[l]), "a0": row(rwkv_a0[l]), "k_k": row(rwkv_k_k[l]),
            "k_a": row(rwkv_k_a[l]), "r_k": row(rwkv_r_k[l]),
            "w2": _rows_at(rwkv_w2[l], 0), "a2": _rows_at(rwkv_a2[l], DECAY_LORA),
            "g2": rwkv_g2[l],
        }
        if has_vres:
            params["v0"] = row(rwkv_v0[l - 1])
            params["v2"] = _rows_at(rwkv_v2[l - 1], DECAY_LORA + ICLR_LORA)
        at, rt, kt, bt, kd, bd, pc, v_cur, g_a, bonus = _rwkv_prep(
            proj, params, v_first if has_vres else None, seq_len, tm_tok)
        if l == 0:
            v_first = v_cur
        y_a = _rwkv_scan(at, rt, kt, bt, kd, bd, v_cur, pc, g_a, bonus,
                         row(rwkv_ln_g[l]), row(rwkv_ln_b[l]), batch, seq_len, cps)

        lru_params = {
            "conv_w": lru_conv_w[l], "conv_b": row(lru_conv_b[l]),
            "wa": _block_diag(lru_wa[l]).astype(BF16), "ba": row(lru_ba[l]),
            "wx": _block_diag(lru_wx[l]).astype(BF16), "bx": row(lru_bx[l]),
            "lam": row(lru_lambda[l]),
        }
        y_b = _rglru(proj, lru_params, batch, seq_len, tm_tok)
        y_c = _retention(proj, posb, invf, row(ret_gn_g[l]), batch, seq_len, cpr)

        xf = _outproj(xf, y_a, y_b, y_c, w_out, l, tm_proj)
        xf = _ffn(xf, row(norm2_g[l]), ffn_w_gate, ffn_w_up, ffn_w_down, row(final_norm_g),
                  l, l == depth - 1, tm_in, 256)
    return xf.reshape(batch, seq_len, d)
```

```python
import functools

import numpy as np
import jax
import jax.numpy as jnp
from jax import lax
from jax.experimental import pallas as pl
from jax.experimental.pallas import tpu as pltpu

F32 = jnp.float32
BF16 = jnp.bfloat16

NORM_EPS = 1e-6
CHUNK = 64

A_HEADS = 12
A_HEAD_DIM = 64
A_WIDTH = A_HEADS * A_HEAD_DIM
DECAY_LORA = 96
ICLR_LORA = 96
VRES_LORA = 64
GATE_LORA = 256
A_GN_EPS = 64e-5
A_COLS = 3 * A_WIDTH + DECAY_LORA + ICLR_LORA + GATE_LORA
B_WIDTH = 512
B_BLOCKS = 8
B_BLOCK_DIM = B_WIDTH // B_BLOCKS
CONV_WIDTH = 4
LRU_C = 8.0
B_COLS = 2 * B_WIDTH
C_HEADS = 6
C_QK_DIM = 64
C_V_DIM = 128
C_QK_WIDTH = C_HEADS * C_QK_DIM
C_WIDTH = C_HEADS * C_V_DIM
C_GN_EPS = 1e-5
ROPE_THETA = 10000.0
C_COLS = 2 * C_QK_WIDTH + 2 * C_WIDTH

LANES = 128
SUBLANES = 8
LORA_PACK = DECAY_LORA + ICLR_LORA + GATE_LORA + VRES_LORA
VMEM_LIMIT = 56 * 1024 * 1024

P_COLS = C_COLS + 3 * A_WIDTH + B_COLS + LORA_PACK


def _cparams(sem):
    return pltpu.CompilerParams(dimension_semantics=sem, vmem_limit_bytes=VMEM_LIMIT)


def _dot(a, b):
    return jnp.dot(a.astype(BF16), b.astype(BF16), preferred_element_type=F32)


def _dot_nt(a, b):
    return lax.dot_general(a.astype(BF16), b.astype(BF16), (((1,), (1,)), ((), ())),
                           preferred_element_type=F32)


def _dot_tn(a, b):
    return lax.dot_general(a.astype(BF16), b.astype(BF16), (((0,), (0,)), ((), ())),
                           preferred_element_type=F32)


def _iota2(shape, axis):
    return lax.broadcasted_iota(jnp.int32, shape, axis)


def _blk(idx, size):
    return idx >> (size.bit_length() - 1)


def _rms(x, g):
    ms = jnp.mean(x * x, axis=-1, keepdims=True)
    return x * lax.rsqrt(ms + NORM_EPS) * g


def _inproj_kernel(x_ref, g_ref, w_ref, o_ref, h_ref):
    @pl.when(pl.program_id(1) == 0)
    def _():
        h_ref[...] = _rms(x_ref[...], g_ref[...]).astype(BF16)

    o_ref[...] = jnp.dot(h_ref[...], w_ref[...], preferred_element_type=F32)


def _inproj(x, g, w, tm, tn):
    t, d = x.shape
    n = w.shape[1]
    return pl.pallas_call(
        _inproj_kernel,
        grid=(t // tm, n // tn),
        in_specs=[pl.BlockSpec((tm, d), lambda i, j: (i, 0)),
                  pl.BlockSpec((1, d), lambda i, j: (0, 0)),
                  pl.BlockSpec((d, tn), lambda i, j: (0, j))],
        out_specs=pl.BlockSpec((tm, tn), lambda i, j: (i, j)),
        out_shape=jax.ShapeDtypeStruct((t, n), F32),
        scratch_shapes=[pltpu.VMEM((tm, d), BF16)],
        compiler_params=_cparams(("parallel", "arbitrary")),
        name="inproj",
    )(x, g, w)


def _shift_rows(cur, prev_row):
    rolled = pltpu.roll(cur, 1, 0)
    row = _iota2(cur.shape, 0)
    return jnp.where(row == 0, prev_row, rolled)


def _split3(x):
    hi = x.astype(BF16)
    r1 = x - hi.astype(F32)
    mid = r1.astype(BF16)
    lo = (r1 - mid.astype(F32)).astype(BF16)
    return hi, mid, lo


def _sel_dot(sel, x):
    hi, mid, lo = _split3(x)
    d = functools.partial(jnp.dot, sel, preferred_element_type=F32)
    return (d(lo) + d(mid)) + d(hi)


def _dot_sel(x, sel):
    hi = x.astype(BF16)
    mid = (x - hi.astype(F32)).astype(BF16)
    d = functools.partial(jnp.dot, preferred_element_type=F32)
    return d(mid, sel) + d(hi, sel)


def _head_sum(x, e):
    parts = [_dot_sel(x[:, i * LANES:(i + 1) * LANES], e) for i in range(x.shape[1] // LANES)]
    return jnp.concatenate(parts, axis=-1)


def _head_ones():
    r = _blk(_iota2((LANES, LANES), 0), A_HEAD_DIM)
    c = _blk(_iota2((LANES, LANES), 1), A_HEAD_DIM)
    return (r == c).astype(BF16)


def _prep_kernel(has_vres, seq_tiles, *refs):
    if has_vres:
        (rkv_ref, lora_ref, prkv_ref, plora_ref, mu_rkv, mu_lora,
         w0, a0, k_k, k_a, r_k, w2, a2, g2, v0, v2, vfirst_ref,
         at_o, rt_o, kt_o, bt_o, kd_o, bd_o, vb_o, pc_o, g_o, bonus_o) = refs
    else:
        (rkv_ref, lora_ref, prkv_ref, plora_ref, mu_rkv, mu_lora,
         w0, a0, k_k, k_a, r_k, w2, a2, g2,
         at_o, rt_o, kt_o, bt_o, kd_o, bd_o, vb_o, pc_o, g_o, bonus_o, v_o) = refs
    first = (pl.program_id(0) % seq_tiles) == 0
    last_row = slice(SUBLANES - 1, SUBLANES)

    def mixed(cur_ref, prev_ref, mu_ref):
        cur = cur_ref[...]
        prev_row = jnp.where(first, 0.0, prev_ref[last_row, :])
        return cur + (_shift_rows(cur, prev_row) - cur) * mu_ref[...]

    rkv = mixed(rkv_ref, prkv_ref, mu_rkv)
    lora = mixed(lora_ref, plora_ref, mu_lora)
    r = rkv[:, 0:A_WIDTH]
    k = rkv[:, A_WIDTH:2 * A_WIDTH]
    v = rkv[:, 2 * A_WIDTH:3 * A_WIDTH]
    zw = -(w0[...] + _dot(jnp.tanh(lora), w2[...]))
    softplus = jnp.maximum(zw, 0.0) + jnp.log1p(jnp.exp(-jnp.abs(zw)))
    lw = -jnp.exp(-softplus - 0.5)
    a = jax.nn.sigmoid(a0[...] + _dot(lora, a2[...]))
    g = _dot(jax.nn.sigmoid(lora), g2[...])
    if has_vres:
        v = v + (vfirst_ref[...] - v) * jax.nn.sigmoid(v0[...] + _dot(lora, v2[...]))

    e = _head_ones()
    kk = k * k_k[...]
    kk = kk * lax.rsqrt(jnp.maximum(_head_sum(kk * kk, e), 1e-24))
    k = k * (1.0 + (a - 1.0) * k_a[...])
    b = kk * a
    bonus = _head_sum(r * k * r_k[...], e) * v

    tm = lw.shape[0]
    row = _iota2((tm, tm), 0)
    col = _iota2((tm, tm), 1)
    tri = (_blk(row, CHUNK) == _blk(col, CHUNK)) & (row >= col)
    cum = _sel_dot(tri.astype(BF16), lw)
    cum3 = cum.reshape(tm // CHUNK, CHUNK, A_WIDTH)
    tot = jnp.broadcast_to(cum3[:, CHUNK - 1:CHUNK, :], cum3.shape).reshape(tm, A_WIDTH)
    e_in = jnp.exp(cum)
    e_out = jnp.exp(-cum)
    e_end = jnp.exp(tot - cum)
    at_o[...] = (-kk * jnp.exp(cum - lw)).astype(BF16)
    rt_o[...] = (r * e_in).astype(BF16)
    kt_o[...] = (k * e_out).astype(BF16)
    bt_o[...] = (b * e_out).astype(BF16)
    kd_o[...] = (k * e_end).astype(BF16)
    bd_o[...] = (b * e_end).astype(BF16)
    vb_o[...] = v.astype(BF16)
    for c in range(tm // CHUNK):
        pc_o[c] = jnp.exp(cum3[c, CHUNK - 1:CHUNK, :])
    g_o[...] = g
    bonus_o[...] = bonus
    if not has_vres:
        v_o[...] = v


def _rwkv_prep(proj, params, v_first, seq_len, tm):
    t = proj.shape[0]
    has_vres = v_first is not None
    seq_tiles = seq_len // tm
    sub = tm // SUBLANES
    rkv_w, lora_w = 3 * A_WIDTH, LORA_PACK
    rkv_blk = (C_COLS) // rkv_w
    lora_blk = (C_COLS + rkv_w + B_COLS) // lora_w

    def cur(width, blk):
        return pl.BlockSpec((tm, width), lambda i: (i, blk))

    def prev(width, blk):
        return pl.BlockSpec((SUBLANES, width), lambda i: (jnp.maximum(i * sub - 1, 0), blk))

    def vec(width):
        return pl.BlockSpec((1, width), lambda i: (0, 0))

    def mat(rows):
        return pl.BlockSpec((rows, A_WIDTH), lambda i: (0, 0))

    tok = pl.BlockSpec((tm, A_WIDTH), lambda i: (i, 0))
    ins = [proj, proj, proj, proj,
           params["mu_rkv"], params["mu_lora"],
           params["w0"], params["a0"], params["k_k"], params["k_a"], params["r_k"],
           params["w2"], params["a2"], params["g2"]]
    specs = [cur(rkv_w, rkv_blk), cur(lora_w, lora_blk),
             prev(rkv_w, rkv_blk), prev(lora_w, lora_blk),
             vec(rkv_w), vec(lora_w),
             vec(A_WIDTH), vec(A_WIDTH), vec(A_WIDTH), vec(A_WIDTH), vec(A_WIDTH),
             mat(LORA_PACK), mat(LORA_PACK), mat(LORA_PACK)]
    if has_vres:
        ins += [params["v0"], params["v2"], v_first]
        specs += [vec(A_WIDTH), mat(LORA_PACK), tok]
    wide = lambda dtype: jax.ShapeDtypeStruct((t, A_WIDTH), dtype)
    pc_spec = pl.BlockSpec((tm // CHUNK, 1, A_WIDTH), lambda i: (i, 0, 0))
    out_specs = [tok] * 7 + [pc_spec, tok, tok]
    out_shape = [wide(BF16)] * 7 + [jax.ShapeDtypeStruct((t // CHUNK, 1, A_WIDTH), F32),
                                    wide(F32), wide(F32)]
    if not has_vres:
        out_specs.append(tok)
        out_shape.append(wide(F32))
    return pl.pallas_call(
        functools.partial(_prep_kernel, has_vres, seq_tiles),
        grid=(t // tm,),
        in_specs=specs,
        out_specs=out_specs,
        out_shape=out_shape,
        compiler_params=_cparams(("parallel",)),
        name="rwkv_prep_vres" if has_vres else "rwkv_prep",
    )(*ins)


def _unit_lower_inverse(a_strict, row, col, eye, pair_diag):
    base = 8
    diag = _blk(row, base) == _blk(col, base)
    n1 = [jnp.where(diag, a, 0.0) for a in a_strict]
    n1d = [pair_diag(x) for x in n1]
    n2 = [_dot(x, d) for x, d in zip(n1, n1d)]
    n2d = [pair_diag(x) for x in n2]
    t = [eye + x for x in n1]
    t = [x + _dot(x, d) for x, d in zip(t, n2d)]
    n4d = [pair_diag(_dot(x, d)) for x, d in zip(n2, n2d)]
    t = [x + _dot(x, d) for x, d in zip(t, n4d)]
    m = base
    while m < CHUNK:
        sel = (_blk(row, 2 * m) == _blk(col, 2 * m)) & (_blk(row, m) != _blk(col, m))
        ed = [pair_diag(jnp.where(sel, a, 0.0)) for a in a_strict]
        te = [_dot(x, d) for x, d in zip(t, ed)]
        t = [x + _dot(y, pair_diag(x)) for x, y in zip(t, te)]
        m *= 2
    return t


def _scan_kernel(cps, at_ref, rt_ref, kt_ref, bt_ref, kd_ref, bd_ref, v_ref, pc_ref,
                 gate_ref, bonus_ref, lng_ref, lnb_ref, o_ref, s_ref):
    @pl.when(pl.program_id(1) == 0)
    def _():
        s_ref[...] = jnp.zeros_like(s_ref)

    n = 2 * A_HEAD_DIM
    pairs = A_HEADS // 2
    row = _iota2((CHUNK, n), 0)
    lane = _iota2((CHUNK, n), 1)
    col = lane & (A_HEAD_DIM - 1)
    lo = lane < A_HEAD_DIM
    strict = row > col
    incl = row >= col
    eye = (row == col).astype(F32)
    same_head = _blk(_iota2((n, n), 0), A_HEAD_DIM) == _blk(_iota2((n, n), 1), A_HEAD_DIM)
    items = [(c, p) for c in range(cps) for p in range(pairs)]

    def load(ref):
        return [ref[0, c * CHUNK:(c + 1) * CHUNK, p * n:(p + 1) * n] for c, p in items]

    def pair_diag(x):
        x = x.astype(BF16)
        zero = jnp.zeros_like(x)
        return jnp.concatenate([jnp.where(lo, x, zero), jnp.where(lo, zero, x)], axis=0)

    def fold(m):
        return jnp.where(lo, m[:CHUNK], m[CHUNK:])

    at = load(at_ref)
    rt = load(rt_ref)
    lhs = [jnp.concatenate([a, r], axis=0) for a, r in zip(at, rt)]
    lk = [_dot_nt(x, pair_diag(k)) for x, k in zip(lhs, load(kt_ref))]
    lb = [_dot_nt(x, pair_diag(b)) for x, b in zip(lhs, load(bt_ref))]
    a_k = [jnp.concatenate([jnp.where(strict, x[:CHUNK], 0.0), jnp.where(incl, x[CHUNK:], 0.0)],
                           axis=0) for x in lk]
    a_ab = [jnp.where(strict, x[:CHUNK], 0.0) for x in lb]
    a_rb = [jnp.where(incl, x[CHUNK:], 0.0) for x in lb]
    t_inv = _unit_lower_inverse(a_ab, row, col, eye, pair_diag)
    v = load(v_ref)
    lv = [_dot(a, pair_diag(x)) for a, x in zip(a_k, v)]
    wu = [_dot(t, jnp.concatenate([pair_diag(a), pair_diag(x[:CHUNK])], axis=1))
          for t, a, x in zip(t_inv, at, lv)]
    z = [_dot(a, jnp.concatenate([pair_diag(x[:, :n]), pair_diag(x[:, n:])], axis=1))
         for a, x in zip(a_rb, wu)]
    rw = [r.astype(F32) + x[:, :n] for r, x in zip(rt, z)]
    y0 = [x[CHUNK:] + zz[:, n:] for x, zz in zip(lv, z)]
    bd = load(bd_ref)
    g = [jnp.where(same_head, _dot_tn(x[:, :n], b), 0.0).astype(BF16)
         for x, b in zip(wu, bd)]
    s_add = [fold(_dot_tn(jnp.concatenate([x[:, n:].astype(BF16), vv], axis=0),
                          jnp.concatenate([b, k], axis=0)))
             for x, vv, b, k in zip(wu, v, bd, load(kd_ref))]

    inv_n = 1.0 / A_HEAD_DIM

    def head_mean(x):
        m_lo = jnp.sum(jnp.where(lo, x, 0.0), axis=-1, keepdims=True)
        m_hi = jnp.sum(jnp.where(lo, 0.0, x), axis=-1, keepdims=True)
        return jnp.where(lo, m_lo, m_hi) * inv_n

    s = [s_ref[p] for p in range(pairs)]
    for c in range(cps):
        rows = slice(c * CHUNK, (c + 1) * CHUNK)
        sd = [pair_diag(x) for x in s]
        ys = [_dot_nt(rw[c * pairs + p], sd[p]) + y0[c * pairs + p] for p in range(pairs)]
        for p in range(pairs):
            i = c * pairs + p
            cols = slice(p * n, (p + 1) * n)
            pc = pc_ref[0, c, :, cols]
            s[p] = s[p] * pc + _dot(s[p], g[i]) + s_add[i]
        for p in range(pairs):
            cols = slice(p * n, (p + 1) * n)
            d = ys[p] - head_mean(ys[p])
            var = head_mean(d * d)
            yn = d * lax.rsqrt(var + A_GN_EPS) * lng_ref[:, cols] + lnb_ref[:, cols]
            o_ref[0, rows, cols] = ((yn + bonus_ref[0, rows, cols])
                                    * gate_ref[0, rows, cols]).astype(o_ref.dtype)
    for p in range(pairs):
        s_ref[p] = s[p]


def _rwkv_scan(at, rt, kt, bt, kd, bd, v, pc, gate, bonus, ln_g, ln_b, batch, seq_len, cps):
    ts = cps * CHUNK
    shp = (batch, seq_len, A_WIDTH)
    blk = pl.BlockSpec((1, ts, A_WIDTH), lambda b, s: (b, s, 0))
    pc_blk = pl.BlockSpec((1, cps, 1, A_WIDTH), lambda b, s: (b, s, 0, 0))
    vec = pl.BlockSpec((1, A_WIDTH), lambda b, s: (0, 0))
    args = [z.reshape(shp) for z in (at, rt, kt, bt, kd, bd, v)]
    args += [pc.reshape(batch, seq_len // CHUNK, 1, A_WIDTH), gate.reshape(shp), bonus.reshape(shp)]
    y = pl.pallas_call(
        functools.partial(_scan_kernel, cps),
        grid=(batch, seq_len // ts),
        in_specs=[blk] * 7 + [pc_blk, blk, blk, vec, vec],
        out_specs=blk,
        out_shape=jax.ShapeDtypeStruct(shp, BF16),
        scratch_shapes=[pltpu.VMEM((A_HEADS // 2, A_HEAD_DIM, 2 * A_HEAD_DIM), F32)],
        compiler_params=_cparams(("parallel", "arbitrary")),
        name="rwkv_scan",
    )(*args, ln_g, ln_b)
    return y.reshape(batch * seq_len, A_WIDTH)


def _lru_kernel(gate_ref, x_ref, px_ref, cw_ref, cb_ref, wa_ref, ba_ref, wx_ref, bx_ref, lam_ref,
                o_ref, h_ref):
    first = pl.program_id(1) == 0

    @pl.when(first)
    def _():
        h_ref[...] = jnp.zeros_like(h_ref)

    x = x_ref[...]
    ts = x.shape[0]
    prev = jnp.where(first, 0.0, px_ref[...])
    row = _iota2(x.shape, 0)
    row8 = _iota2(prev.shape, 0)

    def delayed(d):
        rolled = pltpu.roll(x, d, 0)
        top = jnp.where(row8 < d, pltpu.roll(prev, d, 0), rolled[:SUBLANES])
        return jnp.concatenate([top, rolled[SUBLANES:]], axis=0)

    xc = cw_ref[CONV_WIDTH - 1:CONV_WIDTH, :] * x + cb_ref[...]
    for d in range(1, CONV_WIDTH):
        xc = xc + cw_ref[CONV_WIDTH - 1 - d:CONV_WIDTH - d, :] * delayed(d)

    r = jax.nn.sigmoid(_dot(xc, wa_ref[...]) + ba_ref[...])
    i = jax.nn.sigmoid(_dot(xc, wx_ref[...]) + bx_ref[...])
    log_a = -LRU_C * r * jax.nn.softplus(-lam_ref[...])
    a = jnp.exp(log_a)
    th = jnp.tanh(log_a)
    u = jnp.sqrt(-2.0 * th / (1.0 - th)) * (i * xc)

    d = 1
    while d < ts:
        keep = row >= d
        a_s = jnp.where(keep, pltpu.roll(a, d, 0), 1.0)
        u_s = jnp.where(keep, pltpu.roll(u, d, 0), 0.0)
        u = a * u_s + u
        a = a * a_s
        d *= 2
    h = a * h_ref[...] + u
    h_ref[...] = h[ts - 1:ts, :]
    o_ref[...] = (jax.nn.gelu(gate_ref[...]) * h).astype(o_ref.dtype)


def _rglru(proj, params, batch, seq_len, ts):
    t = proj.shape[0]
    nst = seq_len // ts
    sub = ts // SUBLANES
    gate_blk = (C_COLS + 3 * A_WIDTH) // B_WIDTH
    x_blk = gate_blk + 1
    vec = pl.BlockSpec((1, B_WIDTH), lambda b, s: (0, 0))
    sq = pl.BlockSpec((B_WIDTH, B_WIDTH), lambda b, s: (0, 0))
    return pl.pallas_call(
        _lru_kernel,
        grid=(batch, nst),
        in_specs=[pl.BlockSpec((ts, B_WIDTH), lambda b, s: (b * nst + s, gate_blk)),
                  pl.BlockSpec((ts, B_WIDTH), lambda b, s: (b * nst + s, x_blk)),
                  pl.BlockSpec((SUBLANES, B_WIDTH),
                               lambda b, s: (jnp.maximum((b * nst + s) * sub - 1, 0), x_blk)),
                  pl.BlockSpec((CONV_WIDTH, B_WIDTH), lambda b, s: (0, 0)),
                  vec, sq, vec, sq, vec, vec],
        out_specs=pl.BlockSpec((ts, B_WIDTH), lambda b, s: (b * nst + s, 0)),
        out_shape=jax.ShapeDtypeStruct((t, B_WIDTH), BF16),
        scratch_shapes=[pltpu.VMEM((1, B_WIDTH), F32)],
        compiler_params=_cparams(("parallel", "arbitrary")),
        name="rglru",
    )(proj, proj, proj, params["conv_w"], params["conv_b"], params["wa"], params["ba"],
      params["wx"], params["bx"], params["lam"])


def _log_gammas():
    return np.log1p(-np.exp2(-5.0 - np.arange(C_HEADS, dtype=np.float32))).astype(np.float32)


def _ret_kernel(cpr, q_ref, k_ref, v_ref, g_ref, pos_ref, invf_ref, gng_ref, o_ref, st_ref):
    @pl.when(pl.program_id(1) == 0)
    def _():
        st_ref[...] = jnp.zeros_like(st_ref)

    half = C_QK_DIM // 2
    lane = _iota2((CHUNK, LANES), 1)
    in_first_half = (lane & (C_QK_DIM - 1)) < half
    lo = lane < C_QK_DIM
    idx_r = _iota2((CHUNK, CHUNK), 0).astype(F32)
    idx_c = _iota2((CHUNK, CHUNK), 1).astype(F32)
    dist = jnp.abs(idx_r - idx_c)
    pos_in = _iota2((CHUNK, 1), 0).astype(F32)
    log_gamma = _log_gammas()

    def rope(t, cos, sin_signed):
        swapped = jnp.where(in_first_half, pltpu.roll(t, LANES - half, 1), pltpu.roll(t, half, 1))
        return t * cos + swapped * sin_signed

    intra_decay = [jnp.exp(float(lg) * dist) for lg in log_gamma]
    q_decay = [jnp.exp(float(lg) * (pos_in + 1.0)) for lg in log_gamma]
    k_decay = [jnp.exp(float(lg) * (CHUNK - 1.0 - pos_in)) for lg in log_gamma]
    chunk_decay = [float(np.exp(lg * np.float32(CHUNK))) for lg in log_gamma]
    items = [(c, h) for c in range(cpr) for h in range(C_HEADS)]

    q2, k2 = {}, {}
    for c in range(cpr):
        rows = slice(c * CHUNK, (c + 1) * CHUNK)
        ang = pos_ref[0, rows, :] * invf_ref[...]
        cos = jnp.cos(ang)
        sin = jnp.sin(ang)
        sin_signed = jnp.where(in_first_half, -sin, sin)
        for p in range(C_HEADS // 2):
            qk_cols = slice(p * LANES, (p + 1) * LANES)
            q2[c, p] = rope(q_ref[rows, qk_cols], cos, sin_signed)
            k2[c, p] = rope(k_ref[rows, qk_cols], cos, sin_signed) * (C_QK_DIM ** -0.5)

    def mine(x, h):
        return jnp.where(lo, x, 0.0) if h % 2 == 0 else jnp.where(lo, 0.0, x)

    qh = [mine(q2[c, h // 2], h) for c, h in items]
    kh = [mine(k2[c, h // 2], h) for c, h in items]
    vh = [v_ref[c * CHUNK:(c + 1) * CHUNK, h * C_V_DIM:(h + 1) * C_V_DIM].astype(BF16)
          for c, h in items]
    scores = [_dot_nt(q, k) * intra_decay[h] for q, k, (c, h) in zip(qh, kh, items)]
    kv = [_dot_tn(k * k_decay[h], v) for k, v, (c, h) in zip(kh, vh, items)]
    intra = [_dot(s, v) for s, v in zip(scores, vh)]

    state = [st_ref[h] for h in range(C_HEADS)]
    starts = []
    for i, (c, h) in enumerate(items):
        starts.append(state[h])
        state[h] = state[h] * chunk_decay[h] + kv[i]
    for h in range(C_HEADS):
        st_ref[h] = state[h]
    cross = [_dot(q * q_decay[h], s) for q, s, (c, h) in zip(qh, starts, items)]

    for i, (c, h) in enumerate(items):
        rows = slice(c * CHUNK, (c + 1) * CHUNK)
        v_cols = slice(h * C_V_DIM, (h + 1) * C_V_DIM)
        o = intra[i] + cross[i]
        mu = jnp.mean(o, axis=-1, keepdims=True)
        d = o - mu
        var = jnp.mean(d * d, axis=-1, keepdims=True)
        on = d * lax.rsqrt(var + C_GN_EPS) * gng_ref[:, v_cols]
        o_ref[rows, v_cols] = (jax.nn.silu(g_ref[rows, v_cols]) * on).astype(o_ref.dtype)


def _retention(proj, posb, invf, gn_g, batch, seq_len, cpr):
    t = proj.shape[0]
    ts = cpr * CHUNK
    nst = seq_len // ts
    return pl.pallas_call(
        functools.partial(_ret_kernel, cpr),
        grid=(batch, nst),
        in_specs=[pl.BlockSpec((ts, C_QK_WIDTH), lambda b, s: (b * nst + s, 0)),
                  pl.BlockSpec((ts, C_QK_WIDTH), lambda b, s: (b * nst + s, 1)),
                  pl.BlockSpec((ts, C_WIDTH), lambda b, s: (b * nst + s, 1)),
                  pl.BlockSpec((ts, C_WIDTH), lambda b, s: (b * nst + s, 2)),
                  pl.BlockSpec((1, ts, LANES), lambda b, s: (b, s, 0)),
                  pl.BlockSpec((1, LANES), lambda b, s: (0, 0)),
                  pl.BlockSpec((1, C_WIDTH), lambda b, s: (0, 0))],
        out_specs=pl.BlockSpec((ts, C_WIDTH), lambda b, s: (b * nst + s, 0)),
        out_shape=jax.ShapeDtypeStruct((t, C_WIDTH), BF16),
        scratch_shapes=[pltpu.VMEM((C_HEADS, LANES, C_V_DIM), F32)],
        compiler_params=_cparams(("parallel", "arbitrary")),
        name="retention",
    )(proj, proj, proj, proj, posb, invf, gn_g)


def _outproj_kernel(x_ref, ya_ref, yb_ref, yc_ref, w_ref, o_ref):
    b0, c0 = A_WIDTH, A_WIDTH + B_WIDTH
    acc = jnp.dot(ya_ref[...], w_ref[0:b0, :].astype(BF16), preferred_element_type=F32)
    acc += jnp.dot(yb_ref[...], w_ref[b0:c0, :].astype(BF16), preferred_element_type=F32)
    acc += jnp.dot(yc_ref[...], w_ref[c0:, :].astype(BF16), preferred_element_type=F32)
    o_ref[...] = x_ref[...] + acc


def _outproj(x, ya, yb, yc, w_stack, layer, tm):
    t, d = x.shape

    def tok(width):
        return pl.BlockSpec((tm, width), lambda i: (i, 0))

    return pl.pallas_call(
        _outproj_kernel,
        grid=(t // tm,),
        in_specs=[tok(d), tok(A_WIDTH), tok(B_WIDTH), tok(C_WIDTH),
                  pl.BlockSpec((None, d, d), lambda i: (layer, 0, 0),
                               pipeline_mode=pl.Buffered(1))],
        out_specs=tok(d),
        out_shape=jax.ShapeDtypeStruct((t, d), F32),
        compiler_params=_cparams(("parallel",)),
        name="outproj",
    )(x, ya, yb, yc, w_stack)


def _ffn_kernel(final, x_ref, g_ref, wg_ref, wu_ref, wd_ref, fg_ref, o_ref, h_ref):
    j = pl.program_id(1)

    @pl.when(j == 0)
    def _():
        x = x_ref[...]
        h_ref[...] = _rms(x, g_ref[...]).astype(BF16)
        o_ref[...] = x

    h = h_ref[...]
    gate = jnp.dot(h, wg_ref[...].astype(BF16), preferred_element_type=F32)
    up = jnp.dot(h, wu_ref[...].astype(BF16), preferred_element_type=F32)
    act = (jax.nn.silu(gate) * up).astype(BF16)
    o_ref[...] += jnp.dot(act, wd_ref[...].astype(BF16), preferred_element_type=F32)

    if final:
        @pl.when(j == pl.num_programs(1) - 1)
        def _():
            o_ref[...] = _rms(o_ref[...], fg_ref[...])


def _ffn(x, g, wg, wu, wd, fg, layer, final, tm, tf):
    t, d = x.shape
    f = wg.shape[2]
    return pl.pallas_call(
        functools.partial(_ffn_kernel, final),
        grid=(t // tm, f // tf),
        in_specs=[pl.BlockSpec((tm, d), lambda i, j: (i, 0)),
                  pl.BlockSpec((1, d), lambda i, j: (0, 0)),
                  pl.BlockSpec((None, d, tf), lambda i, j: (layer, 0, j)),
                  pl.BlockSpec((None, d, tf), lambda i, j: (layer, 0, j)),
                  pl.BlockSpec((None, tf, d), lambda i, j: (layer, j, 0)),
                  pl.BlockSpec((1, d), lambda i, j: (0, 0))],
        out_specs=pl.BlockSpec((tm, d), lambda i, j: (i, 0)),
        out_shape=jax.ShapeDtypeStruct((t, d), F32),
        scratch_shapes=[pltpu.VMEM((tm, d), BF16)],
        compiler_params=_cparams(("parallel", "arbitrary")),
        name="ffn_final" if final else "ffn",
    )(x, g, wg, wu, wd, fg)


def _rows_at(w, offset):
    return jnp.pad(w, ((offset, LORA_PACK - offset - w.shape[0]), (0, 0)))


def _relayout_kernel(has_vres, *refs):
    if has_vres:
        w_ref, vd_ref, o_ref = refs
    else:
        w_ref, o_ref = refs
    w = w_ref[...]
    rows = w.shape[0]
    vd = vd_ref[...] if has_vres else jnp.zeros((rows, VRES_LORA), F32)
    cols = [w[:, A_COLS + B_COLS:],
            w[:, :3 * A_WIDTH],
            w[:, A_COLS:A_COLS + B_COLS],
            w[:, 3 * A_WIDTH:A_COLS],
            vd]
    o_ref[...] = jnp.concatenate(cols, axis=1).astype(BF16)


def _layer_in_weight(w_in, w_vres, layer, tr):
    _, d, n = w_in.shape
    has_vres = layer > 0
    ins = [w_in]
    specs = [pl.BlockSpec((None, tr, n), lambda i: (layer, i, 0))]
    if has_vres:
        ins.append(w_vres)
        specs.append(pl.BlockSpec((None, tr, VRES_LORA), lambda i: (layer - 1, i, 0)))
    return pl.pallas_call(
        functools.partial(_relayout_kernel, has_vres),
        grid=(d // tr,),
        in_specs=specs,
        out_specs=pl.BlockSpec((tr, P_COLS), lambda i: (i, 0)),
        out_shape=jax.ShapeDtypeStruct((d, P_COLS), BF16),
        compiler_params=_cparams(("parallel",)),
        name="w_in_relayout",
    )(*ins)


def _block_diag(w):
    g, n, _ = w.shape
    eye = jnp.eye(g, dtype=w.dtype)
    return (eye[:, None, :, None] * w[:, :, None, :]).reshape(g * n, g * n)


def kernel(x, positions, norm1_g, w_in, tshift_mu, rwkv_w0, rwkv_w2, rwkv_a0, rwkv_a2, rwkv_g2, rwkv_k_k, rwkv_k_a, rwkv_r_k, rwkv_ln_g, rwkv_ln_b, w_in_vres, tshift_mu_vres, rwkv_v0, rwkv_v2, lru_conv_w, lru_conv_b, lru_wa, lru_ba, lru_wx, lru_bx, lru_lambda, ret_gn_g, w_out, norm2_g, ffn_w_gate, ffn_w_up, ffn_w_down, final_norm_g):
    batch, seq_len, d = x.shape
    depth = w_in.shape[0]
    t = batch * seq_len
    aw = A_WIDTH
    row = lambda v: v.reshape(1, -1).astype(F32)

    tm_proj = min(512, seq_len)
    tm_in = min(1024, seq_len)
    tm_tok = min(256, seq_len)
    cps = min(4, seq_len // CHUNK)
    cpr = min(4, seq_len // CHUNK)

    inv_freq = ROPE_THETA ** (-jnp.arange(0, C_QK_DIM, 2, dtype=F32) / C_QK_DIM)
    invf = jnp.tile(inv_freq, LANES // inv_freq.shape[0]).reshape(1, LANES)
    posb = jnp.broadcast_to(positions.astype(F32)[..., None], (batch, seq_len, LANES))

    xf = x.reshape(t, d)
    v_first = None
    for l in range(depth):
        has_vres = l > 0
        w_l = _layer_in_weight(w_in, w_in_vres, l, tm_tok)
        proj = _inproj(xf, row(norm1_g[l]), w_l, tm_in, P_COLS // 6)

        mu = tshift_mu[l]
        o_ad, o_gd = 3 * aw + DECAY_LORA, 3 * aw + DECAY_LORA + ICLR_LORA
        mu_vd = tshift_mu_vres[l - 1] if has_vres else jnp.zeros((VRES_LORA,), F32)
        params = {
            "mu_rkv": row(mu[:3 * aw]),
            "mu_lora": row(jnp.concatenate([mu[3 * aw:A_COLS], mu_vd])),
            "w0": row(rwkv_w0[l]), "a0": row(rwkv_a0[l]), "k_k": row(rwkv_k_k[l]),
            "k_a": row(rwkv_k_a[l]), "r_k": row(rwkv_r_k[l]),
            "w2": _rows_at(rwkv_w2[l], 0), "a2": _rows_at(rwkv_a2[l], o_ad - 3 * aw),
            "g2": _rows_at(rwkv_g2[l], o_gd - 3 * aw),
        }
        if has_vres:
            params["v0"] = row(rwkv_v0[l - 1])
            params["v2"] = _rows_at(rwkv_v2[l - 1], A_COLS - 3 * aw)
        at, rt, kt, bt, kd, bd, v_b, pc, g_a, bonus, *v_f32 = _rwkv_prep(
            proj, params, v_first if has_vres else None, seq_len, tm_tok)
        if l == 0:
            v_first = v_f32[0]
        y_a = _rwkv_scan(at, rt, kt, bt, kd, bd, v_b, pc, g_a, bonus,
                         row(rwkv_ln_g[l]), row(rwkv_ln_b[l]), batch, seq_len, cps)

        lru_params = {
            "conv_w": lru_conv_w[l], "conv_b": row(lru_conv_b[l]),
            "wa": _block_diag(lru_wa[l]).astype(BF16), "ba": row(lru_ba[l]),
            "wx": _block_diag(lru_wx[l]).astype(BF16), "bx": row(lru_bx[l]),
            "lam": row(lru_lambda[l]),
        }
        y_b = _rglru(proj, lru_params, batch, seq_len, tm_tok)
        y_c = _retention(proj, posb, invf, row(ret_gn_g[l]), batch, seq_len, cpr)

        xf = _outproj(xf, y_a, y_b, y_c, w_out, l, tm_proj)
        xf = _ffn(xf, row(norm2_g[l]), ffn_w_gate, ffn_w_up, ffn_w_down, row(final_norm_g),
                  l, l == depth - 1, tm_in, 256)
    return xf.reshape(batch, seq_len, d)
```

```python
import functools

import numpy as np
import jax
import jax.numpy as jnp
from jax import lax
from jax.experimental import pallas as pl
from jax.experimental.pallas import tpu as pltpu

F32 = jnp.float32
BF16 = jnp.bfloat16

NORM_EPS = 1e-6
CHUNK = 64

A_HEADS = 12
A_HEAD_DIM = 64
A_WIDTH = A_HEADS * A_HEAD_DIM
DECAY_LORA = 96
ICLR_LORA = 96
VRES_LORA = 64
GATE_LORA = 256
A_GN_EPS = 64e-5
A_COLS = 3 * A_WIDTH + DECAY_LORA + ICLR_LORA + GATE_LORA
B_WIDTH = 512
B_BLOCKS = 8
B_BLOCK_DIM = B_WIDTH // B_BLOCKS
CONV_WIDTH = 4
LRU_C = 8.0
B_COLS = 2 * B_WIDTH
C_HEADS = 6
C_QK_DIM = 64
C_V_DIM = 128
C_QK_WIDTH = C_HEADS * C_QK_DIM
C_WIDTH = C_HEADS * C_V_DIM
C_GN_EPS = 1e-5
ROPE_THETA = 10000.0
C_COLS = 2 * C_QK_WIDTH + 2 * C_WIDTH

LANES = 128
SUBLANES = 8
LORA_PACK = DECAY_LORA + ICLR_LORA + GATE_LORA + VRES_LORA
VMEM_LIMIT = 56 * 1024 * 1024

P_COLS = C_COLS + 3 * A_WIDTH + B_COLS + LORA_PACK


def _cparams(sem):
    return pltpu.CompilerParams(dimension_semantics=sem, vmem_limit_bytes=VMEM_LIMIT)


def _dot(a, b):
    return jnp.dot(a.astype(BF16), b.astype(BF16), preferred_element_type=F32)


def _dot_nt(a, b):
    return lax.dot_general(a.astype(BF16), b.astype(BF16), (((1,), (1,)), ((), ())),
                           preferred_element_type=F32)


def _dot_tn(a, b):
    return lax.dot_general(a.astype(BF16), b.astype(BF16), (((0,), (0,)), ((), ())),
                           preferred_element_type=F32)


def _iota2(shape, axis):
    return lax.broadcasted_iota(jnp.int32, shape, axis)


def _blk(idx, size):
    return idx >> (size.bit_length() - 1)


def _rms(x, g):
    ms = jnp.mean(x * x, axis=-1, keepdims=True)
    return x * lax.rsqrt(ms + NORM_EPS) * g


def _inproj_kernel(x_ref, g_ref, w_ref, o_ref, h_ref):
    @pl.when(pl.program_id(1) == 0)
    def _():
        h_ref[...] = _rms(x_ref[...], g_ref[...]).astype(BF16)

    o_ref[...] = jnp.dot(h_ref[...], w_ref[...], preferred_element_type=F32)


def _inproj(x, g, w, tm, tn):
    t, d = x.shape
    n = w.shape[1]
    return pl.pallas_call(
        _inproj_kernel,
        grid=(t // tm, n // tn),
        in_specs=[pl.BlockSpec((tm, d), lambda i, j: (i, 0)),
                  pl.BlockSpec((1, d), lambda i, j: (0, 0)),
                  pl.BlockSpec((d, tn), lambda i, j: (0, j))],
        out_specs=pl.BlockSpec((tm, tn), lambda i, j: (i, j)),
        out_shape=jax.ShapeDtypeStruct((t, n), F32),
        scratch_shapes=[pltpu.VMEM((tm, d), BF16)],
        compiler_params=_cparams(("parallel", "arbitrary")),
        name="inproj",
    )(x, g, w)


def _shift_rows(cur, prev_row):
    rolled = pltpu.roll(cur, 1, 0)
    row = _iota2(cur.shape, 0)
    return jnp.where(row == 0, prev_row, rolled)


def _split3(x):
    hi = x.astype(BF16)
    r1 = x - hi.astype(F32)
    mid = r1.astype(BF16)
    lo = (r1 - mid.astype(F32)).astype(BF16)
    return hi, mid, lo


def _sel_dot(sel, x):
    hi, mid, lo = _split3(x)
    d = functools.partial(jnp.dot, sel, preferred_element_type=F32)
    return (d(lo) + d(mid)) + d(hi)


def _dot_sel(x, sel):
    hi = x.astype(BF16)
    mid = (x - hi.astype(F32)).astype(BF16)
    d = functools.partial(jnp.dot, preferred_element_type=F32)
    return d(mid, sel) + d(hi, sel)


def _head_sum(x, e):
    parts = [_dot_sel(x[:, i * LANES:(i + 1) * LANES], e) for i in range(x.shape[1] // LANES)]
    return jnp.concatenate(parts, axis=-1)


def _head_ones():
    r = _blk(_iota2((LANES, LANES), 0), A_HEAD_DIM)
    c = _blk(_iota2((LANES, LANES), 1), A_HEAD_DIM)
    return (r == c).astype(BF16)


def _prep_kernel(has_vres, seq_tiles, *refs):
    if has_vres:
        (rkv_ref, lora_ref, prkv_ref, plora_ref, mu_rkv, mu_lora,
         w0, a0, k_k, k_a, r_k, w2, a2, g2, v0, v2, vfirst_ref,
         at_o, rt_o, kt_o, bt_o, kd_o, bd_o, vb_o, pc_o, g_o, bonus_o) = refs
    else:
        (rkv_ref, lora_ref, prkv_ref, plora_ref, mu_rkv, mu_lora,
         w0, a0, k_k, k_a, r_k, w2, a2, g2,
         at_o, rt_o, kt_o, bt_o, kd_o, bd_o, vb_o, pc_o, g_o, bonus_o, v_o) = refs
    first = (pl.program_id(0) % seq_tiles) == 0
    last_row = slice(SUBLANES - 1, SUBLANES)

    def mixed(cur_ref, prev_ref, mu_ref):
        cur = cur_ref[...]
        prev_row = jnp.where(first, 0.0, prev_ref[last_row, :])
        return cur + (_shift_rows(cur, prev_row) - cur) * mu_ref[...]

    rkv = mixed(rkv_ref, prkv_ref, mu_rkv)
    lora = mixed(lora_ref, plora_ref, mu_lora)
    r = rkv[:, 0:A_WIDTH]
    k = rkv[:, A_WIDTH:2 * A_WIDTH]
    v = rkv[:, 2 * A_WIDTH:3 * A_WIDTH]
    zw = -(w0[...] + _dot(jnp.tanh(lora), w2[...]))
    softplus = jnp.maximum(zw, 0.0) + jnp.log1p(jnp.exp(-jnp.abs(zw)))
    lw = -jnp.exp(-softplus - 0.5)
    a = jax.nn.sigmoid(a0[...] + _dot(lora, a2[...]))
    g = _dot(jax.nn.sigmoid(lora), g2[...])
    if has_vres:
        v = v + (vfirst_ref[...] - v) * jax.nn.sigmoid(v0[...] + _dot(lora, v2[...]))

    e = _head_ones()
    kk = k * k_k[...]
    kk = kk * lax.rsqrt(jnp.maximum(_head_sum(kk * kk, e), 1e-24))
    k = k * (1.0 + (a - 1.0) * k_a[...])
    b = kk * a
    bonus = _head_sum(r * k * r_k[...], e) * v

    tm = lw.shape[0]
    row = _iota2((tm, tm), 0)
    col = _iota2((tm, tm), 1)
    tri = (_blk(row, CHUNK) == _blk(col, CHUNK)) & (row >= col)
    cum = _sel_dot(tri.astype(BF16), lw)
    cum3 = cum.reshape(tm // CHUNK, CHUNK, A_WIDTH)
    tot = jnp.broadcast_to(cum3[:, CHUNK - 1:CHUNK, :], cum3.shape).reshape(tm, A_WIDTH)
    e_in = jnp.exp(cum)
    e_out = jnp.exp(-cum)
    e_end = jnp.exp(tot - cum)
    at_o[...] = (-kk * jnp.exp(cum - lw)).astype(BF16)
    rt_o[...] = (r * e_in).astype(BF16)
    kt_o[...] = (k * e_out).astype(BF16)
    bt_o[...] = (b * e_out).astype(BF16)
    kd_o[...] = (k * e_end).astype(BF16)
    bd_o[...] = (b * e_end).astype(BF16)
    vb_o[...] = v.astype(BF16)
    for c in range(tm // CHUNK):
        pc_o[c] = jnp.exp(cum3[c, CHUNK - 1:CHUNK, :])
    g_o[...] = g
    bonus_o[...] = bonus
    if not has_vres:
        v_o[...] = v


def _rwkv_prep(proj, params, v_first, seq_len, tm):
    t = proj.shape[0]
    has_vres = v_first is not None
    seq_tiles = seq_len // tm
    sub = tm // SUBLANES
    rkv_w, lora_w = 3 * A_WIDTH, LORA_PACK
    rkv_blk = (C_COLS) // rkv_w
    lora_blk = (C_COLS + rkv_w + B_COLS) // lora_w

    def cur(width, blk):
        return pl.BlockSpec((tm, width), lambda i: (i, blk))

    def prev(width, blk):
        return pl.BlockSpec((SUBLANES, width), lambda i: (jnp.maximum(i * sub - 1, 0), blk))

    def vec(width):
        return pl.BlockSpec((1, width), lambda i: (0, 0))

    def mat(rows):
        return pl.BlockSpec((rows, A_WIDTH), lambda i: (0, 0))

    tok = pl.BlockSpec((tm, A_WIDTH), lambda i: (i, 0))
    ins = [proj, proj, proj, proj,
           params["mu_rkv"], params["mu_lora"],
           params["w0"], params["a0"], params["k_k"], params["k_a"], params["r_k"],
           params["w2"], params["a2"], params["g2"]]
    specs = [cur(rkv_w, rkv_blk), cur(lora_w, lora_blk),
             prev(rkv_w, rkv_blk), prev(lora_w, lora_blk),
             vec(rkv_w), vec(lora_w),
             vec(A_WIDTH), vec(A_WIDTH), vec(A_WIDTH), vec(A_WIDTH), vec(A_WIDTH),
             mat(LORA_PACK), mat(LORA_PACK), mat(LORA_PACK)]
    if has_vres:
        ins += [params["v0"], params["v2"], v_first]
        specs += [vec(A_WIDTH), mat(LORA_PACK), tok]
    wide = lambda dtype: jax.ShapeDtypeStruct((t, A_WIDTH), dtype)
    pc_spec = pl.BlockSpec((tm // CHUNK, 1, A_WIDTH), lambda i: (i, 0, 0))
    out_specs = [tok] * 7 + [pc_spec, tok, tok]
    out_shape = [wide(BF16)] * 7 + [jax.ShapeDtypeStruct((t // CHUNK, 1, A_WIDTH), F32),
                                    wide(F32), wide(F32)]
    if not has_vres:
        out_specs.append(tok)
        out_shape.append(wide(F32))
    return pl.pallas_call(
        functools.partial(_prep_kernel, has_vres, seq_tiles),
        grid=(t // tm,),
        in_specs=specs,
        out_specs=out_specs,
        out_shape=out_shape,
        compiler_params=_cparams(("parallel",)),
        name="rwkv_prep_vres" if has_vres else "rwkv_prep",
    )(*ins)


def _unit_lower_inverse(a_strict, row, col, eye, pair_diag):
    base = 8
    diag = _blk(row, base) == _blk(col, base)
    n1 = [jnp.where(diag, a, 0.0) for a in a_strict]
    n1d = [pair_diag(x) for x in n1]
    n2 = [_dot(x, d) for x, d in zip(n1, n1d)]
    n2d = [pair_diag(x) for x in n2]
    t = [eye + x for x in n1]
    t = [x + _dot(x, d) for x, d in zip(t, n2d)]
    n4d = [pair_diag(_dot(x, d)) for x, d in zip(n2, n2d)]
    t = [x + _dot(x, d) for x, d in zip(t, n4d)]
    m = base
    while m < CHUNK:
        sel = (_blk(row, 2 * m) == _blk(col, 2 * m)) & (_blk(row, m) != _blk(col, m))
        ed = [pair_diag(jnp.where(sel, a, 0.0)) for a in a_strict]
        te = [_dot(x, d) for x, d in zip(t, ed)]
        t = [x + _dot(y, pair_diag(x)) for x, y in zip(t, te)]
        m *= 2
    return t


def _scan_kernel(cps, at_ref, rt_ref, kt_ref, bt_ref, kd_ref, bd_ref, v_ref, pc_ref,
                 gate_ref, bonus_ref, lng_ref, lnb_ref, o_ref, s_ref):
    @pl.when(pl.program_id(1) == 0)
    def _():
        s_ref[...] = jnp.zeros_like(s_ref)

    n = 2 * A_HEAD_DIM
    pairs = A_HEADS // 2
    row = _iota2((CHUNK, n), 0)
    lane = _iota2((CHUNK, n), 1)
    col = lane & (A_HEAD_DIM - 1)
    lo = lane < A_HEAD_DIM
    strict = row > col
    incl = row >= col
    eye = (row == col).astype(F32)
    same_head = _blk(_iota2((n, n), 0), A_HEAD_DIM) == _blk(_iota2((n, n), 1), A_HEAD_DIM)
    items = [(c, p) for c in range(cps) for p in range(pairs)]

    def load(ref):
        return [ref[0, c * CHUNK:(c + 1) * CHUNK, p * n:(p + 1) * n] for c, p in items]

    def pair_diag(x):
        x = x.astype(BF16)
        zero = jnp.zeros_like(x)
        return jnp.concatenate([jnp.where(lo, x, zero), jnp.where(lo, zero, x)], axis=0)

    def fold(m):
        return jnp.where(lo, m[:CHUNK], m[CHUNK:])

    at = load(at_ref)
    rt = load(rt_ref)
    lhs = [jnp.concatenate([a, r], axis=0) for a, r in zip(at, rt)]
    lk = [_dot_nt(x, pair_diag(k)) for x, k in zip(lhs, load(kt_ref))]
    lb = [_dot_nt(x, pair_diag(b)) for x, b in zip(lhs, load(bt_ref))]
    a_k = [jnp.concatenate([jnp.where(strict, x[:CHUNK], 0.0), jnp.where(incl, x[CHUNK:], 0.0)],
                           axis=0) for x in lk]
    a_ab = [jnp.where(strict, x[:CHUNK], 0.0) for x in lb]
    a_rb = [jnp.where(incl, x[CHUNK:], 0.0) for x in lb]
    t_inv = _unit_lower_inverse(a_ab, row, col, eye, pair_diag)
    v = load(v_ref)
    lv = [_dot(a, pair_diag(x)) for a, x in zip(a_k, v)]
    wu = [_dot(t, jnp.concatenate([pair_diag(a), pair_diag(x[:CHUNK])], axis=1))
          for t, a, x in zip(t_inv, at, lv)]
    z = [_dot(a, jnp.concatenate([pair_diag(x[:, :n]), pair_diag(x[:, n:])], axis=1))
         for a, x in zip(a_rb, wu)]
    rw = [r.astype(F32) + x[:, :n] for r, x in zip(rt, z)]
    y0 = [x[CHUNK:] + zz[:, n:] for x, zz in zip(lv, z)]
    bd = load(bd_ref)
    g = [jnp.where(same_head, _dot_tn(x[:, :n], b), 0.0).astype(BF16)
         for x, b in zip(wu, bd)]
    s_add = [fold(_dot_tn(jnp.concatenate([x[:, n:].astype(BF16), vv], axis=0),
                          jnp.concatenate([b, k], axis=0)))
             for x, vv, b, k in zip(wu, v, bd, load(kd_ref))]

    inv_n = 1.0 / A_HEAD_DIM

    def head_mean(x):
        m_lo = jnp.sum(jnp.where(lo, x, 0.0), axis=-1, keepdims=True)
        m_hi = jnp.sum(jnp.where(lo, 0.0, x), axis=-1, keepdims=True)
        return jnp.where(lo, m_lo, m_hi) * inv_n

    s = [s_ref[p] for p in range(pairs)]
    for c in range(cps):
        rows = slice(c * CHUNK, (c + 1) * CHUNK)
        sd = [pair_diag(x) for x in s]
        ys = [_dot_nt(rw[c * pairs + p], sd[p]) + y0[c * pairs + p] for p in range(pairs)]
        for p in range(pairs):
            i = c * pairs + p
            cols = slice(p * n, (p + 1) * n)
            pc = pc_ref[0, c, :, cols]
            s[p] = s[p] * pc + _dot(s[p], g[i]) + s_add[i]
        for p in range(pairs):
            cols = slice(p * n, (p + 1) * n)
            d = ys[p] - head_mean(ys[p])
            var = head_mean(d * d)
            yn = d * lax.rsqrt(var + A_GN_EPS) * lng_ref[:, cols] + lnb_ref[:, cols]
            o_ref[0, rows, cols] = ((yn + bonus_ref[0, rows, cols])
                                    * gate_ref[0, rows, cols]).astype(o_ref.dtype)
    for p in range(pairs):
        s_ref[p] = s[p]


def _rwkv_scan(at, rt, kt, bt, kd, bd, v, pc, gate, bonus, ln_g, ln_b, batch, seq_len, cps):
    ts = cps * CHUNK
    shp = (batch, seq_len, A_WIDTH)
    blk = pl.BlockSpec((1, ts, A_WIDTH), lambda b, s: (b, s, 0))
    pc_blk = pl.BlockSpec((1, cps, 1, A_WIDTH), lambda b, s: (b, s, 0, 0))
    vec = pl.BlockSpec((1, A_WIDTH), lambda b, s: (0, 0))
    args = [z.reshape(shp) for z in (at, rt, kt, bt, kd, bd, v)]
    args += [pc.reshape(batch, seq_len // CHUNK, 1, A_WIDTH), gate.reshape(shp), bonus.reshape(shp)]
    y = pl.pallas_call(
        functools.partial(_scan_kernel, cps),
        grid=(batch, seq_len // ts),
        in_specs=[blk] * 7 + [pc_blk, blk, blk, vec, vec],
        out_specs=blk,
        out_shape=jax.ShapeDtypeStruct(shp, BF16),
        scratch_shapes=[pltpu.VMEM((A_HEADS // 2, A_HEAD_DIM, 2 * A_HEAD_DIM), F32)],
        compiler_params=_cparams(("parallel", "arbitrary")),
        name="rwkv_scan",
    )(*args, ln_g, ln_b)
    return y.reshape(batch * seq_len, A_WIDTH)


def _lru_kernel(gate_ref, x_ref, px_ref, cw_ref, cb_ref, wa_ref, ba_ref, wx_ref, bx_ref, lam_ref,
                o_ref, h_ref):
    first = pl.program_id(1) == 0

    @pl.when(first)
    def _():
        h_ref[...] = jnp.zeros_like(h_ref)

    x = x_ref[...]
    ts = x.shape[0]
    prev = jnp.where(first, 0.0, px_ref[...])
    row = _iota2(x.shape, 0)
    row8 = _iota2(prev.shape, 0)

    def delayed(d):
        rolled = pltpu.roll(x, d, 0)
        top = jnp.where(row8 < d, pltpu.roll(prev, d, 0), rolled[:SUBLANES])
        return jnp.concatenate([top, rolled[SUBLANES:]], axis=0)

    xc = cw_ref[CONV_WIDTH - 1:CONV_WIDTH, :] * x + cb_ref[...]
    for d in range(1, CONV_WIDTH):
        xc = xc + cw_ref[CONV_WIDTH - 1 - d:CONV_WIDTH - d, :] * delayed(d)

    r = jax.nn.sigmoid(_dot(xc, wa_ref[...]) + ba_ref[...])
    i = jax.nn.sigmoid(_dot(xc, wx_ref[...]) + bx_ref[...])
    log_a = -LRU_C * r * jax.nn.softplus(-lam_ref[...])
    a = jnp.exp(log_a)
    th = jnp.tanh(log_a)
    u = jnp.sqrt(-2.0 * th / (1.0 - th)) * (i * xc)

    d = 1
    while d < ts:
        keep = row >= d
        a_s = jnp.where(keep, pltpu.roll(a, d, 0), 1.0)
        u_s = jnp.where(keep, pltpu.roll(u, d, 0), 0.0)
        u = a * u_s + u
        a = a * a_s
        d *= 2
    h = a * h_ref[...] + u
    h_ref[...] = h[ts - 1:ts, :]
    o_ref[...] = (jax.nn.gelu(gate_ref[...]) * h).astype(o_ref.dtype)


def _rglru(proj, params, batch, seq_len, ts):
    t = proj.shape[0]
    nst = seq_len // ts
    sub = ts // SUBLANES
    gate_blk = (C_COLS + 3 * A_WIDTH) // B_WIDTH
    x_blk = gate_blk + 1
    vec = pl.BlockSpec((1, B_WIDTH), lambda b, s: (0, 0))
    sq = pl.BlockSpec((B_WIDTH, B_WIDTH), lambda b, s: (0, 0))
    return pl.pallas_call(
        _lru_kernel,
        grid=(batch, nst),
        in_specs=[pl.BlockSpec((ts, B_WIDTH), lambda b, s: (b * nst + s, gate_blk)),
                  pl.BlockSpec((ts, B_WIDTH), lambda b, s: (b * nst + s, x_blk)),
                  pl.BlockSpec((SUBLANES, B_WIDTH),
                               lambda b, s: (jnp.maximum((b * nst + s) * sub - 1, 0), x_blk)),
                  pl.BlockSpec((CONV_WIDTH, B_WIDTH), lambda b, s: (0, 0)),
                  vec, sq, vec, sq, vec, vec],
        out_specs=pl.BlockSpec((ts, B_WIDTH), lambda b, s: (b * nst + s, 0)),
        out_shape=jax.ShapeDtypeStruct((t, B_WIDTH), BF16),
        scratch_shapes=[pltpu.VMEM((1, B_WIDTH), F32)],
        compiler_params=_cparams(("parallel", "arbitrary")),
        name="rglru",
    )(proj, proj, proj, params["conv_w"], params["conv_b"], params["wa"], params["ba"],
      params["wx"], params["bx"], params["lam"])


def _log_gammas():
    return np.log1p(-np.exp2(-5.0 - np.arange(C_HEADS, dtype=np.float32))).astype(np.float32)


def _ret_kernel(cpr, q_ref, k_ref, v_ref, g_ref, pos_ref, invf_ref, gng_ref, o_ref, st_ref):
    @pl.when(pl.program_id(1) == 0)
    def _():
        st_ref[...] = jnp.zeros_like(st_ref)

    half = C_QK_DIM // 2
    lane = _iota2((CHUNK, LANES), 1)
    in_first_half = (lane & (C_QK_DIM - 1)) < half
    lo = lane < C_QK_DIM
    idx_r = _iota2((CHUNK, CHUNK), 0).astype(F32)
    idx_c = _iota2((CHUNK, CHUNK), 1).astype(F32)
    dist = jnp.abs(idx_r - idx_c)
    pos_in = _iota2((CHUNK, 1), 0).astype(F32)
    log_gamma = _log_gammas()

    def rope(t, cos, sin_signed):
        swapped = jnp.where(in_first_half, pltpu.roll(t, LANES - half, 1), pltpu.roll(t, half, 1))
        return t * cos + swapped * sin_signed

    intra_decay = [jnp.exp(float(lg) * dist) for lg in log_gamma]
    q_decay = [jnp.exp(float(lg) * (pos_in + 1.0)) for lg in log_gamma]
    k_decay = [jnp.exp(float(lg) * (CHUNK - 1.0 - pos_in)) for lg in log_gamma]
    chunk_decay = [float(np.exp(lg * np.float32(CHUNK))) for lg in log_gamma]
    items = [(c, h) for c in range(cpr) for h in range(C_HEADS)]

    q2, k2 = {}, {}
    for c in range(cpr):
        rows = slice(c * CHUNK, (c + 1) * CHUNK)
        ang = pos_ref[0, rows, :] * invf_ref[...]
        cos = jnp.cos(ang)
        sin = jnp.sin(ang)
        sin_signed = jnp.where(in_first_half, -sin, sin)
        for p in range(C_HEADS // 2):
            qk_cols = slice(p * LANES, (p + 1) * LANES)
            q2[c, p] = rope(q_ref[rows, qk_cols], cos, sin_signed)
            k2[c, p] = rope(k_ref[rows, qk_cols], cos, sin_signed) * (C_QK_DIM ** -0.5)

    def mine(x, h):
        return jnp.where(lo, x, 0.0) if h % 2 == 0 else jnp.where(lo, 0.0, x)

    qh = [mine(q2[c, h // 2], h) for c, h in items]
    kh = [mine(k2[c, h // 2], h) for c, h in items]
    vh = [v_ref[c * CHUNK:(c + 1) * CHUNK, h * C_V_DIM:(h + 1) * C_V_DIM].astype(BF16)
          for c, h in items]
    scores = [_dot_nt(q, k) * intra_decay[h] for q, k, (c, h) in zip(qh, kh, items)]
    kv = [_dot_tn(k * k_decay[h], v) for k, v, (c, h) in zip(kh, vh, items)]
    intra = [_dot(s, v) for s, v in zip(scores, vh)]

    state = [st_ref[h] for h in range(C_HEADS)]
    starts = []
    for i, (c, h) in enumerate(items):
        starts.append(state[h])
        state[h] = state[h] * chunk_decay[h] + kv[i]
    for h in range(C_HEADS):
        st_ref[h] = state[h]
    cross = [_dot(q * q_decay[h], s) for q, s, (c, h) in zip(qh, starts, items)]

    for i, (c, h) in enumerate(items):
        rows = slice(c * CHUNK, (c + 1) * CHUNK)
        v_cols = slice(h * C_V_DIM, (h + 1) * C_V_DIM)
        o = intra[i] + cross[i]
        mu = jnp.mean(o, axis=-1, keepdims=True)
        d = o - mu
        var = jnp.mean(d * d, axis=-1, keepdims=True)
        on = d * lax.rsqrt(var + C_GN_EPS) * gng_ref[:, v_cols]
        o_ref[rows, v_cols] = (jax.nn.silu(g_ref[rows, v_cols]) * on).astype(o_ref.dtype)


def _retention(proj, posb, invf, gn_g, batch, seq_len, cpr):
    t = proj.shape[0]
    ts = cpr * CHUNK
    nst = seq_len // ts
    return pl.pallas_call(
        functools.partial(_ret_kernel, cpr),
        grid=(batch, nst),
        in_specs=[pl.BlockSpec((ts, C_QK_WIDTH), lambda b, s: (b * nst + s, 0)),
                  pl.BlockSpec((ts, C_QK_WIDTH), lambda b, s: (b * nst + s, 1)),
                  pl.BlockSpec((ts, C_WIDTH), lambda b, s: (b * nst + s, 1)),
                  pl.BlockSpec((ts, C_WIDTH), lambda b, s: (b * nst + s, 2)),
                  pl.BlockSpec((1, ts, LANES), lambda b, s: (b, s, 0)),
                  pl.BlockSpec((1, LANES), lambda b, s: (0, 0)),
                  pl.BlockSpec((1, C_WIDTH), lambda b, s: (0, 0))],
        out_specs=pl.BlockSpec((ts, C_WIDTH), lambda b, s: (b * nst + s, 0)),
        out_shape=jax.ShapeDtypeStruct((t, C_WIDTH), BF16),
        scratch_shapes=[pltpu.VMEM((C_HEADS, LANES, C_V_DIM), F32)],
        compiler_params=_cparams(("parallel", "arbitrary")),
        name="retention",
    )(proj, proj, proj, proj, posb, invf, gn_g)


def _outproj_kernel(x_ref, ya_ref, yb_ref, yc_ref, w_ref, o_ref):
    b0, c0 = A_WIDTH, A_WIDTH + B_WIDTH
    acc = jnp.dot(ya_ref[...], w_ref[0:b0, :].astype(BF16), preferred_element_type=F32)
    acc += jnp.dot(yb_ref[...], w_ref[b0:c0, :].astype(BF16), preferred_element_type=F32)
    acc += jnp.dot(yc_ref[...], w_ref[c0:, :].astype(BF16), preferred_element_type=F32)
    o_ref[...] = x_ref[...] + acc


def _outproj(x, ya, yb, yc, w_stack, layer, tm):
    t, d = x.shape

    def tok(width):
        return pl.BlockSpec((tm, width), lambda i: (i, 0))

    return pl.pallas_call(
        _outproj_kernel,
        grid=(t // tm,),
        in_specs=[tok(d), tok(A_WIDTH), tok(B_WIDTH), tok(C_WIDTH),
                  pl.BlockSpec((None, d, d), lambda i: (layer, 0, 0),
                               pipeline_mode=pl.Buffered(1))],
        out_specs=tok(d),
        out_shape=jax.ShapeDtypeStruct((t, d), F32),
        compiler_params=_cparams(("parallel",)),
        name="outproj",
    )(x, ya, yb, yc, w_stack)


def _ffn_kernel(final, x_ref, g_ref, wg_ref, wu_ref, wd_ref, fg_ref, o_ref, h_ref):
    j = pl.program_id(1)

    @pl.when(j == 0)
    def _():
        x = x_ref[...]
        h_ref[...] = _rms(x, g_ref[...]).astype(BF16)
        o_ref[...] = x

    h = h_ref[...]
    gate = jnp.dot(h, wg_ref[...].astype(BF16), preferred_element_type=F32)
    up = jnp.dot(h, wu_ref[...].astype(BF16), preferred_element_type=F32)
    act = (jax.nn.silu(gate) * up).astype(BF16)
    o_ref[...] += jnp.dot(act, wd_ref[...].astype(BF16), preferred_element_type=F32)

    if final:
        @pl.when(j == pl.num_programs(1) - 1)
        def _():
            o_ref[...] = _rms(o_ref[...], fg_ref[...])


def _ffn(x, g, wg, wu, wd, fg, layer, final, tm, tf):
    t, d = x.shape
    f = wg.shape[2]
    return pl.pallas_call(
        functools.partial(_ffn_kernel, final),
        grid=(t // tm, f // tf),
        in_specs=[pl.BlockSpec((tm, d), lambda i, j: (i, 0)),
                  pl.BlockSpec((1, d), lambda i, j: (0, 0)),
                  pl.BlockSpec((None, d, tf), lambda i, j: (layer, 0, j)),
                  pl.BlockSpec((None, d, tf), lambda i, j: (layer, 0, j)),
                  pl.BlockSpec((None, tf, d), lambda i, j: (layer, j, 0)),
                  pl.BlockSpec((1, d), lambda i, j: (0, 0))],
        out_specs=pl.BlockSpec((tm, d), lambda i, j: (i, 0)),
        out_shape=jax.ShapeDtypeStruct((t, d), F32),
        scratch_shapes=[pltpu.VMEM((tm, d), BF16)],
        compiler_params=_cparams(("parallel", "arbitrary")),
        name="ffn_final" if final else "ffn",
    )(x, g, wg, wu, wd, fg)


def _rows_at(w, offset):
    return jnp.pad(w, ((offset, LORA_PACK - offset - w.shape[0]), (0, 0)))


RELAYOUT_UNIT = 64
RELAYOUT_PARTS = 4


def _source_unit(q):
    c_units = C_COLS // RELAYOUT_UNIT
    a_units = 3 * A_WIDTH // RELAYOUT_UNIT
    b_units = B_COLS // RELAYOUT_UNIT
    last_src = (A_COLS + B_COLS + C_COLS) // RELAYOUT_UNIT - 1
    src_c = (A_COLS + B_COLS) // RELAYOUT_UNIT
    src_b = A_COLS // RELAYOUT_UNIT
    lora = jnp.minimum(a_units + (q - c_units - a_units - b_units), last_src)
    return jnp.where(q < c_units, src_c + q,
                     jnp.where(q < c_units + a_units, q - c_units,
                               jnp.where(q < c_units + a_units + b_units,
                                         src_b + (q - c_units - a_units), lora)))


def _relayout_kernel(has_vres, n_steps, *refs):
    parts, rest = refs[:RELAYOUT_PARTS], refs[RELAYOUT_PARTS:]
    o_ref = rest[-1]
    blocks = [p[...] for p in parts]
    tail = rest[0][...] if has_vres else jnp.zeros_like(blocks[-1])
    blocks[-1] = jnp.where(pl.program_id(0) == n_steps - 1, tail, blocks[-1])
    o_ref[...] = jnp.concatenate(blocks, axis=0).T.astype(BF16)


def _layer_in_weight(w_in_t, w_vres_t, layer):
    _, n, d = w_in_t.shape
    has_vres = layer > 0
    n_steps = P_COLS // (RELAYOUT_UNIT * RELAYOUT_PARTS)

    def part(s):
        return pl.BlockSpec((None, RELAYOUT_UNIT, d),
                            lambda j: (layer, _source_unit(RELAYOUT_PARTS * j + s), 0))

    ins = [w_in_t] * RELAYOUT_PARTS
    specs = [part(s) for s in range(RELAYOUT_PARTS)]
    if has_vres:
        ins.append(w_vres_t)
        specs.append(pl.BlockSpec((None, VRES_LORA, d), lambda j: (layer - 1, 0, 0)))
    return pl.pallas_call(
        functools.partial(_relayout_kernel, has_vres, n_steps),
        grid=(n_steps,),
        in_specs=specs,
        out_specs=pl.BlockSpec((d, RELAYOUT_UNIT * RELAYOUT_PARTS), lambda j: (0, j)),
        out_shape=jax.ShapeDtypeStruct((d, P_COLS), BF16),
        compiler_params=_cparams(("parallel",)),
        name="w_in_relayout",
    )(*ins)


def _block_diag(w):
    g, n, _ = w.shape
    eye = jnp.eye(g, dtype=w.dtype)
    return (eye[:, None, :, None] * w[:, :, None, :]).reshape(g * n, g * n)


def kernel(x, positions, norm1_g, w_in, tshift_mu, rwkv_w0, rwkv_w2, rwkv_a0, rwkv_a2, rwkv_g2, rwkv_k_k, rwkv_k_a, rwkv_r_k, rwkv_ln_g, rwkv_ln_b, w_in_vres, tshift_mu_vres, rwkv_v0, rwkv_v2, lru_conv_w, lru_conv_b, lru_wa, lru_ba, lru_wx, lru_bx, lru_lambda, ret_gn_g, w_out, norm2_g, ffn_w_gate, ffn_w_up, ffn_w_down, final_norm_g):
    batch, seq_len, d = x.shape
    depth = w_in.shape[0]
    t = batch * seq_len
    aw = A_WIDTH
    row = lambda v: v.reshape(1, -1).astype(F32)

    tm_proj = min(512, seq_len)
    tm_in = min(1024, seq_len)
    tm_tok = min(256, seq_len)
    cps = min(4, seq_len // CHUNK)
    cpr = min(4, seq_len // CHUNK)

    inv_freq = ROPE_THETA ** (-jnp.arange(0, C_QK_DIM, 2, dtype=F32) / C_QK_DIM)
    invf = jnp.tile(inv_freq, LANES // inv_freq.shape[0]).reshape(1, LANES)
    posb = jnp.broadcast_to(positions.astype(F32)[..., None], (batch, seq_len, LANES))

    w_in_t = jnp.swapaxes(w_in, 1, 2)
    w_vres_t = jnp.swapaxes(w_in_vres, 1, 2)

    xf = x.reshape(t, d)
    v_first = None
    for l in range(depth):
        has_vres = l > 0
        w_l = _layer_in_weight(w_in_t, w_vres_t, l)
        proj = _inproj(xf, row(norm1_g[l]), w_l, tm_in, P_COLS // 6)

        mu = tshift_mu[l]
        o_ad, o_gd = 3 * aw + DECAY_LORA, 3 * aw + DECAY_LORA + ICLR_LORA
        mu_vd = tshift_mu_vres[l - 1] if has_vres else jnp.zeros((VRES_LORA,), F32)
        params = {
            "mu_rkv": row(mu[:3 * aw]),
            "mu_lora": row(jnp.concatenate([mu[3 * aw:A_COLS], mu_vd])),
            "w0": row(rwkv_w0[l]), "a0": row(rwkv_a0[l]), "k_k": row(rwkv_k_k[l]),
            "k_a": row(rwkv_k_a[l]), "r_k": row(rwkv_r_k[l]),
            "w2": _rows_at(rwkv_w2[l], 0), "a2": _rows_at(rwkv_a2[l], o_ad - 3 * aw),
            "g2": _rows_at(rwkv_g2[l], o_gd - 3 * aw),
        }
        if has_vres:
            params["v0"] = row(rwkv_v0[l - 1])
            params["v2"] = _rows_at(rwkv_v2[l - 1], A_COLS - 3 * aw)
        at, rt, kt, bt, kd, bd, v_b, pc, g_a, bonus, *v_f32 = _rwkv_prep(
            proj, params, v_first if has_vres else None, seq_len, tm_tok)
        if l == 0:
            v_first = v_f32[0]
        y_a = _rwkv_scan(at, rt, kt, bt, kd, bd, v_b, pc, g_a, bonus,
                         row(rwkv_ln_g[l]), row(rwkv_ln_b[l]), batch, seq_len, cps)

        lru_params = {
            "conv_w": lru_conv_w[l], "conv_b": row(lru_conv_b[l]),
            "wa": _block_diag(lru_wa[l]).astype(BF16), "ba": row(lru_ba[l]),
            "wx": _block_diag(lru_wx[l]).astype(BF16), "bx": row(lru_bx[l]),
            "lam": row(lru_lambda[l]),
        }
        y_b = _rglru(proj, lru_params, batch, seq_len, tm_tok)
        y_c = _retention(proj, posb, invf, row(ret_gn_g[l]), batch, seq_len, cpr)

        xf = _outproj(xf, y_a, y_b, y_c, w_out, l, tm_proj)
        xf = _ffn(xf, row(norm2_g[l]), ffn_w_gate, ffn_w_up, ffn_w_down, row(final_norm_g),
                  l, l == depth - 1, tm_in, 256)
    return xf.reshape(batch, seq_len, d)
```

```python
import functools

import numpy as np
import jax
import jax.numpy as jnp
from jax import lax
from jax.experimental import pallas as pl
from jax.experimental.pallas import tpu as pltpu

F32 = jnp.float32
BF16 = jnp.bfloat16

NORM_EPS = 1e-6
CHUNK = 64

A_HEADS = 12
A_HEAD_DIM = 64
A_WIDTH = A_HEADS * A_HEAD_DIM
DECAY_LORA = 96
ICLR_LORA = 96
VRES_LORA = 64
GATE_LORA = 256
A_GN_EPS = 64e-5
A_COLS = 3 * A_WIDTH + DECAY_LORA + ICLR_LORA + GATE_LORA
B_WIDTH = 512
B_BLOCKS = 8
B_BLOCK_DIM = B_WIDTH // B_BLOCKS
CONV_WIDTH = 4
LRU_C = 8.0
B_COLS = 2 * B_WIDTH
C_HEADS = 6
C_QK_DIM = 64
C_V_DIM = 128
C_QK_WIDTH = C_HEADS * C_QK_DIM
C_WIDTH = C_HEADS * C_V_DIM
C_GN_EPS = 1e-5
ROPE_THETA = 10000.0
C_COLS = 2 * C_QK_WIDTH + 2 * C_WIDTH

LANES = 128
SUBLANES = 8
LORA_PACK = DECAY_LORA + ICLR_LORA + GATE_LORA + VRES_LORA
VMEM_LIMIT = 56 * 1024 * 1024

P_COLS = C_COLS + 3 * A_WIDTH + B_COLS + LORA_PACK


def _cparams(sem):
    return pltpu.CompilerParams(dimension_semantics=sem, vmem_limit_bytes=VMEM_LIMIT)


def _dot(a, b):
    return jnp.dot(a.astype(BF16), b.astype(BF16), preferred_element_type=F32)


def _dot_nt(a, b):
    return lax.dot_general(a.astype(BF16), b.astype(BF16), (((1,), (1,)), ((), ())),
                           preferred_element_type=F32)


def _dot_tn(a, b):
    return lax.dot_general(a.astype(BF16), b.astype(BF16), (((0,), (0,)), ((), ())),
                           preferred_element_type=F32)


def _iota2(shape, axis):
    return lax.broadcasted_iota(jnp.int32, shape, axis)


def _blk(idx, size):
    return idx >> (size.bit_length() - 1)


def _rms(x, g):
    ms = jnp.mean(x * x, axis=-1, keepdims=True)
    return x * lax.rsqrt(ms + NORM_EPS) * g


def _inproj_kernel(x_ref, g_ref, w_ref, o_ref, h_ref):
    @pl.when(pl.program_id(1) == 0)
    def _():
        h_ref[...] = _rms(x_ref[...], g_ref[...]).astype(BF16)

    o_ref[...] = jnp.dot(h_ref[...], w_ref[...], preferred_element_type=F32)


def _inproj(x, g, w, tm, tn):
    t, d = x.shape
    n = w.shape[1]
    return pl.pallas_call(
        _inproj_kernel,
        grid=(t // tm, n // tn),
        in_specs=[pl.BlockSpec((tm, d), lambda i, j: (i, 0)),
                  pl.BlockSpec((1, d), lambda i, j: (0, 0)),
                  pl.BlockSpec((d, tn), lambda i, j: (0, j))],
        out_specs=pl.BlockSpec((tm, tn), lambda i, j: (i, j)),
        out_shape=jax.ShapeDtypeStruct((t, n), F32),
        scratch_shapes=[pltpu.VMEM((tm, d), BF16)],
        compiler_params=_cparams(("parallel", "arbitrary")),
        name="inproj",
    )(x, g, w)


def _shift_rows(cur, prev_row):
    rolled = pltpu.roll(cur, 1, 0)
    top = rolled[:SUBLANES]
    top = jnp.where(_iota2(top.shape, 0) == 0, prev_row, top)
    return jnp.concatenate([top, rolled[SUBLANES:]], axis=0)


def _split3(x):
    hi = x.astype(BF16)
    r1 = x - hi.astype(F32)
    mid = r1.astype(BF16)
    lo = (r1 - mid.astype(F32)).astype(BF16)
    return hi, mid, lo


def _sel_dot(sel, x):
    hi, mid, lo = _split3(x)
    d = functools.partial(jnp.dot, sel, preferred_element_type=F32)
    return (d(lo) + d(mid)) + d(hi)


def _dot_sel(x, sel):
    hi = x.astype(BF16)
    mid = (x - hi.astype(F32)).astype(BF16)
    d = functools.partial(jnp.dot, preferred_element_type=F32)
    return d(mid, sel) + d(hi, sel)


def _head_sum(x, e):
    parts = [_dot_sel(x[:, i * LANES:(i + 1) * LANES], e) for i in range(x.shape[1] // LANES)]
    return jnp.concatenate(parts, axis=-1)


def _head_ones():
    r = _blk(_iota2((LANES, LANES), 0), A_HEAD_DIM)
    c = _blk(_iota2((LANES, LANES), 1), A_HEAD_DIM)
    return (r == c).astype(BF16)


def _prep_kernel(has_vres, seq_tiles, *refs):
    if has_vres:
        (rkv_ref, lora_ref, prkv_ref, plora_ref, mu_rkv, mu_lora,
         w0, a0, k_k, k_a, r_k, w2, a2, g2, v0, v2, vfirst_ref,
         at_o, rt_o, kt_o, bt_o, kd_o, bd_o, vb_o, pc_o, g_o, bonus_o) = refs
    else:
        (rkv_ref, lora_ref, prkv_ref, plora_ref, mu_rkv, mu_lora,
         w0, a0, k_k, k_a, r_k, w2, a2, g2,
         at_o, rt_o, kt_o, bt_o, kd_o, bd_o, vb_o, pc_o, g_o, bonus_o, v_o) = refs
    first = (pl.program_id(0) % seq_tiles) == 0
    last_row = slice(SUBLANES - 1, SUBLANES)

    def mixed(cur_ref, prev_ref, mu_ref):
        cur = cur_ref[...]
        prev_row = jnp.where(first, 0.0, prev_ref[last_row, :])
        return cur + (_shift_rows(cur, prev_row) - cur) * mu_ref[...]

    rkv = mixed(rkv_ref, prkv_ref, mu_rkv)
    lora = mixed(lora_ref, plora_ref, mu_lora)
    r = rkv[:, 0:A_WIDTH]
    k = rkv[:, A_WIDTH:2 * A_WIDTH]
    v = rkv[:, 2 * A_WIDTH:3 * A_WIDTH]
    zw = -(w0[...] + _dot(jnp.tanh(lora), w2[...]))
    softplus = jnp.maximum(zw, 0.0) + jnp.log1p(jnp.exp(-jnp.abs(zw)))
    lw = -jnp.exp(-softplus - 0.5)
    a = jax.nn.sigmoid(a0[...] + _dot(lora, a2[...]))
    g = _dot(jax.nn.sigmoid(lora), g2[...])
    if has_vres:
        v = v + (vfirst_ref[...] - v) * jax.nn.sigmoid(v0[...] + _dot(lora, v2[...]))

    e = _head_ones()
    kk = k * k_k[...]
    kk = kk * lax.rsqrt(jnp.maximum(_head_sum(kk * kk, e), 1e-24))
    k = k * (1.0 + (a - 1.0) * k_a[...])
    b = kk * a
    bonus = _head_sum(r * k * r_k[...], e) * v

    tm = lw.shape[0]
    row = _iota2((tm, tm), 0)
    col = _iota2((tm, tm), 1)
    tri = (_blk(row, CHUNK) == _blk(col, CHUNK)) & (row >= col)
    cum = _sel_dot(tri.astype(BF16), lw)
    cum3 = cum.reshape(tm // CHUNK, CHUNK, A_WIDTH)
    tot = jnp.broadcast_to(cum3[:, CHUNK - 1:CHUNK, :], cum3.shape).reshape(tm, A_WIDTH)
    e_in = jnp.exp(cum)
    e_out = jnp.exp(-cum)
    e_end = jnp.exp(tot - cum)
    at_o[...] = (-kk * jnp.exp(cum - lw)).astype(BF16)
    rt_o[...] = (r * e_in).astype(BF16)
    kt_o[...] = (k * e_out).astype(BF16)
    bt_o[...] = (b * e_out).astype(BF16)
    kd_o[...] = (k * e_end).astype(BF16)
    bd_o[...] = (b * e_end).astype(BF16)
    vb_o[...] = v.astype(BF16)
    for c in range(tm // CHUNK):
        pc_o[c] = jnp.exp(cum3[c, CHUNK - 1:CHUNK, :])
    g_o[...] = g
    bonus_o[...] = bonus
    if not has_vres:
        v_o[...] = v


def _rwkv_prep(proj, params, v_first, seq_len, tm):
    t = proj.shape[0]
    has_vres = v_first is not None
    seq_tiles = seq_len // tm
    sub = tm // SUBLANES
    rkv_w, lora_w = 3 * A_WIDTH, LORA_PACK
    rkv_blk = (C_COLS) // rkv_w
    lora_blk = (C_COLS + rkv_w + B_COLS) // lora_w

    def cur(width, blk):
        return pl.BlockSpec((tm, width), lambda i: (i, blk))

    def prev(width, blk):
        return pl.BlockSpec((SUBLANES, width), lambda i: (jnp.maximum(i * sub - 1, 0), blk))

    def vec(width):
        return pl.BlockSpec((1, width), lambda i: (0, 0))

    def mat(rows):
        return pl.BlockSpec((rows, A_WIDTH), lambda i: (0, 0))

    tok = pl.BlockSpec((tm, A_WIDTH), lambda i: (i, 0))
    ins = [proj, proj, proj, proj,
           params["mu_rkv"], params["mu_lora"],
           params["w0"], params["a0"], params["k_k"], params["k_a"], params["r_k"],
           params["w2"], params["a2"], params["g2"]]
    specs = [cur(rkv_w, rkv_blk), cur(lora_w, lora_blk),
             prev(rkv_w, rkv_blk), prev(lora_w, lora_blk),
             vec(rkv_w), vec(lora_w),
             vec(A_WIDTH), vec(A_WIDTH), vec(A_WIDTH), vec(A_WIDTH), vec(A_WIDTH),
             mat(LORA_PACK), mat(LORA_PACK), mat(LORA_PACK)]
    if has_vres:
        ins += [params["v0"], params["v2"], v_first]
        specs += [vec(A_WIDTH), mat(LORA_PACK), tok]
    wide = lambda dtype: jax.ShapeDtypeStruct((t, A_WIDTH), dtype)
    pc_spec = pl.BlockSpec((tm // CHUNK, 1, A_WIDTH), lambda i: (i, 0, 0))
    out_specs = [tok] * 7 + [pc_spec, tok, tok]
    out_shape = [wide(BF16)] * 7 + [jax.ShapeDtypeStruct((t // CHUNK, 1, A_WIDTH), F32),
                                    wide(F32), wide(F32)]
    if not has_vres:
        out_specs.append(tok)
        out_shape.append(wide(F32))
    return pl.pallas_call(
        functools.partial(_prep_kernel, has_vres, seq_tiles),
        grid=(t // tm,),
        in_specs=specs,
        out_specs=out_specs,
        out_shape=out_shape,
        compiler_params=_cparams(("parallel",)),
        name="rwkv_prep_vres" if has_vres else "rwkv_prep",
    )(*ins)


def _unit_lower_inverse(a_strict, row, col, eye, pair_diag):
    base = 8
    diag = _blk(row, base) == _blk(col, base)
    n1 = [jnp.where(diag, a, 0.0) for a in a_strict]
    n1d = [pair_diag(x) for x in n1]
    n2 = [_dot(x, d) for x, d in zip(n1, n1d)]
    n2d = [pair_diag(x) for x in n2]
    t = [eye + x for x in n1]
    t = [x + _dot(x, d) for x, d in zip(t, n2d)]
    n4d = [pair_diag(_dot(x, d)) for x, d in zip(n2, n2d)]
    t = [x + _dot(x, d) for x, d in zip(t, n4d)]
    m = base
    while m < CHUNK:
        sel = (_blk(row, 2 * m) == _blk(col, 2 * m)) & (_blk(row, m) != _blk(col, m))
        ed = [pair_diag(jnp.where(sel, a, 0.0)) for a in a_strict]
        te = [_dot(x, d) for x, d in zip(t, ed)]
        t = [x + _dot(y, pair_diag(x)) for x, y in zip(t, te)]
        m *= 2
    return t


def _scan_kernel(cps, at_ref, rt_ref, kt_ref, bt_ref, kd_ref, bd_ref, v_ref, pc_ref,
                 gate_ref, bonus_ref, lng_ref, lnb_ref, o_ref, s_ref):
    @pl.when(pl.program_id(1) == 0)
    def _():
        s_ref[...] = jnp.zeros_like(s_ref)

    n = 2 * A_HEAD_DIM
    pairs = A_HEADS // 2
    row = _iota2((CHUNK, n), 0)
    lane = _iota2((CHUNK, n), 1)
    col = lane & (A_HEAD_DIM - 1)
    lo = lane < A_HEAD_DIM
    strict = row > col
    incl = row >= col
    eye = (row == col).astype(F32)
    same_head = _blk(_iota2((n, n), 0), A_HEAD_DIM) == _blk(_iota2((n, n), 1), A_HEAD_DIM)
    items = [(c, p) for c in range(cps) for p in range(pairs)]

    def load(ref):
        return [ref[0, c * CHUNK:(c + 1) * CHUNK, p * n:(p + 1) * n] for c, p in items]

    def pair_diag(x):
        x = x.astype(BF16)
        zero = jnp.zeros_like(x)
        return jnp.concatenate([jnp.where(lo, x, zero), jnp.where(lo, zero, x)], axis=0)

    def fold(m):
        return jnp.where(lo, m[:CHUNK], m[CHUNK:])

    at = load(at_ref)
    rt = load(rt_ref)
    lhs = [jnp.concatenate([a, r], axis=0) for a, r in zip(at, rt)]
    lk = [_dot_nt(x, pair_diag(k)) for x, k in zip(lhs, load(kt_ref))]
    lb = [_dot_nt(x, pair_diag(b)) for x, b in zip(lhs, load(bt_ref))]
    a_k = [jnp.concatenate([jnp.where(strict, x[:CHUNK], 0.0), jnp.where(incl, x[CHUNK:], 0.0)],
                           axis=0) for x in lk]
    a_ab = [jnp.where(strict, x[:CHUNK], 0.0) for x in lb]
    a_rb = [jnp.where(incl, x[CHUNK:], 0.0) for x in lb]
    t_inv = _unit_lower_inverse(a_ab, row, col, eye, pair_diag)
    v = load(v_ref)
    lv = [_dot(a, pair_diag(x)) for a, x in zip(a_k, v)]
    wu = [_dot(t, jnp.concatenate([pair_diag(a), pair_diag(x[:CHUNK])], axis=1))
          for t, a, x in zip(t_inv, at, lv)]
    z = [_dot(a, jnp.concatenate([pair_diag(x[:, :n]), pair_diag(x[:, n:])], axis=1))
         for a, x in zip(a_rb, wu)]
    rw = [r.astype(F32) + x[:, :n] for r, x in zip(rt, z)]
    y0 = [x[CHUNK:] + zz[:, n:] for x, zz in zip(lv, z)]
    bd = load(bd_ref)
    g = [jnp.where(same_head, _dot_tn(x[:, :n], b), 0.0).astype(BF16)
         for x, b in zip(wu, bd)]
    s_add = [fold(_dot_tn(jnp.concatenate([x[:, n:].astype(BF16), vv], axis=0),
                          jnp.concatenate([b, k], axis=0)))
             for x, vv, b, k in zip(wu, v, bd, load(kd_ref))]

    inv_n = 1.0 / A_HEAD_DIM

    def head_mean(x):
        m_lo = jnp.sum(jnp.where(lo, x, 0.0), axis=-1, keepdims=True)
        m_hi = jnp.sum(jnp.where(lo, 0.0, x), axis=-1, keepdims=True)
        return jnp.where(lo, m_lo, m_hi) * inv_n

    s = [s_ref[p] for p in range(pairs)]
    for c in range(cps):
        rows = slice(c * CHUNK, (c + 1) * CHUNK)
        sd = [pair_diag(x) for x in s]
        ys = [_dot_nt(rw[c * pairs + p], sd[p]) + y0[c * pairs + p] for p in range(pairs)]
        for p in range(pairs):
            i = c * pairs + p
            cols = slice(p * n, (p + 1) * n)
            pc = pc_ref[0, c, :, cols]
            s[p] = s[p] * pc + _dot(s[p], g[i]) + s_add[i]
        for p in range(pairs):
            cols = slice(p * n, (p + 1) * n)
            d = ys[p] - head_mean(ys[p])
            var = head_mean(d * d)
            yn = d * lax.rsqrt(var + A_GN_EPS) * lng_ref[:, cols] + lnb_ref[:, cols]
            o_ref[0, rows, cols] = ((yn + bonus_ref[0, rows, cols])
                                    * gate_ref[0, rows, cols]).astype(o_ref.dtype)
    for p in range(pairs):
        s_ref[p] = s[p]


def _rwkv_scan(at, rt, kt, bt, kd, bd, v, pc, gate, bonus, ln_g, ln_b, batch, seq_len, cps):
    ts = cps * CHUNK
    shp = (batch, seq_len, A_WIDTH)
    blk = pl.BlockSpec((1, ts, A_WIDTH), lambda b, s: (b, s, 0))
    pc_blk = pl.BlockSpec((1, cps, 1, A_WIDTH), lambda b, s: (b, s, 0, 0))
    vec = pl.BlockSpec((1, A_WIDTH), lambda b, s: (0, 0))
    args = [z.reshape(shp) for z in (at, rt, kt, bt, kd, bd, v)]
    args += [pc.reshape(batch, seq_len // CHUNK, 1, A_WIDTH), gate.reshape(shp), bonus.reshape(shp)]
    y = pl.pallas_call(
        functools.partial(_scan_kernel, cps),
        grid=(batch, seq_len // ts),
        in_specs=[blk] * 7 + [pc_blk, blk, blk, vec, vec],
        out_specs=blk,
        out_shape=jax.ShapeDtypeStruct(shp, BF16),
        scratch_shapes=[pltpu.VMEM((A_HEADS // 2, A_HEAD_DIM, 2 * A_HEAD_DIM), F32)],
        compiler_params=_cparams(("parallel", "arbitrary")),
        name="rwkv_scan",
    )(*args, ln_g, ln_b)
    return y.reshape(batch * seq_len, A_WIDTH)


def _lru_kernel(gate_ref, x_ref, px_ref, cw_ref, cb_ref, wa_ref, ba_ref, wx_ref, bx_ref, lam_ref,
                o_ref, h_ref):
    first = pl.program_id(1) == 0

    @pl.when(first)
    def _():
        h_ref[...] = jnp.zeros_like(h_ref)

    x = x_ref[...]
    ts = x.shape[0]
    prev = jnp.where(first, 0.0, px_ref[...])
    row = _iota2(x.shape, 0)
    row8 = _iota2(prev.shape, 0)

    def delayed(d):
        rolled = pltpu.roll(x, d, 0)
        top = jnp.where(row8 < d, pltpu.roll(prev, d, 0), rolled[:SUBLANES])
        return jnp.concatenate([top, rolled[SUBLANES:]], axis=0)

    xc = cw_ref[CONV_WIDTH - 1:CONV_WIDTH, :] * x + cb_ref[...]
    for d in range(1, CONV_WIDTH):
        xc = xc + cw_ref[CONV_WIDTH - 1 - d:CONV_WIDTH - d, :] * delayed(d)

    r = jax.nn.sigmoid(_dot(xc, wa_ref[...]) + ba_ref[...])
    i = jax.nn.sigmoid(_dot(xc, wx_ref[...]) + bx_ref[...])
    log_a = -LRU_C * r * jax.nn.softplus(-lam_ref[...])
    a = jnp.exp(log_a)
    th = jnp.tanh(log_a)
    u = jnp.sqrt(-2.0 * th / (1.0 - th)) * (i * xc)

    in_group = row & (SUBLANES - 1)
    d = 1
    while d < SUBLANES:
        keep = in_group >= d
        u = a * jnp.where(keep, pltpu.roll(u, d, 0), 0.0) + u
        a = a * jnp.where(keep, pltpu.roll(a, d, 0), 1.0)
        d *= 2
    carry = h_ref[...]
    groups = []
    for g0 in range(0, ts, SUBLANES):
        hg = a[g0:g0 + SUBLANES] * carry + u[g0:g0 + SUBLANES]
        carry = hg[SUBLANES - 1:SUBLANES]
        groups.append(hg)
    h_ref[...] = carry
    h = jnp.concatenate(groups, axis=0)
    o_ref[...] = (jax.nn.gelu(gate_ref[...]) * h).astype(o_ref.dtype)


def _rglru(proj, params, batch, seq_len, ts):
    t = proj.shape[0]
    nst = seq_len // ts
    sub = ts // SUBLANES
    gate_blk = (C_COLS + 3 * A_WIDTH) // B_WIDTH
    x_blk = gate_blk + 1
    vec = pl.BlockSpec((1, B_WIDTH), lambda b, s: (0, 0))
    sq = pl.BlockSpec((B_WIDTH, B_WIDTH), lambda b, s: (0, 0))
    return pl.pallas_call(
        _lru_kernel,
        grid=(batch, nst),
        in_specs=[pl.BlockSpec((ts, B_WIDTH), lambda b, s: (b * nst + s, gate_blk)),
                  pl.BlockSpec((ts, B_WIDTH), lambda b, s: (b * nst + s, x_blk)),
                  pl.BlockSpec((SUBLANES, B_WIDTH),
                               lambda b, s: (jnp.maximum((b * nst + s) * sub - 1, 0), x_blk)),
                  pl.BlockSpec((CONV_WIDTH, B_WIDTH), lambda b, s: (0, 0)),
                  vec, sq, vec, sq, vec, vec],
        out_specs=pl.BlockSpec((ts, B_WIDTH), lambda b, s: (b * nst + s, 0)),
        out_shape=jax.ShapeDtypeStruct((t, B_WIDTH), BF16),
        scratch_shapes=[pltpu.VMEM((1, B_WIDTH), F32)],
        compiler_params=_cparams(("parallel", "arbitrary")),
        name="rglru",
    )(proj, proj, proj, params["conv_w"], params["conv_b"], params["wa"], params["ba"],
      params["wx"], params["bx"], params["lam"])


def _log_gammas():
    return np.log1p(-np.exp2(-5.0 - np.arange(C_HEADS, dtype=np.float32))).astype(np.float32)


def _ret_kernel(cpr, q_ref, k_ref, v_ref, g_ref, pos_ref, invf_ref, gng_ref, o_ref, st_ref):
    @pl.when(pl.program_id(1) == 0)
    def _():
        st_ref[...] = jnp.zeros_like(st_ref)

    half = C_QK_DIM // 2
    lane = _iota2((CHUNK, LANES), 1)
    in_first_half = (lane & (C_QK_DIM - 1)) < half
    lo = lane < C_QK_DIM
    idx_r = _iota2((CHUNK, CHUNK), 0).astype(F32)
    idx_c = _iota2((CHUNK, CHUNK), 1).astype(F32)
    dist = jnp.abs(idx_r - idx_c)
    pos_in = _iota2((CHUNK, 1), 0).astype(F32)
    log_gamma = _log_gammas()

    def rope(t, cos, sin_signed):
        swapped = jnp.where(in_first_half, pltpu.roll(t, LANES - half, 1), pltpu.roll(t, half, 1))
        return t * cos + swapped * sin_signed

    intra_decay = [jnp.exp(float(lg) * dist) for lg in log_gamma]
    q_decay = [jnp.exp(float(lg) * (pos_in + 1.0)) for lg in log_gamma]
    k_decay = [jnp.exp(float(lg) * (CHUNK - 1.0 - pos_in)) for lg in log_gamma]
    chunk_decay = [float(np.exp(lg * np.float32(CHUNK))) for lg in log_gamma]
    items = [(c, h) for c in range(cpr) for h in range(C_HEADS)]

    ang = pos_ref[...] * invf_ref[...]
    cos_packed = jnp.cos(ang)
    sin_packed = jnp.sin(ang)
    lane_group = lane >> (half.bit_length() - 1)

    def spread(x, c):
        out = x
        for g in range(LANES // half):
            if g != c:
                out = jnp.where(lane_group == g, pltpu.roll(x, (half * (g - c)) % LANES, 1), out)
        return out

    q2, k2 = {}, {}
    for c in range(cpr):
        rows = slice(c * CHUNK, (c + 1) * CHUNK)
        cos = spread(cos_packed, c)
        sin = spread(sin_packed, c)
        sin_signed = jnp.where(in_first_half, -sin, sin)
        for p in range(C_HEADS // 2):
            qk_cols = slice(p * LANES, (p + 1) * LANES)
            q2[c, p] = rope(q_ref[rows, qk_cols], cos, sin_signed)
            k2[c, p] = rope(k_ref[rows, qk_cols], cos, sin_signed) * (C_QK_DIM ** -0.5)

    def mine(x, h):
        return jnp.where(lo, x, 0.0) if h % 2 == 0 else jnp.where(lo, 0.0, x)

    qh = [mine(q2[c, h // 2], h) for c, h in items]
    kh = [mine(k2[c, h // 2], h) for c, h in items]
    vh = [v_ref[c * CHUNK:(c + 1) * CHUNK, h * C_V_DIM:(h + 1) * C_V_DIM].astype(BF16)
          for c, h in items]
    scores = [_dot_nt(q, k) * intra_decay[h] for q, k, (c, h) in zip(qh, kh, items)]
    kv = [_dot_tn(k * k_decay[h], v) for k, v, (c, h) in zip(kh, vh, items)]
    intra = [_dot(s, v) for s, v in zip(scores, vh)]

    state = [st_ref[h] for h in range(C_HEADS)]
    starts = []
    for i, (c, h) in enumerate(items):
        starts.append(state[h])
        state[h] = state[h] * chunk_decay[h] + kv[i]
    for h in range(C_HEADS):
        st_ref[h] = state[h]
    cross = [_dot(q * q_decay[h], s) for q, s, (c, h) in zip(qh, starts, items)]

    for i, (c, h) in enumerate(items):
        rows = slice(c * CHUNK, (c + 1) * CHUNK)
        v_cols = slice(h * C_V_DIM, (h + 1) * C_V_DIM)
        o = intra[i] + cross[i]
        mu = jnp.mean(o, axis=-1, keepdims=True)
        d = o - mu
        var = jnp.mean(d * d, axis=-1, keepdims=True)
        on = d * lax.rsqrt(var + C_GN_EPS) * gng_ref[:, v_cols]
        o_ref[rows, v_cols] = (jax.nn.silu(g_ref[rows, v_cols]) * on).astype(o_ref.dtype)


def _retention(proj, posb, invf, gn_g, batch, seq_len, cpr):
    t = proj.shape[0]
    ts = cpr * CHUNK
    nst = seq_len // ts
    return pl.pallas_call(
        functools.partial(_ret_kernel, cpr),
        grid=(batch, nst),
        in_specs=[pl.BlockSpec((ts, C_QK_WIDTH), lambda b, s: (b * nst + s, 0)),
                  pl.BlockSpec((ts, C_QK_WIDTH), lambda b, s: (b * nst + s, 1)),
                  pl.BlockSpec((ts, C_WIDTH), lambda b, s: (b * nst + s, 1)),
                  pl.BlockSpec((ts, C_WIDTH), lambda b, s: (b * nst + s, 2)),
                  pl.BlockSpec((CHUNK, LANES), lambda b, s: (b * nst + s, 0)),
                  pl.BlockSpec((1, LANES), lambda b, s: (0, 0)),
                  pl.BlockSpec((1, C_WIDTH), lambda b, s: (0, 0))],
        out_specs=pl.BlockSpec((ts, C_WIDTH), lambda b, s: (b * nst + s, 0)),
        out_shape=jax.ShapeDtypeStruct((t, C_WIDTH), BF16),
        scratch_shapes=[pltpu.VMEM((C_HEADS, LANES, C_V_DIM), F32)],
        compiler_params=_cparams(("parallel", "arbitrary")),
        name="retention",
    )(proj, proj, proj, proj, posb, invf, gn_g)


def _outproj_kernel(x_ref, ya_ref, yb_ref, yc_ref, w_ref, o_ref):
    b0, c0 = A_WIDTH, A_WIDTH + B_WIDTH
    acc = jnp.dot(ya_ref[...], w_ref[0:b0, :].astype(BF16), preferred_element_type=F32)
    acc += jnp.dot(yb_ref[...], w_ref[b0:c0, :].astype(BF16), preferred_element_type=F32)
    acc += jnp.dot(yc_ref[...], w_ref[c0:, :].astype(BF16), preferred_element_type=F32)
    o_ref[...] = x_ref[...] + acc


def _outproj(x, ya, yb, yc, w_stack, layer, tm):
    t, d = x.shape

    def tok(width):
        return pl.BlockSpec((tm, width), lambda i: (i, 0))

    return pl.pallas_call(
        _outproj_kernel,
        grid=(t // tm,),
        in_specs=[tok(d), tok(A_WIDTH), tok(B_WIDTH), tok(C_WIDTH),
                  pl.BlockSpec((None, d, d), lambda i: (layer, 0, 0),
                               pipeline_mode=pl.Buffered(1))],
        out_specs=tok(d),
        out_shape=jax.ShapeDtypeStruct((t, d), F32),
        compiler_params=_cparams(("parallel",)),
        name="outproj",
    )(x, ya, yb, yc, w_stack)


def _ffn_kernel(final, x_ref, g_ref, wg_ref, wu_ref, wd_ref, fg_ref, o_ref, h_ref):
    j = pl.program_id(1)

    @pl.when(j == 0)
    def _():
        x = x_ref[...]
        h_ref[...] = _rms(x, g_ref[...]).astype(BF16)
        o_ref[...] = x

    h = h_ref[...]
    gate = jnp.dot(h, wg_ref[...].astype(BF16), preferred_element_type=F32)
    up = jnp.dot(h, wu_ref[...].astype(BF16), preferred_element_type=F32)
    act = (jax.nn.silu(gate) * up).astype(BF16)
    o_ref[...] += jnp.dot(act, wd_ref[...].astype(BF16), preferred_element_type=F32)

    if final:
        @pl.when(j == pl.num_programs(1) - 1)
        def _():
            o_ref[...] = _rms(o_ref[...], fg_ref[...])


def _ffn(x, g, wg, wu, wd, fg, layer, final, tm, tf):
    t, d = x.shape
    f = wg.shape[2]
    return pl.pallas_call(
        functools.partial(_ffn_kernel, final),
        grid=(t // tm, f // tf),
        in_specs=[pl.BlockSpec((tm, d), lambda i, j: (i, 0)),
                  pl.BlockSpec((1, d), lambda i, j: (0, 0)),
                  pl.BlockSpec((None, d, tf), lambda i, j: (layer, 0, j)),
                  pl.BlockSpec((None, d, tf), lambda i, j: (layer, 0, j)),
                  pl.BlockSpec((None, tf, d), lambda i, j: (layer, j, 0)),
                  pl.BlockSpec((1, d), lambda i, j: (0, 0))],
        out_specs=pl.BlockSpec((tm, d), lambda i, j: (i, 0)),
        out_shape=jax.ShapeDtypeStruct((t, d), F32),
        scratch_shapes=[pltpu.VMEM((tm, d), BF16)],
        compiler_params=_cparams(("parallel", "arbitrary")),
        name="ffn_final" if final else "ffn",
    )(x, g, wg, wu, wd, fg)


def _rows_at(w, offset):
    return jnp.pad(w, ((offset, LORA_PACK - offset - w.shape[0]), (0, 0)))


RELAYOUT_UNIT = 64
RELAYOUT_PARTS = 8


def _source_unit(q):
    c_units = C_COLS // RELAYOUT_UNIT
    a_units = 3 * A_WIDTH // RELAYOUT_UNIT
    b_units = B_COLS // RELAYOUT_UNIT
    last_src = (A_COLS + B_COLS + C_COLS) // RELAYOUT_UNIT - 1
    src_c = (A_COLS + B_COLS) // RELAYOUT_UNIT
    src_b = A_COLS // RELAYOUT_UNIT
    lora = jnp.minimum(a_units + (q - c_units - a_units - b_units), last_src)
    return jnp.where(q < c_units, src_c + q,
                     jnp.where(q < c_units + a_units, q - c_units,
                               jnp.where(q < c_units + a_units + b_units,
                                         src_b + (q - c_units - a_units), lora)))


def _relayout_kernel(has_vres, n_steps, *refs):
    parts, rest = refs[:RELAYOUT_PARTS], refs[RELAYOUT_PARTS:]
    o_ref = rest[-1]
    blocks = [p[...] for p in parts]
    tail = rest[0][...] if has_vres else jnp.zeros_like(blocks[-1])
    blocks[-1] = jnp.where(pl.program_id(0) == n_steps - 1, tail, blocks[-1])
    o_ref[...] = jnp.concatenate(blocks, axis=0).T.astype(BF16)


def _layer_in_weight(w_in_t, w_vres_t, layer):
    _, n, d = w_in_t.shape
    has_vres = layer > 0
    n_steps = P_COLS // (RELAYOUT_UNIT * RELAYOUT_PARTS)

    def part(s):
        return pl.BlockSpec((None, RELAYOUT_UNIT, d),
                            lambda j: (layer, _source_unit(RELAYOUT_PARTS * j + s), 0))

    ins = [w_in_t] * RELAYOUT_PARTS
    specs = [part(s) for s in range(RELAYOUT_PARTS)]
    if has_vres:
        ins.append(w_vres_t)
        specs.append(pl.BlockSpec((None, VRES_LORA, d), lambda j: (layer - 1, 0, 0)))
    return pl.pallas_call(
        functools.partial(_relayout_kernel, has_vres, n_steps),
        grid=(n_steps,),
        in_specs=specs,
        out_specs=pl.BlockSpec((d, RELAYOUT_UNIT * RELAYOUT_PARTS), lambda j: (0, j)),
        out_shape=jax.ShapeDtypeStruct((d, P_COLS), BF16),
        compiler_params=_cparams(("parallel",)),
        name="w_in_relayout",
    )(*ins)


def _block_diag(w):
    g, n, _ = w.shape
    eye = jnp.eye(g, dtype=w.dtype)
    return (eye[:, None, :, None] * w[:, :, None, :]).reshape(g * n, g * n)


def kernel(x, positions, norm1_g, w_in, tshift_mu, rwkv_w0, rwkv_w2, rwkv_a0, rwkv_a2, rwkv_g2, rwkv_k_k, rwkv_k_a, rwkv_r_k, rwkv_ln_g, rwkv_ln_b, w_in_vres, tshift_mu_vres, rwkv_v0, rwkv_v2, lru_conv_w, lru_conv_b, lru_wa, lru_ba, lru_wx, lru_bx, lru_lambda, ret_gn_g, w_out, norm2_g, ffn_w_gate, ffn_w_up, ffn_w_down, final_norm_g):
    batch, seq_len, d = x.shape
    depth = w_in.shape[0]
    t = batch * seq_len
    aw = A_WIDTH
    row = lambda v: v.reshape(1, -1).astype(F32)

    tm_proj = min(512, seq_len)
    tm_in = min(1024, seq_len)
    tm_tok = min(256, seq_len)
    ts_lru = min(256, seq_len)
    cps = min(8, seq_len // CHUNK)
    cpr = min(4, seq_len // CHUNK)

    inv_freq = ROPE_THETA ** (-jnp.arange(0, C_QK_DIM, 2, dtype=F32) / C_QK_DIM)
    invf = jnp.tile(inv_freq, LANES // inv_freq.shape[0]).reshape(1, LANES)
    n_freq = inv_freq.shape[0]
    assert cpr * n_freq == LANES
    posb = positions.astype(F32).reshape(t // (cpr * CHUNK), cpr, CHUNK)
    posb = jnp.repeat(jnp.swapaxes(posb, 1, 2), n_freq, axis=-1).reshape(t // cpr, LANES)

    w_in_t = jnp.swapaxes(w_in, 1, 2)
    w_vres_t = jnp.swapaxes(w_in_vres, 1, 2)

    xf = x.reshape(t, d)
    v_first = None
    for l in range(depth):
        has_vres = l > 0
        w_l = _layer_in_weight(w_in_t, w_vres_t, l)
        proj = _inproj(xf, row(norm1_g[l]), w_l, tm_in, P_COLS // 6)

        mu = tshift_mu[l]
        o_ad, o_gd = 3 * aw + DECAY_LORA, 3 * aw + DECAY_LORA + ICLR_LORA
        mu_vd = tshift_mu_vres[l - 1] if has_vres else jnp.zeros((VRES_LORA,), F32)
        params = {
            "mu_rkv": row(mu[:3 * aw]),
            "mu_lora": row(jnp.concatenate([mu[3 * aw:A_COLS], mu_vd])),
            "w0": row(rwkv_w0[l]), "a0": row(rwkv_a0[l]), "k_k": row(rwkv_k_k[l]),
            "k_a": row(rwkv_k_a[l]), "r_k": row(rwkv_r_k[l]),
            "w2": _rows_at(rwkv_w2[l], 0), "a2": _rows_at(rwkv_a2[l], o_ad - 3 * aw),
            "g2": _rows_at(rwkv_g2[l], o_gd - 3 * aw),
        }
        if has_vres:
            params["v0"] = row(rwkv_v0[l - 1])
            params["v2"] = _rows_at(rwkv_v2[l - 1], A_COLS - 3 * aw)
        at, rt, kt, bt, kd, bd, v_b, pc, g_a, bonus, *v_f32 = _rwkv_prep(
            proj, params, v_first if has_vres else None, seq_len, tm_tok)
        if l == 0:
            v_first = v_f32[0]
        y_a = _rwkv_scan(at, rt, kt, bt, kd, bd, v_b, pc, g_a, bonus,
                         row(rwkv_ln_g[l]), row(rwkv_ln_b[l]), batch, seq_len, cps)

        lru_params = {
            "conv_w": lru_conv_w[l], "conv_b": row(lru_conv_b[l]),
            "wa": _block_diag(lru_wa[l]).astype(BF16), "ba": row(lru_ba[l]),
            "wx": _block_diag(lru_wx[l]).astype(BF16), "bx": row(lru_bx[l]),
            "lam": row(lru_lambda[l]),
        }
        y_b = _rglru(proj, lru_params, batch, seq_len, ts_lru)
        y_c = _retention(proj, posb, invf, row(ret_gn_g[l]), batch, seq_len, cpr)

        xf = _outproj(xf, y_a, y_b, y_c, w_out, l, tm_proj)
        xf = _ffn(xf, row(norm2_g[l]), ffn_w_gate, ffn_w_up, ffn_w_down, row(final_norm_g),
                  l, l == depth - 1, tm_in, 256)
    return xf.reshape(batch, seq_len, d)
```

```python
import functools

import numpy as np
import jax
import jax.numpy as jnp
from jax import lax
from jax.experimental import pallas as pl
from jax.experimental.pallas import tpu as pltpu

F32 = jnp.float32
BF16 = jnp.bfloat16

NORM_EPS = 1e-6
CHUNK = 64

A_HEADS = 12
A_HEAD_DIM = 64
A_WIDTH = A_HEADS * A_HEAD_DIM
DECAY_LORA = 96
ICLR_LORA = 96
VRES_LORA = 64
GATE_LORA = 256
A_GN_EPS = 64e-5
A_COLS = 3 * A_WIDTH + DECAY_LORA + ICLR_LORA + GATE_LORA
B_WIDTH = 512
B_BLOCKS = 8
B_BLOCK_DIM = B_WIDTH // B_BLOCKS
CONV_WIDTH = 4
LRU_C = 8.0
B_COLS = 2 * B_WIDTH
C_HEADS = 6
C_QK_DIM = 64
C_V_DIM = 128
C_QK_WIDTH = C_HEADS * C_QK_DIM
C_WIDTH = C_HEADS * C_V_DIM
C_GN_EPS = 1e-5
ROPE_THETA = 10000.0
C_COLS = 2 * C_QK_WIDTH + 2 * C_WIDTH

LANES = 128
SUBLANES = 8
LORA_PACK = DECAY_LORA + ICLR_LORA + GATE_LORA + VRES_LORA
VMEM_LIMIT = 56 * 1024 * 1024

P_COLS = C_COLS + 3 * A_WIDTH + B_COLS + LORA_PACK


def _cparams(sem):
    return pltpu.CompilerParams(dimension_semantics=sem, vmem_limit_bytes=VMEM_LIMIT)


def _dot(a, b):
    return jnp.dot(a.astype(BF16), b.astype(BF16), preferred_element_type=F32)


def _dot_nt(a, b):
    return lax.dot_general(a.astype(BF16), b.astype(BF16), (((1,), (1,)), ((), ())),
                           preferred_element_type=F32)


def _dot_tn(a, b):
    return lax.dot_general(a.astype(BF16), b.astype(BF16), (((0,), (0,)), ((), ())),
                           preferred_element_type=F32)


def _iota2(shape, axis):
    return lax.broadcasted_iota(jnp.int32, shape, axis)


def _blk(idx, size):
    return idx >> (size.bit_length() - 1)


def _rms(x, g):
    ms = jnp.mean(x * x, axis=-1, keepdims=True)
    return x * lax.rsqrt(ms + NORM_EPS) * g


def _inproj_kernel(x_ref, g_ref, w_ref, o_ref, h_ref):
    @pl.when(pl.program_id(1) == 0)
    def _():
        h_ref[...] = _rms(x_ref[...], g_ref[...]).astype(BF16)

    o_ref[...] = jnp.dot(h_ref[...], w_ref[...], preferred_element_type=F32)


def _inproj(x, g, w, tm, tn):
    t, d = x.shape
    n = w.shape[1]
    return pl.pallas_call(
        _inproj_kernel,
        grid=(t // tm, n // tn),
        in_specs=[pl.BlockSpec((tm, d), lambda i, j: (i, 0)),
                  pl.BlockSpec((1, d), lambda i, j: (0, 0)),
                  pl.BlockSpec((d, tn), lambda i, j: (0, j))],
        out_specs=pl.BlockSpec((tm, tn), lambda i, j: (i, j)),
        out_shape=jax.ShapeDtypeStruct((t, n), F32),
        scratch_shapes=[pltpu.VMEM((tm, d), BF16)],
        compiler_params=_cparams(("parallel", "arbitrary")),
        name="inproj",
    )(x, g, w)


def _shift_rows(cur, prev_row):
    rolled = pltpu.roll(cur, 1, 0)
    top = rolled[:SUBLANES]
    top = jnp.where(_iota2(top.shape, 0) == 0, prev_row, top)
    return jnp.concatenate([top, rolled[SUBLANES:]], axis=0)


def _split3(x):
    hi = x.astype(BF16)
    r1 = x - hi.astype(F32)
    mid = r1.astype(BF16)
    lo = (r1 - mid.astype(F32)).astype(BF16)
    return hi, mid, lo


def _sel_dot(sel, x):
    hi, mid, lo = _split3(x)
    d = functools.partial(jnp.dot, sel, preferred_element_type=F32)
    return (d(lo) + d(mid)) + d(hi)


def _dot_sel(x, sel):
    hi = x.astype(BF16)
    mid = (x - hi.astype(F32)).astype(BF16)
    d = functools.partial(jnp.dot, preferred_element_type=F32)
    return d(mid, sel) + d(hi, sel)


def _head_sum(x, e):
    parts = [_dot_sel(x[:, i * LANES:(i + 1) * LANES], e) for i in range(x.shape[1] // LANES)]
    return jnp.concatenate(parts, axis=-1)


def _head_ones():
    r = _blk(_iota2((LANES, LANES), 0), A_HEAD_DIM)
    c = _blk(_iota2((LANES, LANES), 1), A_HEAD_DIM)
    return (r == c).astype(BF16)


def _prep_kernel(has_vres, seq_tiles, *refs):
    if has_vres:
        (rkv_ref, lora_ref, prkv_ref, plora_ref, mu_rkv, mu_lora,
         w0, a0, k_k, k_a, r_k, w2, a2, g2, v0, v2, vfirst_ref,
         at_o, rt_o, kt_o, bt_o, kd_o, bd_o, vb_o, pc_o, g_o, bonus_o) = refs
    else:
        (rkv_ref, lora_ref, prkv_ref, plora_ref, mu_rkv, mu_lora,
         w0, a0, k_k, k_a, r_k, w2, a2, g2,
         at_o, rt_o, kt_o, bt_o, kd_o, bd_o, vb_o, pc_o, g_o, bonus_o, v_o) = refs
    first = (pl.program_id(0) % seq_tiles) == 0
    last_row = slice(SUBLANES - 1, SUBLANES)

    def mixed(cur_ref, prev_ref, mu_ref):
        cur = cur_ref[...]
        prev_row = jnp.where(first, 0.0, prev_ref[last_row, :])
        return cur + (_shift_rows(cur, prev_row) - cur) * mu_ref[...]

    rkv = mixed(rkv_ref, prkv_ref, mu_rkv)
    lora = mixed(lora_ref, plora_ref, mu_lora)
    r = rkv[:, 0:A_WIDTH]
    k = rkv[:, A_WIDTH:2 * A_WIDTH]
    v = rkv[:, 2 * A_WIDTH:3 * A_WIDTH]
    zw = -(w0[...] + _dot(jnp.tanh(lora), w2[...]))
    softplus = jnp.maximum(zw, 0.0) + jnp.log1p(jnp.exp(-jnp.abs(zw)))
    lw = -jnp.exp(-softplus - 0.5)
    a = jax.nn.sigmoid(a0[...] + _dot(lora, a2[...]))
    g = _dot(jax.nn.sigmoid(lora), g2[...])
    if has_vres:
        v = v + (vfirst_ref[...] - v) * jax.nn.sigmoid(v0[...] + _dot(lora, v2[...]))

    e = _head_ones()
    kk = k * k_k[...]
    kk = kk * lax.rsqrt(jnp.maximum(_head_sum(kk * kk, e), 1e-24))
    k = k * (1.0 + (a - 1.0) * k_a[...])
    b = kk * a
    bonus = _head_sum(r * k * r_k[...], e) * v

    tm = lw.shape[0]
    row = _iota2((tm, tm), 0)
    col = _iota2((tm, tm), 1)
    tri = (_blk(row, CHUNK) == _blk(col, CHUNK)) & (row >= col)
    cum = _sel_dot(tri.astype(BF16), lw)
    cum3 = cum.reshape(tm // CHUNK, CHUNK, A_WIDTH)
    tot = jnp.broadcast_to(cum3[:, CHUNK - 1:CHUNK, :], cum3.shape).reshape(tm, A_WIDTH)
    e_in = jnp.exp(cum)
    e_out = jnp.exp(-cum)
    e_end = jnp.exp(tot - cum)
    at_o[...] = (-kk * jnp.exp(cum - lw)).astype(BF16)
    rt_o[...] = (r * e_in).astype(BF16)
    kt_o[...] = (k * e_out).astype(BF16)
    bt_o[...] = (b * e_out).astype(BF16)
    kd_o[...] = (k * e_end).astype(BF16)
    bd_o[...] = (b * e_end).astype(BF16)
    vb_o[...] = v.astype(BF16)
    for c in range(tm // CHUNK):
        pc_o[c] = jnp.exp(cum3[c, CHUNK - 1:CHUNK, :])
    g_o[...] = g
    bonus_o[...] = bonus
    if not has_vres:
        v_o[...] = v


def _rwkv_prep(proj, params, v_first, seq_len, tm):
    t = proj.shape[0]
    has_vres = v_first is not None
    seq_tiles = seq_len // tm
    sub = tm // SUBLANES
    rkv_w, lora_w = 3 * A_WIDTH, LORA_PACK
    rkv_blk = (C_COLS) // rkv_w
    lora_blk = (C_COLS + rkv_w + B_COLS) // lora_w

    def cur(width, blk):
        return pl.BlockSpec((tm, width), lambda i: (i, blk))

    def prev(width, blk):
        return pl.BlockSpec((SUBLANES, width), lambda i: (jnp.maximum(i * sub - 1, 0), blk))

    def vec(width):
        return pl.BlockSpec((1, width), lambda i: (0, 0))

    def mat(rows):
        return pl.BlockSpec((rows, A_WIDTH), lambda i: (0, 0))

    tok = pl.BlockSpec((tm, A_WIDTH), lambda i: (i, 0))
    ins = [proj, proj, proj, proj,
           params["mu_rkv"], params["mu_lora"],
           params["w0"], params["a0"], params["k_k"], params["k_a"], params["r_k"],
           params["w2"], params["a2"], params["g2"]]
    specs = [cur(rkv_w, rkv_blk), cur(lora_w, lora_blk),
             prev(rkv_w, rkv_blk), prev(lora_w, lora_blk),
             vec(rkv_w), vec(lora_w),
             vec(A_WIDTH), vec(A_WIDTH), vec(A_WIDTH), vec(A_WIDTH), vec(A_WIDTH),
             mat(LORA_PACK), mat(LORA_PACK), mat(LORA_PACK)]
    if has_vres:
        ins += [params["v0"], params["v2"], v_first]
        specs += [vec(A_WIDTH), mat(LORA_PACK), tok]
    wide = lambda dtype: jax.ShapeDtypeStruct((t, A_WIDTH), dtype)
    pc_spec = pl.BlockSpec((tm // CHUNK, 1, A_WIDTH), lambda i: (i, 0, 0))
    out_specs = [tok] * 7 + [pc_spec, tok, tok]
    out_shape = [wide(BF16)] * 7 + [jax.ShapeDtypeStruct((t // CHUNK, 1, A_WIDTH), F32),
                                    wide(F32), wide(F32)]
    if not has_vres:
        out_specs.append(tok)
        out_shape.append(wide(F32))
    return pl.pallas_call(
        functools.partial(_prep_kernel, has_vres, seq_tiles),
        grid=(t // tm,),
        in_specs=specs,
        out_specs=out_specs,
        out_shape=out_shape,
        compiler_params=_cparams(("parallel",)),
        name="rwkv_prep_vres" if has_vres else "rwkv_prep",
    )(*ins)


def _unit_lower_inverse(a_strict, row, col, eye, pair_diag):
    base = 8
    diag = _blk(row, base) == _blk(col, base)
    n1 = [jnp.where(diag, a, 0.0) for a in a_strict]
    n1d = [pair_diag(x) for x in n1]
    n2 = [_dot(x, d) for x, d in zip(n1, n1d)]
    n2d = [pair_diag(x) for x in n2]
    t = [eye + x for x in n1]
    t = [x + _dot(x, d) for x, d in zip(t, n2d)]
    n4d = [pair_diag(_dot(x, d)) for x, d in zip(n2, n2d)]
    t = [x + _dot(x, d) for x, d in zip(t, n4d)]
    m = base
    while m < CHUNK:
        sel = (_blk(row, 2 * m) == _blk(col, 2 * m)) & (_blk(row, m) != _blk(col, m))
        ed = [pair_diag(jnp.where(sel, a, 0.0)) for a in a_strict]
        te = [_dot(x, d) for x, d in zip(t, ed)]
        t = [x + _dot(y, pair_diag(x)) for x, y in zip(t, te)]
        m *= 2
    return t


def _scan_kernel(cps, at_ref, rt_ref, kt_ref, bt_ref, kd_ref, bd_ref, v_ref, pc_ref,
                 gate_ref, bonus_ref, lng_ref, lnb_ref, o_ref, s_ref):
    @pl.when(pl.program_id(1) == 0)
    def _():
        s_ref[...] = jnp.zeros_like(s_ref)

    n = 2 * A_HEAD_DIM
    pairs = A_HEADS // 2
    row = _iota2((CHUNK, n), 0)
    lane = _iota2((CHUNK, n), 1)
    col = lane & (A_HEAD_DIM - 1)
    lo = lane < A_HEAD_DIM
    strict = row > col
    incl = row >= col
    eye = (row == col).astype(F32)
    same_head = _blk(_iota2((n, n), 0), A_HEAD_DIM) == _blk(_iota2((n, n), 1), A_HEAD_DIM)
    items = [(c, p) for c in range(cps) for p in range(pairs)]

    def load(ref):
        return [ref[0, c * CHUNK:(c + 1) * CHUNK, p * n:(p + 1) * n] for c, p in items]

    def pair_diag(x):
        x = x.astype(BF16)
        zero = jnp.zeros_like(x)
        return jnp.concatenate([jnp.where(lo, x, zero), jnp.where(lo, zero, x)], axis=0)

    def fold(m):
        return jnp.where(lo, m[:CHUNK], m[CHUNK:])

    at = load(at_ref)
    rt = load(rt_ref)
    lhs = [jnp.concatenate([a, r], axis=0) for a, r in zip(at, rt)]
    lk = [_dot_nt(x, pair_diag(k)) for x, k in zip(lhs, load(kt_ref))]
    lb = [_dot_nt(x, pair_diag(b)) for x, b in zip(lhs, load(bt_ref))]
    a_k = [jnp.concatenate([jnp.where(strict, x[:CHUNK], 0.0), jnp.where(incl, x[CHUNK:], 0.0)],
                           axis=0) for x in lk]
    a_ab = [jnp.where(strict, x[:CHUNK], 0.0) for x in lb]
    a_rb = [jnp.where(incl, x[CHUNK:], 0.0) for x in lb]
    t_inv = _unit_lower_inverse(a_ab, row, col, eye, pair_diag)
    v = load(v_ref)
    lv = [_dot(a, pair_diag(x)) for a, x in zip(a_k, v)]
    wu = [_dot(t, jnp.concatenate([pair_diag(a), pair_diag(x[:CHUNK])], axis=1))
          for t, a, x in zip(t_inv, at, lv)]
    z = [_dot(a, jnp.concatenate([pair_diag(x[:, :n]), pair_diag(x[:, n:])], axis=1))
         for a, x in zip(a_rb, wu)]
    rw = [r.astype(F32) + x[:, :n] for r, x in zip(rt, z)]
    y0 = [x[CHUNK:] + zz[:, n:] for x, zz in zip(lv, z)]
    bd = load(bd_ref)
    g = [jnp.where(same_head, _dot_tn(x[:, :n], b), 0.0).astype(BF16)
         for x, b in zip(wu, bd)]
    s_add = [fold(_dot_tn(jnp.concatenate([x[:, n:].astype(BF16), vv], axis=0),
                          jnp.concatenate([b, k], axis=0)))
             for x, vv, b, k in zip(wu, v, bd, load(kd_ref))]

    inv_n = 1.0 / A_HEAD_DIM

    def head_mean(x):
        m_lo = jnp.sum(jnp.where(lo, x, 0.0), axis=-1, keepdims=True)
        m_hi = jnp.sum(jnp.where(lo, 0.0, x), axis=-1, keepdims=True)
        return jnp.where(lo, m_lo, m_hi) * inv_n

    s = [s_ref[p] for p in range(pairs)]
    for c in range(cps):
        rows = slice(c * CHUNK, (c + 1) * CHUNK)
        sd = [pair_diag(x) for x in s]
        ys = [_dot_nt(rw[c * pairs + p], sd[p]) + y0[c * pairs + p] for p in range(pairs)]
        for p in range(pairs):
            i = c * pairs + p
            cols = slice(p * n, (p + 1) * n)
            pc = pc_ref[0, c, :, cols]
            s[p] = s[p] * pc + _dot(s[p], g[i]) + s_add[i]
        for p in range(pairs):
            cols = slice(p * n, (p + 1) * n)
            d = ys[p] - head_mean(ys[p])
            var = head_mean(d * d)
            yn = d * lax.rsqrt(var + A_GN_EPS) * lng_ref[:, cols] + lnb_ref[:, cols]
            o_ref[0, rows, cols] = ((yn + bonus_ref[0, rows, cols])
                                    * gate_ref[0, rows, cols]).astype(o_ref.dtype)
    for p in range(pairs):
        s_ref[p] = s[p]


def _rwkv_scan(at, rt, kt, bt, kd, bd, v, pc, gate, bonus, ln_g, ln_b, batch, seq_len, cps):
    ts = cps * CHUNK
    shp = (batch, seq_len, A_WIDTH)
    blk = pl.BlockSpec((1, ts, A_WIDTH), lambda b, s: (b, s, 0))
    pc_blk = pl.BlockSpec((1, cps, 1, A_WIDTH), lambda b, s: (b, s, 0, 0))
    vec = pl.BlockSpec((1, A_WIDTH), lambda b, s: (0, 0))
    args = [z.reshape(shp) for z in (at, rt, kt, bt, kd, bd, v)]
    args += [pc.reshape(batch, seq_len // CHUNK, 1, A_WIDTH), gate.reshape(shp), bonus.reshape(shp)]
    y = pl.pallas_call(
        functools.partial(_scan_kernel, cps),
        grid=(batch, seq_len // ts),
        in_specs=[blk] * 7 + [pc_blk, blk, blk, vec, vec],
        out_specs=blk,
        out_shape=jax.ShapeDtypeStruct(shp, BF16),
        scratch_shapes=[pltpu.VMEM((A_HEADS // 2, A_HEAD_DIM, 2 * A_HEAD_DIM), F32)],
        compiler_params=_cparams(("parallel", "arbitrary")),
        name="rwkv_scan",
    )(*args, ln_g, ln_b)
    return y.reshape(batch * seq_len, A_WIDTH)


def _lru_body(first, gate_ref, x_ref, px_ref, cw_ref, cb_ref, wa_ref, ba_ref, wx_ref, bx_ref,
              lam_ref, o_ref, h_ref):
    x = x_ref[...]
    ts = x.shape[0]
    prev = jnp.where(first, 0.0, px_ref[...])
    row = _iota2(x.shape, 0)
    row8 = _iota2(prev.shape, 0)

    def delayed(d):
        rolled = pltpu.roll(x, d, 0)
        top = jnp.where(row8 < d, pltpu.roll(prev, d, 0), rolled[:SUBLANES])
        return jnp.concatenate([top, rolled[SUBLANES:]], axis=0)

    xc = cw_ref[CONV_WIDTH - 1:CONV_WIDTH, :] * x + cb_ref[...]
    for d in range(1, CONV_WIDTH):
        xc = xc + cw_ref[CONV_WIDTH - 1 - d:CONV_WIDTH - d, :] * delayed(d)

    r = jax.nn.sigmoid(_dot(xc, wa_ref[...]) + ba_ref[...])
    i = jax.nn.sigmoid(_dot(xc, wx_ref[...]) + bx_ref[...])
    log_a = -LRU_C * r * jax.nn.softplus(-lam_ref[...])
    a = jnp.exp(log_a)
    th = jnp.tanh(log_a)
    u = jnp.sqrt(-2.0 * th / (1.0 - th)) * (i * xc)

    grouped = (ts // SUBLANES, SUBLANES, B_WIDTH)
    in_group = _iota2(grouped, 1)

    def group_delay(z, d, fill):
        return jnp.where(in_group >= d, pltpu.roll(z.reshape(grouped), d, 1), fill).reshape(z.shape)

    d = 1
    while d < SUBLANES:
        u = a * group_delay(u, d, 0.0) + u
        a = a * group_delay(a, d, 1.0)
        d *= 2
    carry = h_ref[...]
    groups = []
    for g0 in range(0, ts, SUBLANES):
        hg = a[g0:g0 + SUBLANES] * carry + u[g0:g0 + SUBLANES]
        carry = hg[SUBLANES - 1:SUBLANES]
        groups.append(hg)
    h_ref[...] = carry
    h = jnp.concatenate(groups, axis=0)
    o_ref[...] = (jax.nn.gelu(gate_ref[...]) * h).astype(o_ref.dtype)


def _log_gammas():
    return np.log1p(-np.exp2(-5.0 - np.arange(C_HEADS, dtype=np.float32))).astype(np.float32)


def _ret_body(cpr, q_ref, k_ref, v_ref, g_ref, pos_ref, invf_ref, gng_ref, o_ref, st_ref):
    half = C_QK_DIM // 2
    lane = _iota2((CHUNK, LANES), 1)
    in_first_half = (lane & (C_QK_DIM - 1)) < half
    lo = lane < C_QK_DIM
    idx_r = _iota2((CHUNK, CHUNK), 0).astype(F32)
    idx_c = _iota2((CHUNK, CHUNK), 1).astype(F32)
    dist = jnp.abs(idx_r - idx_c)
    pos_in = _iota2((CHUNK, 1), 0).astype(F32)
    log_gamma = _log_gammas()

    def rope(t, cos, sin_signed):
        swapped = jnp.where(in_first_half, pltpu.roll(t, LANES - half, 1), pltpu.roll(t, half, 1))
        return t * cos + swapped * sin_signed

    intra_decay = [jnp.exp(float(lg) * dist) for lg in log_gamma]
    q_decay = [jnp.exp(float(lg) * (pos_in + 1.0)) for lg in log_gamma]
    k_decay = [jnp.exp(float(lg) * (CHUNK - 1.0 - pos_in)) for lg in log_gamma]
    chunk_decay = [float(np.exp(lg * np.float32(CHUNK))) for lg in log_gamma]
    items = [(c, h) for c in range(cpr) for h in range(C_HEADS)]

    ang = pos_ref[...] * invf_ref[...]
    cos_packed = jnp.cos(ang)
    sin_packed = jnp.sin(ang)
    lane_group = lane >> (half.bit_length() - 1)

    def spread(x, c):
        out = x
        for g in range(LANES // half):
            if g != c:
                out = jnp.where(lane_group == g, pltpu.roll(x, (half * (g - c)) % LANES, 1), out)
        return out

    q2, k2 = {}, {}
    for c in range(cpr):
        rows = slice(c * CHUNK, (c + 1) * CHUNK)
        cos = spread(cos_packed, c)
        sin = spread(sin_packed, c)
        sin_signed = jnp.where(in_first_half, -sin, sin)
        for p in range(C_HEADS // 2):
            qk_cols = slice(p * LANES, (p + 1) * LANES)
            q2[c, p] = rope(q_ref[rows, qk_cols], cos, sin_signed)
            k2[c, p] = rope(k_ref[rows, qk_cols], cos, sin_signed) * (C_QK_DIM ** -0.5)

    def mine(x, h):
        return jnp.where(lo, x, 0.0) if h % 2 == 0 else jnp.where(lo, 0.0, x)

    qh = [mine(q2[c, h // 2], h) for c, h in items]
    kh = [mine(k2[c, h // 2], h) for c, h in items]
    vh = [v_ref[c * CHUNK:(c + 1) * CHUNK, h * C_V_DIM:(h + 1) * C_V_DIM].astype(BF16)
          for c, h in items]
    scores = [_dot_nt(q, k) * intra_decay[h] for q, k, (c, h) in zip(qh, kh, items)]
    kv = [_dot_tn(k * k_decay[h], v) for k, v, (c, h) in zip(kh, vh, items)]
    intra = [_dot(s, v) for s, v in zip(scores, vh)]

    state = [st_ref[h] for h in range(C_HEADS)]
    starts = []
    for i, (c, h) in enumerate(items):
        starts.append(state[h])
        state[h] = state[h] * chunk_decay[h] + kv[i]
    for h in range(C_HEADS):
        st_ref[h] = state[h]
    cross = [_dot(q * q_decay[h], s) for q, s, (c, h) in zip(qh, starts, items)]

    for i, (c, h) in enumerate(items):
        rows = slice(c * CHUNK, (c + 1) * CHUNK)
        v_cols = slice(h * C_V_DIM, (h + 1) * C_V_DIM)
        o = intra[i] + cross[i]
        mu = jnp.mean(o, axis=-1, keepdims=True)
        d = o - mu
        var = jnp.mean(d * d, axis=-1, keepdims=True)
        on = d * lax.rsqrt(var + C_GN_EPS) * gng_ref[:, v_cols]
        o_ref[rows, v_cols] = (jax.nn.silu(g_ref[rows, v_cols]) * on).astype(o_ref.dtype)


def _mix_bc_kernel(cpr, gate_ref, x_ref, px_ref, cw_ref, cb_ref, wa_ref, ba_ref, wx_ref, bx_ref,
                   lam_ref, q_ref, k_ref, v_ref, g_ref, pos_ref, invf_ref, gng_ref,
                   yb_ref, yc_ref, h_ref, st_ref):
    first = pl.program_id(1) == 0

    @pl.when(first)
    def _():
        h_ref[...] = jnp.zeros_like(h_ref)
        st_ref[...] = jnp.zeros_like(st_ref)

    _ret_body(cpr, q_ref, k_ref, v_ref, g_ref, pos_ref, invf_ref, gng_ref, yc_ref, st_ref)
    _lru_body(first, gate_ref, x_ref, px_ref, cw_ref, cb_ref, wa_ref, ba_ref, wx_ref, bx_ref,
              lam_ref, yb_ref, h_ref)


def _mix_bc(proj, lru, posb, invf, gn_g, batch, seq_len, cpr):
    t = proj.shape[0]
    ts = cpr * CHUNK
    nst = seq_len // ts
    sub = ts // SUBLANES
    gate_blk = (C_COLS + 3 * A_WIDTH) // B_WIDTH
    x_blk = gate_blk + 1

    def tok(width, blk):
        return pl.BlockSpec((ts, width), lambda b, s: (b * nst + s, blk))

    def const(shape):
        return pl.BlockSpec(shape, lambda b, s: (0,) * len(shape))

    return pl.pallas_call(
        functools.partial(_mix_bc_kernel, cpr),
        grid=(batch, nst),
        in_specs=[tok(B_WIDTH, gate_blk), tok(B_WIDTH, x_blk),
                  pl.BlockSpec((SUBLANES, B_WIDTH),
                               lambda b, s: (jnp.maximum((b * nst + s) * sub - 1, 0), x_blk)),
                  const((CONV_WIDTH, B_WIDTH)), const((1, B_WIDTH)),
                  const((B_WIDTH, B_WIDTH)), const((1, B_WIDTH)),
                  const((B_WIDTH, B_WIDTH)), const((1, B_WIDTH)), const((1, B_WIDTH)),
                  tok(C_QK_WIDTH, 0), tok(C_QK_WIDTH, 1), tok(C_WIDTH, 1), tok(C_WIDTH, 2),
                  pl.BlockSpec((CHUNK, LANES), lambda b, s: (b * nst + s, 0)),
                  const((1, LANES)), const((1, C_WIDTH))],
        out_specs=[tok(B_WIDTH, 0), tok(C_WIDTH, 0)],
        out_shape=[jax.ShapeDtypeStruct((t, B_WIDTH), BF16),
                   jax.ShapeDtypeStruct((t, C_WIDTH), BF16)],
        scratch_shapes=[pltpu.VMEM((1, B_WIDTH), F32),
                        pltpu.VMEM((C_HEADS, LANES, C_V_DIM), F32)],
        compiler_params=_cparams(("parallel", "arbitrary")),
        name="mix_bc",
    )(proj, proj, proj, lru["conv_w"], lru["conv_b"], lru["wa"], lru["ba"], lru["wx"],
      lru["bx"], lru["lam"], proj, proj, proj, proj, posb, invf, gn_g)


def _outproj_kernel(x_ref, ya_ref, yb_ref, yc_ref, w_ref, o_ref):
    b0, c0 = A_WIDTH, A_WIDTH + B_WIDTH
    acc = jnp.dot(ya_ref[...], w_ref[0:b0, :].astype(BF16), preferred_element_type=F32)
    acc += jnp.dot(yb_ref[...], w_ref[b0:c0, :].astype(BF16), preferred_element_type=F32)
    acc += jnp.dot(yc_ref[...], w_ref[c0:, :].astype(BF16), preferred_element_type=F32)
    o_ref[...] = x_ref[...] + acc


def _outproj(x, ya, yb, yc, w_stack, layer, tm):
    t, d = x.shape

    def tok(width):
        return pl.BlockSpec((tm, width), lambda i: (i, 0))

    return pl.pallas_call(
        _outproj_kernel,
        grid=(t // tm,),
        in_specs=[tok(d), tok(A_WIDTH), tok(B_WIDTH), tok(C_WIDTH),
                  pl.BlockSpec((None, d, d), lambda i: (layer, 0, 0),
                               pipeline_mode=pl.Buffered(1))],
        out_specs=tok(d),
        out_shape=jax.ShapeDtypeStruct((t, d), F32),
        compiler_params=_cparams(("parallel",)),
        name="outproj",
    )(x, ya, yb, yc, w_stack)


def _ffn_kernel(final, x_ref, g_ref, wg_ref, wu_ref, wd_ref, fg_ref, o_ref, h_ref):
    j = pl.program_id(1)

    @pl.when(j == 0)
    def _():
        x = x_ref[...]
        h_ref[...] = _rms(x, g_ref[...]).astype(BF16)
        o_ref[...] = x

    h = h_ref[...]
    gate = jnp.dot(h, wg_ref[...].astype(BF16), preferred_element_type=F32)
    up = jnp.dot(h, wu_ref[...].astype(BF16), preferred_element_type=F32)
    act = (jax.nn.silu(gate) * up).astype(BF16)
    o_ref[...] += jnp.dot(act, wd_ref[...].astype(BF16), preferred_element_type=F32)

    if final:
        @pl.when(j == pl.num_programs(1) - 1)
        def _():
            o_ref[...] = _rms(o_ref[...], fg_ref[...])


def _ffn(x, g, wg, wu, wd, fg, layer, final, tm, tf):
    t, d = x.shape
    f = wg.shape[2]
    return pl.pallas_call(
        functools.partial(_ffn_kernel, final),
        grid=(t // tm, f // tf),
        in_specs=[pl.BlockSpec((tm, d), lambda i, j: (i, 0)),
                  pl.BlockSpec((1, d), lambda i, j: (0, 0)),
                  pl.BlockSpec((None, d, tf), lambda i, j: (layer, 0, j)),
                  pl.BlockSpec((None, d, tf), lambda i, j: (layer, 0, j)),
                  pl.BlockSpec((None, tf, d), lambda i, j: (layer, j, 0)),
                  pl.BlockSpec((1, d), lambda i, j: (0, 0))],
        out_specs=pl.BlockSpec((tm, d), lambda i, j: (i, 0)),
        out_shape=jax.ShapeDtypeStruct((t, d), F32),
        scratch_shapes=[pltpu.VMEM((tm, d), BF16)],
        compiler_params=_cparams(("parallel", "arbitrary")),
        name="ffn_final" if final else "ffn",
    )(x, g, wg, wu, wd, fg)


def _rows_at(w, offset):
    return jnp.pad(w, ((offset, LORA_PACK - offset - w.shape[0]), (0, 0)))


RELAYOUT_UNIT = 64
RELAYOUT_PARTS = 8


def _source_unit(q):
    c_units = C_COLS // RELAYOUT_UNIT
    a_units = 3 * A_WIDTH // RELAYOUT_UNIT
    b_units = B_COLS // RELAYOUT_UNIT
    last_src = (A_COLS + B_COLS + C_COLS) // RELAYOUT_UNIT - 1
    src_c = (A_COLS + B_COLS) // RELAYOUT_UNIT
    src_b = A_COLS // RELAYOUT_UNIT
    lora = jnp.minimum(a_units + (q - c_units - a_units - b_units), last_src)
    return jnp.where(q < c_units, src_c + q,
                     jnp.where(q < c_units + a_units, q - c_units,
                               jnp.where(q < c_units + a_units + b_units,
                                         src_b + (q - c_units - a_units), lora)))


def _relayout_kernel(has_vres, n_steps, *refs):
    parts, rest = refs[:RELAYOUT_PARTS], refs[RELAYOUT_PARTS:]
    o_ref = rest[-1]
    blocks = [p[...] for p in parts]
    tail = rest[0][...] if has_vres else jnp.zeros_like(blocks[-1])
    blocks[-1] = jnp.where(pl.program_id(0) == n_steps - 1, tail, blocks[-1])
    o_ref[...] = jnp.concatenate(blocks, axis=0).T.astype(BF16)


def _layer_in_weight(w_in_t, w_vres_t, layer):
    _, n, d = w_in_t.shape
    has_vres = layer > 0
    n_steps = P_COLS // (RELAYOUT_UNIT * RELAYOUT_PARTS)

    def part(s):
        return pl.BlockSpec((None, RELAYOUT_UNIT, d),
                            lambda j: (layer, _source_unit(RELAYOUT_PARTS * j + s), 0))

    ins = [w_in_t] * RELAYOUT_PARTS
    specs = [part(s) for s in range(RELAYOUT_PARTS)]
    if has_vres:
        ins.append(w_vres_t)
        specs.append(pl.BlockSpec((None, VRES_LORA, d), lambda j: (layer - 1, 0, 0)))
    return pl.pallas_call(
        functools.partial(_relayout_kernel, has_vres, n_steps),
        grid=(n_steps,),
        in_specs=specs,
        out_specs=pl.BlockSpec((d, RELAYOUT_UNIT * RELAYOUT_PARTS), lambda j: (0, j)),
        out_shape=jax.ShapeDtypeStruct((d, P_COLS), BF16),
        compiler_params=_cparams(("parallel",)),
        name="w_in_relayout",
    )(*ins)


def _block_diag(w):
    g, n, _ = w.shape
    eye = jnp.eye(g, dtype=w.dtype)
    return (eye[:, None, :, None] * w[:, :, None, :]).reshape(g * n, g * n)


def kernel(x, positions, norm1_g, w_in, tshift_mu, rwkv_w0, rwkv_w2, rwkv_a0, rwkv_a2, rwkv_g2, rwkv_k_k, rwkv_k_a, rwkv_r_k, rwkv_ln_g, rwkv_ln_b, w_in_vres, tshift_mu_vres, rwkv_v0, rwkv_v2, lru_conv_w, lru_conv_b, lru_wa, lru_ba, lru_wx, lru_bx, lru_lambda, ret_gn_g, w_out, norm2_g, ffn_w_gate, ffn_w_up, ffn_w_down, final_norm_g):
    batch, seq_len, d = x.shape
    depth = w_in.shape[0]
    t = batch * seq_len
    aw = A_WIDTH
    row = lambda v: v.reshape(1, -1).astype(F32)

    tm_proj = min(512, seq_len)
    tm_in = min(1024, seq_len)
    tm_tok = min(256, seq_len)
    cps = min(8, seq_len // CHUNK)
    cpr = min(4, seq_len // CHUNK)

    inv_freq = ROPE_THETA ** (-jnp.arange(0, C_QK_DIM, 2, dtype=F32) / C_QK_DIM)
    invf = jnp.tile(inv_freq, LANES // inv_freq.shape[0]).reshape(1, LANES)
    n_freq = inv_freq.shape[0]
    assert cpr * n_freq == LANES
    posb = positions.astype(F32).reshape(t // (cpr * CHUNK), cpr, CHUNK)
    posb = jnp.repeat(jnp.swapaxes(posb, 1, 2), n_freq, axis=-1).reshape(t // cpr, LANES)

    w_in_t = jnp.swapaxes(w_in, 1, 2)
    w_vres_t = jnp.swapaxes(w_in_vres, 1, 2)

    xf = x.reshape(t, d)
    v_first = None
    for l in range(depth):
        has_vres = l > 0
        w_l = _layer_in_weight(w_in_t, w_vres_t, l)
        proj = _inproj(xf, row(norm1_g[l]), w_l, tm_in, P_COLS // 6)

        mu = tshift_mu[l]
        o_ad, o_gd = 3 * aw + DECAY_LORA, 3 * aw + DECAY_LORA + ICLR_LORA
        mu_vd = tshift_mu_vres[l - 1] if has_vres else jnp.zeros((VRES_LORA,), F32)
        params = {
            "mu_rkv": row(mu[:3 * aw]),
            "mu_lora": row(jnp.concatenate([mu[3 * aw:A_COLS], mu_vd])),
            "w0": row(rwkv_w0[l]), "a0": row(rwkv_a0[l]), "k_k": row(rwkv_k_k[l]),
            "k_a": row(rwkv_k_a[l]), "r_k": row(rwkv_r_k[l]),
            "w2": _rows_at(rwkv_w2[l], 0), "a2": _rows_at(rwkv_a2[l], o_ad - 3 * aw),
            "g2": _rows_at(rwkv_g2[l], o_gd - 3 * aw),
        }
        if has_vres:
            params["v0"] = row(rwkv_v0[l - 1])
            params["v2"] = _rows_at(rwkv_v2[l - 1], A_COLS - 3 * aw)
        at, rt, kt, bt, kd, bd, v_b, pc, g_a, bonus, *v_f32 = _rwkv_prep(
            proj, params, v_first if has_vres else None, seq_len, tm_tok)
        if l == 0:
            v_first = v_f32[0]
        y_a = _rwkv_scan(at, rt, kt, bt, kd, bd, v_b, pc, g_a, bonus,
                         row(rwkv_ln_g[l]), row(rwkv_ln_b[l]), batch, seq_len, cps)

        lru_params = {
            "conv_w": lru_conv_w[l], "conv_b": row(lru_conv_b[l]),
            "wa": _block_diag(lru_wa[l]).astype(BF16), "ba": row(lru_ba[l]),
            "wx": _block_diag(lru_wx[l]).astype(BF16), "bx": row(lru_bx[l]),
            "lam": row(lru_lambda[l]),
        }
        y_b, y_c = _mix_bc(proj, lru_params, posb, invf, row(ret_gn_g[l]), batch, seq_len, cpr)

        xf = _outproj(xf, y_a, y_b, y_c, w_out, l, tm_proj)
        xf = _ffn(xf, row(norm2_g[l]), ffn_w_gate, ffn_w_up, ffn_w_down, row(final_norm_g),
                  l, l == depth - 1, tm_in, 256)
    return xf.reshape(batch, seq_len, d)
```

```python
import functools

import numpy as np
import jax
import jax.numpy as jnp
from jax import lax
from jax.experimental import pallas as pl
from jax.experimental.pallas import tpu as pltpu

F32 = jnp.float32
BF16 = jnp.bfloat16

NORM_EPS = 1e-6
CHUNK = 64

A_HEADS = 12
A_HEAD_DIM = 64
A_WIDTH = A_HEADS * A_HEAD_DIM
DECAY_LORA = 96
ICLR_LORA = 96
VRES_LORA = 64
GATE_LORA = 256
A_GN_EPS = 64e-5
DECAY_SCALE = float(np.exp(np.float32(-0.5)))
A_COLS = 3 * A_WIDTH + DECAY_LORA + ICLR_LORA + GATE_LORA
B_WIDTH = 512
B_BLOCKS = 8
B_BLOCK_DIM = B_WIDTH // B_BLOCKS
CONV_WIDTH = 4
LRU_C = 8.0
B_COLS = 2 * B_WIDTH
C_HEADS = 6
C_QK_DIM = 64
C_V_DIM = 128
C_QK_WIDTH = C_HEADS * C_QK_DIM
C_WIDTH = C_HEADS * C_V_DIM
C_GN_EPS = 1e-5
ROPE_THETA = 10000.0
C_COLS = 2 * C_QK_WIDTH + 2 * C_WIDTH

LANES = 128
SUBLANES = 8
LORA_PACK = DECAY_LORA + ICLR_LORA + GATE_LORA + VRES_LORA
VMEM_LIMIT = 56 * 1024 * 1024

P_COLS = C_COLS + 3 * A_WIDTH + B_COLS + LORA_PACK


def _cparams(sem):
    return pltpu.CompilerParams(dimension_semantics=sem, vmem_limit_bytes=VMEM_LIMIT)


def _dot(a, b):
    return jnp.dot(a.astype(BF16), b.astype(BF16), preferred_element_type=F32)


def _dot_nt(a, b):
    return lax.dot_general(a.astype(BF16), b.astype(BF16), (((1,), (1,)), ((), ())),
                           preferred_element_type=F32)


def _dot_tn(a, b):
    return lax.dot_general(a.astype(BF16), b.astype(BF16), (((0,), (0,)), ((), ())),
                           preferred_element_type=F32)


def _iota2(shape, axis):
    return lax.broadcasted_iota(jnp.int32, shape, axis)


def _blk(idx, size):
    return idx >> (size.bit_length() - 1)


def _rms(x, g):
    ms = jnp.mean(x * x, axis=-1, keepdims=True)
    return x * lax.rsqrt(ms + NORM_EPS) * g


def _inproj_kernel(x_ref, g_ref, w_ref, o_ref, h_ref):
    @pl.when(pl.program_id(1) == 0)
    def _():
        h_ref[...] = _rms(x_ref[...], g_ref[...]).astype(BF16)

    o_ref[...] = jnp.dot(h_ref[...], w_ref[...], preferred_element_type=F32)


def _inproj(x, g, w, tm, tn):
    t, d = x.shape
    n = w.shape[1]
    return pl.pallas_call(
        _inproj_kernel,
        grid=(t // tm, n // tn),
        in_specs=[pl.BlockSpec((tm, d), lambda i, j: (i, 0)),
                  pl.BlockSpec((1, d), lambda i, j: (0, 0)),
                  pl.BlockSpec((d, tn), lambda i, j: (0, j))],
        out_specs=pl.BlockSpec((tm, tn), lambda i, j: (i, j)),
        out_shape=jax.ShapeDtypeStruct((t, n), F32),
        scratch_shapes=[pltpu.VMEM((tm, d), BF16)],
        compiler_params=_cparams(("parallel", "arbitrary")),
        name="inproj",
    )(x, g, w)


def _shift_rows(cur, prev_row):
    rolled = pltpu.roll(cur, 1, 0)
    top = rolled[:SUBLANES]
    top = jnp.where(_iota2(top.shape, 0) == 0, prev_row, top)
    return jnp.concatenate([top, rolled[SUBLANES:]], axis=0)


def _split3(x):
    hi = x.astype(BF16)
    r1 = x - hi.astype(F32)
    mid = r1.astype(BF16)
    lo = (r1 - mid.astype(F32)).astype(BF16)
    return hi, mid, lo


def _sel_dot(sel, x):
    hi, mid, lo = _split3(x)
    d = functools.partial(jnp.dot, sel, preferred_element_type=F32)
    return (d(lo) + d(mid)) + d(hi)


def _dot_sel(x, sel):
    hi = x.astype(BF16)
    mid = (x - hi.astype(F32)).astype(BF16)
    d = functools.partial(jnp.dot, preferred_element_type=F32)
    return d(mid, sel) + d(hi, sel)


def _head_sum(x, e):
    parts = [_dot_sel(x[:, i * LANES:(i + 1) * LANES], e) for i in range(x.shape[1] // LANES)]
    return jnp.concatenate(parts, axis=-1)


def _head_ones():
    r = _blk(_iota2((LANES, LANES), 0), A_HEAD_DIM)
    c = _blk(_iota2((LANES, LANES), 1), A_HEAD_DIM)
    return (r == c).astype(BF16)


def _prep_kernel(has_vres, seq_tiles, *refs):
    if has_vres:
        (rkv_ref, lora_ref, prkv_ref, plora_ref, mu_rkv, mu_lora,
         w0, a0, k_k, k_a, r_k, w2, a2, g2, v0, v2, vfirst_ref,
         at_o, rt_o, kt_o, bt_o, vb_o, pc_o, g_o, bonus_o) = refs
    else:
        (rkv_ref, lora_ref, prkv_ref, plora_ref, mu_rkv, mu_lora,
         w0, a0, k_k, k_a, r_k, w2, a2, g2,
         at_o, rt_o, kt_o, bt_o, vb_o, pc_o, g_o, bonus_o, v_o) = refs
    first = (pl.program_id(0) % seq_tiles) == 0
    last_row = slice(SUBLANES - 1, SUBLANES)

    def mixed(cur_ref, prev_ref, mu_ref):
        cur = cur_ref[...]
        prev_row = jnp.where(first, 0.0, prev_ref[last_row, :])
        return cur + (_shift_rows(cur, prev_row) - cur) * mu_ref[...]

    rkv = mixed(rkv_ref, prkv_ref, mu_rkv)
    lora = mixed(lora_ref, plora_ref, mu_lora)
    r = rkv[:, 0:A_WIDTH]
    k = rkv[:, A_WIDTH:2 * A_WIDTH]
    v = rkv[:, 2 * A_WIDTH:3 * A_WIDTH]
    y_w = w0[...] + _dot(jnp.tanh(lora), w2[...])
    lw = -DECAY_SCALE * jax.nn.sigmoid(y_w)
    a = jax.nn.sigmoid(a0[...] + _dot(lora, a2[...]))
    g = _dot(jax.nn.sigmoid(lora), g2[...])
    if has_vres:
        v = v + (vfirst_ref[...] - v) * jax.nn.sigmoid(v0[...] + _dot(lora, v2[...]))

    e = _head_ones()
    kk = k * k_k[...]
    kk = kk * lax.rsqrt(jnp.maximum(_head_sum(kk * kk, e), 1e-24))
    k = k * (1.0 + (a - 1.0) * k_a[...])
    b = kk * a
    bonus = _head_sum(r * k * r_k[...], e) * v

    tm = lw.shape[0]
    row = _iota2((tm, tm), 0)
    col = _iota2((tm, tm), 1)
    tri = (_blk(row, CHUNK) == _blk(col, CHUNK)) & (row >= col)
    cum = _sel_dot(tri.astype(BF16), lw)
    cum3 = cum.reshape(tm // CHUNK, CHUNK, A_WIDTH)
    e_in = jnp.exp(cum)
    e_out = jnp.exp(-cum)
    at_o[...] = (-kk * jnp.exp(cum - lw)).astype(BF16)
    rt_o[...] = (r * e_in).astype(BF16)
    kt_o[...] = (k * e_out).astype(BF16)
    bt_o[...] = (b * e_out).astype(BF16)
    vb_o[...] = v.astype(BF16)
    for c in range(tm // CHUNK):
        pc_o[c] = jnp.exp(cum3[c, CHUNK - 1:CHUNK, :])
    g_o[...] = g
    bonus_o[...] = bonus
    if not has_vres:
        v_o[...] = v


def _rwkv_prep(proj, params, v_first, seq_len, tm):
    t = proj.shape[0]
    has_vres = v_first is not None
    seq_tiles = seq_len // tm
    sub = tm // SUBLANES
    rkv_w, lora_w = 3 * A_WIDTH, LORA_PACK
    rkv_blk = (C_COLS) // rkv_w
    lora_blk = (C_COLS + rkv_w + B_COLS) // lora_w

    def cur(width, blk):
        return pl.BlockSpec((tm, width), lambda i: (i, blk))

    def prev(width, blk):
        return pl.BlockSpec((SUBLANES, width), lambda i: (jnp.maximum(i * sub - 1, 0), blk))

    def vec(width):
        return pl.BlockSpec((1, width), lambda i: (0, 0))

    def mat(rows):
        return pl.BlockSpec((rows, A_WIDTH), lambda i: (0, 0))

    tok = pl.BlockSpec((tm, A_WIDTH), lambda i: (i, 0))
    ins = [proj, proj, proj, proj,
           params["mu_rkv"], params["mu_lora"],
           params["w0"], params["a0"], params["k_k"], params["k_a"], params["r_k"],
           params["w2"], params["a2"], params["g2"]]
    specs = [cur(rkv_w, rkv_blk), cur(lora_w, lora_blk),
             prev(rkv_w, rkv_blk), prev(lora_w, lora_blk),
             vec(rkv_w), vec(lora_w),
             vec(A_WIDTH), vec(A_WIDTH), vec(A_WIDTH), vec(A_WIDTH), vec(A_WIDTH),
             mat(LORA_PACK), mat(LORA_PACK), mat(LORA_PACK)]
    if has_vres:
        ins += [params["v0"], params["v2"], v_first]
        specs += [vec(A_WIDTH), mat(LORA_PACK), tok]
    wide = lambda dtype: jax.ShapeDtypeStruct((t, A_WIDTH), dtype)
    pc_spec = pl.BlockSpec((tm // CHUNK, 1, A_WIDTH), lambda i: (i, 0, 0))
    out_specs = [tok] * 5 + [pc_spec, tok, tok]
    out_shape = [wide(BF16)] * 5 + [jax.ShapeDtypeStruct((t // CHUNK, 1, A_WIDTH), F32),
                                    wide(F32), wide(F32)]
    if not has_vres:
        out_specs.append(tok)
        out_shape.append(wide(F32))
    return pl.pallas_call(
        functools.partial(_prep_kernel, has_vres, seq_tiles),
        grid=(t // tm,),
        in_specs=specs,
        out_specs=out_specs,
        out_shape=out_shape,
        compiler_params=_cparams(("parallel",)),
        name="rwkv_prep_vres" if has_vres else "rwkv_prep",
    )(*ins)


def _unit_lower_inverse(a_strict, row, col, eye, pair_diag):
    base = 8
    diag = _blk(row, base) == _blk(col, base)
    n1 = [jnp.where(diag, a, 0.0) for a in a_strict]
    n1d = [pair_diag(x) for x in n1]
    n2 = [_dot(x, d) for x, d in zip(n1, n1d)]
    n2d = [pair_diag(x) for x in n2]
    t = [eye + x for x in n1]
    t = [x + _dot(x, d) for x, d in zip(t, n2d)]
    n4d = [pair_diag(_dot(x, d)) for x, d in zip(n2, n2d)]
    t = [x + _dot(x, d) for x, d in zip(t, n4d)]
    m = base
    while m < CHUNK:
        sel = (_blk(row, 2 * m) == _blk(col, 2 * m)) & (_blk(row, m) != _blk(col, m))
        ed = [pair_diag(jnp.where(sel, a, 0.0)) for a in a_strict]
        te = [_dot(x, d) for x, d in zip(t, ed)]
        t = [x + _dot(y, pair_diag(x)) for x, y in zip(t, te)]
        m *= 2
    return t


def _scan_kernel(cps, at_ref, rt_ref, kt_ref, bt_ref, v_ref, pc_ref,
                 gate_ref, bonus_ref, lng_ref, lnb_ref, o_ref, s_ref):
    @pl.when(pl.program_id(1) == 0)
    def _():
        s_ref[...] = jnp.zeros_like(s_ref)

    n = 2 * A_HEAD_DIM
    pairs = A_HEADS // 2
    row = _iota2((CHUNK, n), 0)
    lane = _iota2((CHUNK, n), 1)
    col = lane & (A_HEAD_DIM - 1)
    lo = lane < A_HEAD_DIM
    strict = row > col
    incl = row >= col
    eye = (row == col).astype(F32)
    same_head = _blk(_iota2((n, n), 0), A_HEAD_DIM) == _blk(_iota2((n, n), 1), A_HEAD_DIM)
    items = [(c, p) for c in range(cps) for p in range(pairs)]

    def load(ref):
        return [ref[0, c * CHUNK:(c + 1) * CHUNK, p * n:(p + 1) * n] for c, p in items]

    def pair_diag(x):
        x = x.astype(BF16)
        zero = jnp.zeros_like(x)
        return jnp.concatenate([jnp.where(lo, x, zero), jnp.where(lo, zero, x)], axis=0)

    def fold(m):
        return jnp.where(lo, m[:CHUNK], m[CHUNK:])

    at = load(at_ref)
    rt = load(rt_ref)
    lhs = [jnp.concatenate([a, r], axis=0) for a, r in zip(at, rt)]
    kt = load(kt_ref)
    bt = load(bt_ref)
    lk = [_dot_nt(x, pair_diag(k)) for x, k in zip(lhs, kt)]
    lb = [_dot_nt(x, pair_diag(b)) for x, b in zip(lhs, bt)]
    a_k = [jnp.concatenate([jnp.where(strict, x[:CHUNK], 0.0), jnp.where(incl, x[CHUNK:], 0.0)],
                           axis=0) for x in lk]
    a_ab = [jnp.where(strict, x[:CHUNK], 0.0) for x in lb]
    a_rb = [jnp.where(incl, x[CHUNK:], 0.0) for x in lb]
    t_inv = _unit_lower_inverse(a_ab, row, col, eye, pair_diag)
    v = load(v_ref)
    lv = [_dot(a, pair_diag(x)) for a, x in zip(a_k, v)]
    wu = [_dot(t, jnp.concatenate([pair_diag(a), pair_diag(x[:CHUNK])], axis=1))
          for t, a, x in zip(t_inv, at, lv)]
    z = [_dot(a, jnp.concatenate([pair_diag(x[:, :n]), pair_diag(x[:, n:])], axis=1))
         for a, x in zip(a_rb, wu)]
    rw = [r.astype(F32) + x[:, :n] for r, x in zip(rt, z)]
    y0 = [x[CHUNK:] + zz[:, n:] for x, zz in zip(lv, z)]
    g = [jnp.where(same_head, _dot_tn(x[:, :n], b), 0.0).astype(BF16)
         for x, b in zip(wu, bt)]
    s_add = [fold(_dot_tn(jnp.concatenate([x[:, n:].astype(BF16), vv], axis=0),
                          jnp.concatenate([b, k], axis=0)))
             for x, vv, b, k in zip(wu, v, bt, kt)]

    inv_n = 1.0 / A_HEAD_DIM

    def head_mean(x):
        m_lo = jnp.sum(jnp.where(lo, x, 0.0), axis=-1, keepdims=True)
        m_hi = jnp.sum(jnp.where(lo, 0.0, x), axis=-1, keepdims=True)
        return jnp.where(lo, m_lo, m_hi) * inv_n

    s = [s_ref[p] for p in range(pairs)]
    for c in range(cps):
        rows = slice(c * CHUNK, (c + 1) * CHUNK)
        sd = [pair_diag(x) for x in s]
        ys = [_dot_nt(rw[c * pairs + p], sd[p]) + y0[c * pairs + p] for p in range(pairs)]
        for p in range(pairs):
            i = c * pairs + p
            cols = slice(p * n, (p + 1) * n)
            pc = pc_ref[0, c, :, cols]
            s[p] = (s[p] + _dot(s[p], g[i]) + s_add[i]) * pc
        for p in range(pairs):
            cols = slice(p * n, (p + 1) * n)
            d = ys[p] - head_mean(ys[p])
            var = head_mean(d * d)
            yn = d * lax.rsqrt(var + A_GN_EPS) * lng_ref[:, cols] + lnb_ref[:, cols]
            o_ref[0, rows, cols] = ((yn + bonus_ref[0, rows, cols])
                                    * gate_ref[0, rows, cols]).astype(o_ref.dtype)
    for p in range(pairs):
        s_ref[p] = s[p]


def _rwkv_scan(at, rt, kt, bt, v, pc, gate, bonus, ln_g, ln_b, batch, seq_len, cps):
    ts = cps * CHUNK
    shp = (batch, seq_len, A_WIDTH)
    blk = pl.BlockSpec((1, ts, A_WIDTH), lambda b, s: (b, s, 0))
    pc_blk = pl.BlockSpec((1, cps, 1, A_WIDTH), lambda b, s: (b, s, 0, 0))
    vec = pl.BlockSpec((1, A_WIDTH), lambda b, s: (0, 0))
    args = [z.reshape(shp) for z in (at, rt, kt, bt, v)]
    args += [pc.reshape(batch, seq_len // CHUNK, 1, A_WIDTH), gate.reshape(shp), bonus.reshape(shp)]
    y = pl.pallas_call(
        functools.partial(_scan_kernel, cps),
        grid=(batch, seq_len // ts),
        in_specs=[blk] * 5 + [pc_blk, blk, blk, vec, vec],
        out_specs=blk,
        out_shape=jax.ShapeDtypeStruct(shp, BF16),
        scratch_shapes=[pltpu.VMEM((A_HEADS // 2, A_HEAD_DIM, 2 * A_HEAD_DIM), F32)],
        compiler_params=_cparams(("parallel", "arbitrary")),
        name="rwkv_scan",
    )(*args, ln_g, ln_b)
    return y.reshape(batch * seq_len, A_WIDTH)


def _lru_body(first, gate_ref, x_ref, px_ref, cw_ref, cb_ref, wa_ref, ba_ref, wx_ref, bx_ref,
              lam_ref, o_ref, h_ref):
    x = x_ref[...]
    ts = x.shape[0]
    prev = jnp.where(first, 0.0, px_ref[...])
    row = _iota2(x.shape, 0)
    row8 = _iota2(prev.shape, 0)

    def delayed(d):
        rolled = pltpu.roll(x, d, 0)
        top = jnp.where(row8 < d, pltpu.roll(prev, d, 0), rolled[:SUBLANES])
        return jnp.concatenate([top, rolled[SUBLANES:]], axis=0)

    xc = cw_ref[CONV_WIDTH - 1:CONV_WIDTH, :] * x + cb_ref[...]
    for d in range(1, CONV_WIDTH):
        xc = xc + cw_ref[CONV_WIDTH - 1 - d:CONV_WIDTH - d, :] * delayed(d)

    r = jax.nn.sigmoid(_dot(xc, wa_ref[...]) + ba_ref[...])
    i = jax.nn.sigmoid(_dot(xc, wx_ref[...]) + bx_ref[...])
    log_a = -LRU_C * r * jax.nn.softplus(-lam_ref[...])
    a = jnp.exp(log_a)
    th = jnp.tanh(log_a)
    u = jnp.sqrt(-2.0 * th / (1.0 - th)) * (i * xc)

    grouped = (ts // SUBLANES, SUBLANES, B_WIDTH)
    in_group = _iota2(grouped, 1)

    def group_delay(z, d, fill):
        return jnp.where(in_group >= d, pltpu.roll(z.reshape(grouped), d, 1), fill).reshape(z.shape)

    d = 1
    while d < SUBLANES:
        u = a * group_delay(u, d, 0.0) + u
        a = a * group_delay(a, d, 1.0)
        d *= 2
    carry = h_ref[...]
    groups = []
    for g0 in range(0, ts, SUBLANES):
        hg = a[g0:g0 + SUBLANES] * carry + u[g0:g0 + SUBLANES]
        carry = hg[SUBLANES - 1:SUBLANES]
        groups.append(hg)
    h_ref[...] = carry
    h = jnp.concatenate(groups, axis=0)
    o_ref[...] = (jax.nn.gelu(gate_ref[...]) * h).astype(o_ref.dtype)


def _log_gammas():
    return np.log1p(-np.exp2(-5.0 - np.arange(C_HEADS, dtype=np.float32))).astype(np.float32)


def _ret_body(cpr, q_ref, k_ref, v_ref, g_ref, pos_ref, invf_ref, gng_ref, o_ref, st_ref):
    half = C_QK_DIM // 2
    lane = _iota2((CHUNK, LANES), 1)
    in_first_half = (lane & (C_QK_DIM - 1)) < half
    lo = lane < C_QK_DIM
    idx_r = _iota2((CHUNK, CHUNK), 0).astype(F32)
    idx_c = _iota2((CHUNK, CHUNK), 1).astype(F32)
    dist = jnp.abs(idx_r - idx_c)
    pos_in = _iota2((CHUNK, 1), 0).astype(F32)
    log_gamma = _log_gammas()

    def rope(t, cos, sin_signed):
        swapped = jnp.where(in_first_half, pltpu.roll(t, LANES - half, 1), pltpu.roll(t, half, 1))
        return t * cos + swapped * sin_signed

    intra_decay = [jnp.exp(float(lg) * dist) for lg in log_gamma]
    q_decay = [jnp.exp(float(lg) * (pos_in + 1.0)) for lg in log_gamma]
    k_decay = [jnp.exp(float(lg) * (CHUNK - 1.0 - pos_in)) for lg in log_gamma]
    chunk_decay = [float(np.exp(lg * np.float32(CHUNK))) for lg in log_gamma]
    items = [(c, h) for c in range(cpr) for h in range(C_HEADS)]

    ang = pos_ref[...] * invf_ref[...]
    cos_packed = jnp.cos(ang)
    sin_packed = jnp.sin(ang)
    lane_group = lane >> (half.bit_length() - 1)

    def spread(x, c):
        out = x
        for g in range(LANES // half):
            if g != c:
                out = jnp.where(lane_group == g, pltpu.roll(x, (half * (g - c)) % LANES, 1), out)
        return out

    q2, k2 = {}, {}
    for c in range(cpr):
        rows = slice(c * CHUNK, (c + 1) * CHUNK)
        cos = spread(cos_packed, c)
        sin = spread(sin_packed, c)
        sin_signed = jnp.where(in_first_half, -sin, sin)
        for p in range(C_HEADS // 2):
            qk_cols = slice(p * LANES, (p + 1) * LANES)
            q2[c, p] = rope(q_ref[rows, qk_cols], cos, sin_signed)
            k2[c, p] = rope(k_ref[rows, qk_cols], cos, sin_signed) * (C_QK_DIM ** -0.5)

    def mine(x, h):
        return jnp.where(lo, x, 0.0) if h % 2 == 0 else jnp.where(lo, 0.0, x)

    qh = [mine(q2[c, h // 2], h) for c, h in items]
    kh = [mine(k2[c, h // 2], h) for c, h in items]
    vh = [v_ref[c * CHUNK:(c + 1) * CHUNK, h * C_V_DIM:(h + 1) * C_V_DIM].astype(BF16)
          for c, h in items]
    scores = [_dot_nt(q, k) * intra_decay[h] for q, k, (c, h) in zip(qh, kh, items)]
    kv = [_dot_tn(k * k_decay[h], v) for k, v, (c, h) in zip(kh, vh, items)]
    intra = [_dot(s, v) for s, v in zip(scores, vh)]

    state = [st_ref[h] for h in range(C_HEADS)]
    starts = []
    for i, (c, h) in enumerate(items):
        starts.append(state[h])
        state[h] = state[h] * chunk_decay[h] + kv[i]
    for h in range(C_HEADS):
        st_ref[h] = state[h]
    cross = [_dot(q * q_decay[h], s) for q, s, (c, h) in zip(qh, starts, items)]

    for i, (c, h) in enumerate(items):
        rows = slice(c * CHUNK, (c + 1) * CHUNK)
        v_cols = slice(h * C_V_DIM, (h + 1) * C_V_DIM)
        o = intra[i] + cross[i]
        mu = jnp.mean(o, axis=-1, keepdims=True)
        d = o - mu
        var = jnp.mean(d * d, axis=-1, keepdims=True)
        on = d * lax.rsqrt(var + C_GN_EPS) * gng_ref[:, v_cols]
        o_ref[rows, v_cols] = (jax.nn.silu(g_ref[rows, v_cols]) * on).astype(o_ref.dtype)


def _mix_bc_kernel(cpr, gate_ref, x_ref, px_ref, cw_ref, cb_ref, wa_ref, ba_ref, wx_ref, bx_ref,
                   lam_ref, q_ref, k_ref, v_ref, g_ref, pos_ref, invf_ref, gng_ref,
                   yb_ref, yc_ref, h_ref, st_ref):
    first = pl.program_id(1) == 0

    @pl.when(first)
    def _():
        h_ref[...] = jnp.zeros_like(h_ref)
        st_ref[...] = jnp.zeros_like(st_ref)

    _ret_body(cpr, q_ref, k_ref, v_ref, g_ref, pos_ref, invf_ref, gng_ref, yc_ref, st_ref)
    _lru_body(first, gate_ref, x_ref, px_ref, cw_ref, cb_ref, wa_ref, ba_ref, wx_ref, bx_ref,
              lam_ref, yb_ref, h_ref)


def _mix_bc(proj, lru, posb, invf, gn_g, batch, seq_len, cpr):
    t = proj.shape[0]
    ts = cpr * CHUNK
    nst = seq_len // ts
    sub = ts // SUBLANES
    gate_blk = (C_COLS + 3 * A_WIDTH) // B_WIDTH
    x_blk = gate_blk + 1

    def tok(width, blk):
        return pl.BlockSpec((ts, width), lambda b, s: (b * nst + s, blk))

    def const(shape):
        return pl.BlockSpec(shape, lambda b, s: (0,) * len(shape))

    return pl.pallas_call(
        functools.partial(_mix_bc_kernel, cpr),
        grid=(batch, nst),
        in_specs=[tok(B_WIDTH, gate_blk), tok(B_WIDTH, x_blk),
                  pl.BlockSpec((SUBLANES, B_WIDTH),
                               lambda b, s: (jnp.maximum((b * nst + s) * sub - 1, 0), x_blk)),
                  const((CONV_WIDTH, B_WIDTH)), const((1, B_WIDTH)),
                  const((B_WIDTH, B_WIDTH)), const((1, B_WIDTH)),
                  const((B_WIDTH, B_WIDTH)), const((1, B_WIDTH)), const((1, B_WIDTH)),
                  tok(C_QK_WIDTH, 0), tok(C_QK_WIDTH, 1), tok(C_WIDTH, 1), tok(C_WIDTH, 2),
                  pl.BlockSpec((CHUNK, LANES), lambda b, s: (b * nst + s, 0)),
                  const((1, LANES)), const((1, C_WIDTH))],
        out_specs=[tok(B_WIDTH, 0), tok(C_WIDTH, 0)],
        out_shape=[jax.ShapeDtypeStruct((t, B_WIDTH), BF16),
                   jax.ShapeDtypeStruct((t, C_WIDTH), BF16)],
        scratch_shapes=[pltpu.VMEM((1, B_WIDTH), F32),
                        pltpu.VMEM((C_HEADS, LANES, C_V_DIM), F32)],
        compiler_params=_cparams(("parallel", "arbitrary")),
        name="mix_bc",
    )(proj, proj, proj, lru["conv_w"], lru["conv_b"], lru["wa"], lru["ba"], lru["wx"],
      lru["bx"], lru["lam"], proj, proj, proj, proj, posb, invf, gn_g)


def _outproj_kernel(x_ref, ya_ref, yb_ref, yc_ref, w_ref, o_ref):
    b0, c0 = A_WIDTH, A_WIDTH + B_WIDTH
    acc = jnp.dot(ya_ref[...], w_ref[0:b0, :].astype(BF16), preferred_element_type=F32)
    acc += jnp.dot(yb_ref[...], w_ref[b0:c0, :].astype(BF16), preferred_element_type=F32)
    acc += jnp.dot(yc_ref[...], w_ref[c0:, :].astype(BF16), preferred_element_type=F32)
    o_ref[...] = x_ref[...] + acc


def _outproj(x, ya, yb, yc, w_stack, layer, tm):
    t, d = x.shape

    def tok(width):
        return pl.BlockSpec((tm, width), lambda i: (i, 0))

    return pl.pallas_call(
        _outproj_kernel,
        grid=(t // tm,),
        in_specs=[tok(d), tok(A_WIDTH), tok(B_WIDTH), tok(C_WIDTH),
                  pl.BlockSpec((None, d, d), lambda i: (layer, 0, 0),
                               pipeline_mode=pl.Buffered(1))],
        out_specs=tok(d),
        out_shape=jax.ShapeDtypeStruct((t, d), F32),
        compiler_params=_cparams(("parallel",)),
        name="outproj",
    )(x, ya, yb, yc, w_stack)


def _ffn_kernel(final, x_ref, g_ref, wg_ref, wu_ref, wd_ref, fg_ref, o_ref, h_ref):
    j = pl.program_id(1)

    @pl.when(j == 0)
    def _():
        x = x_ref[...]
        h_ref[...] = _rms(x, g_ref[...]).astype(BF16)
        o_ref[...] = x

    h = h_ref[...]
    gate = jnp.dot(h, wg_ref[...].astype(BF16), preferred_element_type=F32)
    up = jnp.dot(h, wu_ref[...].astype(BF16), preferred_element_type=F32)
    act = (jax.nn.silu(gate) * up).astype(BF16)
    o_ref[...] += jnp.dot(act, wd_ref[...].astype(BF16), preferred_element_type=F32)

    if final:
        @pl.when(j == pl.num_programs(1) - 1)
        def _():
            o_ref[...] = _rms(o_ref[...], fg_ref[...])


def _ffn(x, g, wg, wu, wd, fg, layer, final, tm, tf):
    t, d = x.shape
    f = wg.shape[2]
    return pl.pallas_call(
        functools.partial(_ffn_kernel, final),
        grid=(t // tm, f // tf),
        in_specs=[pl.BlockSpec((tm, d), lambda i, j: (i, 0)),
                  pl.BlockSpec((1, d), lambda i, j: (0, 0)),
                  pl.BlockSpec((None, d, tf), lambda i, j: (layer, 0, j)),
                  pl.BlockSpec((None, d, tf), lambda i, j: (layer, 0, j)),
                  pl.BlockSpec((None, tf, d), lambda i, j: (layer, j, 0)),
                  pl.BlockSpec((1, d), lambda i, j: (0, 0))],
        out_specs=pl.BlockSpec((tm, d), lambda i, j: (i, 0)),
        out_shape=jax.ShapeDtypeStruct((t, d), F32),
        scratch_shapes=[pltpu.VMEM((tm, d), BF16)],
        compiler_params=_cparams(("parallel", "arbitrary")),
        name="ffn_final" if final else "ffn",
    )(x, g, wg, wu, wd, fg)


def _rows_at(w, offset):
    return jnp.pad(w, ((offset, LORA_PACK - offset - w.shape[0]), (0, 0)))


RELAYOUT_UNIT = 64
RELAYOUT_PARTS = 8


def _source_unit(q):
    c_units = C_COLS // RELAYOUT_UNIT
    a_units = 3 * A_WIDTH // RELAYOUT_UNIT
    b_units = B_COLS // RELAYOUT_UNIT
    last_src = (A_COLS + B_COLS + C_COLS) // RELAYOUT_UNIT - 1
    src_c = (A_COLS + B_COLS) // RELAYOUT_UNIT
    src_b = A_COLS // RELAYOUT_UNIT
    lora = jnp.minimum(a_units + (q - c_units - a_units - b_units), last_src)
    return jnp.where(q < c_units, src_c + q,
                     jnp.where(q < c_units + a_units, q - c_units,
                               jnp.where(q < c_units + a_units + b_units,
                                         src_b + (q - c_units - a_units), lora)))


def _relayout_kernel(has_vres, n_steps, *refs):
    parts, rest = refs[:RELAYOUT_PARTS], refs[RELAYOUT_PARTS:]
    o_ref = rest[-1]
    blocks = [p[...] for p in parts]
    tail = rest[0][...] if has_vres else jnp.zeros_like(blocks[-1])
    blocks[-1] = jnp.where(pl.program_id(0) == n_steps - 1, tail, blocks[-1])
    o_ref[...] = jnp.concatenate(blocks, axis=0).T.astype(BF16)


def _layer_in_weight(w_in_t, w_vres_t, layer):
    _, n, d = w_in_t.shape
    has_vres = layer > 0
    n_steps = P_COLS // (RELAYOUT_UNIT * RELAYOUT_PARTS)

    def part(s):
        return pl.BlockSpec((None, RELAYOUT_UNIT, d),
                            lambda j: (layer, _source_unit(RELAYOUT_PARTS * j + s), 0))

    ins = [w_in_t] * RELAYOUT_PARTS
    specs = [part(s) for s in range(RELAYOUT_PARTS)]
    if has_vres:
        ins.append(w_vres_t)
        specs.append(pl.BlockSpec((None, VRES_LORA, d), lambda j: (layer - 1, 0, 0)))
    return pl.pallas_call(
        functools.partial(_relayout_kernel, has_vres, n_steps),
        grid=(n_steps,),
        in_specs=specs,
        out_specs=pl.BlockSpec((d, RELAYOUT_UNIT * RELAYOUT_PARTS), lambda j: (0, j)),
        out_shape=jax.ShapeDtypeStruct((d, P_COLS), BF16),
        compiler_params=_cparams(("parallel",)),
        name="w_in_relayout",
    )(*ins)


def _block_diag(w):
    g, n, _ = w.shape
    eye = jnp.eye(g, dtype=w.dtype)
    return (eye[:, None, :, None] * w[:, :, None, :]).reshape(g * n, g * n)


def kernel(x, positions, norm1_g, w_in, tshift_mu, rwkv_w0, rwkv_w2, rwkv_a0, rwkv_a2, rwkv_g2, rwkv_k_k, rwkv_k_a, rwkv_r_k, rwkv_ln_g, rwkv_ln_b, w_in_vres, tshift_mu_vres, rwkv_v0, rwkv_v2, lru_conv_w, lru_conv_b, lru_wa, lru_ba, lru_wx, lru_bx, lru_lambda, ret_gn_g, w_out, norm2_g, ffn_w_gate, ffn_w_up, ffn_w_down, final_norm_g):
    batch, seq_len, d = x.shape
    depth = w_in.shape[0]
    t = batch * seq_len
    aw = A_WIDTH
    row = lambda v: v.reshape(1, -1).astype(F32)

    tm_proj = min(512, seq_len)
    tm_in = min(1024, seq_len)
    tm_tok = min(256, seq_len)
    cps = min(8, seq_len // CHUNK)
    cpr = min(4, seq_len // CHUNK)

    inv_freq = ROPE_THETA ** (-jnp.arange(0, C_QK_DIM, 2, dtype=F32) / C_QK_DIM)
    invf = jnp.tile(inv_freq, LANES // inv_freq.shape[0]).reshape(1, LANES)
    n_freq = inv_freq.shape[0]
    assert cpr * n_freq == LANES
    posb = positions.astype(F32).reshape(t // (cpr * CHUNK), cpr, CHUNK)
    posb = jnp.repeat(jnp.swapaxes(posb, 1, 2), n_freq, axis=-1).reshape(t // cpr, LANES)

    w_in_t = jnp.swapaxes(w_in, 1, 2)
    w_vres_t = jnp.swapaxes(w_in_vres, 1, 2)

    xf = x.reshape(t, d)
    v_first = None
    for l in range(depth):
        has_vres = l > 0
        w_l = _layer_in_weight(w_in_t, w_vres_t, l)
        proj = _inproj(xf, row(norm1_g[l]), w_l, tm_in, P_COLS // 6)

        mu = tshift_mu[l]
        o_ad, o_gd = 3 * aw + DECAY_LORA, 3 * aw + DECAY_LORA + ICLR_LORA
        mu_vd = tshift_mu_vres[l - 1] if has_vres else jnp.zeros((VRES_LORA,), F32)
        params = {
            "mu_rkv": row(mu[:3 * aw]),
            "mu_lora": row(jnp.concatenate([mu[3 * aw:A_COLS], mu_vd])),
            "w0": row(rwkv_w0[l]), "a0": row(rwkv_a0[l]), "k_k": row(rwkv_k_k[l]),
            "k_a": row(rwkv_k_a[l]), "r_k": row(rwkv_r_k[l]),
            "w2": _rows_at(rwkv_w2[l], 0), "a2": _rows_at(rwkv_a2[l], o_ad - 3 * aw),
            "g2": _rows_at(rwkv_g2[l], o_gd - 3 * aw),
        }
        if has_vres:
            params["v0"] = row(rwkv_v0[l - 1])
            params["v2"] = _rows_at(rwkv_v2[l - 1], A_COLS - 3 * aw)
        at, rt, kt, bt, v_b, pc, g_a, bonus, *v_f32 = _rwkv_prep(
            proj, params, v_first if has_vres else None, seq_len, tm_tok)
        if l == 0:
            v_first = v_f32[0]
        y_a = _rwkv_scan(at, rt, kt, bt, v_b, pc, g_a, bonus,
                         row(rwkv_ln_g[l]), row(rwkv_ln_b[l]), batch, seq_len, cps)

        lru_params = {
            "conv_w": lru_conv_w[l], "conv_b": row(lru_conv_b[l]),
            "wa": _block_diag(lru_wa[l]).astype(BF16), "ba": row(lru_ba[l]),
            "wx": _block_diag(lru_wx[l]).astype(BF16), "bx": row(lru_bx[l]),
            "lam": row(lru_lambda[l]),
        }
        y_b, y_c = _mix_bc(proj, lru_params, posb, invf, row(ret_gn_g[l]), batch, seq_len, cpr)

        xf = _outproj(xf, y_a, y_b, y_c, w_out, l, tm_proj)
        xf = _ffn(xf, row(norm2_g[l]), ffn_w_gate, ffn_w_up, ffn_w_down, row(final_norm_g),
                  l, l == depth - 1, tm_in, 256)
    return xf.reshape(batch, seq_len, d)
```

```python
import functools

import numpy as np
import jax
import jax.numpy as jnp
from jax import lax
from jax.experimental import pallas as pl
from jax.experimental.pallas import tpu as pltpu

F32 = jnp.float32
BF16 = jnp.bfloat16

NORM_EPS = 1e-6
F32_TINY = float(np.finfo(np.float32).tiny)
CHUNK = 64

A_HEADS = 12
A_HEAD_DIM = 64
A_WIDTH = A_HEADS * A_HEAD_DIM
DECAY_LORA = 96
ICLR_LORA = 96
VRES_LORA = 64
GATE_LORA = 256
A_GN_EPS = 64e-5
DECAY_SCALE = float(np.exp(np.float32(-0.5)))
A_COLS = 3 * A_WIDTH + DECAY_LORA + ICLR_LORA + GATE_LORA
B_WIDTH = 512
B_BLOCKS = 8
B_BLOCK_DIM = B_WIDTH // B_BLOCKS
CONV_WIDTH = 4
LRU_C = 8.0
B_COLS = 2 * B_WIDTH
C_HEADS = 6
C_QK_DIM = 64
C_V_DIM = 128
C_QK_WIDTH = C_HEADS * C_QK_DIM
C_WIDTH = C_HEADS * C_V_DIM
C_GN_EPS = 1e-5
ROPE_THETA = 10000.0
C_COLS = 2 * C_QK_WIDTH + 2 * C_WIDTH

LANES = 128
SUBLANES = 8
LORA_PACK = DECAY_LORA + ICLR_LORA + GATE_LORA + VRES_LORA
VMEM_LIMIT = 56 * 1024 * 1024

P_COLS = C_COLS + 3 * A_WIDTH + B_COLS + LORA_PACK


def _cparams(sem):
    return pltpu.CompilerParams(dimension_semantics=sem, vmem_limit_bytes=VMEM_LIMIT)


def _dot(a, b):
    return jnp.dot(a.astype(BF16), b.astype(BF16), preferred_element_type=F32)


def _dot_nt(a, b):
    return lax.dot_general(a.astype(BF16), b.astype(BF16), (((1,), (1,)), ((), ())),
                           preferred_element_type=F32)


def _dot_tn(a, b):
    return lax.dot_general(a.astype(BF16), b.astype(BF16), (((0,), (0,)), ((), ())),
                           preferred_element_type=F32)


def _iota2(shape, axis):
    return lax.broadcasted_iota(jnp.int32, shape, axis)


def _blk(idx, size):
    return idx >> (size.bit_length() - 1)


def _rms(x, g):
    ms = jnp.mean(x * x, axis=-1, keepdims=True)
    return x * lax.rsqrt(ms + NORM_EPS) * g


def _inproj_kernel(x_ref, g_ref, w_ref, o_ref, h_ref):
    @pl.when(pl.program_id(1) == 0)
    def _():
        h_ref[...] = _rms(x_ref[...], g_ref[...]).astype(BF16)

    o_ref[...] = jnp.dot(h_ref[...], w_ref[...], preferred_element_type=F32)


def _inproj(x, g, w, tm, tn):
    t, d = x.shape
    n = w.shape[1]
    return pl.pallas_call(
        _inproj_kernel,
        grid=(t // tm, n // tn),
        in_specs=[pl.BlockSpec((tm, d), lambda i, j: (i, 0)),
                  pl.BlockSpec((1, d), lambda i, j: (0, 0)),
                  pl.BlockSpec((d, tn), lambda i, j: (0, j))],
        out_specs=pl.BlockSpec((tm, tn), lambda i, j: (i, j)),
        out_shape=jax.ShapeDtypeStruct((t, n), F32),
        scratch_shapes=[pltpu.VMEM((tm, d), BF16)],
        compiler_params=_cparams(("parallel", "arbitrary")),
        name="inproj",
    )(x, g, w)


def _shift_rows(cur, prev_row):
    rolled = pltpu.roll(cur, 1, 0)
    top = rolled[:SUBLANES]
    top = jnp.where(_iota2(top.shape, 0) == 0, prev_row, top)
    return jnp.concatenate([top, rolled[SUBLANES:]], axis=0)


def _split3(x):
    hi = x.astype(BF16)
    r1 = x - hi.astype(F32)
    mid = r1.astype(BF16)
    lo = (r1 - mid.astype(F32)).astype(BF16)
    return hi, mid, lo


def _sel_dot(sel, x):
    hi, mid, lo = _split3(x)
    d = functools.partial(jnp.dot, sel, preferred_element_type=F32)
    return (d(lo) + d(mid)) + d(hi)


def _dot_sel(x, sel):
    return jnp.dot(x.astype(BF16), sel, preferred_element_type=F32)


def _head_sum(x, e):
    parts = [_dot_sel(x[:, i * LANES:(i + 1) * LANES], e) for i in range(x.shape[1] // LANES)]
    return jnp.concatenate(parts, axis=-1)


def _head_ones():
    r = _blk(_iota2((LANES, LANES), 0), A_HEAD_DIM)
    c = _blk(_iota2((LANES, LANES), 1), A_HEAD_DIM)
    return (r == c).astype(BF16)


def _prep_kernel(has_vres, seq_tiles, *refs):
    if has_vres:
        (rkv_ref, lora_ref, prkv_ref, plora_ref, mu_rkv, mu_lora,
         w0, a0, k_k, k_a, r_k, w2, a2, g2, v0, v2, vfirst_ref,
         at_o, rt_o, kt_o, bt_o, vb_o, pc_o, g_o, bonus_o) = refs
    else:
        (rkv_ref, lora_ref, prkv_ref, plora_ref, mu_rkv, mu_lora,
         w0, a0, k_k, k_a, r_k, w2, a2, g2,
         at_o, rt_o, kt_o, bt_o, vb_o, pc_o, g_o, bonus_o, v_o) = refs
    first = (pl.program_id(0) % seq_tiles) == 0
    last_row = slice(SUBLANES - 1, SUBLANES)

    def mixed(cur_ref, prev_ref, mu_ref):
        cur = cur_ref[...]
        prev_row = jnp.where(first, 0.0, prev_ref[last_row, :])
        return cur + (_shift_rows(cur, prev_row) - cur) * mu_ref[...]

    rkv = mixed(rkv_ref, prkv_ref, mu_rkv)
    lora = mixed(lora_ref, plora_ref, mu_lora)
    r = rkv[:, 0:A_WIDTH]
    k = rkv[:, A_WIDTH:2 * A_WIDTH]
    v = rkv[:, 2 * A_WIDTH:3 * A_WIDTH]
    y_w = w0[...] + _dot(jnp.tanh(lora), w2[...])
    lw = -DECAY_SCALE * jax.nn.sigmoid(y_w)
    a = jax.nn.sigmoid(a0[...] + _dot(lora, a2[...]))
    g = _dot(jax.nn.sigmoid(lora), g2[...])
    if has_vres:
        v = v + (vfirst_ref[...] - v) * jax.nn.sigmoid(v0[...] + _dot(lora, v2[...]))

    e = _head_ones()
    kk = k * k_k[...]
    kk = kk * lax.rsqrt(jnp.maximum(_head_sum(kk * kk, e), 1e-24))
    k = k * (1.0 + (a - 1.0) * k_a[...])
    b = kk * a
    bonus = _head_sum(r * k * r_k[...], e) * v

    tm = lw.shape[0]
    row = _iota2((tm, tm), 0)
    col = _iota2((tm, tm), 1)
    tri = (_blk(row, CHUNK) == _blk(col, CHUNK)) & (row >= col)
    cum = _sel_dot(tri.astype(BF16), lw)
    cum3 = cum.reshape(tm // CHUNK, CHUNK, A_WIDTH)
    e_in = jnp.exp(cum)
    e_out = jnp.exp(-cum)
    at_o[...] = (-kk * jnp.exp(cum - lw)).astype(BF16)
    rt_o[...] = (r * e_in).astype(BF16)
    kt_o[...] = (k * e_out).astype(BF16)
    bt_o[...] = (b * e_out).astype(BF16)
    vb_o[...] = v.astype(BF16)
    for c in range(tm // CHUNK):
        pc_o[c] = jnp.exp(cum3[c, CHUNK - 1:CHUNK, :])
    g_o[...] = g
    bonus_o[...] = bonus
    if not has_vres:
        v_o[...] = v


def _rwkv_prep(proj, params, v_first, seq_len, tm):
    t = proj.shape[0]
    has_vres = v_first is not None
    seq_tiles = seq_len // tm
    sub = tm // SUBLANES
    rkv_w, lora_w = 3 * A_WIDTH, LORA_PACK
    rkv_blk = (C_COLS) // rkv_w
    lora_blk = (C_COLS + rkv_w + B_COLS) // lora_w

    def cur(width, blk):
        return pl.BlockSpec((tm, width), lambda i: (i, blk))

    def prev(width, blk):
        return pl.BlockSpec((SUBLANES, width), lambda i: (jnp.maximum(i * sub - 1, 0), blk))

    def vec(width):
        return pl.BlockSpec((1, width), lambda i: (0, 0))

    def mat(rows):
        return pl.BlockSpec((rows, A_WIDTH), lambda i: (0, 0))

    tok = pl.BlockSpec((tm, A_WIDTH), lambda i: (i, 0))
    ins = [proj, proj, proj, proj,
           params["mu_rkv"], params["mu_lora"],
           params["w0"], params["a0"], params["k_k"], params["k_a"], params["r_k"],
           params["w2"], params["a2"], params["g2"]]
    specs = [cur(rkv_w, rkv_blk), cur(lora_w, lora_blk),
             prev(rkv_w, rkv_blk), prev(lora_w, lora_blk),
             vec(rkv_w), vec(lora_w),
             vec(A_WIDTH), vec(A_WIDTH), vec(A_WIDTH), vec(A_WIDTH), vec(A_WIDTH),
             mat(LORA_PACK), mat(LORA_PACK), mat(LORA_PACK)]
    if has_vres:
        ins += [params["v0"], params["v2"], v_first]
        specs += [vec(A_WIDTH), mat(LORA_PACK), tok]
    wide = lambda dtype: jax.ShapeDtypeStruct((t, A_WIDTH), dtype)
    pc_spec = pl.BlockSpec((tm // CHUNK, 1, A_WIDTH), lambda i: (i, 0, 0))
    out_specs = [tok] * 5 + [pc_spec, tok, tok]
    out_shape = [wide(BF16)] * 5 + [jax.ShapeDtypeStruct((t // CHUNK, 1, A_WIDTH), F32),
                                    wide(F32), wide(F32)]
    if not has_vres:
        out_specs.append(tok)
        out_shape.append(wide(F32))
    return pl.pallas_call(
        functools.partial(_prep_kernel, has_vres, seq_tiles),
        grid=(t // tm,),
        in_specs=specs,
        out_specs=out_specs,
        out_shape=out_shape,
        compiler_params=_cparams(("parallel",)),
        name="rwkv_prep_vres" if has_vres else "rwkv_prep",
    )(*ins)


def _unit_lower_inverse(a_strict, row, col, eye, pair_diag):
    base = 8
    diag = _blk(row, base) == _blk(col, base)
    n1 = [jnp.where(diag, a, 0.0) for a in a_strict]
    n1d = [pair_diag(x) for x in n1]
    n2 = [_dot(x, d) for x, d in zip(n1, n1d)]
    n2d = [pair_diag(x) for x in n2]
    t = [eye + x for x in n1]
    t = [x + _dot(x, d) for x, d in zip(t, n2d)]
    n4d = [pair_diag(_dot(x, d)) for x, d in zip(n2, n2d)]
    t = [x + _dot(x, d) for x, d in zip(t, n4d)]
    m = base
    while m < CHUNK:
        sel = (_blk(row, 2 * m) == _blk(col, 2 * m)) & (_blk(row, m) != _blk(col, m))
        ed = [pair_diag(jnp.where(sel, a, 0.0)) for a in a_strict]
        te = [_dot(x, d) for x, d in zip(t, ed)]
        t = [x + _dot(y, pair_diag(x)) for x, y in zip(t, te)]
        m *= 2
    return t


def _scan_kernel(cps, at_ref, rt_ref, kt_ref, bt_ref, v_ref, pc_ref,
                 gate_ref, bonus_ref, lng_ref, lnb_ref, o_ref, s_ref):
    @pl.when(pl.program_id(1) == 0)
    def _():
        s_ref[...] = jnp.zeros_like(s_ref)

    n = 2 * A_HEAD_DIM
    pairs = A_HEADS // 2
    row = _iota2((CHUNK, n), 0)
    lane = _iota2((CHUNK, n), 1)
    col = lane & (A_HEAD_DIM - 1)
    lo = lane < A_HEAD_DIM
    strict = row > col
    incl = row >= col
    eye = (row == col).astype(F32)
    same_head = _blk(_iota2((n, n), 0), A_HEAD_DIM) == _blk(_iota2((n, n), 1), A_HEAD_DIM)
    items = [(c, p) for c in range(cps) for p in range(pairs)]

    def load(ref):
        return [ref[0, c * CHUNK:(c + 1) * CHUNK, p * n:(p + 1) * n] for c, p in items]

    def pair_diag(x):
        x = x.astype(BF16)
        zero = jnp.zeros_like(x)
        return jnp.concatenate([jnp.where(lo, x, zero), jnp.where(lo, zero, x)], axis=0)

    def fold(m):
        return jnp.where(lo, m[:CHUNK], m[CHUNK:])

    at = load(at_ref)
    rt = load(rt_ref)
    lhs = [jnp.concatenate([a, r], axis=0) for a, r in zip(at, rt)]
    kt = load(kt_ref)
    bt = load(bt_ref)
    lk = [_dot_nt(x, pair_diag(k)) for x, k in zip(lhs, kt)]
    lb = [_dot_nt(x, pair_diag(b)) for x, b in zip(lhs, bt)]
    a_k = [jnp.concatenate([jnp.where(strict, x[:CHUNK], 0.0), jnp.where(incl, x[CHUNK:], 0.0)],
                           axis=0) for x in lk]
    a_ab = [jnp.where(strict, x[:CHUNK], 0.0) for x in lb]
    a_rb = [jnp.where(incl, x[CHUNK:], 0.0) for x in lb]
    t_inv = _unit_lower_inverse(a_ab, row, col, eye, pair_diag)
    v = load(v_ref)
    lv = [_dot(a, pair_diag(x)) for a, x in zip(a_k, v)]
    wu = [_dot(t, jnp.concatenate([pair_diag(a), pair_diag(x[:CHUNK])], axis=1))
          for t, a, x in zip(t_inv, at, lv)]
    z = [_dot(a, jnp.concatenate([pair_diag(x[:, :n]), pair_diag(x[:, n:])], axis=1))
         for a, x in zip(a_rb, wu)]
    rw = [r.astype(F32) + x[:, :n] for r, x in zip(rt, z)]
    y0 = [x[CHUNK:] + zz[:, n:] for x, zz in zip(lv, z)]
    g = [jnp.where(same_head, _dot_tn(x[:, :n], b), 0.0).astype(BF16)
         for x, b in zip(wu, bt)]
    s_add = [fold(_dot_tn(jnp.concatenate([x[:, n:].astype(BF16), vv], axis=0),
                          jnp.concatenate([b, k], axis=0)))
             for x, vv, b, k in zip(wu, v, bt, kt)]

    inv_n = 1.0 / A_HEAD_DIM

    def head_mean(x):
        m_lo = jnp.sum(jnp.where(lo, x, 0.0), axis=-1, keepdims=True)
        m_hi = jnp.sum(jnp.where(lo, 0.0, x), axis=-1, keepdims=True)
        return jnp.where(lo, m_lo, m_hi) * inv_n

    s = [s_ref[p] for p in range(pairs)]
    for c in range(cps):
        rows = slice(c * CHUNK, (c + 1) * CHUNK)
        sd = [pair_diag(x) for x in s]
        ys = [_dot_nt(rw[c * pairs + p], sd[p]) + y0[c * pairs + p] for p in range(pairs)]
        for p in range(pairs):
            i = c * pairs + p
            cols = slice(p * n, (p + 1) * n)
            pc = pc_ref[0, c, :, cols]
            s[p] = (s[p] + _dot(s[p], g[i]) + s_add[i]) * pc
        for p in range(pairs):
            cols = slice(p * n, (p + 1) * n)
            d = ys[p] - head_mean(ys[p])
            var = head_mean(d * d)
            yn = d * lax.rsqrt(var + A_GN_EPS) * lng_ref[:, cols] + lnb_ref[:, cols]
            o_ref[0, rows, cols] = ((yn + bonus_ref[0, rows, cols])
                                    * gate_ref[0, rows, cols]).astype(o_ref.dtype)
    for p in range(pairs):
        s_ref[p] = s[p]


def _rwkv_scan(at, rt, kt, bt, v, pc, gate, bonus, ln_g, ln_b, batch, seq_len, cps):
    ts = cps * CHUNK
    shp = (batch, seq_len, A_WIDTH)
    blk = pl.BlockSpec((1, ts, A_WIDTH), lambda b, s: (b, s, 0))
    pc_blk = pl.BlockSpec((1, cps, 1, A_WIDTH), lambda b, s: (b, s, 0, 0))
    vec = pl.BlockSpec((1, A_WIDTH), lambda b, s: (0, 0))
    args = [z.reshape(shp) for z in (at, rt, kt, bt, v)]
    args += [pc.reshape(batch, seq_len // CHUNK, 1, A_WIDTH), gate.reshape(shp), bonus.reshape(shp)]
    y = pl.pallas_call(
        functools.partial(_scan_kernel, cps),
        grid=(batch, seq_len // ts),
        in_specs=[blk] * 5 + [pc_blk, blk, blk, vec, vec],
        out_specs=blk,
        out_shape=jax.ShapeDtypeStruct(shp, BF16),
        scratch_shapes=[pltpu.VMEM((A_HEADS // 2, A_HEAD_DIM, 2 * A_HEAD_DIM), F32)],
        compiler_params=_cparams(("parallel", "arbitrary")),
        name="rwkv_scan",
    )(*args, ln_g, ln_b)
    return y.reshape(batch * seq_len, A_WIDTH)


def _lru_body(first, gate_ref, x_ref, px_ref, cw_ref, cb_ref, wa_ref, ba_ref, wx_ref, bx_ref,
              lam_ref, o_ref, h_ref):
    x = x_ref[...]
    ts = x.shape[0]
    prev = jnp.where(first, 0.0, px_ref[...])
    row = _iota2(x.shape, 0)
    row8 = _iota2(prev.shape, 0)

    def delayed(d):
        rolled = pltpu.roll(x, d, 0)
        top = jnp.where(row8 < d, pltpu.roll(prev, d, 0), rolled[:SUBLANES])
        return jnp.concatenate([top, rolled[SUBLANES:]], axis=0)

    xc = cw_ref[CONV_WIDTH - 1:CONV_WIDTH, :] * x + cb_ref[...]
    for d in range(1, CONV_WIDTH):
        xc = xc + cw_ref[CONV_WIDTH - 1 - d:CONV_WIDTH - d, :] * delayed(d)

    r = jax.nn.sigmoid(_dot(xc, wa_ref[...]) + ba_ref[...])
    i = jax.nn.sigmoid(_dot(xc, wx_ref[...]) + bx_ref[...])
    log_a = -LRU_C * r * jax.nn.softplus(-lam_ref[...])
    a = jnp.exp(log_a)
    th = jnp.tanh(log_a)
    q = -2.0 * th / (1.0 - th)
    u = (q * lax.rsqrt(jnp.maximum(q, F32_TINY))) * (i * xc)

    grouped = (ts // SUBLANES, SUBLANES, B_WIDTH)
    in_group = _iota2(grouped, 1)

    def group_delay(z, d, fill):
        return jnp.where(in_group >= d, pltpu.roll(z.reshape(grouped), d, 1), fill).reshape(z.shape)

    d = 1
    while d < SUBLANES:
        u = a * group_delay(u, d, 0.0) + u
        a = a * group_delay(a, d, 1.0)
        d *= 2
    carry = h_ref[...]
    groups = []
    for g0 in range(0, ts, SUBLANES):
        hg = a[g0:g0 + SUBLANES] * carry + u[g0:g0 + SUBLANES]
        carry = hg[SUBLANES - 1:SUBLANES]
        groups.append(hg)
    h_ref[...] = carry
    h = jnp.concatenate(groups, axis=0)
    o_ref[...] = (jax.nn.gelu(gate_ref[...]) * h).astype(o_ref.dtype)


def _log_gammas():
    return np.log1p(-np.exp2(-5.0 - np.arange(C_HEADS, dtype=np.float32))).astype(np.float32)


def _ret_body(cpr, q_ref, k_ref, v_ref, g_ref, pos_ref, invf_ref, gng_ref, o_ref, st_ref):
    half = C_QK_DIM // 2
    lane = _iota2((CHUNK, LANES), 1)
    in_first_half = (lane & (C_QK_DIM - 1)) < half
    lo = lane < C_QK_DIM
    idx_r = _iota2((CHUNK, CHUNK), 0).astype(F32)
    idx_c = _iota2((CHUNK, CHUNK), 1).astype(F32)
    dist = jnp.abs(idx_r - idx_c)
    pos_in = _iota2((CHUNK, 1), 0).astype(F32)
    log_gamma = _log_gammas()

    def rope(t, cos, sin_signed):
        swapped = jnp.where(in_first_half, pltpu.roll(t, LANES - half, 1), pltpu.roll(t, half, 1))
        return t * cos + swapped * sin_signed

    intra_decay = [jnp.exp(float(lg) * dist) for lg in log_gamma]
    q_decay = [jnp.exp(float(lg) * (pos_in + 1.0)) for lg in log_gamma]
    k_decay = [jnp.exp(float(lg) * (CHUNK - 1.0 - pos_in)) for lg in log_gamma]
    chunk_decay = [float(np.exp(lg * np.float32(CHUNK))) for lg in log_gamma]
    items = [(c, h) for c in range(cpr) for h in range(C_HEADS)]

    ang = pos_ref[...] * invf_ref[...]
    cos_packed = jnp.cos(ang)
    sin_packed = jnp.sin(ang)
    lane_group = lane >> (half.bit_length() - 1)

    def spread(x, c):
        out = x
        for g in range(LANES // half):
            if g != c:
                out = jnp.where(lane_group == g, pltpu.roll(x, (half * (g - c)) % LANES, 1), out)
        return out

    q2, k2 = {}, {}
    for c in range(cpr):
        rows = slice(c * CHUNK, (c + 1) * CHUNK)
        cos = spread(cos_packed, c)
        sin = spread(sin_packed, c)
        sin_signed = jnp.where(in_first_half, -sin, sin)
        for p in range(C_HEADS // 2):
            qk_cols = slice(p * LANES, (p + 1) * LANES)
            q2[c, p] = rope(q_ref[rows, qk_cols], cos, sin_signed)
            k2[c, p] = rope(k_ref[rows, qk_cols], cos, sin_signed) * (C_QK_DIM ** -0.5)

    def mine(x, h):
        return jnp.where(lo, x, 0.0) if h % 2 == 0 else jnp.where(lo, 0.0, x)

    qh = [mine(q2[c, h // 2], h) for c, h in items]
    kh = [mine(k2[c, h // 2], h) for c, h in items]
    vh = [v_ref[c * CHUNK:(c + 1) * CHUNK, h * C_V_DIM:(h + 1) * C_V_DIM].astype(BF16)
          for c, h in items]
    scores = [_dot_nt(q, k) * intra_decay[h] for q, k, (c, h) in zip(qh, kh, items)]
    kv = [_dot_tn(k * k_decay[h], v) for k, v, (c, h) in zip(kh, vh, items)]
    intra = [_dot(s, v) for s, v in zip(scores, vh)]

    state = [st_ref[h] for h in range(C_HEADS)]
    starts = []
    for i, (c, h) in enumerate(items):
        starts.append(state[h])
        state[h] = state[h] * chunk_decay[h] + kv[i]
    for h in range(C_HEADS):
        st_ref[h] = state[h]
    cross = [_dot(q * q_decay[h], s) for q, s, (c, h) in zip(qh, starts, items)]

    for i, (c, h) in enumerate(items):
        rows = slice(c * CHUNK, (c + 1) * CHUNK)
        v_cols = slice(h * C_V_DIM, (h + 1) * C_V_DIM)
        o = intra[i] + cross[i]
        mu = jnp.mean(o, axis=-1, keepdims=True)
        d = o - mu
        var = jnp.mean(d * d, axis=-1, keepdims=True)
        on = d * lax.rsqrt(var + C_GN_EPS) * gng_ref[:, v_cols]
        o_ref[rows, v_cols] = (jax.nn.silu(g_ref[rows, v_cols]) * on).astype(o_ref.dtype)


def _mix_bc_kernel(cpr, gate_ref, x_ref, px_ref, cw_ref, cb_ref, wa_ref, ba_ref, wx_ref, bx_ref,
                   lam_ref, q_ref, k_ref, v_ref, g_ref, pos_ref, invf_ref, gng_ref,
                   yb_ref, yc_ref, h_ref, st_ref):
    first = pl.program_id(1) == 0

    @pl.when(first)
    def _():
        h_ref[...] = jnp.zeros_like(h_ref)
        st_ref[...] = jnp.zeros_like(st_ref)

    _ret_body(cpr, q_ref, k_ref, v_ref, g_ref, pos_ref, invf_ref, gng_ref, yc_ref, st_ref)
    _lru_body(first, gate_ref, x_ref, px_ref, cw_ref, cb_ref, wa_ref, ba_ref, wx_ref, bx_ref,
              lam_ref, yb_ref, h_ref)


def _mix_bc(proj, lru, posb, invf, gn_g, batch, seq_len, cpr):
    t = proj.shape[0]
    ts = cpr * CHUNK
    nst = seq_len // ts
    sub = ts // SUBLANES
    gate_blk = (C_COLS + 3 * A_WIDTH) // B_WIDTH
    x_blk = gate_blk + 1

    def tok(width, blk):
        return pl.BlockSpec((ts, width), lambda b, s: (b * nst + s, blk))

    def const(shape):
        return pl.BlockSpec(shape, lambda b, s: (0,) * len(shape))

    return pl.pallas_call(
        functools.partial(_mix_bc_kernel, cpr),
        grid=(batch, nst),
        in_specs=[tok(B_WIDTH, gate_blk), tok(B_WIDTH, x_blk),
                  pl.BlockSpec((SUBLANES, B_WIDTH),
                               lambda b, s: (jnp.maximum((b * nst + s) * sub - 1, 0), x_blk)),
                  const((CONV_WIDTH, B_WIDTH)), const((1, B_WIDTH)),
                  const((B_WIDTH, B_WIDTH)), const((1, B_WIDTH)),
                  const((B_WIDTH, B_WIDTH)), const((1, B_WIDTH)), const((1, B_WIDTH)),
                  tok(C_QK_WIDTH, 0), tok(C_QK_WIDTH, 1), tok(C_WIDTH, 1), tok(C_WIDTH, 2),
                  pl.BlockSpec((CHUNK, LANES), lambda b, s: (b * nst + s, 0)),
                  const((1, LANES)), const((1, C_WIDTH))],
        out_specs=[tok(B_WIDTH, 0), tok(C_WIDTH, 0)],
        out_shape=[jax.ShapeDtypeStruct((t, B_WIDTH), BF16),
                   jax.ShapeDtypeStruct((t, C_WIDTH), BF16)],
        scratch_shapes=[pltpu.VMEM((1, B_WIDTH), F32),
                        pltpu.VMEM((C_HEADS, LANES, C_V_DIM), F32)],
        compiler_params=_cparams(("parallel", "arbitrary")),
        name="mix_bc",
    )(proj, proj, proj, lru["conv_w"], lru["conv_b"], lru["wa"], lru["ba"], lru["wx"],
      lru["bx"], lru["lam"], proj, proj, proj, proj, posb, invf, gn_g)


def _outproj_kernel(x_ref, ya_ref, yb_ref, yc_ref, w_ref, o_ref):
    b0, c0 = A_WIDTH, A_WIDTH + B_WIDTH
    acc = jnp.dot(ya_ref[...], w_ref[0:b0, :].astype(BF16), preferred_element_type=F32)
    acc += jnp.dot(yb_ref[...], w_ref[b0:c0, :].astype(BF16), preferred_element_type=F32)
    acc += jnp.dot(yc_ref[...], w_ref[c0:, :].astype(BF16), preferred_element_type=F32)
    o_ref[...] = x_ref[...] + acc


def _outproj(x, ya, yb, yc, w_stack, layer, tm):
    t, d = x.shape

    def tok(width):
        return pl.BlockSpec((tm, width), lambda i: (i, 0))

    return pl.pallas_call(
        _outproj_kernel,
        grid=(t // tm,),
        in_specs=[tok(d), tok(A_WIDTH), tok(B_WIDTH), tok(C_WIDTH),
                  pl.BlockSpec((None, d, d), lambda i: (layer, 0, 0),
                               pipeline_mode=pl.Buffered(1))],
        out_specs=tok(d),
        out_shape=jax.ShapeDtypeStruct((t, d), F32),
        compiler_params=_cparams(("parallel",)),
        name="outproj",
    )(x, ya, yb, yc, w_stack)


def _ffn_kernel(final, x_ref, g_ref, wg_ref, wu_ref, wd_ref, fg_ref, o_ref, h_ref):
    j = pl.program_id(1)

    @pl.when(j == 0)
    def _():
        x = x_ref[...]
        h_ref[...] = _rms(x, g_ref[...]).astype(BF16)
        o_ref[...] = x

    h = h_ref[...]
    gate = jnp.dot(h, wg_ref[...].astype(BF16), preferred_element_type=F32)
    up = jnp.dot(h, wu_ref[...].astype(BF16), preferred_element_type=F32)
    act = (jax.nn.silu(gate) * up).astype(BF16)
    o_ref[...] += jnp.dot(act, wd_ref[...].astype(BF16), preferred_element_type=F32)

    if final:
        @pl.when(j == pl.num_programs(1) - 1)
        def _():
            o_ref[...] = _rms(o_ref[...], fg_ref[...])


def _ffn(x, g, wg, wu, wd, fg, layer, final, tm, tf):
    t, d = x.shape
    f = wg.shape[2]
    return pl.pallas_call(
        functools.partial(_ffn_kernel, final),
        grid=(t // tm, f // tf),
        in_specs=[pl.BlockSpec((tm, d), lambda i, j: (i, 0)),
                  pl.BlockSpec((1, d), lambda i, j: (0, 0)),
                  pl.BlockSpec((None, d, tf), lambda i, j: (layer, 0, j)),
                  pl.BlockSpec((None, d, tf), lambda i, j: (layer, 0, j)),
                  pl.BlockSpec((None, tf, d), lambda i, j: (layer, j, 0)),
                  pl.BlockSpec((1, d), lambda i, j: (0, 0))],
        out_specs=pl.BlockSpec((tm, d), lambda i, j: (i, 0)),
        out_shape=jax.ShapeDtypeStruct((t, d), F32),
        scratch_shapes=[pltpu.VMEM((tm, d), BF16)],
        compiler_params=_cparams(("parallel", "arbitrary")),
        name="ffn_final" if final else "ffn",
    )(x, g, wg, wu, wd, fg)


def _rows_at(w, offset):
    return jnp.pad(w, ((offset, LORA_PACK - offset - w.shape[0]), (0, 0)))


RELAYOUT_UNIT = 64
RELAYOUT_PARTS = 8


def _source_unit(q):
    c_units = C_COLS // RELAYOUT_UNIT
    a_units = 3 * A_WIDTH // RELAYOUT_UNIT
    b_units = B_COLS // RELAYOUT_UNIT
    last_src = (A_COLS + B_COLS + C_COLS) // RELAYOUT_UNIT - 1
    src_c = (A_COLS + B_COLS) // RELAYOUT_UNIT
    src_b = A_COLS // RELAYOUT_UNIT
    lora = jnp.minimum(a_units + (q - c_units - a_units - b_units), last_src)
    return jnp.where(q < c_units, src_c + q,
                     jnp.where(q < c_units + a_units, q - c_units,
                               jnp.where(q < c_units + a_units + b_units,
                                         src_b + (q - c_units - a_units), lora)))


def _relayout_kernel(has_vres, n_steps, *refs):
    parts, rest = refs[:RELAYOUT_PARTS], refs[RELAYOUT_PARTS:]
    o_ref = rest[-1]
    blocks = [p[...] for p in parts]
    tail = rest[0][...] if has_vres else jnp.zeros_like(blocks[-1])
    blocks[-1] = jnp.where(pl.program_id(0) == n_steps - 1, tail, blocks[-1])
    o_ref[...] = jnp.concatenate(blocks, axis=0).T.astype(BF16)


def _layer_in_weight(w_in_t, w_vres_t, layer):
    _, n, d = w_in_t.shape
    has_vres = layer > 0
    n_steps = P_COLS // (RELAYOUT_UNIT * RELAYOUT_PARTS)

    def part(s):
        return pl.BlockSpec((None, RELAYOUT_UNIT, d),
                            lambda j: (layer, _source_unit(RELAYOUT_PARTS * j + s), 0))

    ins = [w_in_t] * RELAYOUT_PARTS
    specs = [part(s) for s in range(RELAYOUT_PARTS)]
    if has_vres:
        ins.append(w_vres_t)
        specs.append(pl.BlockSpec((None, VRES_LORA, d), lambda j: (layer - 1, 0, 0)))
    return pl.pallas_call(
        functools.partial(_relayout_kernel, has_vres, n_steps),
        grid=(n_steps,),
        in_specs=specs,
        out_specs=pl.BlockSpec((d, RELAYOUT_UNIT * RELAYOUT_PARTS), lambda j: (0, j)),
        out_shape=jax.ShapeDtypeStruct((d, P_COLS), BF16),
        compiler_params=_cparams(("parallel",)),
        name="w_in_relayout",
    )(*ins)


def _block_diag(w):
    g, n, _ = w.shape
    eye = jnp.eye(g, dtype=w.dtype)
    return (eye[:, None, :, None] * w[:, :, None, :]).reshape(g * n, g * n)


def kernel(x, positions, norm1_g, w_in, tshift_mu, rwkv_w0, rwkv_w2, rwkv_a0, rwkv_a2, rwkv_g2, rwkv_k_k, rwkv_k_a, rwkv_r_k, rwkv_ln_g, rwkv_ln_b, w_in_vres, tshift_mu_vres, rwkv_v0, rwkv_v2, lru_conv_w, lru_conv_b, lru_wa, lru_ba, lru_wx, lru_bx, lru_lambda, ret_gn_g, w_out, norm2_g, ffn_w_gate, ffn_w_up, ffn_w_down, final_norm_g):
    batch, seq_len, d = x.shape
    depth = w_in.shape[0]
    t = batch * seq_len
    aw = A_WIDTH
    row = lambda v: v.reshape(1, -1).astype(F32)

    tm_proj = min(512, seq_len)
    tm_in = min(1024, seq_len)
    tm_tok = min(256, seq_len)
    cps = min(8, seq_len // CHUNK)
    cpr = min(4, seq_len // CHUNK)

    inv_freq = ROPE_THETA ** (-jnp.arange(0, C_QK_DIM, 2, dtype=F32) / C_QK_DIM)
    invf = jnp.tile(inv_freq, LANES // inv_freq.shape[0]).reshape(1, LANES)
    n_freq = inv_freq.shape[0]
    assert cpr * n_freq == LANES
    posb = positions.astype(F32).reshape(t // (cpr * CHUNK), cpr, CHUNK)
    posb = jnp.repeat(jnp.swapaxes(posb, 1, 2), n_freq, axis=-1).reshape(t // cpr, LANES)

    w_in_t = jnp.swapaxes(w_in, 1, 2)
    w_vres_t = jnp.swapaxes(w_in_vres, 1, 2)

    xf = x.reshape(t, d)
    v_first = None
    for l in range(depth):
        has_vres = l > 0
        w_l = _layer_in_weight(w_in_t, w_vres_t, l)
        proj = _inproj(xf, row(norm1_g[l]), w_l, tm_in, P_COLS // 6)

        mu = tshift_mu[l]
        o_ad, o_gd = 3 * aw + DECAY_LORA, 3 * aw + DECAY_LORA + ICLR_LORA
        mu_vd = tshift_mu_vres[l - 1] if has_vres else jnp.zeros((VRES_LORA,), F32)
        params = {
            "mu_rkv": row(mu[:3 * aw]),
            "mu_lora": row(jnp.concatenate([mu[3 * aw:A_COLS], mu_vd])),
            "w0": row(rwkv_w0[l]), "a0": row(rwkv_a0[l]), "k_k": row(rwkv_k_k[l]),
            "k_a": row(rwkv_k_a[l]), "r_k": row(rwkv_r_k[l]),
            "w2": _rows_at(rwkv_w2[l], 0), "a2": _rows_at(rwkv_a2[l], o_ad - 3 * aw),
            "g2": _rows_at(rwkv_g2[l], o_gd - 3 * aw),
        }
        if has_vres:
            params["v0"] = row(rwkv_v0[l - 1])
            params["v2"] = _rows_at(rwkv_v2[l - 1], A_COLS - 3 * aw)
        at, rt, kt, bt, v_b, pc, g_a, bonus, *v_f32 = _rwkv_prep(
            proj, params, v_first if has_vres else None, seq_len, tm_tok)
        if l == 0:
            v_first = v_f32[0]
        y_a = _rwkv_scan(at, rt, kt, bt, v_b, pc, g_a, bonus,
                         row(rwkv_ln_g[l]), row(rwkv_ln_b[l]), batch, seq_len, cps)

        lru_params = {
            "conv_w": lru_conv_w[l], "conv_b": row(lru_conv_b[l]),
            "wa": _block_diag(lru_wa[l]).astype(BF16), "ba": row(lru_ba[l]),
            "wx": _block_diag(lru_wx[l]).astype(BF16), "bx": row(lru_bx[l]),
            "lam": row(lru_lambda[l]),
        }
        y_b, y_c = _mix_bc(proj, lru_params, posb, invf, row(ret_gn_g[l]), batch, seq_len, cpr)

        xf = _outproj(xf, y_a, y_b, y_c, w_out, l, tm_proj)
        xf = _ffn(xf, row(norm2_g[l]), ffn_w_gate, ffn_w_up, ffn_w_down, row(final_norm_g),
                  l, l == depth - 1, tm_in, 256)
    return xf.reshape(batch, seq_len, d)
```

```python
import functools

import numpy as np
import jax
import jax.numpy as jnp
from jax import lax
from jax.experimental import pallas as pl
from jax.experimental.pallas import tpu as pltpu

F32 = jnp.float32
BF16 = jnp.bfloat16

NORM_EPS = 1e-6
CHUNK = 64

A_HEADS = 12
A_HEAD_DIM = 64
A_WIDTH = A_HEADS * A_HEAD_DIM
DECAY_LORA = 96
ICLR_LORA = 96
VRES_LORA = 64
GATE_LORA = 256
A_GN_EPS = 64e-5
DECAY_SCALE = float(np.exp(np.float32(-0.5)))
A_COLS = 3 * A_WIDTH + DECAY_LORA + ICLR_LORA + GATE_LORA
B_WIDTH = 512
CONV_WIDTH = 4
LRU_C = 8.0
B_COLS = 2 * B_WIDTH
C_HEADS = 6
C_QK_DIM = 64
C_V_DIM = 128
C_QK_WIDTH = C_HEADS * C_QK_DIM
C_WIDTH = C_HEADS * C_V_DIM
C_GN_EPS = 1e-5
ROPE_THETA = 10000.0
C_COLS = 2 * C_QK_WIDTH + 2 * C_WIDTH

LANES = 128
SUBLANES = 8
LORA_PACK = DECAY_LORA + ICLR_LORA + GATE_LORA + VRES_LORA
VMEM_LIMIT = 56 * 1024 * 1024

P_COLS = C_COLS + 3 * A_WIDTH + B_COLS + LORA_PACK


def _cparams(sem):
    return pltpu.CompilerParams(dimension_semantics=sem, vmem_limit_bytes=VMEM_LIMIT)


def _dot(a, b):
    return jnp.dot(a.astype(BF16), b.astype(BF16), preferred_element_type=F32)


def _dot_nt(a, b):
    return lax.dot_general(a.astype(BF16), b.astype(BF16), (((1,), (1,)), ((), ())),
                           preferred_element_type=F32)


def _dot_tn(a, b):
    return lax.dot_general(a.astype(BF16), b.astype(BF16), (((0,), (0,)), ((), ())),
                           preferred_element_type=F32)


def _iota2(shape, axis):
    return lax.broadcasted_iota(jnp.int32, shape, axis)


def _blk(idx, size):
    return idx >> (size.bit_length() - 1)


def _rms(x, g):
    ms = jnp.mean(x * x, axis=-1, keepdims=True)
    return x * lax.rsqrt(ms + NORM_EPS) * g


def _inproj_kernel(x_ref, g_ref, w_ref, o_ref, h_ref):
    @pl.when(pl.program_id(1) == 0)
    def _():
        h_ref[...] = _rms(x_ref[...], g_ref[...]).astype(BF16)

    o_ref[...] = jnp.dot(h_ref[...], w_ref[...], preferred_element_type=F32)


def _inproj(x, g, w, tm, tn):
    t, d = x.shape
    n = w.shape[1]
    return pl.pallas_call(
        _inproj_kernel,
        grid=(t // tm, n // tn),
        in_specs=[pl.BlockSpec((tm, d), lambda i, j: (i, 0)),
                  pl.BlockSpec((1, d), lambda i, j: (0, 0)),
                  pl.BlockSpec((d, tn), lambda i, j: (0, j))],
        out_specs=pl.BlockSpec((tm, tn), lambda i, j: (i, j)),
        out_shape=jax.ShapeDtypeStruct((t, n), F32),
        scratch_shapes=[pltpu.VMEM((tm, d), BF16)],
        compiler_params=_cparams(("parallel", "arbitrary")),
        name="inproj",
    )(x, g, w)


def _shift_rows(cur, prev_row):
    rolled = pltpu.roll(cur, 1, 0)
    top = rolled[:SUBLANES]
    top = jnp.where(_iota2(top.shape, 0) == 0, prev_row, top)
    return jnp.concatenate([top, rolled[SUBLANES:]], axis=0)


def _split3(x):
    hi = x.astype(BF16)
    r1 = x - hi.astype(F32)
    mid = r1.astype(BF16)
    lo = (r1 - mid.astype(F32)).astype(BF16)
    return hi, mid, lo


def _sel_dot(sel, x):
    hi, mid, lo = _split3(x)
    d = functools.partial(jnp.dot, sel, preferred_element_type=F32)
    return (d(lo) + d(mid)) + d(hi)


def _dot_sel(x, sel):
    hi = x.astype(BF16)
    mid = (x - hi.astype(F32)).astype(BF16)
    d = functools.partial(jnp.dot, preferred_element_type=F32)
    return d(mid, sel) + d(hi, sel)


def _head_sum(x, e):
    parts = [_dot_sel(x[:, i * LANES:(i + 1) * LANES], e) for i in range(x.shape[1] // LANES)]
    return jnp.concatenate(parts, axis=-1)


def _head_ones():
    r = _blk(_iota2((LANES, LANES), 0), A_HEAD_DIM)
    c = _blk(_iota2((LANES, LANES), 1), A_HEAD_DIM)
    return (r == c).astype(BF16)


def _prep_kernel(has_vres, seq_tiles, *refs):
    if has_vres:
        (rkv_ref, lora_ref, prkv_ref, plora_ref, mu_rkv, mu_lora,
         w0, a0, k_k, k_a, r_k, w2, a2, g2, v0, v2, vfirst_ref,
         at_o, rt_o, kt_o, bt_o, vb_o, pc_o, g_o, bonus_o) = refs
    else:
        (rkv_ref, lora_ref, prkv_ref, plora_ref, mu_rkv, mu_lora,
         w0, a0, k_k, k_a, r_k, w2, a2, g2,
         at_o, rt_o, kt_o, bt_o, vb_o, pc_o, g_o, bonus_o, v_o) = refs
    first = (pl.program_id(0) % seq_tiles) == 0
    last_row = slice(SUBLANES - 1, SUBLANES)

    def mixed(cur_ref, prev_ref, mu_ref):
        cur = cur_ref[...]
        prev_row = jnp.where(first, 0.0, prev_ref[last_row, :])
        return cur + (_shift_rows(cur, prev_row) - cur) * mu_ref[...]

    rkv = mixed(rkv_ref, prkv_ref, mu_rkv)
    lora = mixed(lora_ref, plora_ref, mu_lora)
    r = rkv[:, 0:A_WIDTH]
    k = rkv[:, A_WIDTH:2 * A_WIDTH]
    v = rkv[:, 2 * A_WIDTH:3 * A_WIDTH]
    y_w = w0[...] + _dot(jnp.tanh(lora), w2[...])
    lw = -DECAY_SCALE * jax.nn.sigmoid(y_w)
    a = jax.nn.sigmoid(a0[...] + _dot(lora, a2[...]))
    g = _dot(jax.nn.sigmoid(lora), g2[...])
    if has_vres:
        v = v + (vfirst_ref[...] - v) * jax.nn.sigmoid(v0[...] + _dot(lora, v2[...]))

    e = _head_ones()
    kk = k * k_k[...]
    kk = kk * lax.rsqrt(jnp.maximum(_head_sum(kk * kk, e), 1e-24))
    k = k * (1.0 + (a - 1.0) * k_a[...])
    b = kk * a
    bonus = _head_sum(r * k * r_k[...], e) * v

    tm = lw.shape[0]
    row = _iota2((tm, tm), 0)
    col = _iota2((tm, tm), 1)
    tri = (_blk(row, CHUNK) == _blk(col, CHUNK)) & (row >= col)
    cum = _sel_dot(tri.astype(BF16), lw)
    cum3 = cum.reshape(tm // CHUNK, CHUNK, A_WIDTH)
    e_in = jnp.exp(cum)
    e_out = jnp.exp(-cum)
    at_o[...] = (-kk * jnp.exp(cum - lw)).astype(BF16)
    rt_o[...] = (r * e_in).astype(BF16)
    kt_o[...] = (k * e_out).astype(BF16)
    bt_o[...] = (b * e_out).astype(BF16)
    vb_o[...] = v.astype(BF16)
    for c in range(tm // CHUNK):
        pc_o[c] = jnp.exp(cum3[c, CHUNK - 1:CHUNK, :])
    g_o[...] = g
    bonus_o[...] = bonus
    if not has_vres:
        v_o[...] = v


def _rwkv_prep(proj, params, v_first, seq_len, tm):
    t = proj.shape[0]
    has_vres = v_first is not None
    seq_tiles = seq_len // tm
    sub = tm // SUBLANES
    rkv_w, lora_w = 3 * A_WIDTH, LORA_PACK
    rkv_blk = (C_COLS) // rkv_w
    lora_blk = (C_COLS + rkv_w + B_COLS) // lora_w

    def cur(width, blk):
        return pl.BlockSpec((tm, width), lambda i: (i, blk))

    def prev(width, blk):
        return pl.BlockSpec((SUBLANES, width), lambda i: (jnp.maximum(i * sub - 1, 0), blk))

    def vec(width):
        return pl.BlockSpec((1, width), lambda i: (0, 0))

    def mat(rows):
        return pl.BlockSpec((rows, A_WIDTH), lambda i: (0, 0))

    tok = pl.BlockSpec((tm, A_WIDTH), lambda i: (i, 0))
    ins = [proj, proj, proj, proj,
           params["mu_rkv"], params["mu_lora"],
           params["w0"], params["a0"], params["k_k"], params["k_a"], params["r_k"],
           params["w2"], params["a2"], params["g2"]]
    specs = [cur(rkv_w, rkv_blk), cur(lora_w, lora_blk),
             prev(rkv_w, rkv_blk), prev(lora_w, lora_blk),
             vec(rkv_w), vec(lora_w),
             vec(A_WIDTH), vec(A_WIDTH), vec(A_WIDTH), vec(A_WIDTH), vec(A_WIDTH),
             mat(LORA_PACK), mat(LORA_PACK), mat(LORA_PACK)]
    if has_vres:
        ins += [params["v0"], params["v2"], v_first]
        specs += [vec(A_WIDTH), mat(LORA_PACK), tok]
    wide = lambda dtype: jax.ShapeDtypeStruct((t, A_WIDTH), dtype)
    pc_spec = pl.BlockSpec((tm // CHUNK, 1, A_WIDTH), lambda i: (i, 0, 0))
    out_specs = [tok] * 5 + [pc_spec, tok, tok]
    out_shape = [wide(BF16)] * 5 + [jax.ShapeDtypeStruct((t // CHUNK, 1, A_WIDTH), F32),
                                    wide(F32), wide(F32)]
    if not has_vres:
        out_specs.append(tok)
        out_shape.append(wide(F32))
    return pl.pallas_call(
        functools.partial(_prep_kernel, has_vres, seq_tiles),
        grid=(t // tm,),
        in_specs=specs,
        out_specs=out_specs,
        out_shape=out_shape,
        compiler_params=_cparams(("parallel",)),
        name="rwkv_prep_vres" if has_vres else "rwkv_prep",
    )(*ins)


def _unit_lower_inverse(a_strict, row, col, eye, pair_diag):
    base = 8
    diag = _blk(row, base) == _blk(col, base)
    n1 = [jnp.where(diag, a, 0.0) for a in a_strict]
    n1d = [pair_diag(x) for x in n1]
    n2 = [_dot(x, d) for x, d in zip(n1, n1d)]
    n2d = [pair_diag(x) for x in n2]
    t = [eye + x for x in n1]
    t = [x + _dot(x, d) for x, d in zip(t, n2d)]
    n4d = [pair_diag(_dot(x, d)) for x, d in zip(n2, n2d)]
    t = [x + _dot(x, d) for x, d in zip(t, n4d)]
    m = base
    while m < CHUNK:
        sel = (_blk(row, 2 * m) == _blk(col, 2 * m)) & (_blk(row, m) != _blk(col, m))
        ed = [pair_diag(jnp.where(sel, a, 0.0)) for a in a_strict]
        te = [_dot(x, d) for x, d in zip(t, ed)]
        t = [x + _dot(y, pair_diag(x)) for x, y in zip(t, te)]
        m *= 2
    return t


def _scan_kernel(cps, at_ref, rt_ref, kt_ref, bt_ref, v_ref, pc_ref,
                 gate_ref, bonus_ref, lng_ref, lnb_ref, o_ref, s_ref):
    @pl.when(pl.program_id(1) == 0)
    def _():
        s_ref[...] = jnp.zeros_like(s_ref)

    n = 2 * A_HEAD_DIM
    pairs = A_HEADS // 2
    row = _iota2((CHUNK, n), 0)
    lane = _iota2((CHUNK, n), 1)
    col = lane & (A_HEAD_DIM - 1)
    lo = lane < A_HEAD_DIM
    strict = row > col
    incl = row >= col
    eye = (row == col).astype(F32)
    same_head = _blk(_iota2((n, n), 0), A_HEAD_DIM) == _blk(_iota2((n, n), 1), A_HEAD_DIM)
    items = [(c, p) for c in range(cps) for p in range(pairs)]

    def load(ref):
        return [ref[0, c * CHUNK:(c + 1) * CHUNK, p * n:(p + 1) * n] for c, p in items]

    def pair_diag(x):
        x = x.astype(BF16)
        zero = jnp.zeros_like(x)
        return jnp.concatenate([jnp.where(lo, x, zero), jnp.where(lo, zero, x)], axis=0)

    def fold(m):
        return jnp.where(lo, m[:CHUNK], m[CHUNK:])

    at = load(at_ref)
    rt = load(rt_ref)
    lhs = [jnp.concatenate([a, r], axis=0) for a, r in zip(at, rt)]
    kt = load(kt_ref)
    bt = load(bt_ref)
    lk = [_dot_nt(x, pair_diag(k)) for x, k in zip(lhs, kt)]
    lb = [_dot_nt(x, pair_diag(b)) for x, b in zip(lhs, bt)]
    a_k = [jnp.concatenate([jnp.where(strict, x[:CHUNK], 0.0), jnp.where(incl, x[CHUNK:], 0.0)],
                           axis=0) for x in lk]
    a_ab = [jnp.where(strict, x[:CHUNK], 0.0) for x in lb]
    a_rb = [jnp.where(incl, x[CHUNK:], 0.0) for x in lb]
    t_inv = _unit_lower_inverse(a_ab, row, col, eye, pair_diag)
    v = load(v_ref)
    lv = [_dot(a, pair_diag(x)) for a, x in zip(a_k, v)]
    wu = [_dot(t, jnp.concatenate([pair_diag(a), pair_diag(x[:CHUNK])], axis=1))
          for t, a, x in zip(t_inv, at, lv)]
    z = [_dot(a, jnp.concatenate([pair_diag(x[:, :n]), pair_diag(x[:, n:])], axis=1))
         for a, x in zip(a_rb, wu)]
    rw = [r.astype(F32) + x[:, :n] for r, x in zip(rt, z)]
    y0 = [x[CHUNK:] + zz[:, n:] for x, zz in zip(lv, z)]
    g = [jnp.where(same_head, _dot_tn(x[:, :n], b), 0.0).astype(BF16)
         for x, b in zip(wu, bt)]
    s_add = [fold(_dot_tn(jnp.concatenate([x[:, n:].astype(BF16), vv], axis=0),
                          jnp.concatenate([b, k], axis=0)))
             for x, vv, b, k in zip(wu, v, bt, kt)]

    inv_n = 1.0 / A_HEAD_DIM

    def head_mean(x):
        m_lo = jnp.sum(jnp.where(lo, x, 0.0), axis=-1, keepdims=True)
        m_hi = jnp.sum(jnp.where(lo, 0.0, x), axis=-1, keepdims=True)
        return jnp.where(lo, m_lo, m_hi) * inv_n

    s = [s_ref[p] for p in range(pairs)]
    for c in range(cps):
        rows = slice(c * CHUNK, (c + 1) * CHUNK)
        sd = [pair_diag(x) for x in s]
        ys = [_dot_nt(rw[c * pairs + p], sd[p]) + y0[c * pairs + p] for p in range(pairs)]
        for p in range(pairs):
            i = c * pairs + p
            cols = slice(p * n, (p + 1) * n)
            pc = pc_ref[0, c, :, cols]
            s[p] = (s[p] + _dot(s[p], g[i]) + s_add[i]) * pc
        for p in range(pairs):
            cols = slice(p * n, (p + 1) * n)
            d = ys[p] - head_mean(ys[p])
            var = head_mean(d * d)
            yn = d * lax.rsqrt(var + A_GN_EPS) * lng_ref[:, cols] + lnb_ref[:, cols]
            o_ref[0, rows, cols] = ((yn + bonus_ref[0, rows, cols])
                                    * gate_ref[0, rows, cols]).astype(o_ref.dtype)
    for p in range(pairs):
        s_ref[p] = s[p]


def _rwkv_scan(at, rt, kt, bt, v, pc, gate, bonus, ln_g, ln_b, batch, seq_len, cps):
    ts = cps * CHUNK
    shp = (batch, seq_len, A_WIDTH)
    blk = pl.BlockSpec((1, ts, A_WIDTH), lambda b, s: (b, s, 0))
    pc_blk = pl.BlockSpec((1, cps, 1, A_WIDTH), lambda b, s: (b, s, 0, 0))
    vec = pl.BlockSpec((1, A_WIDTH), lambda b, s: (0, 0))
    args = [z.reshape(shp) for z in (at, rt, kt, bt, v)]
    args += [pc.reshape(batch, seq_len // CHUNK, 1, A_WIDTH), gate.reshape(shp), bonus.reshape(shp)]
    y = pl.pallas_call(
        functools.partial(_scan_kernel, cps),
        grid=(batch, seq_len // ts),
        in_specs=[blk] * 5 + [pc_blk, blk, blk, vec, vec],
        out_specs=blk,
        out_shape=jax.ShapeDtypeStruct(shp, BF16),
        scratch_shapes=[pltpu.VMEM((A_HEADS // 2, A_HEAD_DIM, 2 * A_HEAD_DIM), F32)],
        compiler_params=_cparams(("parallel", "arbitrary")),
        name="rwkv_scan",
    )(*args, ln_g, ln_b)
    return y.reshape(batch * seq_len, A_WIDTH)


def _lru_body(first, gate_ref, x_ref, px_ref, cw_ref, cb_ref, wa_ref, ba_ref, wx_ref, bx_ref,
              lam_ref, o_ref, h_ref):
    x = x_ref[...]
    ts = x.shape[0]
    prev = jnp.where(first, 0.0, px_ref[...])
    row = _iota2(x.shape, 0)
    row8 = _iota2(prev.shape, 0)

    def delayed(d):
        rolled = pltpu.roll(x, d, 0)
        top = jnp.where(row8 < d, pltpu.roll(prev, d, 0), rolled[:SUBLANES])
        return jnp.concatenate([top, rolled[SUBLANES:]], axis=0)

    xc = cw_ref[CONV_WIDTH - 1:CONV_WIDTH, :] * x + cb_ref[...]
    for d in range(1, CONV_WIDTH):
        xc = xc + cw_ref[CONV_WIDTH - 1 - d:CONV_WIDTH - d, :] * delayed(d)

    r = jax.nn.sigmoid(_dot(xc, wa_ref[...]) + ba_ref[...])
    i = jax.nn.sigmoid(_dot(xc, wx_ref[...]) + bx_ref[...])
    log_a = -LRU_C * r * jax.nn.softplus(-lam_ref[...])
    a = jnp.exp(log_a)
    th = jnp.tanh(log_a)
    u = jnp.sqrt(-2.0 * th / (1.0 - th)) * (i * xc)

    grouped = (ts // SUBLANES, SUBLANES, B_WIDTH)
    in_group = _iota2(grouped, 1)

    def group_delay(z, d, fill):
        return jnp.where(in_group >= d, pltpu.roll(z.reshape(grouped), d, 1), fill).reshape(z.shape)

    d = 1
    while d < SUBLANES:
        u = a * group_delay(u, d, 0.0) + u
        a = a * group_delay(a, d, 1.0)
        d *= 2
    carry = h_ref[...]
    groups = []
    for g0 in range(0, ts, SUBLANES):
        hg = a[g0:g0 + SUBLANES] * carry + u[g0:g0 + SUBLANES]
        carry = hg[SUBLANES - 1:SUBLANES]
        groups.append(hg)
    h_ref[...] = carry
    h = jnp.concatenate(groups, axis=0)
    o_ref[...] = (jax.nn.gelu(gate_ref[...]) * h).astype(o_ref.dtype)


def _log_gammas():
    return np.log1p(-np.exp2(-5.0 - np.arange(C_HEADS, dtype=np.float32))).astype(np.float32)


def _ret_body(cpr, q_ref, k_ref, v_ref, g_ref, pos_ref, invf_ref, gng_ref, o_ref, st_ref):
    half = C_QK_DIM // 2
    lane = _iota2((CHUNK, LANES), 1)
    in_first_half = (lane & (C_QK_DIM - 1)) < half
    lo = lane < C_QK_DIM
    idx_r = _iota2((CHUNK, CHUNK), 0).astype(F32)
    idx_c = _iota2((CHUNK, CHUNK), 1).astype(F32)
    dist = jnp.abs(idx_r - idx_c)
    pos_in = _iota2((CHUNK, 1), 0).astype(F32)
    log_gamma = _log_gammas()

    def rope(t, cos, sin_signed):
        swapped = jnp.where(in_first_half, pltpu.roll(t, LANES - half, 1), pltpu.roll(t, half, 1))
        return t * cos + swapped * sin_signed

    intra_decay = [jnp.exp(float(lg) * dist) for lg in log_gamma]
    q_decay = [jnp.exp(float(lg) * (pos_in + 1.0)) for lg in log_gamma]
    k_decay = [jnp.exp(float(lg) * (CHUNK - 1.0 - pos_in)) for lg in log_gamma]
    chunk_decay = [float(np.exp(lg * np.float32(CHUNK))) for lg in log_gamma]
    items = [(c, h) for c in range(cpr) for h in range(C_HEADS)]

    ang = pos_ref[...] * invf_ref[...]
    cos_packed = jnp.cos(ang)
    sin_packed = jnp.sin(ang)
    lane_group = lane >> (half.bit_length() - 1)

    def spread(x, c):
        out = x
        for g in range(LANES // half):
            if g != c:
                out = jnp.where(lane_group == g, pltpu.roll(x, (half * (g - c)) % LANES, 1), out)
        return out

    q2, k2 = {}, {}
    for c in range(cpr):
        rows = slice(c * CHUNK, (c + 1) * CHUNK)
        cos = spread(cos_packed, c)
        sin = spread(sin_packed, c)
        sin_signed = jnp.where(in_first_half, -sin, sin)
        for p in range(C_HEADS // 2):
            qk_cols = slice(p * LANES, (p + 1) * LANES)
            q2[c, p] = rope(q_ref[rows, qk_cols], cos, sin_signed)
            k2[c, p] = rope(k_ref[rows, qk_cols], cos, sin_signed) * (C_QK_DIM ** -0.5)

    def mine(x, h):
        return jnp.where(lo, x, 0.0) if h % 2 == 0 else jnp.where(lo, 0.0, x)

    qh = [mine(q2[c, h // 2], h) for c, h in items]
    kh = [mine(k2[c, h // 2], h) for c, h in items]
    vh = [v_ref[c * CHUNK:(c + 1) * CHUNK, h * C_V_DIM:(h + 1) * C_V_DIM].astype(BF16)
          for c, h in items]
    scores = [_dot_nt(q, k) * intra_decay[h] for q, k, (c, h) in zip(qh, kh, items)]
    kv = [_dot_tn(k * k_decay[h], v) for k, v, (c, h) in zip(kh, vh, items)]
    intra = [_dot(s, v) for s, v in zip(scores, vh)]

    state = [st_ref[h] for h in range(C_HEADS)]
    starts = []
    for i, (c, h) in enumerate(items):
        starts.append(state[h])
        state[h] = state[h] * chunk_decay[h] + kv[i]
    for h in range(C_HEADS):
        st_ref[h] = state[h]
    cross = [_dot(q * q_decay[h], s) for q, s, (c, h) in zip(qh, starts, items)]

    for i, (c, h) in enumerate(items):
        rows = slice(c * CHUNK, (c + 1) * CHUNK)
        v_cols = slice(h * C_V_DIM, (h + 1) * C_V_DIM)
        o = intra[i] + cross[i]
        mu = jnp.mean(o, axis=-1, keepdims=True)
        d = o - mu
        var = jnp.mean(d * d, axis=-1, keepdims=True)
        on = d * lax.rsqrt(var + C_GN_EPS) * gng_ref[:, v_cols]
        o_ref[rows, v_cols] = (jax.nn.silu(g_ref[rows, v_cols]) * on).astype(o_ref.dtype)


def _mix_bc_kernel(cpr, gate_ref, x_ref, px_ref, cw_ref, cb_ref, wa_ref, ba_ref, wx_ref, bx_ref,
                   lam_ref, q_ref, k_ref, v_ref, g_ref, pos_ref, invf_ref, gng_ref,
                   yb_ref, yc_ref, h_ref, st_ref):
    first = pl.program_id(1) == 0

    @pl.when(first)
    def _():
        h_ref[...] = jnp.zeros_like(h_ref)
        st_ref[...] = jnp.zeros_like(st_ref)

    _ret_body(cpr, q_ref, k_ref, v_ref, g_ref, pos_ref, invf_ref, gng_ref, yc_ref, st_ref)
    _lru_body(first, gate_ref, x_ref, px_ref, cw_ref, cb_ref, wa_ref, ba_ref, wx_ref, bx_ref,
              lam_ref, yb_ref, h_ref)


def _mix_bc(proj, lru, posb, invf, gn_g, batch, seq_len, cpr):
    t = proj.shape[0]
    ts = cpr * CHUNK
    nst = seq_len // ts
    sub = ts // SUBLANES
    gate_blk = (C_COLS + 3 * A_WIDTH) // B_WIDTH
    x_blk = gate_blk + 1

    def tok(width, blk):
        return pl.BlockSpec((ts, width), lambda b, s: (b * nst + s, blk))

    def const(shape):
        return pl.BlockSpec(shape, lambda b, s: (0,) * len(shape))

    return pl.pallas_call(
        functools.partial(_mix_bc_kernel, cpr),
        grid=(batch, nst),
        in_specs=[tok(B_WIDTH, gate_blk), tok(B_WIDTH, x_blk),
                  pl.BlockSpec((SUBLANES, B_WIDTH),
                               lambda b, s: (jnp.maximum((b * nst + s) * sub - 1, 0), x_blk)),
                  const((CONV_WIDTH, B_WIDTH)), const((1, B_WIDTH)),
                  const((B_WIDTH, B_WIDTH)), const((1, B_WIDTH)),
                  const((B_WIDTH, B_WIDTH)), const((1, B_WIDTH)), const((1, B_WIDTH)),
                  tok(C_QK_WIDTH, 0), tok(C_QK_WIDTH, 1), tok(C_WIDTH, 1), tok(C_WIDTH, 2),
                  pl.BlockSpec((CHUNK, LANES), lambda b, s: (b * nst + s, 0)),
                  const((1, LANES)), const((1, C_WIDTH))],
        out_specs=[tok(B_WIDTH, 0), tok(C_WIDTH, 0)],
        out_shape=[jax.ShapeDtypeStruct((t, B_WIDTH), BF16),
                   jax.ShapeDtypeStruct((t, C_WIDTH), BF16)],
        scratch_shapes=[pltpu.VMEM((1, B_WIDTH), F32),
                        pltpu.VMEM((C_HEADS, LANES, C_V_DIM), F32)],
        compiler_params=_cparams(("parallel", "arbitrary")),
        name="mix_bc",
    )(proj, proj, proj, lru["conv_w"], lru["conv_b"], lru["wa"], lru["ba"], lru["wx"],
      lru["bx"], lru["lam"], proj, proj, proj, proj, posb, invf, gn_g)


def _outproj_kernel(x_ref, ya_ref, yb_ref, yc_ref, w_ref, o_ref):
    b0, c0 = A_WIDTH, A_WIDTH + B_WIDTH
    acc = jnp.dot(ya_ref[...], w_ref[0:b0, :].astype(BF16), preferred_element_type=F32)
    acc += jnp.dot(yb_ref[...], w_ref[b0:c0, :].astype(BF16), preferred_element_type=F32)
    acc += jnp.dot(yc_ref[...], w_ref[c0:, :].astype(BF16), preferred_element_type=F32)
    o_ref[...] = x_ref[...] + acc


def _outproj(x, ya, yb, yc, w_stack, layer, tm):
    t, d = x.shape

    def tok(width):
        return pl.BlockSpec((tm, width), lambda i: (i, 0))

    return pl.pallas_call(
        _outproj_kernel,
        grid=(t // tm,),
        in_specs=[tok(d), tok(A_WIDTH), tok(B_WIDTH), tok(C_WIDTH),
                  pl.BlockSpec((None, d, d), lambda i: (layer, 0, 0),
                               pipeline_mode=pl.Buffered(1))],
        out_specs=tok(d),
        out_shape=jax.ShapeDtypeStruct((t, d), F32),
        compiler_params=_cparams(("parallel",)),
        name="outproj",
    )(x, ya, yb, yc, w_stack)


def _ffn_kernel(final, x_ref, g_ref, wg_ref, wu_ref, wd_ref, fg_ref, o_ref, h_ref):
    j = pl.program_id(1)

    @pl.when(j == 0)
    def _():
        x = x_ref[...]
        h_ref[...] = _rms(x, g_ref[...]).astype(BF16)
        o_ref[...] = x

    h = h_ref[...]
    gate = jnp.dot(h, wg_ref[...].astype(BF16), preferred_element_type=F32)
    up = jnp.dot(h, wu_ref[...].astype(BF16), preferred_element_type=F32)
    act = (jax.nn.silu(gate) * up).astype(BF16)
    o_ref[...] += jnp.dot(act, wd_ref[...].astype(BF16), preferred_element_type=F32)

    if final:
        @pl.when(j == pl.num_programs(1) - 1)
        def _():
            o_ref[...] = _rms(o_ref[...], fg_ref[...])


def _ffn(x, g, wg, wu, wd, fg, layer, final, tm, tf):
    t, d = x.shape
    f = wg.shape[2]
    return pl.pallas_call(
        functools.partial(_ffn_kernel, final),
        grid=(t // tm, f // tf),
        in_specs=[pl.BlockSpec((tm, d), lambda i, j: (i, 0)),
                  pl.BlockSpec((1, d), lambda i, j: (0, 0)),
                  pl.BlockSpec((None, d, tf), lambda i, j: (layer, 0, j)),
                  pl.BlockSpec((None, d, tf), lambda i, j: (layer, 0, j)),
                  pl.BlockSpec((None, tf, d), lambda i, j: (layer, j, 0)),
                  pl.BlockSpec((1, d), lambda i, j: (0, 0))],
        out_specs=pl.BlockSpec((tm, d), lambda i, j: (i, 0)),
        out_shape=jax.ShapeDtypeStruct((t, d), F32),
        scratch_shapes=[pltpu.VMEM((tm, d), BF16)],
        compiler_params=_cparams(("parallel", "arbitrary")),
        name="ffn_final" if final else "ffn",
    )(x, g, wg, wu, wd, fg)


def _rows_at(w, offset):
    return jnp.pad(w, ((offset, LORA_PACK - offset - w.shape[0]), (0, 0)))


RELAYOUT_UNIT = 64
RELAYOUT_PARTS = 8


def _source_unit(q):
    c_units = C_COLS // RELAYOUT_UNIT
    a_units = 3 * A_WIDTH // RELAYOUT_UNIT
    b_units = B_COLS // RELAYOUT_UNIT
    last_src = (A_COLS + B_COLS + C_COLS) // RELAYOUT_UNIT - 1
    src_c = (A_COLS + B_COLS) // RELAYOUT_UNIT
    src_b = A_COLS // RELAYOUT_UNIT
    lora = jnp.minimum(a_units + (q - c_units - a_units - b_units), last_src)
    return jnp.where(q < c_units, src_c + q,
                     jnp.where(q < c_units + a_units, q - c_units,
                               jnp.where(q < c_units + a_units + b_units,
                                         src_b + (q - c_units - a_units), lora)))


def _relayout_kernel(has_vres, n_steps, *refs):
    parts, rest = refs[:RELAYOUT_PARTS], refs[RELAYOUT_PARTS:]
    o_ref = rest[-1]
    blocks = [p[...] for p in parts]
    tail = rest[0][...] if has_vres else jnp.zeros_like(blocks[-1])
    blocks[-1] = jnp.where(pl.program_id(0) == n_steps - 1, tail, blocks[-1])
    o_ref[...] = jnp.concatenate(blocks, axis=0).T.astype(BF16)


def _layer_in_weight(w_in_t, w_vres_t, layer):
    _, n, d = w_in_t.shape
    has_vres = layer > 0
    n_steps = P_COLS // (RELAYOUT_UNIT * RELAYOUT_PARTS)

    def part(s):
        return pl.BlockSpec((None, RELAYOUT_UNIT, d),
                            lambda j: (layer, _source_unit(RELAYOUT_PARTS * j + s), 0))

    ins = [w_in_t] * RELAYOUT_PARTS
    specs = [part(s) for s in range(RELAYOUT_PARTS)]
    if has_vres:
        ins.append(w_vres_t)
        specs.append(pl.BlockSpec((None, VRES_LORA, d), lambda j: (layer - 1, 0, 0)))
    return pl.pallas_call(
        functools.partial(_relayout_kernel, has_vres, n_steps),
        grid=(n_steps,),
        in_specs=specs,
        out_specs=pl.BlockSpec((d, RELAYOUT_UNIT * RELAYOUT_PARTS), lambda j: (0, j)),
        out_shape=jax.ShapeDtypeStruct((d, P_COLS), BF16),
        compiler_params=_cparams(("parallel",)),
        name="w_in_relayout",
    )(*ins)


def _block_diag(w):
    g, n, _ = w.shape
    eye = jnp.eye(g, dtype=w.dtype)
    return (eye[:, None, :, None] * w[:, :, None, :]).reshape(g * n, g * n)


def kernel(x, positions, norm1_g, w_in, tshift_mu, rwkv_w0, rwkv_w2, rwkv_a0, rwkv_a2, rwkv_g2, rwkv_k_k, rwkv_k_a, rwkv_r_k, rwkv_ln_g, rwkv_ln_b, w_in_vres, tshift_mu_vres, rwkv_v0, rwkv_v2, lru_conv_w, lru_conv_b, lru_wa, lru_ba, lru_wx, lru_bx, lru_lambda, ret_gn_g, w_out, norm2_g, ffn_w_gate, ffn_w_up, ffn_w_down, final_norm_g):
    batch, seq_len, d = x.shape
    depth = w_in.shape[0]
    t = batch * seq_len
    aw = A_WIDTH
    row = lambda v: v.reshape(1, -1).astype(F32)

    tm_proj = min(512, seq_len)
    tm_in = min(1024, seq_len)
    tm_tok = min(256, seq_len)
    cps = min(8, seq_len // CHUNK)
    cpr = min(4, seq_len // CHUNK)

    inv_freq = ROPE_THETA ** (-jnp.arange(0, C_QK_DIM, 2, dtype=F32) / C_QK_DIM)
    invf = jnp.tile(inv_freq, LANES // inv_freq.shape[0]).reshape(1, LANES)
    n_freq = inv_freq.shape[0]
    assert cpr * n_freq == LANES
    posb = positions.astype(F32).reshape(t // (cpr * CHUNK), cpr, CHUNK)
    posb = jnp.repeat(jnp.swapaxes(posb, 1, 2), n_freq, axis=-1).reshape(t // cpr, LANES)

    w_in_t = jnp.swapaxes(w_in, 1, 2)
    w_vres_t = jnp.swapaxes(w_in_vres, 1, 2)

    xf = x.reshape(t, d)
    v_first = None
    for l in range(depth):
        has_vres = l > 0
        w_l = _layer_in_weight(w_in_t, w_vres_t, l)
        proj = _inproj(xf, row(norm1_g[l]), w_l, tm_in, P_COLS // 3)

        mu = tshift_mu[l]
        o_ad, o_gd = 3 * aw + DECAY_LORA, 3 * aw + DECAY_LORA + ICLR_LORA
        mu_vd = tshift_mu_vres[l - 1] if has_vres else jnp.zeros((VRES_LORA,), F32)
        params = {
            "mu_rkv": row(mu[:3 * aw]),
            "mu_lora": row(jnp.concatenate([mu[3 * aw:A_COLS], mu_vd])),
            "w0": row(rwkv_w0[l]), "a0": row(rwkv_a0[l]), "k_k": row(rwkv_k_k[l]),
            "k_a": row(rwkv_k_a[l]), "r_k": row(rwkv_r_k[l]),
            "w2": _rows_at(rwkv_w2[l], 0), "a2": _rows_at(rwkv_a2[l], o_ad - 3 * aw),
            "g2": _rows_at(rwkv_g2[l], o_gd - 3 * aw),
        }
        if has_vres:
            params["v0"] = row(rwkv_v0[l - 1])
            params["v2"] = _rows_at(rwkv_v2[l - 1], A_COLS - 3 * aw)
        at, rt, kt, bt, v_b, pc, g_a, bonus, *v_f32 = _rwkv_prep(
            proj, params, v_first if has_vres else None, seq_len, tm_tok)
        if l == 0:
            v_first = v_f32[0]
        y_a = _rwkv_scan(at, rt, kt, bt, v_b, pc, g_a, bonus,
                         row(rwkv_ln_g[l]), row(rwkv_ln_b[l]), batch, seq_len, cps)

        lru_params = {
            "conv_w": lru_conv_w[l], "conv_b": row(lru_conv_b[l]),
            "wa": _block_diag(lru_wa[l]).astype(BF16), "ba": row(lru_ba[l]),
            "wx": _block_diag(lru_wx[l]).astype(BF16), "bx": row(lru_bx[l]),
            "lam": row(lru_lambda[l]),
        }
        y_b, y_c = _mix_bc(proj, lru_params, posb, invf, row(ret_gn_g[l]), batch, seq_len, cpr)

        xf = _outproj(xf, y_a, y_b, y_c, w_out, l, tm_proj)
        xf = _ffn(xf, row(norm2_g[l]), ffn_w_gate, ffn_w_up, ffn_w_down, row(final_norm_g),
                  l, l == depth - 1, tm_in, 256)
    return xf.reshape(batch, seq_len, d)
```

```python
import functools

import numpy as np
import jax
import jax.numpy as jnp
from jax import lax
from jax.experimental import pallas as pl
from jax.experimental.pallas import tpu as pltpu

F32 = jnp.float32
BF16 = jnp.bfloat16

NORM_EPS = 1e-6
CHUNK = 64

A_HEADS = 12
A_HEAD_DIM = 64
A_WIDTH = A_HEADS * A_HEAD_DIM
DECAY_LORA = 96
ICLR_LORA = 96
VRES_LORA = 64
GATE_LORA = 256
A_GN_EPS = 64e-5
DECAY_SCALE = float(np.exp(np.float32(-0.5)))
A_COLS = 3 * A_WIDTH + DECAY_LORA + ICLR_LORA + GATE_LORA
B_WIDTH = 512
CONV_WIDTH = 4
LRU_C = 8.0
B_COLS = 2 * B_WIDTH
C_HEADS = 6
C_QK_DIM = 64
C_V_DIM = 128
C_QK_WIDTH = C_HEADS * C_QK_DIM
C_WIDTH = C_HEADS * C_V_DIM
C_GN_EPS = 1e-5
ROPE_THETA = 10000.0
C_COLS = 2 * C_QK_WIDTH + 2 * C_WIDTH

LANES = 128
SUBLANES = 8
LORA_PACK = DECAY_LORA + ICLR_LORA + GATE_LORA + VRES_LORA
VMEM_LIMIT = 56 * 1024 * 1024

P_COLS = C_COLS + 3 * A_WIDTH + B_COLS + LORA_PACK


def _cparams(sem):
    return pltpu.CompilerParams(dimension_semantics=sem, vmem_limit_bytes=VMEM_LIMIT)


def _dot(a, b):
    return jnp.dot(a.astype(BF16), b.astype(BF16), preferred_element_type=F32)


def _dot_nt(a, b):
    return lax.dot_general(a.astype(BF16), b.astype(BF16), (((1,), (1,)), ((), ())),
                           preferred_element_type=F32)


def _dot_tn(a, b):
    return lax.dot_general(a.astype(BF16), b.astype(BF16), (((0,), (0,)), ((), ())),
                           preferred_element_type=F32)


def _iota2(shape, axis):
    return lax.broadcasted_iota(jnp.int32, shape, axis)


def _blk(idx, size):
    return idx >> (size.bit_length() - 1)


def _rms(x, g):
    ms = jnp.mean(x * x, axis=-1, keepdims=True)
    return x * lax.rsqrt(ms + NORM_EPS) * g


def _inproj_kernel(x_ref, g_ref, w_ref, o_ref, h_ref):
    @pl.when(pl.program_id(1) == 0)
    def _():
        h_ref[...] = _rms(x_ref[...], g_ref[...]).astype(BF16)

    o_ref[...] = jnp.dot(h_ref[...], w_ref[...], preferred_element_type=F32)


def _inproj(x, g, w, tm, tn):
    t, d = x.shape
    n = w.shape[1]
    return pl.pallas_call(
        _inproj_kernel,
        grid=(t // tm, n // tn),
        in_specs=[pl.BlockSpec((tm, d), lambda i, j: (i, 0)),
                  pl.BlockSpec((1, d), lambda i, j: (0, 0)),
                  pl.BlockSpec((d, tn), lambda i, j: (0, j))],
        out_specs=pl.BlockSpec((tm, tn), lambda i, j: (i, j)),
        out_shape=jax.ShapeDtypeStruct((t, n), F32),
        scratch_shapes=[pltpu.VMEM((tm, d), BF16)],
        compiler_params=_cparams(("parallel", "arbitrary")),
        name="inproj",
    )(x, g, w)


def _shift_rows(cur, prev_row):
    rolled = pltpu.roll(cur, 1, 0)
    top = rolled[:SUBLANES]
    top = jnp.where(_iota2(top.shape, 0) == 0, prev_row, top)
    return jnp.concatenate([top, rolled[SUBLANES:]], axis=0)


def _split3(x):
    hi = x.astype(BF16)
    r1 = x - hi.astype(F32)
    mid = r1.astype(BF16)
    lo = (r1 - mid.astype(F32)).astype(BF16)
    return hi, mid, lo


def _sel_dot(sel, x):
    hi, mid, lo = _split3(x)
    d = functools.partial(jnp.dot, sel, preferred_element_type=F32)
    return (d(lo) + d(mid)) + d(hi)


def _dot_sel(x, sel):
    hi = x.astype(BF16)
    mid = (x - hi.astype(F32)).astype(BF16)
    d = functools.partial(jnp.dot, preferred_element_type=F32)
    return d(mid, sel) + d(hi, sel)


def _head_sum(x, e):
    parts = [_dot_sel(x[:, i * LANES:(i + 1) * LANES], e) for i in range(x.shape[1] // LANES)]
    return jnp.concatenate(parts, axis=-1)


def _head_ones():
    r = _blk(_iota2((LANES, LANES), 0), A_HEAD_DIM)
    c = _blk(_iota2((LANES, LANES), 1), A_HEAD_DIM)
    return (r == c).astype(BF16)


def _prep_kernel(has_vres, seq_tiles, *refs):
    if has_vres:
        (rkv_ref, lora_ref, prkv_ref, plora_ref, mu_rkv, mu_lora,
         w0, a0, k_k, k_a, r_k, w2, a2, g2, v0, v2, vfirst_ref,
         at_o, rt_o, kt_o, bt_o, vb_o, pc_o, g_o, bonus_o) = refs
    else:
        (rkv_ref, lora_ref, prkv_ref, plora_ref, mu_rkv, mu_lora,
         w0, a0, k_k, k_a, r_k, w2, a2, g2,
         at_o, rt_o, kt_o, bt_o, vb_o, pc_o, g_o, bonus_o, v_o) = refs
    first = (pl.program_id(0) % seq_tiles) == 0
    last_row = slice(SUBLANES - 1, SUBLANES)

    def mixed(cur_ref, prev_ref, mu_ref):
        cur = cur_ref[...]
        prev_row = jnp.where(first, 0.0, prev_ref[last_row, :])
        return cur + (_shift_rows(cur, prev_row) - cur) * mu_ref[...]

    rkv = mixed(rkv_ref, prkv_ref, mu_rkv)
    lora = mixed(lora_ref, plora_ref, mu_lora)
    r = rkv[:, 0:A_WIDTH]
    k = rkv[:, A_WIDTH:2 * A_WIDTH]
    v = rkv[:, 2 * A_WIDTH:3 * A_WIDTH]
    y_w = w0[...] + _dot(jnp.tanh(lora), w2[...])
    lw = -DECAY_SCALE * jax.nn.sigmoid(y_w)
    a = jax.nn.sigmoid(a0[...] + _dot(lora, a2[...]))
    g = _dot(jax.nn.sigmoid(lora), g2[...])
    if has_vres:
        v = v + (vfirst_ref[...] - v) * jax.nn.sigmoid(v0[...] + _dot(lora, v2[...]))

    e = _head_ones()
    kk = k * k_k[...]
    kk = kk * lax.rsqrt(jnp.maximum(_head_sum(kk * kk, e), 1e-24))
    k = k * (1.0 + (a - 1.0) * k_a[...])
    b = kk * a
    bonus = _head_sum(r * k * r_k[...], e) * v

    tm = lw.shape[0]
    row = _iota2((tm, tm), 0)
    col = _iota2((tm, tm), 1)
    tri = (_blk(row, CHUNK) == _blk(col, CHUNK)) & (row >= col)
    cum = _sel_dot(tri.astype(BF16), lw)
    cum3 = cum.reshape(tm // CHUNK, CHUNK, A_WIDTH)
    e_in = jnp.exp(cum)
    e_out = jnp.exp(-cum)
    at_o[...] = (-kk * jnp.exp(cum - lw)).astype(BF16)
    rt_o[...] = (r * e_in).astype(BF16)
    kt_o[...] = (k * e_out).astype(BF16)
    bt_o[...] = (b * e_out).astype(BF16)
    vb_o[...] = v.astype(BF16)
    for c in range(tm // CHUNK):
        pc_o[c] = jnp.exp(cum3[c, CHUNK - 1:CHUNK, :])
    g_o[...] = g
    bonus_o[...] = bonus
    if not has_vres:
        v_o[...] = v


def _rwkv_prep(proj, params, v_first, seq_len, tm):
    t = proj.shape[0]
    has_vres = v_first is not None
    seq_tiles = seq_len // tm
    sub = tm // SUBLANES
    rkv_w, lora_w = 3 * A_WIDTH, LORA_PACK
    rkv_blk = (C_COLS) // rkv_w
    lora_blk = (C_COLS + rkv_w + B_COLS) // lora_w

    def cur(width, blk):
        return pl.BlockSpec((tm, width), lambda i: (i, blk))

    def prev(width, blk):
        return pl.BlockSpec((SUBLANES, width), lambda i: (jnp.maximum(i * sub - 1, 0), blk))

    def vec(width):
        return pl.BlockSpec((1, width), lambda i: (0, 0))

    def mat(rows):
        return pl.BlockSpec((rows, A_WIDTH), lambda i: (0, 0))

    tok = pl.BlockSpec((tm, A_WIDTH), lambda i: (i, 0))
    ins = [proj, proj, proj, proj,
           params["mu_rkv"], params["mu_lora"],
           params["w0"], params["a0"], params["k_k"], params["k_a"], params["r_k"],
           params["w2"], params["a2"], params["g2"]]
    specs = [cur(rkv_w, rkv_blk), cur(lora_w, lora_blk),
             prev(rkv_w, rkv_blk), prev(lora_w, lora_blk),
             vec(rkv_w), vec(lora_w),
             vec(A_WIDTH), vec(A_WIDTH), vec(A_WIDTH), vec(A_WIDTH), vec(A_WIDTH),
             mat(LORA_PACK), mat(LORA_PACK), mat(LORA_PACK)]
    if has_vres:
        ins += [params["v0"], params["v2"], v_first]
        specs += [vec(A_WIDTH), mat(LORA_PACK), tok]
    wide = lambda dtype: jax.ShapeDtypeStruct((t, A_WIDTH), dtype)
    pc_spec = pl.BlockSpec((tm // CHUNK, 1, A_WIDTH), lambda i: (i, 0, 0))
    out_specs = [tok] * 5 + [pc_spec, tok, tok]
    out_shape = [wide(BF16)] * 5 + [jax.ShapeDtypeStruct((t // CHUNK, 1, A_WIDTH), F32),
                                    wide(F32), wide(F32)]
    if not has_vres:
        out_specs.append(tok)
        out_shape.append(wide(F32))
    return pl.pallas_call(
        functools.partial(_prep_kernel, has_vres, seq_tiles),
        grid=(t // tm,),
        in_specs=specs,
        out_specs=out_specs,
        out_shape=out_shape,
        compiler_params=_cparams(("parallel",)),
        name="rwkv_prep_vres" if has_vres else "rwkv_prep",
    )(*ins)


def _unit_lower_inverse(a_strict, row, col, eye, pair_diag):
    base = 8
    diag = _blk(row, base) == _blk(col, base)
    n1 = [jnp.where(diag, a, 0.0) for a in a_strict]
    n1d = [pair_diag(x) for x in n1]
    n2 = [_dot(x, d) for x, d in zip(n1, n1d)]
    n2d = [pair_diag(x) for x in n2]
    t = [eye + x for x in n1]
    t = [x + _dot(x, d) for x, d in zip(t, n2d)]
    n4d = [pair_diag(_dot(x, d)) for x, d in zip(n2, n2d)]
    t = [x + _dot(x, d) for x, d in zip(t, n4d)]
    m = base
    while m < CHUNK:
        sel = (_blk(row, 2 * m) == _blk(col, 2 * m)) & (_blk(row, m) != _blk(col, m))
        ed = [pair_diag(jnp.where(sel, a, 0.0)) for a in a_strict]
        te = [_dot(x, d) for x, d in zip(t, ed)]
        t = [x + _dot(y, pair_diag(x)) for x, y in zip(t, te)]
        m *= 2
    return t


def _scan_kernel(cps, at_ref, rt_ref, kt_ref, bt_ref, v_ref, pc_ref,
                 gate_ref, bonus_ref, lng_ref, lnb_ref, o_ref, s_ref):
    @pl.when(pl.program_id(1) == 0)
    def _():
        s_ref[...] = jnp.zeros_like(s_ref)

    n = 2 * A_HEAD_DIM
    pairs = A_HEADS // 2
    row = _iota2((CHUNK, n), 0)
    lane = _iota2((CHUNK, n), 1)
    col = lane & (A_HEAD_DIM - 1)
    lo = lane < A_HEAD_DIM
    strict = row > col
    incl = row >= col
    eye = (row == col).astype(F32)
    same_head = _blk(_iota2((n, n), 0), A_HEAD_DIM) == _blk(_iota2((n, n), 1), A_HEAD_DIM)
    items = [(c, p) for c in range(cps) for p in range(pairs)]

    def load(ref):
        return [ref[0, c * CHUNK:(c + 1) * CHUNK, p * n:(p + 1) * n] for c, p in items]

    def pair_diag(x):
        x = x.astype(BF16)
        zero = jnp.zeros_like(x)
        return jnp.concatenate([jnp.where(lo, x, zero), jnp.where(lo, zero, x)], axis=0)

    def fold(m):
        return jnp.where(lo, m[:CHUNK], m[CHUNK:])

    at = load(at_ref)
    rt = load(rt_ref)
    lhs = [jnp.concatenate([a, r], axis=0) for a, r in zip(at, rt)]
    kt = load(kt_ref)
    bt = load(bt_ref)
    lkb = [_dot_nt(x, jnp.concatenate([pair_diag(k), pair_diag(b)], axis=0))
           for x, k, b in zip(lhs, kt, bt)]
    lk = [x[:, :n] for x in lkb]
    lb = [x[:, n:] for x in lkb]
    a_k = [jnp.concatenate([jnp.where(strict, x[:CHUNK], 0.0), jnp.where(incl, x[CHUNK:], 0.0)],
                           axis=0) for x in lk]
    a_ab = [jnp.where(strict, x[:CHUNK], 0.0) for x in lb]
    a_rb = [jnp.where(incl, x[CHUNK:], 0.0) for x in lb]
    t_inv = _unit_lower_inverse(a_ab, row, col, eye, pair_diag)
    v = load(v_ref)
    lv = [_dot(a, pair_diag(x)) for a, x in zip(a_k, v)]
    wu = [_dot(t, jnp.concatenate([pair_diag(a), pair_diag(x[:CHUNK])], axis=1))
          for t, a, x in zip(t_inv, at, lv)]
    z = [_dot(a, jnp.concatenate([pair_diag(x[:, :n]), pair_diag(x[:, n:])], axis=1))
         for a, x in zip(a_rb, wu)]
    rw = [r.astype(F32) + x[:, :n] for r, x in zip(rt, z)]
    y0 = [x[CHUNK:] + zz[:, n:] for x, zz in zip(lv, z)]
    g = [jnp.where(same_head, _dot_tn(x[:, :n], b), 0.0).astype(BF16)
         for x, b in zip(wu, bt)]
    s_add = [fold(_dot_tn(jnp.concatenate([x[:, n:].astype(BF16), vv], axis=0),
                          jnp.concatenate([b, k], axis=0)))
             for x, vv, b, k in zip(wu, v, bt, kt)]

    inv_n = 1.0 / A_HEAD_DIM

    def head_mean(x):
        m_lo = jnp.sum(jnp.where(lo, x, 0.0), axis=-1, keepdims=True)
        m_hi = jnp.sum(jnp.where(lo, 0.0, x), axis=-1, keepdims=True)
        return jnp.where(lo, m_lo, m_hi) * inv_n

    s = [s_ref[p] for p in range(pairs)]
    for c in range(cps):
        rows = slice(c * CHUNK, (c + 1) * CHUNK)
        sd = [pair_diag(x) for x in s]
        ys = [_dot_nt(rw[c * pairs + p], sd[p]) + y0[c * pairs + p] for p in range(pairs)]
        for p in range(pairs):
            i = c * pairs + p
            cols = slice(p * n, (p + 1) * n)
            pc = pc_ref[0, c, :, cols]
            s[p] = (s[p] + _dot(s[p], g[i]) + s_add[i]) * pc
        for p in range(pairs):
            cols = slice(p * n, (p + 1) * n)
            d = ys[p] - head_mean(ys[p])
            var = head_mean(d * d)
            yn = d * lax.rsqrt(var + A_GN_EPS) * lng_ref[:, cols] + lnb_ref[:, cols]
            o_ref[0, rows, cols] = ((yn + bonus_ref[0, rows, cols])
                                    * gate_ref[0, rows, cols]).astype(o_ref.dtype)
    for p in range(pairs):
        s_ref[p] = s[p]


def _rwkv_scan(at, rt, kt, bt, v, pc, gate, bonus, ln_g, ln_b, batch, seq_len, cps):
    ts = cps * CHUNK
    shp = (batch, seq_len, A_WIDTH)
    blk = pl.BlockSpec((1, ts, A_WIDTH), lambda b, s: (b, s, 0))
    pc_blk = pl.BlockSpec((1, cps, 1, A_WIDTH), lambda b, s: (b, s, 0, 0))
    vec = pl.BlockSpec((1, A_WIDTH), lambda b, s: (0, 0))
    args = [z.reshape(shp) for z in (at, rt, kt, bt, v)]
    args += [pc.reshape(batch, seq_len // CHUNK, 1, A_WIDTH), gate.reshape(shp), bonus.reshape(shp)]
    y = pl.pallas_call(
        functools.partial(_scan_kernel, cps),
        grid=(batch, seq_len // ts),
        in_specs=[blk] * 5 + [pc_blk, blk, blk, vec, vec],
        out_specs=blk,
        out_shape=jax.ShapeDtypeStruct(shp, BF16),
        scratch_shapes=[pltpu.VMEM((A_HEADS // 2, A_HEAD_DIM, 2 * A_HEAD_DIM), F32)],
        compiler_params=_cparams(("parallel", "arbitrary")),
        name="rwkv_scan",
    )(*args, ln_g, ln_b)
    return y.reshape(batch * seq_len, A_WIDTH)


def _lru_body(first, gate_ref, x_ref, px_ref, cw_ref, cb_ref, wa_ref, ba_ref, wx_ref, bx_ref,
              lam_ref, o_ref, h_ref):
    x = x_ref[...]
    ts = x.shape[0]
    prev = jnp.where(first, 0.0, px_ref[...])
    row = _iota2(x.shape, 0)
    row8 = _iota2(prev.shape, 0)

    def delayed(d):
        rolled = pltpu.roll(x, d, 0)
        top = jnp.where(row8 < d, pltpu.roll(prev, d, 0), rolled[:SUBLANES])
        return jnp.concatenate([top, rolled[SUBLANES:]], axis=0)

    xc = cw_ref[CONV_WIDTH - 1:CONV_WIDTH, :] * x + cb_ref[...]
    for d in range(1, CONV_WIDTH):
        xc = xc + cw_ref[CONV_WIDTH - 1 - d:CONV_WIDTH - d, :] * delayed(d)

    r = jax.nn.sigmoid(_dot(xc, wa_ref[...]) + ba_ref[...])
    i = jax.nn.sigmoid(_dot(xc, wx_ref[...]) + bx_ref[...])
    log_a = -LRU_C * r * jax.nn.softplus(-lam_ref[...])
    a = jnp.exp(log_a)
    th = jnp.tanh(log_a)
    u = jnp.sqrt(-2.0 * th / (1.0 - th)) * (i * xc)

    grouped = (ts // SUBLANES, SUBLANES, B_WIDTH)
    in_group = _iota2(grouped, 1)

    def group_delay(z, d, fill):
        return jnp.where(in_group >= d, pltpu.roll(z.reshape(grouped), d, 1), fill).reshape(z.shape)

    d = 1
    while d < SUBLANES:
        u = a * group_delay(u, d, 0.0) + u
        a = a * group_delay(a, d, 1.0)
        d *= 2
    carry = h_ref[...]
    groups = []
    for g0 in range(0, ts, SUBLANES):
        hg = a[g0:g0 + SUBLANES] * carry + u[g0:g0 + SUBLANES]
        carry = hg[SUBLANES - 1:SUBLANES]
        groups.append(hg)
    h_ref[...] = carry
    h = jnp.concatenate(groups, axis=0)
    o_ref[...] = (jax.nn.gelu(gate_ref[...]) * h).astype(o_ref.dtype)


def _log_gammas():
    return np.log1p(-np.exp2(-5.0 - np.arange(C_HEADS, dtype=np.float32))).astype(np.float32)


def _ret_body(cpr, q_ref, k_ref, v_ref, g_ref, pos_ref, invf_ref, gng_ref, o_ref, st_ref):
    half = C_QK_DIM // 2
    lane = _iota2((CHUNK, LANES), 1)
    in_first_half = (lane & (C_QK_DIM - 1)) < half
    lo = lane < C_QK_DIM
    idx_r = _iota2((CHUNK, CHUNK), 0).astype(F32)
    idx_c = _iota2((CHUNK, CHUNK), 1).astype(F32)
    dist = jnp.abs(idx_r - idx_c)
    pos_in = _iota2((CHUNK, 1), 0).astype(F32)
    log_gamma = _log_gammas()

    def rope(t, cos, sin_signed):
        swapped = jnp.where(in_first_half, pltpu.roll(t, LANES - half, 1), pltpu.roll(t, half, 1))
        return t * cos + swapped * sin_signed

    intra_decay = [jnp.exp(float(lg) * dist) for lg in log_gamma]
    q_decay = [jnp.exp(float(lg) * (pos_in + 1.0)) for lg in log_gamma]
    k_decay = [jnp.exp(float(lg) * (CHUNK - 1.0 - pos_in)) for lg in log_gamma]
    chunk_decay = [float(np.exp(lg * np.float32(CHUNK))) for lg in log_gamma]
    items = [(c, h) for c in range(cpr) for h in range(C_HEADS)]

    ang = pos_ref[...] * invf_ref[...]
    cos_packed = jnp.cos(ang)
    sin_packed = jnp.sin(ang)
    lane_group = lane >> (half.bit_length() - 1)

    def spread(x, c):
        out = x
        for g in range(LANES // half):
            if g != c:
                out = jnp.where(lane_group == g, pltpu.roll(x, (half * (g - c)) % LANES, 1), out)
        return out

    q2, k2 = {}, {}
    for c in range(cpr):
        rows = slice(c * CHUNK, (c + 1) * CHUNK)
        cos = spread(cos_packed, c)
        sin = spread(sin_packed, c)
        sin_signed = jnp.where(in_first_half, -sin, sin)
        for p in range(C_HEADS // 2):
            qk_cols = slice(p * LANES, (p + 1) * LANES)
            q2[c, p] = rope(q_ref[rows, qk_cols], cos, sin_signed)
            k2[c, p] = rope(k_ref[rows, qk_cols], cos, sin_signed) * (C_QK_DIM ** -0.5)

    def mine(x, h):
        return jnp.where(lo, x, 0.0) if h % 2 == 0 else jnp.where(lo, 0.0, x)

    qh = [mine(q2[c, h // 2], h) for c, h in items]
    kh = [mine(k2[c, h // 2], h) for c, h in items]
    vh = [v_ref[c * CHUNK:(c + 1) * CHUNK, h * C_V_DIM:(h + 1) * C_V_DIM].astype(BF16)
          for c, h in items]
    scores = [_dot_nt(q, k) * intra_decay[h] for q, k, (c, h) in zip(qh, kh, items)]
    kv = [_dot_tn(k * k_decay[h], v) for k, v, (c, h) in zip(kh, vh, items)]
    intra = [_dot(s, v) for s, v in zip(scores, vh)]

    state = [st_ref[h] for h in range(C_HEADS)]
    starts = []
    for i, (c, h) in enumerate(items):
        starts.append(state[h])
        state[h] = state[h] * chunk_decay[h] + kv[i]
    for h in range(C_HEADS):
        st_ref[h] = state[h]
    cross = [_dot(q * q_decay[h], s) for q, s, (c, h) in zip(qh, starts, items)]

    for i, (c, h) in enumerate(items):
        rows = slice(c * CHUNK, (c + 1) * CHUNK)
        v_cols = slice(h * C_V_DIM, (h + 1) * C_V_DIM)
        o = intra[i] + cross[i]
        mu = jnp.mean(o, axis=-1, keepdims=True)
        d = o - mu
        var = jnp.mean(d * d, axis=-1, keepdims=True)
        on = d * lax.rsqrt(var + C_GN_EPS) * gng_ref[:, v_cols]
        o_ref[rows, v_cols] = (jax.nn.silu(g_ref[rows, v_cols]) * on).astype(o_ref.dtype)


def _mix_bc_kernel(cpr, gate_ref, x_ref, px_ref, cw_ref, cb_ref, wa_ref, ba_ref, wx_ref, bx_ref,
                   lam_ref, q_ref, k_ref, v_ref, g_ref, pos_ref, invf_ref, gng_ref,
                   yb_ref, yc_ref, h_ref, st_ref):
    first = pl.program_id(1) == 0

    @pl.when(first)
    def _():
        h_ref[...] = jnp.zeros_like(h_ref)
        st_ref[...] = jnp.zeros_like(st_ref)

    _ret_body(cpr, q_ref, k_ref, v_ref, g_ref, pos_ref, invf_ref, gng_ref, yc_ref, st_ref)
    _lru_body(first, gate_ref, x_ref, px_ref, cw_ref, cb_ref, wa_ref, ba_ref, wx_ref, bx_ref,
              lam_ref, yb_ref, h_ref)


def _mix_bc(proj, lru, posb, invf, gn_g, batch, seq_len, cpr):
    t = proj.shape[0]
    ts = cpr * CHUNK
    nst = seq_len // ts
    sub = ts // SUBLANES
    gate_blk = (C_COLS + 3 * A_WIDTH) // B_WIDTH
    x_blk = gate_blk + 1

    def tok(width, blk):
        return pl.BlockSpec((ts, width), lambda b, s: (b * nst + s, blk))

    def const(shape):
        return pl.BlockSpec(shape, lambda b, s: (0,) * len(shape))

    return pl.pallas_call(
        functools.partial(_mix_bc_kernel, cpr),
        grid=(batch, nst),
        in_specs=[tok(B_WIDTH, gate_blk), tok(B_WIDTH, x_blk),
                  pl.BlockSpec((SUBLANES, B_WIDTH),
                               lambda b, s: (jnp.maximum((b * nst + s) * sub - 1, 0), x_blk)),
                  const((CONV_WIDTH, B_WIDTH)), const((1, B_WIDTH)),
                  const((B_WIDTH, B_WIDTH)), const((1, B_WIDTH)),
                  const((B_WIDTH, B_WIDTH)), const((1, B_WIDTH)), const((1, B_WIDTH)),
                  tok(C_QK_WIDTH, 0), tok(C_QK_WIDTH, 1), tok(C_WIDTH, 1), tok(C_WIDTH, 2),
                  pl.BlockSpec((CHUNK, LANES), lambda b, s: (b * nst + s, 0)),
                  const((1, LANES)), const((1, C_WIDTH))],
        out_specs=[tok(B_WIDTH, 0), tok(C_WIDTH, 0)],
        out_shape=[jax.ShapeDtypeStruct((t, B_WIDTH), BF16),
                   jax.ShapeDtypeStruct((t, C_WIDTH), BF16)],
        scratch_shapes=[pltpu.VMEM((1, B_WIDTH), F32),
                        pltpu.VMEM((C_HEADS, LANES, C_V_DIM), F32)],
        compiler_params=_cparams(("parallel", "arbitrary")),
        name="mix_bc",
    )(proj, proj, proj, lru["conv_w"], lru["conv_b"], lru["wa"], lru["ba"], lru["wx"],
      lru["bx"], lru["lam"], proj, proj, proj, proj, posb, invf, gn_g)


def _outproj_kernel(x_ref, ya_ref, yb_ref, yc_ref, w_ref, o_ref):
    b0, c0 = A_WIDTH, A_WIDTH + B_WIDTH
    acc = jnp.dot(ya_ref[...], w_ref[0:b0, :].astype(BF16), preferred_element_type=F32)
    acc += jnp.dot(yb_ref[...], w_ref[b0:c0, :].astype(BF16), preferred_element_type=F32)
    acc += jnp.dot(yc_ref[...], w_ref[c0:, :].astype(BF16), preferred_element_type=F32)
    o_ref[...] = x_ref[...] + acc


def _outproj(x, ya, yb, yc, w_stack, layer, tm):
    t, d = x.shape

    def tok(width):
        return pl.BlockSpec((tm, width), lambda i: (i, 0))

    return pl.pallas_call(
        _outproj_kernel,
        grid=(t // tm,),
        in_specs=[tok(d), tok(A_WIDTH), tok(B_WIDTH), tok(C_WIDTH),
                  pl.BlockSpec((None, d, d), lambda i: (layer, 0, 0),
                               pipeline_mode=pl.Buffered(1))],
        out_specs=tok(d),
        out_shape=jax.ShapeDtypeStruct((t, d), F32),
        compiler_params=_cparams(("parallel",)),
        name="outproj",
    )(x, ya, yb, yc, w_stack)


def _ffn_kernel(final, x_ref, g_ref, wg_ref, wu_ref, wd_ref, fg_ref, o_ref, h_ref):
    j = pl.program_id(1)

    @pl.when(j == 0)
    def _():
        x = x_ref[...]
        h_ref[...] = _rms(x, g_ref[...]).astype(BF16)
        o_ref[...] = x

    h = h_ref[...]
    gate = jnp.dot(h, wg_ref[...].astype(BF16), preferred_element_type=F32)
    up = jnp.dot(h, wu_ref[...].astype(BF16), preferred_element_type=F32)
    act = (jax.nn.silu(gate) * up).astype(BF16)
    o_ref[...] += jnp.dot(act, wd_ref[...].astype(BF16), preferred_element_type=F32)

    if final:
        @pl.when(j == pl.num_programs(1) - 1)
        def _():
            o_ref[...] = _rms(o_ref[...], fg_ref[...])


def _ffn(x, g, wg, wu, wd, fg, layer, final, tm, tf):
    t, d = x.shape
    f = wg.shape[2]
    return pl.pallas_call(
        functools.partial(_ffn_kernel, final),
        grid=(t // tm, f // tf),
        in_specs=[pl.BlockSpec((tm, d), lambda i, j: (i, 0)),
                  pl.BlockSpec((1, d), lambda i, j: (0, 0)),
                  pl.BlockSpec((None, d, tf), lambda i, j: (layer, 0, j)),
                  pl.BlockSpec((None, d, tf), lambda i, j: (layer, 0, j)),
                  pl.BlockSpec((None, tf, d), lambda i, j: (layer, j, 0)),
                  pl.BlockSpec((1, d), lambda i, j: (0, 0))],
        out_specs=pl.BlockSpec((tm, d), lambda i, j: (i, 0)),
        out_shape=jax.ShapeDtypeStruct((t, d), F32),
        scratch_shapes=[pltpu.VMEM((tm, d), BF16)],
        compiler_params=_cparams(("parallel", "arbitrary")),
        name="ffn_final" if final else "ffn",
    )(x, g, wg, wu, wd, fg)


def _rows_at(w, offset):
    return jnp.pad(w, ((offset, LORA_PACK - offset - w.shape[0]), (0, 0)))


RELAYOUT_UNIT = 64
RELAYOUT_PARTS = 8


def _source_unit(q):
    c_units = C_COLS // RELAYOUT_UNIT
    a_units = 3 * A_WIDTH // RELAYOUT_UNIT
    b_units = B_COLS // RELAYOUT_UNIT
    last_src = (A_COLS + B_COLS + C_COLS) // RELAYOUT_UNIT - 1
    src_c = (A_COLS + B_COLS) // RELAYOUT_UNIT
    src_b = A_COLS // RELAYOUT_UNIT
    lora = jnp.minimum(a_units + (q - c_units - a_units - b_units), last_src)
    return jnp.where(q < c_units, src_c + q,
                     jnp.where(q < c_units + a_units, q - c_units,
                               jnp.where(q < c_units + a_units + b_units,
                                         src_b + (q - c_units - a_units), lora)))


def _relayout_kernel(has_vres, n_steps, *refs):
    parts, rest = refs[:RELAYOUT_PARTS], refs[RELAYOUT_PARTS:]
    o_ref = rest[-1]
    blocks = [p[...] for p in parts]
    tail = rest[0][...] if has_vres else jnp.zeros_like(blocks[-1])
    blocks[-1] = jnp.where(pl.program_id(0) == n_steps - 1, tail, blocks[-1])
    o_ref[...] = jnp.concatenate(blocks, axis=0).T.astype(BF16)


def _layer_in_weight(w_in_t, w_vres_t, layer):
    _, n, d = w_in_t.shape
    has_vres = layer > 0
    n_steps = P_COLS // (RELAYOUT_UNIT * RELAYOUT_PARTS)

    def part(s):
        return pl.BlockSpec((None, RELAYOUT_UNIT, d),
                            lambda j: (layer, _source_unit(RELAYOUT_PARTS * j + s), 0))

    ins = [w_in_t] * RELAYOUT_PARTS
    specs = [part(s) for s in range(RELAYOUT_PARTS)]
    if has_vres:
        ins.append(w_vres_t)
        specs.append(pl.BlockSpec((None, VRES_LORA, d), lambda j: (layer - 1, 0, 0)))
    return pl.pallas_call(
        functools.partial(_relayout_kernel, has_vres, n_steps),
        grid=(n_steps,),
        in_specs=specs,
        out_specs=pl.BlockSpec((d, RELAYOUT_UNIT * RELAYOUT_PARTS), lambda j: (0, j)),
        out_shape=jax.ShapeDtypeStruct((d, P_COLS), BF16),
        compiler_params=_cparams(("parallel",)),
        name="w_in_relayout",
    )(*ins)


def _block_diag(w):
    g, n, _ = w.shape
    eye = jnp.eye(g, dtype=w.dtype)
    return (eye[:, None, :, None] * w[:, :, None, :]).reshape(g * n, g * n)


def kernel(x, positions, norm1_g, w_in, tshift_mu, rwkv_w0, rwkv_w2, rwkv_a0, rwkv_a2, rwkv_g2, rwkv_k_k, rwkv_k_a, rwkv_r_k, rwkv_ln_g, rwkv_ln_b, w_in_vres, tshift_mu_vres, rwkv_v0, rwkv_v2, lru_conv_w, lru_conv_b, lru_wa, lru_ba, lru_wx, lru_bx, lru_lambda, ret_gn_g, w_out, norm2_g, ffn_w_gate, ffn_w_up, ffn_w_down, final_norm_g):
    batch, seq_len, d = x.shape
    depth = w_in.shape[0]
    t = batch * seq_len
    aw = A_WIDTH
    row = lambda v: v.reshape(1, -1).astype(F32)

    tm_proj = min(512, seq_len)
    tm_in = min(1024, seq_len)
    tm_tok = min(256, seq_len)
    cps = min(8, seq_len // CHUNK)
    cpr = min(4, seq_len // CHUNK)

    inv_freq = ROPE_THETA ** (-jnp.arange(0, C_QK_DIM, 2, dtype=F32) / C_QK_DIM)
    invf = jnp.tile(inv_freq, LANES // inv_freq.shape[0]).reshape(1, LANES)
    n_freq = inv_freq.shape[0]
    assert cpr * n_freq == LANES
    posb = positions.astype(F32).reshape(t // (cpr * CHUNK), cpr, CHUNK)
    posb = jnp.repeat(jnp.swapaxes(posb, 1, 2), n_freq, axis=-1).reshape(t // cpr, LANES)

    w_in_t = jnp.swapaxes(w_in, 1, 2)
    w_vres_t = jnp.swapaxes(w_in_vres, 1, 2)

    xf = x.reshape(t, d)
    v_first = None
    for l in range(depth):
        has_vres = l > 0
        w_l = _layer_in_weight(w_in_t, w_vres_t, l)
        proj = _inproj(xf, row(norm1_g[l]), w_l, tm_in, P_COLS // 3)

        mu = tshift_mu[l]
        o_ad, o_gd = 3 * aw + DECAY_LORA, 3 * aw + DECAY_LORA + ICLR_LORA
        mu_vd = tshift_mu_vres[l - 1] if has_vres else jnp.zeros((VRES_LORA,), F32)
        params = {
            "mu_rkv": row(mu[:3 * aw]),
            "mu_lora": row(jnp.concatenate([mu[3 * aw:A_COLS], mu_vd])),
            "w0": row(rwkv_w0[l]), "a0": row(rwkv_a0[l]), "k_k": row(rwkv_k_k[l]),
            "k_a": row(rwkv_k_a[l]), "r_k": row(rwkv_r_k[l]),
            "w2": _rows_at(rwkv_w2[l], 0), "a2": _rows_at(rwkv_a2[l], o_ad - 3 * aw),
            "g2": _rows_at(rwkv_g2[l], o_gd - 3 * aw),
        }
        if has_vres:
            params["v0"] = row(rwkv_v0[l - 1])
            params["v2"] = _rows_at(rwkv_v2[l - 1], A_COLS - 3 * aw)
        at, rt, kt, bt, v_b, pc, g_a, bonus, *v_f32 = _rwkv_prep(
            proj, params, v_first if has_vres else None, seq_len, tm_tok)
        if l == 0:
            v_first = v_f32[0]
        y_a = _rwkv_scan(at, rt, kt, bt, v_b, pc, g_a, bonus,
                         row(rwkv_ln_g[l]), row(rwkv_ln_b[l]), batch, seq_len, cps)

        lru_params = {
            "conv_w": lru_conv_w[l], "conv_b": row(lru_conv_b[l]),
            "wa": _block_diag(lru_wa[l]).astype(BF16), "ba": row(lru_ba[l]),
            "wx": _block_diag(lru_wx[l]).astype(BF16), "bx": row(lru_bx[l]),
            "lam": row(lru_lambda[l]),
        }
        y_b, y_c = _mix_bc(proj, lru_params, posb, invf, row(ret_gn_g[l]), batch, seq_len, cpr)

        xf = _outproj(xf, y_a, y_b, y_c, w_out, l, tm_proj)
        xf = _ffn(xf, row(norm2_g[l]), ffn_w_gate, ffn_w_up, ffn_w_down, row(final_norm_g),
                  l, l == depth - 1, tm_in, 256)
    return xf.reshape(batch, seq_len, d)
```

```python
import functools

import numpy as np
import jax
import jax.numpy as jnp
from jax import lax
from jax.experimental import pallas as pl
from jax.experimental.pallas import tpu as pltpu

F32 = jnp.float32
BF16 = jnp.bfloat16

NORM_EPS = 1e-6
CHUNK = 64

A_HEADS = 12
A_HEAD_DIM = 64
A_WIDTH = A_HEADS * A_HEAD_DIM
DECAY_LORA = 96
ICLR_LORA = 96
VRES_LORA = 64
GATE_LORA = 256
A_GN_EPS = 64e-5
DECAY_SCALE = float(np.exp(np.float32(-0.5)))
A_COLS = 3 * A_WIDTH + DECAY_LORA + ICLR_LORA + GATE_LORA
B_WIDTH = 512
CONV_WIDTH = 4
LRU_C = 8.0
B_COLS = 2 * B_WIDTH
C_HEADS = 6
C_QK_DIM = 64
C_V_DIM = 128
C_QK_WIDTH = C_HEADS * C_QK_DIM
C_WIDTH = C_HEADS * C_V_DIM
C_GN_EPS = 1e-5
ROPE_THETA = 10000.0
C_COLS = 2 * C_QK_WIDTH + 2 * C_WIDTH

LANES = 128
SUBLANES = 8
LORA_PACK = DECAY_LORA + ICLR_LORA + GATE_LORA + VRES_LORA
VMEM_LIMIT = 56 * 1024 * 1024

P_COLS = C_COLS + 3 * A_WIDTH + B_COLS + LORA_PACK


def _cparams(sem):
    return pltpu.CompilerParams(dimension_semantics=sem, vmem_limit_bytes=VMEM_LIMIT)


def _dot(a, b):
    return jnp.dot(a.astype(BF16), b.astype(BF16), preferred_element_type=F32)


def _dot_nt(a, b):
    return lax.dot_general(a.astype(BF16), b.astype(BF16), (((1,), (1,)), ((), ())),
                           preferred_element_type=F32)


def _dot_tn(a, b):
    return lax.dot_general(a.astype(BF16), b.astype(BF16), (((0,), (0,)), ((), ())),
                           preferred_element_type=F32)


def _iota2(shape, axis):
    return lax.broadcasted_iota(jnp.int32, shape, axis)


def _blk(idx, size):
    return idx >> (size.bit_length() - 1)


def _rms(x, g):
    ms = jnp.mean(x * x, axis=-1, keepdims=True)
    return x * lax.rsqrt(ms + NORM_EPS) * g


def _inproj_kernel(x_ref, g_ref, w_ref, o_ref, h_ref):
    @pl.when(pl.program_id(1) == 0)
    def _():
        h_ref[...] = _rms(x_ref[...], g_ref[...]).astype(BF16)

    o_ref[...] = jnp.dot(h_ref[...], w_ref[...], preferred_element_type=F32)


def _inproj(x, g, w, tm, tn):
    t, d = x.shape
    n = w.shape[1]
    return pl.pallas_call(
        _inproj_kernel,
        grid=(t // tm, n // tn),
        in_specs=[pl.BlockSpec((tm, d), lambda i, j: (i, 0)),
                  pl.BlockSpec((1, d), lambda i, j: (0, 0)),
                  pl.BlockSpec((d, tn), lambda i, j: (0, j))],
        out_specs=pl.BlockSpec((tm, tn), lambda i, j: (i, j)),
        out_shape=jax.ShapeDtypeStruct((t, n), F32),
        scratch_shapes=[pltpu.VMEM((tm, d), BF16)],
        compiler_params=_cparams(("parallel", "arbitrary")),
        name="inproj",
    )(x, g, w)


def _shift_rows(cur, prev_row):
    rolled = pltpu.roll(cur, 1, 0)
    top = rolled[:SUBLANES]
    top = jnp.where(_iota2(top.shape, 0) == 0, prev_row, top)
    return jnp.concatenate([top, rolled[SUBLANES:]], axis=0)


def _split3(x):
    hi = x.astype(BF16)
    r1 = x - hi.astype(F32)
    mid = r1.astype(BF16)
    lo = (r1 - mid.astype(F32)).astype(BF16)
    return hi, mid, lo


def _sel_dot(sel, x):
    hi, mid, lo = _split3(x)
    d = functools.partial(jnp.dot, sel, preferred_element_type=F32)
    return (d(lo) + d(mid)) + d(hi)


def _dot_sel(x, sel):
    hi = x.astype(BF16)
    mid = (x - hi.astype(F32)).astype(BF16)
    d = functools.partial(jnp.dot, preferred_element_type=F32)
    return d(mid, sel) + d(hi, sel)


def _head_sum(x, e):
    parts = [_dot_sel(x[:, i * LANES:(i + 1) * LANES], e) for i in range(x.shape[1] // LANES)]
    return jnp.concatenate(parts, axis=-1)


def _head_ones():
    r = _blk(_iota2((LANES, LANES), 0), A_HEAD_DIM)
    c = _blk(_iota2((LANES, LANES), 1), A_HEAD_DIM)
    return (r == c).astype(BF16)


def _prep_kernel(has_vres, seq_tiles, *refs):
    if has_vres:
        (rkv_ref, lora_ref, prkv_ref, plora_ref, mu_rkv, mu_lora,
         w0, a0, k_k, k_a, r_k, w2, a2, g2, v0, v2, vfirst_ref,
         at_o, rt_o, kt_o, bt_o, vb_o, pc_o, g_o, bonus_o) = refs
    else:
        (rkv_ref, lora_ref, prkv_ref, plora_ref, mu_rkv, mu_lora,
         w0, a0, k_k, k_a, r_k, w2, a2, g2,
         at_o, rt_o, kt_o, bt_o, vb_o, pc_o, g_o, bonus_o, v_o) = refs
    first = (pl.program_id(0) % seq_tiles) == 0
    last_row = slice(SUBLANES - 1, SUBLANES)

    def mixed(cur_ref, prev_ref, mu_ref):
        cur = cur_ref[...]
        prev_row = jnp.where(first, 0.0, prev_ref[last_row, :])
        return cur + (_shift_rows(cur, prev_row) - cur) * mu_ref[...]

    rkv = mixed(rkv_ref, prkv_ref, mu_rkv)
    lora = mixed(lora_ref, plora_ref, mu_lora)
    r = rkv[:, 0:A_WIDTH]
    k = rkv[:, A_WIDTH:2 * A_WIDTH]
    v = rkv[:, 2 * A_WIDTH:3 * A_WIDTH]
    y_w = w0[...] + _dot(jnp.tanh(lora), w2[...])
    lw = -DECAY_SCALE * jax.nn.sigmoid(y_w)
    a = jax.nn.sigmoid(a0[...] + _dot(lora, a2[...]))
    g = _dot(jax.nn.sigmoid(lora), g2[...])
    if has_vres:
        v = v + (vfirst_ref[...] - v) * jax.nn.sigmoid(v0[...] + _dot(lora, v2[...]))

    e = _head_ones()
    kk = k * k_k[...]
    kk = kk * lax.rsqrt(jnp.maximum(_head_sum(kk * kk, e), 1e-24))
    k = k * (1.0 + (a - 1.0) * k_a[...])
    b = kk * a
    bonus = _head_sum(r * k * r_k[...], e) * v

    tm = lw.shape[0]
    row = _iota2((tm, tm), 0)
    col = _iota2((tm, tm), 1)
    tri = (_blk(row, CHUNK) == _blk(col, CHUNK)) & (row >= col)
    cum = _sel_dot(tri.astype(BF16), lw)
    cum3 = cum.reshape(tm // CHUNK, CHUNK, A_WIDTH)
    e_in = jnp.exp(cum)
    e_out = jnp.exp(-cum)
    at_o[...] = (-kk * jnp.exp(cum - lw)).astype(BF16)
    rt_o[...] = (r * e_in).astype(BF16)
    kt_o[...] = (k * e_out).astype(BF16)
    bt_o[...] = (b * e_out).astype(BF16)
    vb_o[...] = v.astype(BF16)
    for c in range(tm // CHUNK):
        pc_o[c] = jnp.exp(cum3[c, CHUNK - 1:CHUNK, :])
    g_o[...] = g
    bonus_o[...] = bonus
    if not has_vres:
        v_o[...] = v


def _rwkv_prep(proj, params, v_first, seq_len, tm):
    t = proj.shape[0]
    has_vres = v_first is not None
    seq_tiles = seq_len // tm
    sub = tm // SUBLANES
    rkv_w, lora_w = 3 * A_WIDTH, LORA_PACK
    rkv_blk = (C_COLS) // rkv_w
    lora_blk = (C_COLS + rkv_w + B_COLS) // lora_w

    def cur(width, blk):
        return pl.BlockSpec((tm, width), lambda i: (i, blk))

    def prev(width, blk):
        return pl.BlockSpec((SUBLANES, width), lambda i: (jnp.maximum(i * sub - 1, 0), blk))

    def vec(width):
        return pl.BlockSpec((1, width), lambda i: (0, 0))

    def mat(rows):
        return pl.BlockSpec((rows, A_WIDTH), lambda i: (0, 0))

    tok = pl.BlockSpec((tm, A_WIDTH), lambda i: (i, 0))
    ins = [proj, proj, proj, proj,
           params["mu_rkv"], params["mu_lora"],
           params["w0"], params["a0"], params["k_k"], params["k_a"], params["r_k"],
           params["w2"], params["a2"], params["g2"]]
    specs = [cur(rkv_w, rkv_blk), cur(lora_w, lora_blk),
             prev(rkv_w, rkv_blk), prev(lora_w, lora_blk),
             vec(rkv_w), vec(lora_w),
             vec(A_WIDTH), vec(A_WIDTH), vec(A_WIDTH), vec(A_WIDTH), vec(A_WIDTH),
             mat(LORA_PACK), mat(LORA_PACK), mat(LORA_PACK)]
    if has_vres:
        ins += [params["v0"], params["v2"], v_first]
        specs += [vec(A_WIDTH), mat(LORA_PACK), tok]
    wide = lambda dtype: jax.ShapeDtypeStruct((t, A_WIDTH), dtype)
    pc_spec = pl.BlockSpec((tm // CHUNK, 1, A_WIDTH), lambda i: (i, 0, 0))
    out_specs = [tok] * 5 + [pc_spec, tok, tok]
    out_shape = [wide(BF16)] * 5 + [jax.ShapeDtypeStruct((t // CHUNK, 1, A_WIDTH), F32),
                                    wide(F32), wide(F32)]
    if not has_vres:
        out_specs.append(tok)
        out_shape.append(wide(F32))
    return pl.pallas_call(
        functools.partial(_prep_kernel, has_vres, seq_tiles),
        grid=(t // tm,),
        in_specs=specs,
        out_specs=out_specs,
        out_shape=out_shape,
        compiler_params=_cparams(("parallel",)),
        name="rwkv_prep_vres" if has_vres else "rwkv_prep",
    )(*ins)


def _unit_lower_inverse(a_strict, row, col, eye, pair_diag):
    base = 8
    diag = _blk(row, base) == _blk(col, base)
    n1 = [jnp.where(diag, a, 0.0) for a in a_strict]
    n1d = [pair_diag(x) for x in n1]
    n2 = [_dot(x, d) for x, d in zip(n1, n1d)]
    n2d = [pair_diag(x) for x in n2]
    t = [eye + x for x in n1]
    t = [x + _dot(x, d) for x, d in zip(t, n2d)]
    n4d = [pair_diag(_dot(x, d)) for x, d in zip(n2, n2d)]
    t = [x + _dot(x, d) for x, d in zip(t, n4d)]
    m = base
    while m < CHUNK:
        sel = (_blk(row, 2 * m) == _blk(col, 2 * m)) & (_blk(row, m) != _blk(col, m))
        ed = [pair_diag(jnp.where(sel, a, 0.0)) for a in a_strict]
        te = [_dot(x, d) for x, d in zip(t, ed)]
        t = [x + _dot(y, pair_diag(x)) for x, y in zip(t, te)]
        m *= 2
    return t


def _scan_kernel(cps, at_ref, rt_ref, kt_ref, bt_ref, v_ref, pc_ref,
                 gate_ref, bonus_ref, lng_ref, lnb_ref, o_ref, s_ref):
    @pl.when(pl.program_id(1) == 0)
    def _():
        s_ref[...] = jnp.zeros_like(s_ref)

    n = 2 * A_HEAD_DIM
    pairs = A_HEADS // 2
    row = _iota2((CHUNK, n), 0)
    lane = _iota2((CHUNK, n), 1)
    col = lane & (A_HEAD_DIM - 1)
    lo = lane < A_HEAD_DIM
    strict = row > col
    incl = row >= col
    eye = (row == col).astype(F32)
    same_head = _blk(_iota2((n, n), 0), A_HEAD_DIM) == _blk(_iota2((n, n), 1), A_HEAD_DIM)
    items = [(c, p) for c in range(cps) for p in range(pairs)]

    def load(ref):
        return [ref[0, c * CHUNK:(c + 1) * CHUNK, p * n:(p + 1) * n] for c, p in items]

    def pair_diag(x):
        x = x.astype(BF16)
        zero = jnp.zeros_like(x)
        return jnp.concatenate([jnp.where(lo, x, zero), jnp.where(lo, zero, x)], axis=0)

    def fold(m):
        return jnp.where(lo, m[:CHUNK], m[CHUNK:])

    at = load(at_ref)
    rt = load(rt_ref)
    lhs = [jnp.concatenate([a, r], axis=0) for a, r in zip(at, rt)]
    kt = load(kt_ref)
    bt = load(bt_ref)
    lkb = [_dot_nt(x, jnp.concatenate([pair_diag(k), pair_diag(b)], axis=0))
           for x, k, b in zip(lhs, kt, bt)]
    lk = [x[:, :n] for x in lkb]
    lb = [x[:, n:] for x in lkb]
    a_k = [jnp.concatenate([jnp.where(strict, x[:CHUNK], 0.0), jnp.where(incl, x[CHUNK:], 0.0)],
                           axis=0) for x in lk]
    a_ab = [jnp.where(strict, x[:CHUNK], 0.0) for x in lb]
    a_rb = [jnp.where(incl, x[CHUNK:], 0.0) for x in lb]
    t_inv = _unit_lower_inverse(a_ab, row, col, eye, pair_diag)
    v = load(v_ref)
    lv = [_dot(a, pair_diag(x)) for a, x in zip(a_k, v)]
    wu = [_dot(t, jnp.concatenate([pair_diag(a), pair_diag(x[:CHUNK])], axis=1))
          for t, a, x in zip(t_inv, at, lv)]
    z = [_dot(a, jnp.concatenate([pair_diag(x[:, :n]), pair_diag(x[:, n:])], axis=1))
         for a, x in zip(a_rb, wu)]
    rw = [r.astype(F32) + x[:, :n] for r, x in zip(rt, z)]
    y0 = [x[CHUNK:] + zz[:, n:] for x, zz in zip(lv, z)]
    g = [jnp.where(same_head, _dot_tn(x[:, :n], b), 0.0).astype(BF16)
         for x, b in zip(wu, bt)]
    s_add = [fold(_dot_tn(jnp.concatenate([x[:, n:].astype(BF16), vv], axis=0),
                          jnp.concatenate([b, k], axis=0)))
             for x, vv, b, k in zip(wu, v, bt, kt)]

    inv_n = 1.0 / A_HEAD_DIM

    def head_mean(x):
        m_lo = jnp.sum(jnp.where(lo, x, 0.0), axis=-1, keepdims=True)
        m_hi = jnp.sum(jnp.where(lo, 0.0, x), axis=-1, keepdims=True)
        return jnp.where(lo, m_lo, m_hi) * inv_n

    s = [s_ref[p] for p in range(pairs)]
    for c in range(cps):
        rows = slice(c * CHUNK, (c + 1) * CHUNK)
        sd = [pair_diag(x) for x in s]
        ys = [_dot_nt(rw[c * pairs + p], sd[p]) + y0[c * pairs + p] for p in range(pairs)]
        for p in range(pairs):
            i = c * pairs + p
            cols = slice(p * n, (p + 1) * n)
            pc = pc_ref[0, c, :, cols]
            s[p] = (s[p] + _dot(s[p], g[i]) + s_add[i]) * pc
        for p in range(pairs):
            cols = slice(p * n, (p + 1) * n)
            d = ys[p] - head_mean(ys[p])
            var = head_mean(d * d)
            yn = d * lax.rsqrt(var + A_GN_EPS) * lng_ref[:, cols] + lnb_ref[:, cols]
            o_ref[0, rows, cols] = ((yn + bonus_ref[0, rows, cols])
                                    * gate_ref[0, rows, cols]).astype(o_ref.dtype)
    for p in range(pairs):
        s_ref[p] = s[p]


def _rwkv_scan(at, rt, kt, bt, v, pc, gate, bonus, ln_g, ln_b, batch, seq_len, cps):
    ts = cps * CHUNK
    shp = (batch, seq_len, A_WIDTH)
    blk = pl.BlockSpec((1, ts, A_WIDTH), lambda b, s: (b, s, 0))
    pc_blk = pl.BlockSpec((1, cps, 1, A_WIDTH), lambda b, s: (b, s, 0, 0))
    vec = pl.BlockSpec((1, A_WIDTH), lambda b, s: (0, 0))
    args = [z.reshape(shp) for z in (at, rt, kt, bt, v)]
    args += [pc.reshape(batch, seq_len // CHUNK, 1, A_WIDTH), gate.reshape(shp), bonus.reshape(shp)]
    y = pl.pallas_call(
        functools.partial(_scan_kernel, cps),
        grid=(batch, seq_len // ts),
        in_specs=[blk] * 5 + [pc_blk, blk, blk, vec, vec],
        out_specs=blk,
        out_shape=jax.ShapeDtypeStruct(shp, BF16),
        scratch_shapes=[pltpu.VMEM((A_HEADS // 2, A_HEAD_DIM, 2 * A_HEAD_DIM), F32)],
        compiler_params=_cparams(("parallel", "arbitrary")),
        name="rwkv_scan",
    )(*args, ln_g, ln_b)
    return y.reshape(batch * seq_len, A_WIDTH)


def _lru_body(first, gate_ref, x_ref, px_ref, cw_ref, cb_ref, wa_ref, ba_ref, wx_ref, bx_ref,
              lam_ref, o_ref, h_ref):
    x = x_ref[...]
    ts = x.shape[0]
    prev = jnp.where(first, 0.0, px_ref[...])
    row = _iota2(x.shape, 0)
    row8 = _iota2(prev.shape, 0)

    def delayed(d):
        rolled = pltpu.roll(x, d, 0)
        top = jnp.where(row8 < d, pltpu.roll(prev, d, 0), rolled[:SUBLANES])
        return jnp.concatenate([top, rolled[SUBLANES:]], axis=0)

    xc = cw_ref[CONV_WIDTH - 1:CONV_WIDTH, :] * x + cb_ref[...]
    for d in range(1, CONV_WIDTH):
        xc = xc + cw_ref[CONV_WIDTH - 1 - d:CONV_WIDTH - d, :] * delayed(d)

    r = jax.nn.sigmoid(_dot(xc, wa_ref[...]) + ba_ref[...])
    i = jax.nn.sigmoid(_dot(xc, wx_ref[...]) + bx_ref[...])
    log_a = -LRU_C * r * jax.nn.softplus(-lam_ref[...])
    a = jnp.exp(log_a)
    th = jnp.tanh(log_a)
    u = jnp.sqrt(-2.0 * th / (1.0 - th)) * (i * xc)

    grouped = (ts // SUBLANES, SUBLANES, B_WIDTH)
    in_group = _iota2(grouped, 1)

    def group_delay(z, d, fill):
        return jnp.where(in_group >= d, pltpu.roll(z.reshape(grouped), d, 1), fill).reshape(z.shape)

    d = 1
    while d < SUBLANES:
        u = a * group_delay(u, d, 0.0) + u
        a = a * group_delay(a, d, 1.0)
        d *= 2
    carry = h_ref[...]
    groups = []
    for g0 in range(0, ts, SUBLANES):
        hg = a[g0:g0 + SUBLANES] * carry + u[g0:g0 + SUBLANES]
        carry = hg[SUBLANES - 1:SUBLANES]
        groups.append(hg)
    h_ref[...] = carry
    h = jnp.concatenate(groups, axis=0)
    o_ref[...] = (jax.nn.gelu(gate_ref[...]) * h).astype(o_ref.dtype)


def _log_gammas():
    return np.log1p(-np.exp2(-5.0 - np.arange(C_HEADS, dtype=np.float32))).astype(np.float32)


def _ret_body(cpr, q_ref, k_ref, v_ref, g_ref, pos_ref, invf_ref, gng_ref, o_ref, st_ref):
    half = C_QK_DIM // 2
    lane = _iota2((CHUNK, LANES), 1)
    in_first_half = (lane & (C_QK_DIM - 1)) < half
    lo = lane < C_QK_DIM
    idx_r = _iota2((CHUNK, CHUNK), 0).astype(F32)
    idx_c = _iota2((CHUNK, CHUNK), 1).astype(F32)
    dist = jnp.abs(idx_r - idx_c)
    pos_in = _iota2((CHUNK, 1), 0).astype(F32)
    log_gamma = _log_gammas()

    def rope(t, cos, sin_signed):
        swapped = jnp.where(in_first_half, pltpu.roll(t, LANES - half, 1), pltpu.roll(t, half, 1))
        return t * cos + swapped * sin_signed

    intra_decay = [jnp.exp(float(lg) * dist) for lg in log_gamma]
    q_decay = [jnp.exp(float(lg) * (pos_in + 1.0)) for lg in log_gamma]
    k_decay = [jnp.exp(float(lg) * (CHUNK - 1.0 - pos_in)) for lg in log_gamma]
    chunk_decay = [float(np.exp(lg * np.float32(CHUNK))) for lg in log_gamma]
    items = [(c, h) for c in range(cpr) for h in range(C_HEADS)]

    ang = pos_ref[...] * invf_ref[...]
    cos_packed = jnp.cos(ang)
    sin_packed = jnp.sin(ang)
    lane_group = lane >> (half.bit_length() - 1)

    def spread(x, c):
        out = x
        for g in range(LANES // half):
            if g != c:
                out = jnp.where(lane_group == g, pltpu.roll(x, (half * (g - c)) % LANES, 1), out)
        return out

    q2, k2 = {}, {}
    for c in range(cpr):
        rows = slice(c * CHUNK, (c + 1) * CHUNK)
        cos = spread(cos_packed, c)
        sin = spread(sin_packed, c)
        sin_signed = jnp.where(in_first_half, -sin, sin)
        for p in range(C_HEADS // 2):
            qk_cols = slice(p * LANES, (p + 1) * LANES)
            q2[c, p] = rope(q_ref[rows, qk_cols], cos, sin_signed)
            k2[c, p] = rope(k_ref[rows, qk_cols], cos, sin_signed) * (C_QK_DIM ** -0.5)

    def mine(x, h):
        return jnp.where(lo, x, 0.0) if h % 2 == 0 else jnp.where(lo, 0.0, x)

    qh = [mine(q2[c, h // 2], h) for c, h in items]
    kh = [mine(k2[c, h // 2], h) for c, h in items]
    vh = [v_ref[c * CHUNK:(c + 1) * CHUNK, h * C_V_DIM:(h + 1) * C_V_DIM].astype(BF16)
          for c, h in items]
    scores = [_dot_nt(q, k) * intra_decay[h] for q, k, (c, h) in zip(qh, kh, items)]
    kv = [_dot_tn(k * k_decay[h], v) for k, v, (c, h) in zip(kh, vh, items)]
    intra = [_dot(s, v) for s, v in zip(scores, vh)]

    state = [st_ref[h] for h in range(C_HEADS)]
    starts = []
    for i, (c, h) in enumerate(items):
        starts.append(state[h])
        state[h] = state[h] * chunk_decay[h] + kv[i]
    for h in range(C_HEADS):
        st_ref[h] = state[h]
    cross = [_dot(q * q_decay[h], s) for q, s, (c, h) in zip(qh, starts, items)]

    for i, (c, h) in enumerate(items):
        rows = slice(c * CHUNK, (c + 1) * CHUNK)
        v_cols = slice(h * C_V_DIM, (h + 1) * C_V_DIM)
        o = intra[i] + cross[i]
        mu = jnp.mean(o, axis=-1, keepdims=True)
        d = o - mu
        var = jnp.mean(d * d, axis=-1, keepdims=True)
        on = d * lax.rsqrt(var + C_GN_EPS) * gng_ref[:, v_cols]
        o_ref[rows, v_cols] = (jax.nn.silu(g_ref[rows, v_cols]) * on).astype(o_ref.dtype)


def _mix_bc_kernel(cpr, gate_ref, x_ref, px_ref, cw_ref, cb_ref, wa_ref, ba_ref, wx_ref, bx_ref,
                   lam_ref, q_ref, k_ref, v_ref, g_ref, pos_ref, invf_ref, gng_ref,
                   yb_ref, yc_ref, h_ref, st_ref):
    first = pl.program_id(1) == 0

    @pl.when(first)
    def _():
        h_ref[...] = jnp.zeros_like(h_ref)
        st_ref[...] = jnp.zeros_like(st_ref)

    _ret_body(cpr, q_ref, k_ref, v_ref, g_ref, pos_ref, invf_ref, gng_ref, yc_ref, st_ref)
    _lru_body(first, gate_ref, x_ref, px_ref, cw_ref, cb_ref, wa_ref, ba_ref, wx_ref, bx_ref,
              lam_ref, yb_ref, h_ref)


def _mix_bc(proj, lru, posb, invf, gn_g, batch, seq_len, cpr):
    t = proj.shape[0]
    ts = cpr * CHUNK
    nst = seq_len // ts
    sub = ts // SUBLANES
    gate_blk = (C_COLS + 3 * A_WIDTH) // B_WIDTH
    x_blk = gate_blk + 1

    def tok(width, blk):
        return pl.BlockSpec((ts, width), lambda b, s: (b * nst + s, blk))

    def const(shape):
        return pl.BlockSpec(shape, lambda b, s: (0,) * len(shape))

    return pl.pallas_call(
        functools.partial(_mix_bc_kernel, cpr),
        grid=(batch, nst),
        in_specs=[tok(B_WIDTH, gate_blk), tok(B_WIDTH, x_blk),
                  pl.BlockSpec((SUBLANES, B_WIDTH),
                               lambda b, s: (jnp.maximum((b * nst + s) * sub - 1, 0), x_blk)),
                  const((CONV_WIDTH, B_WIDTH)), const((1, B_WIDTH)),
                  const((B_WIDTH, B_WIDTH)), const((1, B_WIDTH)),
                  const((B_WIDTH, B_WIDTH)), const((1, B_WIDTH)), const((1, B_WIDTH)),
                  tok(C_QK_WIDTH, 0), tok(C_QK_WIDTH, 1), tok(C_WIDTH, 1), tok(C_WIDTH, 2),
                  pl.BlockSpec((CHUNK, LANES), lambda b, s: (b * nst + s, 0)),
                  const((1, LANES)), const((1, C_WIDTH))],
        out_specs=[tok(B_WIDTH, 0), tok(C_WIDTH, 0)],
        out_shape=[jax.ShapeDtypeStruct((t, B_WIDTH), BF16),
                   jax.ShapeDtypeStruct((t, C_WIDTH), BF16)],
        scratch_shapes=[pltpu.VMEM((1, B_WIDTH), F32),
                        pltpu.VMEM((C_HEADS, LANES, C_V_DIM), F32)],
        compiler_params=_cparams(("parallel", "arbitrary")),
        name="mix_bc",
    )(proj, proj, proj, lru["conv_w"], lru["conv_b"], lru["wa"], lru["ba"], lru["wx"],
      lru["bx"], lru["lam"], proj, proj, proj, proj, posb, invf, gn_g)


def _outproj_kernel(x_ref, ya_ref, yb_ref, yc_ref, w_ref, o_ref):
    b0, c0 = A_WIDTH, A_WIDTH + B_WIDTH
    acc = jnp.dot(ya_ref[...], w_ref[0:b0, :].astype(BF16), preferred_element_type=F32)
    acc += jnp.dot(yb_ref[...], w_ref[b0:c0, :].astype(BF16), preferred_element_type=F32)
    acc += jnp.dot(yc_ref[...], w_ref[c0:, :].astype(BF16), preferred_element_type=F32)
    o_ref[...] = x_ref[...] + acc


def _outproj(x, ya, yb, yc, w_stack, layer, tm):
    t, d = x.shape

    def tok(width):
        return pl.BlockSpec((tm, width), lambda i: (i, 0))

    return pl.pallas_call(
        _outproj_kernel,
        grid=(t // tm,),
        in_specs=[tok(d), tok(A_WIDTH), tok(B_WIDTH), tok(C_WIDTH),
                  pl.BlockSpec((None, d, d), lambda i: (layer, 0, 0),
                               pipeline_mode=pl.Buffered(1))],
        out_specs=tok(d),
        out_shape=jax.ShapeDtypeStruct((t, d), F32),
        compiler_params=_cparams(("parallel",)),
        name="outproj",
    )(x, ya, yb, yc, w_stack)


def _ffn_kernel(final, x_ref, g_ref, wg_ref, wu_ref, wd_ref, fg_ref, o_ref, h_ref, act_ref):
    j = pl.program_id(1)
    last = pl.num_programs(1) - 1

    def activation():
        h = h_ref[...]
        gate = jnp.dot(h, wg_ref[...].astype(BF16), preferred_element_type=F32)
        up = jnp.dot(h, wu_ref[...].astype(BF16), preferred_element_type=F32)
        return (jax.nn.silu(gate) * up).astype(BF16)

    def down(act):
        o_ref[...] += jnp.dot(act, wd_ref[...].astype(BF16), preferred_element_type=F32)

    @pl.when(j == 0)
    def _():
        x = x_ref[...]
        h_ref[...] = _rms(x, g_ref[...]).astype(BF16)
        o_ref[...] = x
        act_ref[...] = activation()

    @pl.when((j > 0) & (j < last))
    def _():
        prev = act_ref[...]
        act_ref[...] = activation()
        down(prev)

    @pl.when(j == last)
    def _():
        down(act_ref[...])
        if final:
            o_ref[...] = _rms(o_ref[...], fg_ref[...])


def _ffn(x, g, wg, wu, wd, fg, layer, final, tm, tf):
    t, d = x.shape
    f = wg.shape[2]
    nj = f // tf
    return pl.pallas_call(
        functools.partial(_ffn_kernel, final),
        grid=(t // tm, nj + 1),
        in_specs=[pl.BlockSpec((tm, d), lambda i, j: (i, 0)),
                  pl.BlockSpec((1, d), lambda i, j: (0, 0)),
                  pl.BlockSpec((None, d, tf), lambda i, j: (layer, 0, jnp.minimum(j, nj - 1))),
                  pl.BlockSpec((None, d, tf), lambda i, j: (layer, 0, jnp.minimum(j, nj - 1))),
                  pl.BlockSpec((None, tf, d), lambda i, j: (layer, jnp.maximum(j - 1, 0), 0)),
                  pl.BlockSpec((1, d), lambda i, j: (0, 0))],
        out_specs=pl.BlockSpec((tm, d), lambda i, j: (i, 0)),
        out_shape=jax.ShapeDtypeStruct((t, d), F32),
        scratch_shapes=[pltpu.VMEM((tm, d), BF16), pltpu.VMEM((tm, tf), BF16)],
        compiler_params=_cparams(("parallel", "arbitrary")),
        name="ffn_final" if final else "ffn",
    )(x, g, wg, wu, wd, fg)


def _rows_at(w, offset):
    return jnp.pad(w, ((offset, LORA_PACK - offset - w.shape[0]), (0, 0)))


RELAYOUT_UNIT = 64
RELAYOUT_PARTS = 8


def _source_unit(q):
    c_units = C_COLS // RELAYOUT_UNIT
    a_units = 3 * A_WIDTH // RELAYOUT_UNIT
    b_units = B_COLS // RELAYOUT_UNIT
    last_src = (A_COLS + B_COLS + C_COLS) // RELAYOUT_UNIT - 1
    src_c = (A_COLS + B_COLS) // RELAYOUT_UNIT
    src_b = A_COLS // RELAYOUT_UNIT
    lora = jnp.minimum(a_units + (q - c_units - a_units - b_units), last_src)
    return jnp.where(q < c_units, src_c + q,
                     jnp.where(q < c_units + a_units, q - c_units,
                               jnp.where(q < c_units + a_units + b_units,
                                         src_b + (q - c_units - a_units), lora)))


def _relayout_kernel(has_vres, n_steps, *refs):
    parts, rest = refs[:RELAYOUT_PARTS], refs[RELAYOUT_PARTS:]
    o_ref = rest[-1]
    blocks = [p[...] for p in parts]
    tail = rest[0][...] if has_vres else jnp.zeros_like(blocks[-1])
    blocks[-1] = jnp.where(pl.program_id(0) == n_steps - 1, tail, blocks[-1])
    o_ref[...] = jnp.concatenate(blocks, axis=0).T.astype(BF16)


def _layer_in_weight(w_in_t, w_vres_t, layer):
    _, n, d = w_in_t.shape
    has_vres = layer > 0
    n_steps = P_COLS // (RELAYOUT_UNIT * RELAYOUT_PARTS)

    def part(s):
        return pl.BlockSpec((None, RELAYOUT_UNIT, d),
                            lambda j: (layer, _source_unit(RELAYOUT_PARTS * j + s), 0))

    ins = [w_in_t] * RELAYOUT_PARTS
    specs = [part(s) for s in range(RELAYOUT_PARTS)]
    if has_vres:
        ins.append(w_vres_t)
        specs.append(pl.BlockSpec((None, VRES_LORA, d), lambda j: (layer - 1, 0, 0)))
    return pl.pallas_call(
        functools.partial(_relayout_kernel, has_vres, n_steps),
        grid=(n_steps,),
        in_specs=specs,
        out_specs=pl.BlockSpec((d, RELAYOUT_UNIT * RELAYOUT_PARTS), lambda j: (0, j)),
        out_shape=jax.ShapeDtypeStruct((d, P_COLS), BF16),
        compiler_params=_cparams(("parallel",)),
        name="w_in_relayout",
    )(*ins)


def _block_diag(w):
    g, n, _ = w.shape
    eye = jnp.eye(g, dtype=w.dtype)
    return (eye[:, None, :, None] * w[:, :, None, :]).reshape(g * n, g * n)


def kernel(x, positions, norm1_g, w_in, tshift_mu, rwkv_w0, rwkv_w2, rwkv_a0, rwkv_a2, rwkv_g2, rwkv_k_k, rwkv_k_a, rwkv_r_k, rwkv_ln_g, rwkv_ln_b, w_in_vres, tshift_mu_vres, rwkv_v0, rwkv_v2, lru_conv_w, lru_conv_b, lru_wa, lru_ba, lru_wx, lru_bx, lru_lambda, ret_gn_g, w_out, norm2_g, ffn_w_gate, ffn_w_up, ffn_w_down, final_norm_g):
    batch, seq_len, d = x.shape
    depth = w_in.shape[0]
    t = batch * seq_len
    aw = A_WIDTH
    row = lambda v: v.reshape(1, -1).astype(F32)

    tm_proj = min(512, seq_len)
    tm_in = min(1024, seq_len)
    tm_tok = min(256, seq_len)
    cps = min(8, seq_len // CHUNK)
    cpr = min(4, seq_len // CHUNK)

    inv_freq = ROPE_THETA ** (-jnp.arange(0, C_QK_DIM, 2, dtype=F32) / C_QK_DIM)
    invf = jnp.tile(inv_freq, LANES // inv_freq.shape[0]).reshape(1, LANES)
    n_freq = inv_freq.shape[0]
    assert cpr * n_freq == LANES
    posb = positions.astype(F32).reshape(t // (cpr * CHUNK), cpr, CHUNK)
    posb = jnp.repeat(jnp.swapaxes(posb, 1, 2), n_freq, axis=-1).reshape(t // cpr, LANES)

    w_in_t = jnp.swapaxes(w_in, 1, 2)
    w_vres_t = jnp.swapaxes(w_in_vres, 1, 2)

    xf = x.reshape(t, d)
    v_first = None
    for l in range(depth):
        has_vres = l > 0
        w_l = _layer_in_weight(w_in_t, w_vres_t, l)
        proj = _inproj(xf, row(norm1_g[l]), w_l, tm_in, P_COLS // 3)

        mu = tshift_mu[l]
        o_ad, o_gd = 3 * aw + DECAY_LORA, 3 * aw + DECAY_LORA + ICLR_LORA
        mu_vd = tshift_mu_vres[l - 1] if has_vres else jnp.zeros((VRES_LORA,), F32)
        params = {
            "mu_rkv": row(mu[:3 * aw]),
            "mu_lora": row(jnp.concatenate([mu[3 * aw:A_COLS], mu_vd])),
            "w0": row(rwkv_w0[l]), "a0": row(rwkv_a0[l]), "k_k": row(rwkv_k_k[l]),
            "k_a": row(rwkv_k_a[l]), "r_k": row(rwkv_r_k[l]),
            "w2": _rows_at(rwkv_w2[l], 0), "a2": _rows_at(rwkv_a2[l], o_ad - 3 * aw),
            "g2": _rows_at(rwkv_g2[l], o_gd - 3 * aw),
        }
        if has_vres:
            params["v0"] = row(rwkv_v0[l - 1])
            params["v2"] = _rows_at(rwkv_v2[l - 1], A_COLS - 3 * aw)
        at, rt, kt, bt, v_b, pc, g_a, bonus, *v_f32 = _rwkv_prep(
            proj, params, v_first if has_vres else None, seq_len, tm_tok)
        if l == 0:
            v_first = v_f32[0]
        y_a = _rwkv_scan(at, rt, kt, bt, v_b, pc, g_a, bonus,
                         row(rwkv_ln_g[l]), row(rwkv_ln_b[l]), batch, seq_len, cps)

        lru_params = {
            "conv_w": lru_conv_w[l], "conv_b": row(lru_conv_b[l]),
            "wa": _block_diag(lru_wa[l]).astype(BF16), "ba": row(lru_ba[l]),
            "wx": _block_diag(lru_wx[l]).astype(BF16), "bx": row(lru_bx[l]),
            "lam": row(lru_lambda[l]),
        }
        y_b, y_c = _mix_bc(proj, lru_params, posb, invf, row(ret_gn_g[l]), batch, seq_len, cpr)

        xf = _outproj(xf, y_a, y_b, y_c, w_out, l, tm_proj)
        xf = _ffn(xf, row(norm2_g[l]), ffn_w_gate, ffn_w_up, ffn_w_down, row(final_norm_g),
                  l, l == depth - 1, tm_in, 256)
    return xf.reshape(batch, seq_len, d)
```

```python
import functools

import numpy as np
import jax
import jax.numpy as jnp
from jax import lax
from jax.experimental import pallas as pl
from jax.experimental.pallas import tpu as pltpu

F32 = jnp.float32
BF16 = jnp.bfloat16

NORM_EPS = 1e-6
CHUNK = 64

A_HEADS = 12
A_HEAD_DIM = 64
A_WIDTH = A_HEADS * A_HEAD_DIM
DECAY_LORA = 96
ICLR_LORA = 96
VRES_LORA = 64
GATE_LORA = 256
A_GN_EPS = 64e-5
DECAY_SCALE = float(np.exp(np.float32(-0.5)))
A_COLS = 3 * A_WIDTH + DECAY_LORA + ICLR_LORA + GATE_LORA
B_WIDTH = 512
CONV_WIDTH = 4
LRU_C = 8.0
B_COLS = 2 * B_WIDTH
C_HEADS = 6
C_QK_DIM = 64
C_V_DIM = 128
C_QK_WIDTH = C_HEADS * C_QK_DIM
C_WIDTH = C_HEADS * C_V_DIM
C_GN_EPS = 1e-5
ROPE_THETA = 10000.0
C_COLS = 2 * C_QK_WIDTH + 2 * C_WIDTH

LANES = 128
SUBLANES = 8
LORA_PACK = DECAY_LORA + ICLR_LORA + GATE_LORA + VRES_LORA
VMEM_LIMIT = 56 * 1024 * 1024

P_COLS = C_COLS + 3 * A_WIDTH + B_COLS + LORA_PACK


def _cparams(sem):
    return pltpu.CompilerParams(dimension_semantics=sem, vmem_limit_bytes=VMEM_LIMIT)


def _dot(a, b):
    return jnp.dot(a.astype(BF16), b.astype(BF16), preferred_element_type=F32)


def _dot_nt(a, b):
    return lax.dot_general(a.astype(BF16), b.astype(BF16), (((1,), (1,)), ((), ())),
                           preferred_element_type=F32)


def _dot_tn(a, b):
    return lax.dot_general(a.astype(BF16), b.astype(BF16), (((0,), (0,)), ((), ())),
                           preferred_element_type=F32)


def _iota2(shape, axis):
    return lax.broadcasted_iota(jnp.int32, shape, axis)


def _blk(idx, size):
    return idx >> (size.bit_length() - 1)


def _rms(x, g):
    ms = jnp.mean(x * x, axis=-1, keepdims=True)
    return x * lax.rsqrt(ms + NORM_EPS) * g


def _inproj_kernel(x_ref, g_ref, w_ref, o_ref, h_ref):
    @pl.when(pl.program_id(1) == 0)
    def _():
        h_ref[...] = _rms(x_ref[...], g_ref[...]).astype(BF16)

    o_ref[...] = jnp.dot(h_ref[...], w_ref[...], preferred_element_type=F32)


def _inproj(x, g, w, tm, tn):
    t, d = x.shape
    n = w.shape[1]
    return pl.pallas_call(
        _inproj_kernel,
        grid=(t // tm, n // tn),
        in_specs=[pl.BlockSpec((tm, d), lambda i, j: (i, 0)),
                  pl.BlockSpec((1, d), lambda i, j: (0, 0)),
                  pl.BlockSpec((d, tn), lambda i, j: (0, j))],
        out_specs=pl.BlockSpec((tm, tn), lambda i, j: (i, j)),
        out_shape=jax.ShapeDtypeStruct((t, n), F32),
        scratch_shapes=[pltpu.VMEM((tm, d), BF16)],
        compiler_params=_cparams(("parallel", "arbitrary")),
        name="inproj",
    )(x, g, w)


def _shift_rows(cur, prev_row):
    rolled = pltpu.roll(cur, 1, 0)
    top = rolled[:SUBLANES]
    top = jnp.where(_iota2(top.shape, 0) == 0, prev_row, top)
    return jnp.concatenate([top, rolled[SUBLANES:]], axis=0)


def _split3(x):
    hi = x.astype(BF16)
    r1 = x - hi.astype(F32)
    mid = r1.astype(BF16)
    lo = (r1 - mid.astype(F32)).astype(BF16)
    return hi, mid, lo


def _sel_dot(sel, x):
    hi, mid, lo = _split3(x)
    d = functools.partial(jnp.dot, sel, preferred_element_type=F32)
    return (d(lo) + d(mid)) + d(hi)


def _dot_sel(x, sel):
    hi = x.astype(BF16)
    mid = (x - hi.astype(F32)).astype(BF16)
    d = functools.partial(jnp.dot, preferred_element_type=F32)
    return d(mid, sel) + d(hi, sel)


def _head_sum(x, e):
    parts = [_dot_sel(x[:, i * LANES:(i + 1) * LANES], e) for i in range(x.shape[1] // LANES)]
    return jnp.concatenate(parts, axis=-1)


def _head_ones():
    r = _blk(_iota2((LANES, LANES), 0), A_HEAD_DIM)
    c = _blk(_iota2((LANES, LANES), 1), A_HEAD_DIM)
    return (r == c).astype(BF16)


def _prep_kernel(has_vres, seq_tiles, *refs):
    if has_vres:
        (rkv_ref, lora_ref, prkv_ref, plora_ref, mu_rkv, mu_lora,
         w0, a0, k_k, k_a, r_k, w2, a2, g2, v0, v2, vfirst_ref,
         at_o, rt_o, kt_o, bt_o, vb_o, pc_o, g_o, bonus_o) = refs
    else:
        (rkv_ref, lora_ref, prkv_ref, plora_ref, mu_rkv, mu_lora,
         w0, a0, k_k, k_a, r_k, w2, a2, g2,
         at_o, rt_o, kt_o, bt_o, vb_o, pc_o, g_o, bonus_o, v_o) = refs
    first = (pl.program_id(0) % seq_tiles) == 0
    last_row = slice(SUBLANES - 1, SUBLANES)

    def mixed(cur_ref, prev_ref, mu_ref):
        cur = cur_ref[...]
        prev_row = jnp.where(first, 0.0, prev_ref[last_row, :])
        return cur + (_shift_rows(cur, prev_row) - cur) * mu_ref[...]

    rkv = mixed(rkv_ref, prkv_ref, mu_rkv)
    lora = mixed(lora_ref, plora_ref, mu_lora)
    r = rkv[:, 0:A_WIDTH]
    k = rkv[:, A_WIDTH:2 * A_WIDTH]
    v = rkv[:, 2 * A_WIDTH:3 * A_WIDTH]
    y_w = w0[...] + _dot(jnp.tanh(lora), w2[...])
    lw = -DECAY_SCALE * jax.nn.sigmoid(y_w)
    a = jax.nn.sigmoid(a0[...] + _dot(lora, a2[...]))
    g = _dot(jax.nn.sigmoid(lora), g2[...])
    if has_vres:
        v = v + (vfirst_ref[...] - v) * jax.nn.sigmoid(v0[...] + _dot(lora, v2[...]))

    e = _head_ones()
    kk = k * k_k[...]
    kk = kk * lax.rsqrt(jnp.maximum(_head_sum(kk * kk, e), 1e-24))
    k = k * (1.0 + (a - 1.0) * k_a[...])
    b = kk * a
    bonus = _head_sum(r * k * r_k[...], e) * v

    tm = lw.shape[0]
    row = _iota2((tm, tm), 0)
    col = _iota2((tm, tm), 1)
    tri = (_blk(row, CHUNK) == _blk(col, CHUNK)) & (row >= col)
    cum = _sel_dot(tri.astype(BF16), lw)
    cum3 = cum.reshape(tm // CHUNK, CHUNK, A_WIDTH)
    e_in = jnp.exp(cum)
    e_out = jnp.exp(-cum)
    at_o[...] = (-kk * jnp.exp(cum - lw)).astype(BF16)
    rt_o[...] = (r * e_in).astype(BF16)
    kt_o[...] = (k * e_out).astype(BF16)
    bt_o[...] = (b * e_out).astype(BF16)
    vb_o[...] = v.astype(BF16)
    for c in range(tm // CHUNK):
        pc_o[c] = jnp.exp(cum3[c, CHUNK - 1:CHUNK, :])
    g_o[...] = g
    bonus_o[...] = bonus
    if not has_vres:
        v_o[...] = v


def _rwkv_prep(proj, params, v_first, seq_len, tm):
    t = proj.shape[0]
    has_vres = v_first is not None
    seq_tiles = seq_len // tm
    sub = tm // SUBLANES
    rkv_w, lora_w = 3 * A_WIDTH, LORA_PACK
    rkv_blk = (C_COLS) // rkv_w
    lora_blk = (C_COLS + rkv_w + B_COLS) // lora_w

    def cur(width, blk):
        return pl.BlockSpec((tm, width), lambda i: (i, blk))

    def prev(width, blk):
        return pl.BlockSpec((SUBLANES, width), lambda i: (jnp.maximum(i * sub - 1, 0), blk))

    def vec(width):
        return pl.BlockSpec((1, width), lambda i: (0, 0))

    def mat(rows):
        return pl.BlockSpec((rows, A_WIDTH), lambda i: (0, 0))

    tok = pl.BlockSpec((tm, A_WIDTH), lambda i: (i, 0))
    ins = [proj, proj, proj, proj,
           params["mu_rkv"], params["mu_lora"],
           params["w0"], params["a0"], params["k_k"], params["k_a"], params["r_k"],
           params["w2"], params["a2"], params["g2"]]
    specs = [cur(rkv_w, rkv_blk), cur(lora_w, lora_blk),
             prev(rkv_w, rkv_blk), prev(lora_w, lora_blk),
             vec(rkv_w), vec(lora_w),
             vec(A_WIDTH), vec(A_WIDTH), vec(A_WIDTH), vec(A_WIDTH), vec(A_WIDTH),
             mat(LORA_PACK), mat(LORA_PACK), mat(LORA_PACK)]
    if has_vres:
        ins += [params["v0"], params["v2"], v_first]
        specs += [vec(A_WIDTH), mat(LORA_PACK), tok]
    wide = lambda dtype: jax.ShapeDtypeStruct((t, A_WIDTH), dtype)
    pc_spec = pl.BlockSpec((tm // CHUNK, 1, A_WIDTH), lambda i: (i, 0, 0))
    out_specs = [tok] * 5 + [pc_spec, tok, tok]
    out_shape = [wide(BF16)] * 5 + [jax.ShapeDtypeStruct((t // CHUNK, 1, A_WIDTH), F32),
                                    wide(F32), wide(F32)]
    if not has_vres:
        out_specs.append(tok)
        out_shape.append(wide(F32))
    return pl.pallas_call(
        functools.partial(_prep_kernel, has_vres, seq_tiles),
        grid=(t // tm,),
        in_specs=specs,
        out_specs=out_specs,
        out_shape=out_shape,
        compiler_params=_cparams(("parallel",)),
        name="rwkv_prep_vres" if has_vres else "rwkv_prep",
    )(*ins)


def _unit_lower_inverse(a_strict, row, col, eye, pair_diag):
    base = 8
    diag = _blk(row, base) == _blk(col, base)
    n1 = [jnp.where(diag, a, 0.0) for a in a_strict]
    n1d = [pair_diag(x) for x in n1]
    n2 = [_dot(x, d) for x, d in zip(n1, n1d)]
    n2d = [pair_diag(x) for x in n2]
    t = [eye + x for x in n1]
    t = [x + _dot(x, d) for x, d in zip(t, n2d)]
    n4d = [pair_diag(_dot(x, d)) for x, d in zip(n2, n2d)]
    t = [x + _dot(x, d) for x, d in zip(t, n4d)]
    m = base
    while m < CHUNK:
        sel = (_blk(row, 2 * m) == _blk(col, 2 * m)) & (_blk(row, m) != _blk(col, m))
        ed = [pair_diag(jnp.where(sel, a, 0.0)) for a in a_strict]
        te = [_dot(x, d) for x, d in zip(t, ed)]
        t = [x + _dot(y, pair_diag(x)) for x, y in zip(t, te)]
        m *= 2
    return t


def _scan_kernel(cps, at_ref, rt_ref, kt_ref, bt_ref, v_ref, pc_ref,
                 gate_ref, bonus_ref, lng_ref, lnb_ref, o_ref, s_ref):
    @pl.when(pl.program_id(1) == 0)
    def _():
        s_ref[...] = jnp.zeros_like(s_ref)

    n = 2 * A_HEAD_DIM
    pairs = A_HEADS // 2
    row = _iota2((CHUNK, n), 0)
    lane = _iota2((CHUNK, n), 1)
    col = lane & (A_HEAD_DIM - 1)
    lo = lane < A_HEAD_DIM
    strict = row > col
    incl = row >= col
    eye = (row == col).astype(F32)
    same_head = _blk(_iota2((n, n), 0), A_HEAD_DIM) == _blk(_iota2((n, n), 1), A_HEAD_DIM)
    items = [(c, p) for c in range(cps) for p in range(pairs)]

    def load(ref):
        return [ref[0, c * CHUNK:(c + 1) * CHUNK, p * n:(p + 1) * n] for c, p in items]

    def pair_diag(x):
        x = x.astype(BF16)
        zero = jnp.zeros_like(x)
        return jnp.concatenate([jnp.where(lo, x, zero), jnp.where(lo, zero, x)], axis=0)

    def fold(m):
        return jnp.where(lo, m[:CHUNK], m[CHUNK:])

    at = load(at_ref)
    rt = load(rt_ref)
    lhs = [jnp.concatenate([a, r], axis=0) for a, r in zip(at, rt)]
    kt = load(kt_ref)
    bt = load(bt_ref)
    lkb = [_dot_nt(x, jnp.concatenate([pair_diag(k), pair_diag(b)], axis=0))
           for x, k, b in zip(lhs, kt, bt)]
    lk = [x[:, :n] for x in lkb]
    lb = [x[:, n:] for x in lkb]
    a_k = [jnp.concatenate([jnp.where(strict, x[:CHUNK], 0.0), jnp.where(incl, x[CHUNK:], 0.0)],
                           axis=0) for x in lk]
    a_ab = [jnp.where(strict, x[:CHUNK], 0.0) for x in lb]
    a_rb = [jnp.where(incl, x[CHUNK:], 0.0) for x in lb]
    t_inv = _unit_lower_inverse(a_ab, row, col, eye, pair_diag)
    v = load(v_ref)
    lv = [_dot(a, pair_diag(x)) for a, x in zip(a_k, v)]
    wu = [_dot(t, jnp.concatenate([pair_diag(a), pair_diag(x[:CHUNK])], axis=1))
          for t, a, x in zip(t_inv, at, lv)]
    z = [_dot(a, jnp.concatenate([pair_diag(x[:, :n]), pair_diag(x[:, n:])], axis=1))
         for a, x in zip(a_rb, wu)]
    rw = [r.astype(F32) + x[:, :n] for r, x in zip(rt, z)]
    y0 = [x[CHUNK:] + zz[:, n:] for x, zz in zip(lv, z)]
    g = [jnp.where(same_head, _dot_tn(x[:, :n], b), 0.0).astype(BF16)
         for x, b in zip(wu, bt)]
    s_add = [fold(_dot_tn(jnp.concatenate([x[:, n:].astype(BF16), vv], axis=0),
                          jnp.concatenate([b, k], axis=0)))
             for x, vv, b, k in zip(wu, v, bt, kt)]

    inv_n = 1.0 / A_HEAD_DIM

    def head_mean(x):
        m_lo = jnp.sum(jnp.where(lo, x, 0.0), axis=-1, keepdims=True)
        m_hi = jnp.sum(jnp.where(lo, 0.0, x), axis=-1, keepdims=True)
        return jnp.where(lo, m_lo, m_hi) * inv_n

    s = [s_ref[p] for p in range(pairs)]
    for c in range(cps):
        rows = slice(c * CHUNK, (c + 1) * CHUNK)
        sd = [pair_diag(x) for x in s]
        ys = [_dot_nt(rw[c * pairs + p], sd[p]) + y0[c * pairs + p] for p in range(pairs)]
        for p in range(pairs):
            i = c * pairs + p
            cols = slice(p * n, (p + 1) * n)
            pc = pc_ref[0, c, :, cols]
            s[p] = (s[p] + _dot(s[p], g[i]) + s_add[i]) * pc
        for p in range(pairs):
            cols = slice(p * n, (p + 1) * n)
            d = ys[p] - head_mean(ys[p])
            var = head_mean(d * d)
            yn = d * lax.rsqrt(var + A_GN_EPS) * lng_ref[:, cols] + lnb_ref[:, cols]
            o_ref[0, rows, cols] = ((yn + bonus_ref[0, rows, cols])
                                    * gate_ref[0, rows, cols]).astype(o_ref.dtype)
    for p in range(pairs):
        s_ref[p] = s[p]


def _rwkv_scan(at, rt, kt, bt, v, pc, gate, bonus, ln_g, ln_b, batch, seq_len, cps):
    ts = cps * CHUNK
    shp = (batch, seq_len, A_WIDTH)
    blk = pl.BlockSpec((1, ts, A_WIDTH), lambda b, s: (b, s, 0))
    pc_blk = pl.BlockSpec((1, cps, 1, A_WIDTH), lambda b, s: (b, s, 0, 0))
    vec = pl.BlockSpec((1, A_WIDTH), lambda b, s: (0, 0))
    args = [z.reshape(shp) for z in (at, rt, kt, bt, v)]
    args += [pc.reshape(batch, seq_len // CHUNK, 1, A_WIDTH), gate.reshape(shp), bonus.reshape(shp)]
    y = pl.pallas_call(
        functools.partial(_scan_kernel, cps),
        grid=(batch, seq_len // ts),
        in_specs=[blk] * 5 + [pc_blk, blk, blk, vec, vec],
        out_specs=blk,
        out_shape=jax.ShapeDtypeStruct(shp, BF16),
        scratch_shapes=[pltpu.VMEM((A_HEADS // 2, A_HEAD_DIM, 2 * A_HEAD_DIM), F32)],
        compiler_params=_cparams(("parallel", "arbitrary")),
        name="rwkv_scan",
    )(*args, ln_g, ln_b)
    return y.reshape(batch * seq_len, A_WIDTH)


def _lru_body(first, gate_ref, x_ref, px_ref, cw_ref, cb_ref, wa_ref, ba_ref, wx_ref, bx_ref,
              lam_ref, o_ref, h_ref):
    x = x_ref[...]
    ts = x.shape[0]
    prev = jnp.where(first, 0.0, px_ref[...])
    row = _iota2(x.shape, 0)
    row8 = _iota2(prev.shape, 0)

    def delayed(d):
        rolled = pltpu.roll(x, d, 0)
        top = jnp.where(row8 < d, pltpu.roll(prev, d, 0), rolled[:SUBLANES])
        return jnp.concatenate([top, rolled[SUBLANES:]], axis=0)

    xc = cw_ref[CONV_WIDTH - 1:CONV_WIDTH, :] * x + cb_ref[...]
    for d in range(1, CONV_WIDTH):
        xc = xc + cw_ref[CONV_WIDTH - 1 - d:CONV_WIDTH - d, :] * delayed(d)

    r = jax.nn.sigmoid(_dot(xc, wa_ref[...]) + ba_ref[...])
    i = jax.nn.sigmoid(_dot(xc, wx_ref[...]) + bx_ref[...])
    log_a = -LRU_C * r * jax.nn.softplus(-lam_ref[...])
    a = jnp.exp(log_a)
    th = jnp.tanh(log_a)
    u = jnp.sqrt(-2.0 * th / (1.0 - th)) * (i * xc)

    grouped = (ts // SUBLANES, SUBLANES, B_WIDTH)
    in_group = _iota2(grouped, 1)

    def group_delay(z, d, fill):
        return jnp.where(in_group >= d, pltpu.roll(z.reshape(grouped), d, 1), fill).reshape(z.shape)

    d = 1
    while d < SUBLANES:
        u = a * group_delay(u, d, 0.0) + u
        a = a * group_delay(a, d, 1.0)
        d *= 2
    carry = h_ref[...]
    groups = []
    for g0 in range(0, ts, SUBLANES):
        hg = a[g0:g0 + SUBLANES] * carry + u[g0:g0 + SUBLANES]
        carry = hg[SUBLANES - 1:SUBLANES]
        groups.append(hg)
    h_ref[...] = carry
    h = jnp.concatenate(groups, axis=0)
    o_ref[...] = (jax.nn.gelu(gate_ref[...]) * h).astype(o_ref.dtype)


def _log_gammas():
    return np.log1p(-np.exp2(-5.0 - np.arange(C_HEADS, dtype=np.float32))).astype(np.float32)


def _ret_body(cpr, q_ref, k_ref, v_ref, g_ref, pos_ref, invf_ref, gng_ref, o_ref, st_ref):
    half = C_QK_DIM // 2
    lane = _iota2((CHUNK, LANES), 1)
    in_first_half = (lane & (C_QK_DIM - 1)) < half
    lo = lane < C_QK_DIM
    idx_r = _iota2((CHUNK, CHUNK), 0).astype(F32)
    idx_c = _iota2((CHUNK, CHUNK), 1).astype(F32)
    dist = jnp.abs(idx_r - idx_c)
    pos_in = _iota2((CHUNK, 1), 0).astype(F32)
    log_gamma = _log_gammas()

    def rope(t, cos, sin_signed):
        swapped = jnp.where(in_first_half, pltpu.roll(t, LANES - half, 1), pltpu.roll(t, half, 1))
        return t * cos + swapped * sin_signed

    intra_decay = [jnp.exp(float(lg) * dist) for lg in log_gamma]
    q_decay = [jnp.exp(float(lg) * (pos_in + 1.0)) for lg in log_gamma]
    k_decay = [jnp.exp(float(lg) * (CHUNK - 1.0 - pos_in)) for lg in log_gamma]
    chunk_decay = [float(np.exp(lg * np.float32(CHUNK))) for lg in log_gamma]
    items = [(c, h) for c in range(cpr) for h in range(C_HEADS)]

    ang = pos_ref[...] * invf_ref[...]
    cos_packed = jnp.cos(ang)
    sin_packed = jnp.sin(ang)
    lane_group = lane >> (half.bit_length() - 1)

    def spread(x, c):
        out = x
        for g in range(LANES // half):
            if g != c:
                out = jnp.where(lane_group == g, pltpu.roll(x, (half * (g - c)) % LANES, 1), out)
        return out

    q2, k2 = {}, {}
    for c in range(cpr):
        rows = slice(c * CHUNK, (c + 1) * CHUNK)
        cos = spread(cos_packed, c)
        sin = spread(sin_packed, c)
        sin_signed = jnp.where(in_first_half, -sin, sin)
        for p in range(C_HEADS // 2):
            qk_cols = slice(p * LANES, (p + 1) * LANES)
            q2[c, p] = rope(q_ref[rows, qk_cols], cos, sin_signed)
            k2[c, p] = rope(k_ref[rows, qk_cols], cos, sin_signed) * (C_QK_DIM ** -0.5)

    def mine(x, h):
        return jnp.where(lo, x, 0.0) if h % 2 == 0 else jnp.where(lo, 0.0, x)

    qh = [mine(q2[c, h // 2], h) for c, h in items]
    kh = [mine(k2[c, h // 2], h) for c, h in items]
    vh = [v_ref[c * CHUNK:(c + 1) * CHUNK, h * C_V_DIM:(h + 1) * C_V_DIM].astype(BF16)
          for c, h in items]
    scores = [_dot_nt(q, k) * intra_decay[h] for q, k, (c, h) in zip(qh, kh, items)]
    kv = [_dot_tn(k * k_decay[h], v) for k, v, (c, h) in zip(kh, vh, items)]
    intra = [_dot(s, v) for s, v in zip(scores, vh)]

    state = [st_ref[h] for h in range(C_HEADS)]
    starts = []
    for i, (c, h) in enumerate(items):
        starts.append(state[h])
        state[h] = state[h] * chunk_decay[h] + kv[i]
    for h in range(C_HEADS):
        st_ref[h] = state[h]
    cross = [_dot(q * q_decay[h], s) for q, s, (c, h) in zip(qh, starts, items)]

    for i, (c, h) in enumerate(items):
        rows = slice(c * CHUNK, (c + 1) * CHUNK)
        v_cols = slice(h * C_V_DIM, (h + 1) * C_V_DIM)
        o = intra[i] + cross[i]
        mu = jnp.mean(o, axis=-1, keepdims=True)
        d = o - mu
        var = jnp.mean(d * d, axis=-1, keepdims=True)
        on = d * lax.rsqrt(var + C_GN_EPS) * gng_ref[:, v_cols]
        o_ref[rows, v_cols] = (jax.nn.silu(g_ref[rows, v_cols]) * on).astype(o_ref.dtype)


def _mix_bc_kernel(cpr, gate_ref, x_ref, px_ref, cw_ref, cb_ref, wa_ref, ba_ref, wx_ref, bx_ref,
                   lam_ref, q_ref, k_ref, v_ref, g_ref, pos_ref, invf_ref, gng_ref,
                   yb_ref, yc_ref, h_ref, st_ref):
    first = pl.program_id(1) == 0

    @pl.when(first)
    def _():
        h_ref[...] = jnp.zeros_like(h_ref)
        st_ref[...] = jnp.zeros_like(st_ref)

    _ret_body(cpr, q_ref, k_ref, v_ref, g_ref, pos_ref, invf_ref, gng_ref, yc_ref, st_ref)
    _lru_body(first, gate_ref, x_ref, px_ref, cw_ref, cb_ref, wa_ref, ba_ref, wx_ref, bx_ref,
              lam_ref, yb_ref, h_ref)


def _mix_bc(proj, lru, posb, invf, gn_g, batch, seq_len, cpr):
    t = proj.shape[0]
    ts = cpr * CHUNK
    nst = seq_len // ts
    sub = ts // SUBLANES
    gate_blk = (C_COLS + 3 * A_WIDTH) // B_WIDTH
    x_blk = gate_blk + 1

    def tok(width, blk):
        return pl.BlockSpec((ts, width), lambda b, s: (b * nst + s, blk))

    def const(shape):
        return pl.BlockSpec(shape, lambda b, s: (0,) * len(shape))

    return pl.pallas_call(
        functools.partial(_mix_bc_kernel, cpr),
        grid=(batch, nst),
        in_specs=[tok(B_WIDTH, gate_blk), tok(B_WIDTH, x_blk),
                  pl.BlockSpec((SUBLANES, B_WIDTH),
                               lambda b, s: (jnp.maximum((b * nst + s) * sub - 1, 0), x_blk)),
                  const((CONV_WIDTH, B_WIDTH)), const((1, B_WIDTH)),
                  const((B_WIDTH, B_WIDTH)), const((1, B_WIDTH)),
                  const((B_WIDTH, B_WIDTH)), const((1, B_WIDTH)), const((1, B_WIDTH)),
                  tok(C_QK_WIDTH, 0), tok(C_QK_WIDTH, 1), tok(C_WIDTH, 1), tok(C_WIDTH, 2),
                  pl.BlockSpec((CHUNK, LANES), lambda b, s: (b * nst + s, 0)),
                  const((1, LANES)), const((1, C_WIDTH))],
        out_specs=[tok(B_WIDTH, 0), tok(C_WIDTH, 0)],
        out_shape=[jax.ShapeDtypeStruct((t, B_WIDTH), BF16),
                   jax.ShapeDtypeStruct((t, C_WIDTH), BF16)],
        scratch_shapes=[pltpu.VMEM((1, B_WIDTH), F32),
                        pltpu.VMEM((C_HEADS, LANES, C_V_DIM), F32)],
        compiler_params=_cparams(("parallel", "arbitrary")),
        name="mix_bc",
    )(proj, proj, proj, lru["conv_w"], lru["conv_b"], lru["wa"], lru["ba"], lru["wx"],
      lru["bx"], lru["lam"], proj, proj, proj, proj, posb, invf, gn_g)


def _outproj_kernel(x_ref, ya_ref, yb_ref, yc_ref, w_ref, o_ref):
    b0, c0 = A_WIDTH, A_WIDTH + B_WIDTH
    acc = jnp.dot(ya_ref[...], w_ref[0:b0, :].astype(BF16), preferred_element_type=F32)
    acc += jnp.dot(yb_ref[...], w_ref[b0:c0, :].astype(BF16), preferred_element_type=F32)
    acc += jnp.dot(yc_ref[...], w_ref[c0:, :].astype(BF16), preferred_element_type=F32)
    o_ref[...] = x_ref[...] + acc


def _outproj(x, ya, yb, yc, w_stack, layer, tm):
    t, d = x.shape

    def tok(width):
        return pl.BlockSpec((tm, width), lambda i: (i, 0))

    return pl.pallas_call(
        _outproj_kernel,
        grid=(t // tm,),
        in_specs=[tok(d), tok(A_WIDTH), tok(B_WIDTH), tok(C_WIDTH),
                  pl.BlockSpec((None, d, d), lambda i: (layer, 0, 0),
                               pipeline_mode=pl.Buffered(1))],
        out_specs=tok(d),
        out_shape=jax.ShapeDtypeStruct((t, d), F32),
        compiler_params=_cparams(("parallel",)),
        name="outproj",
    )(x, ya, yb, yc, w_stack)


def _ffn_kernel(final, x_ref, g_ref, wg_ref, wu_ref, wd_ref, fg_ref, o_ref, h_ref):
    j = pl.program_id(1)

    @pl.when(j == 0)
    def _():
        x = x_ref[...]
        h_ref[...] = _rms(x, g_ref[...]).astype(BF16)
        o_ref[...] = x

    h = h_ref[...]
    gate = jnp.dot(h, wg_ref[...].astype(BF16), preferred_element_type=F32)
    up = jnp.dot(h, wu_ref[...].astype(BF16), preferred_element_type=F32)
    act = (jax.nn.silu(gate) * up).astype(BF16)
    o_ref[...] += jnp.dot(act, wd_ref[...].astype(BF16), preferred_element_type=F32)

    if final:
        @pl.when(j == pl.num_programs(1) - 1)
        def _():
            o_ref[...] = _rms(o_ref[...], fg_ref[...])


def _ffn(x, g, wg, wu, wd, fg, layer, final, tm, tf):
    t, d = x.shape
    f = wg.shape[2]
    return pl.pallas_call(
        functools.partial(_ffn_kernel, final),
        grid=(t // tm, f // tf),
        in_specs=[pl.BlockSpec((tm, d), lambda i, j: (i, 0)),
                  pl.BlockSpec((1, d), lambda i, j: (0, 0)),
                  pl.BlockSpec((None, d, tf), lambda i, j: (layer, 0, j)),
                  pl.BlockSpec((None, d, tf), lambda i, j: (layer, 0, j)),
                  pl.BlockSpec((None, tf, d), lambda i, j: (layer, j, 0)),
                  pl.BlockSpec((1, d), lambda i, j: (0, 0))],
        out_specs=pl.BlockSpec((tm, d), lambda i, j: (i, 0)),
        out_shape=jax.ShapeDtypeStruct((t, d), F32),
        scratch_shapes=[pltpu.VMEM((tm, d), BF16)],
        compiler_params=_cparams(("parallel", "arbitrary")),
        name="ffn_final" if final else "ffn",
    )(x, g, wg, wu, wd, fg)


def _rows_at(w, offset):
    return jnp.pad(w, ((offset, LORA_PACK - offset - w.shape[0]), (0, 0)))


RELAYOUT_UNIT = 64
RELAYOUT_PARTS = 8


def _source_unit(q):
    c_units = C_COLS // RELAYOUT_UNIT
    a_units = 3 * A_WIDTH // RELAYOUT_UNIT
    b_units = B_COLS // RELAYOUT_UNIT
    last_src = (A_COLS + B_COLS + C_COLS) // RELAYOUT_UNIT - 1
    src_c = (A_COLS + B_COLS) // RELAYOUT_UNIT
    src_b = A_COLS // RELAYOUT_UNIT
    lora = jnp.minimum(a_units + (q - c_units - a_units - b_units), last_src)
    return jnp.where(q < c_units, src_c + q,
                     jnp.where(q < c_units + a_units, q - c_units,
                               jnp.where(q < c_units + a_units + b_units,
                                         src_b + (q - c_units - a_units), lora)))


def _relayout_kernel(has_vres, n_steps, *refs):
    parts, rest = refs[:RELAYOUT_PARTS], refs[RELAYOUT_PARTS:]
    o_ref = rest[-1]
    blocks = [p[...] for p in parts]
    tail = rest[0][...] if has_vres else jnp.zeros_like(blocks[-1])
    blocks[-1] = jnp.where(pl.program_id(0) == n_steps - 1, tail, blocks[-1])
    o_ref[...] = jnp.concatenate(blocks, axis=0).T.astype(BF16)


def _layer_in_weight(w_in_t, w_vres_t, layer):
    _, n, d = w_in_t.shape
    has_vres = layer > 0
    n_steps = P_COLS // (RELAYOUT_UNIT * RELAYOUT_PARTS)

    def part(s):
        return pl.BlockSpec((None, RELAYOUT_UNIT, d),
                            lambda j: (layer, _source_unit(RELAYOUT_PARTS * j + s), 0))

    ins = [w_in_t] * RELAYOUT_PARTS
    specs = [part(s) for s in range(RELAYOUT_PARTS)]
    if has_vres:
        ins.append(w_vres_t)
        specs.append(pl.BlockSpec((None, VRES_LORA, d), lambda j: (layer - 1, 0, 0)))
    return pl.pallas_call(
        functools.partial(_relayout_kernel, has_vres, n_steps),
        grid=(n_steps,),
        in_specs=specs,
        out_specs=pl.BlockSpec((d, RELAYOUT_UNIT * RELAYOUT_PARTS), lambda j: (0, j)),
        out_shape=jax.ShapeDtypeStruct((d, P_COLS), BF16),
        compiler_params=_cparams(("parallel",)),
        name="w_in_relayout",
    )(*ins)


def _block_diag(w):
    g, n, _ = w.shape
    eye = jnp.eye(g, dtype=w.dtype)
    return (eye[:, None, :, None] * w[:, :, None, :]).reshape(g * n, g * n)


def kernel(x, positions, norm1_g, w_in, tshift_mu, rwkv_w0, rwkv_w2, rwkv_a0, rwkv_a2, rwkv_g2, rwkv_k_k, rwkv_k_a, rwkv_r_k, rwkv_ln_g, rwkv_ln_b, w_in_vres, tshift_mu_vres, rwkv_v0, rwkv_v2, lru_conv_w, lru_conv_b, lru_wa, lru_ba, lru_wx, lru_bx, lru_lambda, ret_gn_g, w_out, norm2_g, ffn_w_gate, ffn_w_up, ffn_w_down, final_norm_g):
    batch, seq_len, d = x.shape
    depth = w_in.shape[0]
    t = batch * seq_len
    aw = A_WIDTH
    row = lambda v: v.reshape(1, -1).astype(F32)

    tm_proj = min(512, seq_len)
    tm_in = min(1024, seq_len)
    tm_tok = min(512, seq_len)
    cps = min(8, seq_len // CHUNK)
    cpr = min(4, seq_len // CHUNK)

    inv_freq = ROPE_THETA ** (-jnp.arange(0, C_QK_DIM, 2, dtype=F32) / C_QK_DIM)
    invf = jnp.tile(inv_freq, LANES // inv_freq.shape[0]).reshape(1, LANES)
    n_freq = inv_freq.shape[0]
    assert cpr * n_freq == LANES
    posb = positions.astype(F32).reshape(t // (cpr * CHUNK), cpr, CHUNK)
    posb = jnp.repeat(jnp.swapaxes(posb, 1, 2), n_freq, axis=-1).reshape(t // cpr, LANES)

    w_in_t = jnp.swapaxes(w_in, 1, 2)
    w_vres_t = jnp.swapaxes(w_in_vres, 1, 2)

    xf = x.reshape(t, d)
    v_first = None
    for l in range(depth):
        has_vres = l > 0
        w_l = _layer_in_weight(w_in_t, w_vres_t, l)
        proj = _inproj(xf, row(norm1_g[l]), w_l, tm_in, P_COLS // 3)

        mu = tshift_mu[l]
        o_ad, o_gd = 3 * aw + DECAY_LORA, 3 * aw + DECAY_LORA + ICLR_LORA
        mu_vd = tshift_mu_vres[l - 1] if has_vres else jnp.zeros((VRES_LORA,), F32)
        params = {
            "mu_rkv": row(mu[:3 * aw]),
            "mu_lora": row(jnp.concatenate([mu[3 * aw:A_COLS], mu_vd])),
            "w0": row(rwkv_w0[l]), "a0": row(rwkv_a0[l]), "k_k": row(rwkv_k_k[l]),
            "k_a": row(rwkv_k_a[l]), "r_k": row(rwkv_r_k[l]),
            "w2": _rows_at(rwkv_w2[l], 0), "a2": _rows_at(rwkv_a2[l], o_ad - 3 * aw),
            "g2": _rows_at(rwkv_g2[l], o_gd - 3 * aw),
        }
        if has_vres:
            params["v0"] = row(rwkv_v0[l - 1])
            params["v2"] = _rows_at(rwkv_v2[l - 1], A_COLS - 3 * aw)
        at, rt, kt, bt, v_b, pc, g_a, bonus, *v_f32 = _rwkv_prep(
            proj, params, v_first if has_vres else None, seq_len, tm_tok)
        if l == 0:
            v_first = v_f32[0]
        y_a = _rwkv_scan(at, rt, kt, bt, v_b, pc, g_a, bonus,
                         row(rwkv_ln_g[l]), row(rwkv_ln_b[l]), batch, seq_len, cps)

        lru_params = {
            "conv_w": lru_conv_w[l], "conv_b": row(lru_conv_b[l]),
            "wa": _block_diag(lru_wa[l]).astype(BF16), "ba": row(lru_ba[l]),
            "wx": _block_diag(lru_wx[l]).astype(BF16), "bx": row(lru_bx[l]),
            "lam": row(lru_lambda[l]),
        }
        y_b, y_c = _mix_bc(proj, lru_params, posb, invf, row(ret_gn_g[l]), batch, seq_len, cpr)

        xf = _outproj(xf, y_a, y_b, y_c, w_out, l, tm_proj)
        xf = _ffn(xf, row(norm2_g[l]), ffn_w_gate, ffn_w_up, ffn_w_down, row(final_norm_g),
                  l, l == depth - 1, tm_in, 256)
    return xf.reshape(batch, seq_len, d)
```

```python
import functools

import numpy as np
import jax
import jax.numpy as jnp
from jax import lax
from jax.experimental import pallas as pl
from jax.experimental.pallas import tpu as pltpu

F32 = jnp.float32
BF16 = jnp.bfloat16

NORM_EPS = 1e-6
CHUNK = 64

A_HEADS = 12
A_HEAD_DIM = 64
A_WIDTH = A_HEADS * A_HEAD_DIM
DECAY_LORA = 96
ICLR_LORA = 96
VRES_LORA = 64
GATE_LORA = 256
A_GN_EPS = 64e-5
DECAY_SCALE = float(np.exp(np.float32(-0.5)))
A_COLS = 3 * A_WIDTH + DECAY_LORA + ICLR_LORA + GATE_LORA
B_WIDTH = 512
CONV_WIDTH = 4
LRU_C = 8.0
B_COLS = 2 * B_WIDTH
C_HEADS = 6
C_QK_DIM = 64
C_V_DIM = 128
C_QK_WIDTH = C_HEADS * C_QK_DIM
C_WIDTH = C_HEADS * C_V_DIM
C_GN_EPS = 1e-5
ROPE_THETA = 10000.0
C_COLS = 2 * C_QK_WIDTH + 2 * C_WIDTH

LANES = 128
SUBLANES = 8
LORA_PACK = DECAY_LORA + ICLR_LORA + GATE_LORA + VRES_LORA
VMEM_LIMIT = 56 * 1024 * 1024

P_COLS = C_COLS + 3 * A_WIDTH + B_COLS + LORA_PACK


def _cparams(sem):
    return pltpu.CompilerParams(dimension_semantics=sem, vmem_limit_bytes=VMEM_LIMIT)


def _dot(a, b):
    return jnp.dot(a.astype(BF16), b.astype(BF16), preferred_element_type=F32)


def _dot_nt(a, b):
    return lax.dot_general(a.astype(BF16), b.astype(BF16), (((1,), (1,)), ((), ())),
                           preferred_element_type=F32)


def _dot_tn(a, b):
    return lax.dot_general(a.astype(BF16), b.astype(BF16), (((0,), (0,)), ((), ())),
                           preferred_element_type=F32)


def _iota2(shape, axis):
    return lax.broadcasted_iota(jnp.int32, shape, axis)


def _blk(idx, size):
    return idx >> (size.bit_length() - 1)


def _rms(x, g):
    ms = jnp.mean(x * x, axis=-1, keepdims=True)
    return x * lax.rsqrt(ms + NORM_EPS) * g


def _inproj_kernel(x_ref, g_ref, w_ref, o_ref, h_ref):
    @pl.when(pl.program_id(1) == 0)
    def _():
        h_ref[...] = _rms(x_ref[...], g_ref[...]).astype(BF16)

    o_ref[...] = jnp.dot(h_ref[...], w_ref[...], preferred_element_type=F32)


def _inproj(x, g, w, tm, tn):
    t, d = x.shape
    n = w.shape[1]
    return pl.pallas_call(
        _inproj_kernel,
        grid=(t // tm, n // tn),
        in_specs=[pl.BlockSpec((tm, d), lambda i, j: (i, 0)),
                  pl.BlockSpec((1, d), lambda i, j: (0, 0)),
                  pl.BlockSpec((d, tn), lambda i, j: (0, j))],
        out_specs=pl.BlockSpec((tm, tn), lambda i, j: (i, j)),
        out_shape=jax.ShapeDtypeStruct((t, n), F32),
        scratch_shapes=[pltpu.VMEM((tm, d), BF16)],
        compiler_params=_cparams(("parallel", "arbitrary")),
        name="inproj",
    )(x, g, w)


def _shift_rows(cur, prev_row):
    rolled = pltpu.roll(cur, 1, 0)
    top = rolled[:SUBLANES]
    top = jnp.where(_iota2(top.shape, 0) == 0, prev_row, top)
    return jnp.concatenate([top, rolled[SUBLANES:]], axis=0)


def _split3(x):
    hi = x.astype(BF16)
    r1 = x - hi.astype(F32)
    mid = r1.astype(BF16)
    lo = (r1 - mid.astype(F32)).astype(BF16)
    return hi, mid, lo


def _sel_dot(sel, x):
    hi, mid, lo = _split3(x)
    d = functools.partial(jnp.dot, sel, preferred_element_type=F32)
    return (d(lo) + d(mid)) + d(hi)


def _dot_sel(x, sel):
    hi = x.astype(BF16)
    mid = (x - hi.astype(F32)).astype(BF16)
    d = functools.partial(jnp.dot, preferred_element_type=F32)
    return d(mid, sel) + d(hi, sel)


def _head_sum(x, e):
    parts = [_dot_sel(x[:, i * LANES:(i + 1) * LANES], e) for i in range(x.shape[1] // LANES)]
    return jnp.concatenate(parts, axis=-1)


def _head_ones():
    r = _blk(_iota2((LANES, LANES), 0), A_HEAD_DIM)
    c = _blk(_iota2((LANES, LANES), 1), A_HEAD_DIM)
    return (r == c).astype(BF16)


def _prep_kernel(has_vres, seq_tiles, *refs):
    if has_vres:
        (rkv_ref, lora_ref, prkv_ref, plora_ref, mu_rkv, mu_lora,
         w0, a0, k_k, k_a, r_k, w2, a2, g2, v0, v2, vfirst_ref,
         at_o, rt_o, kt_o, bt_o, vb_o, pc_o, g_o, bonus_o) = refs
    else:
        (rkv_ref, lora_ref, prkv_ref, plora_ref, mu_rkv, mu_lora,
         w0, a0, k_k, k_a, r_k, w2, a2, g2,
         at_o, rt_o, kt_o, bt_o, vb_o, pc_o, g_o, bonus_o, v_o) = refs
    first = (pl.program_id(0) % seq_tiles) == 0
    last_row = slice(SUBLANES - 1, SUBLANES)

    def mixed(cur_ref, prev_ref, mu_ref):
        cur = cur_ref[...]
        prev_row = jnp.where(first, 0.0, prev_ref[last_row, :])
        return cur + (_shift_rows(cur, prev_row) - cur) * mu_ref[...]

    rkv = mixed(rkv_ref, prkv_ref, mu_rkv)
    lora = mixed(lora_ref, plora_ref, mu_lora)
    r = rkv[:, 0:A_WIDTH]
    k = rkv[:, A_WIDTH:2 * A_WIDTH]
    v = rkv[:, 2 * A_WIDTH:3 * A_WIDTH]
    y_w = w0[...] + _dot(jnp.tanh(lora), w2[...])
    lw = -DECAY_SCALE * jax.nn.sigmoid(y_w)
    a = jax.nn.sigmoid(a0[...] + _dot(lora, a2[...]))
    g = _dot(jax.nn.sigmoid(lora), g2[...])
    if has_vres:
        v = v + (vfirst_ref[...] - v) * jax.nn.sigmoid(v0[...] + _dot(lora, v2[...]))

    e = _head_ones()
    kk = k * k_k[...]
    kk = kk * lax.rsqrt(jnp.maximum(_head_sum(kk * kk, e), 1e-24))
    k = k * (1.0 + (a - 1.0) * k_a[...])
    b = kk * a
    bonus = _head_sum(r * k * r_k[...], e) * v

    tm = lw.shape[0]
    row = _iota2((tm, tm), 0)
    col = _iota2((tm, tm), 1)
    tri = (_blk(row, CHUNK) == _blk(col, CHUNK)) & (row >= col)
    cum = _sel_dot(tri.astype(BF16), lw)
    cum3 = cum.reshape(tm // CHUNK, CHUNK, A_WIDTH)
    e_in = jnp.exp(cum)
    e_out = jnp.exp(-cum)
    at_o[...] = (-kk * jnp.exp(cum - lw)).astype(BF16)
    rt_o[...] = (r * e_in).astype(BF16)
    kt_o[...] = (k * e_out).astype(BF16)
    bt_o[...] = (b * e_out).astype(BF16)
    vb_o[...] = v.astype(BF16)
    for c in range(tm // CHUNK):
        pc_o[c] = jnp.exp(cum3[c, CHUNK - 1:CHUNK, :])
    g_o[...] = g
    bonus_o[...] = bonus
    if not has_vres:
        v_o[...] = v


def _rwkv_prep(proj, params, v_first, seq_len, tm):
    t = proj.shape[0]
    has_vres = v_first is not None
    seq_tiles = seq_len // tm
    sub = tm // SUBLANES
    rkv_w, lora_w = 3 * A_WIDTH, LORA_PACK
    rkv_blk = (C_COLS) // rkv_w
    lora_blk = (C_COLS + rkv_w + B_COLS) // lora_w

    def cur(width, blk):
        return pl.BlockSpec((tm, width), lambda i: (i, blk))

    def prev(width, blk):
        return pl.BlockSpec((SUBLANES, width), lambda i: (jnp.maximum(i * sub - 1, 0), blk))

    def vec(width):
        return pl.BlockSpec((1, width), lambda i: (0, 0))

    def mat(rows):
        return pl.BlockSpec((rows, A_WIDTH), lambda i: (0, 0))

    tok = pl.BlockSpec((tm, A_WIDTH), lambda i: (i, 0))
    ins = [proj, proj, proj, proj,
           params["mu_rkv"], params["mu_lora"],
           params["w0"], params["a0"], params["k_k"], params["k_a"], params["r_k"],
           params["w2"], params["a2"], params["g2"]]
    specs = [cur(rkv_w, rkv_blk), cur(lora_w, lora_blk),
             prev(rkv_w, rkv_blk), prev(lora_w, lora_blk),
             vec(rkv_w), vec(lora_w),
             vec(A_WIDTH), vec(A_WIDTH), vec(A_WIDTH), vec(A_WIDTH), vec(A_WIDTH),
             mat(LORA_PACK), mat(LORA_PACK), mat(LORA_PACK)]
    if has_vres:
        ins += [params["v0"], params["v2"], v_first]
        specs += [vec(A_WIDTH), mat(LORA_PACK), tok]
    wide = lambda dtype: jax.ShapeDtypeStruct((t, A_WIDTH), dtype)
    pc_spec = pl.BlockSpec((tm // CHUNK, 1, A_WIDTH), lambda i: (i, 0, 0))
    out_specs = [tok] * 5 + [pc_spec, tok, tok]
    out_shape = [wide(BF16)] * 5 + [jax.ShapeDtypeStruct((t // CHUNK, 1, A_WIDTH), F32),
                                    wide(F32), wide(F32)]
    if not has_vres:
        out_specs.append(tok)
        out_shape.append(wide(F32))
    return pl.pallas_call(
        functools.partial(_prep_kernel, has_vres, seq_tiles),
        grid=(t // tm,),
        in_specs=specs,
        out_specs=out_specs,
        out_shape=out_shape,
        compiler_params=_cparams(("parallel",)),
        name="rwkv_prep_vres" if has_vres else "rwkv_prep",
    )(*ins)


def _unit_lower_inverse(a_strict, row, col, eye, pair_diag):
    base = 8
    diag = _blk(row, base) == _blk(col, base)
    n1 = [jnp.where(diag, a, 0.0) for a in a_strict]
    n1d = [pair_diag(x) for x in n1]
    n2 = [_dot(x, d) for x, d in zip(n1, n1d)]
    n2d = [pair_diag(x) for x in n2]
    t = [eye + x for x in n1]
    t = [x + _dot(x, d) for x, d in zip(t, n2d)]
    n4d = [pair_diag(_dot(x, d)) for x, d in zip(n2, n2d)]
    t = [x + _dot(x, d) for x, d in zip(t, n4d)]
    m = base
    while m < CHUNK:
        sel = (_blk(row, 2 * m) == _blk(col, 2 * m)) & (_blk(row, m) != _blk(col, m))
        ed = [pair_diag(jnp.where(sel, a, 0.0)) for a in a_strict]
        te = [_dot(x, d) for x, d in zip(t, ed)]
        t = [x + _dot(y, pair_diag(x)) for x, y in zip(t, te)]
        m *= 2
    return t


def _scan_kernel(cps, at_ref, rt_ref, kt_ref, bt_ref, v_ref, pc_ref,
                 gate_ref, bonus_ref, lng_ref, lnb_ref, o_ref, s_ref):
    @pl.when(pl.program_id(1) == 0)
    def _():
        s_ref[...] = jnp.zeros_like(s_ref)

    n = 2 * A_HEAD_DIM
    pairs = A_HEADS // 2
    row = _iota2((CHUNK, n), 0)
    lane = _iota2((CHUNK, n), 1)
    col = lane & (A_HEAD_DIM - 1)
    lo = lane < A_HEAD_DIM
    strict = row > col
    incl = row >= col
    eye = (row == col).astype(F32)
    same_head = _blk(_iota2((n, n), 0), A_HEAD_DIM) == _blk(_iota2((n, n), 1), A_HEAD_DIM)
    items = [(c, p) for c in range(cps) for p in range(pairs)]

    def load(ref):
        return [ref[0, c * CHUNK:(c + 1) * CHUNK, p * n:(p + 1) * n] for c, p in items]

    def pair_diag(x):
        x = x.astype(BF16)
        zero = jnp.zeros_like(x)
        return jnp.concatenate([jnp.where(lo, x, zero), jnp.where(lo, zero, x)], axis=0)

    def fold(m):
        return jnp.where(lo, m[:CHUNK], m[CHUNK:])

    at = load(at_ref)
    rt = load(rt_ref)
    lhs = [jnp.concatenate([a, r], axis=0) for a, r in zip(at, rt)]
    kt = load(kt_ref)
    bt = load(bt_ref)
    lkb = [_dot_nt(x, jnp.concatenate([pair_diag(k), pair_diag(b)], axis=0))
           for x, k, b in zip(lhs, kt, bt)]
    lk = [x[:, :n] for x in lkb]
    lb = [x[:, n:] for x in lkb]
    a_k = [jnp.concatenate([jnp.where(strict, x[:CHUNK], 0.0), jnp.where(incl, x[CHUNK:], 0.0)],
                           axis=0) for x in lk]
    a_ab = [jnp.where(strict, x[:CHUNK], 0.0) for x in lb]
    a_rb = [jnp.where(incl, x[CHUNK:], 0.0) for x in lb]
    t_inv = _unit_lower_inverse(a_ab, row, col, eye, pair_diag)
    v = load(v_ref)
    lv = [_dot(a, pair_diag(x)) for a, x in zip(a_k, v)]
    wu = [_dot(t, jnp.concatenate([pair_diag(a), pair_diag(x[:CHUNK])], axis=1))
          for t, a, x in zip(t_inv, at, lv)]
    z = [_dot(a, jnp.concatenate([pair_diag(x[:, :n]), pair_diag(x[:, n:])], axis=1))
         for a, x in zip(a_rb, wu)]
    rw = [r.astype(F32) + x[:, :n] for r, x in zip(rt, z)]
    y0 = [x[CHUNK:] + zz[:, n:] for x, zz in zip(lv, z)]
    g = [jnp.where(same_head, _dot_tn(x[:, :n], b), 0.0).astype(BF16)
         for x, b in zip(wu, bt)]
    s_add = [fold(_dot_tn(jnp.concatenate([x[:, n:].astype(BF16), vv], axis=0),
                          jnp.concatenate([b, k], axis=0)))
             for x, vv, b, k in zip(wu, v, bt, kt)]

    inv_n = 1.0 / A_HEAD_DIM

    def head_mean(x):
        m_lo = jnp.sum(jnp.where(lo, x, 0.0), axis=-1, keepdims=True)
        m_hi = jnp.sum(jnp.where(lo, 0.0, x), axis=-1, keepdims=True)
        return jnp.where(lo, m_lo, m_hi) * inv_n

    s = [s_ref[p] for p in range(pairs)]
    for c in range(cps):
        rows = slice(c * CHUNK, (c + 1) * CHUNK)
        sd = [pair_diag(x) for x in s]
        ys = [_dot_nt(rw[c * pairs + p], sd[p]) + y0[c * pairs + p] for p in range(pairs)]
        for p in range(pairs):
            i = c * pairs + p
            cols = slice(p * n, (p + 1) * n)
            pc = pc_ref[0, c, :, cols]
            s[p] = (s[p] + _dot(s[p], g[i]) + s_add[i]) * pc
        for p in range(pairs):
            cols = slice(p * n, (p + 1) * n)
            d = ys[p] - head_mean(ys[p])
            var = head_mean(d * d)
            yn = d * lax.rsqrt(var + A_GN_EPS) * lng_ref[:, cols] + lnb_ref[:, cols]
            o_ref[0, rows, cols] = ((yn + bonus_ref[0, rows, cols])
                                    * gate_ref[0, rows, cols]).astype(o_ref.dtype)
    for p in range(pairs):
        s_ref[p] = s[p]


def _rwkv_scan(at, rt, kt, bt, v, pc, gate, bonus, ln_g, ln_b, batch, seq_len, cps):
    ts = cps * CHUNK
    shp = (batch, seq_len, A_WIDTH)
    blk = pl.BlockSpec((1, ts, A_WIDTH), lambda b, s: (b, s, 0))
    pc_blk = pl.BlockSpec((1, cps, 1, A_WIDTH), lambda b, s: (b, s, 0, 0))
    vec = pl.BlockSpec((1, A_WIDTH), lambda b, s: (0, 0))
    args = [z.reshape(shp) for z in (at, rt, kt, bt, v)]
    args += [pc.reshape(batch, seq_len // CHUNK, 1, A_WIDTH), gate.reshape(shp), bonus.reshape(shp)]
    y = pl.pallas_call(
        functools.partial(_scan_kernel, cps),
        grid=(batch, seq_len // ts),
        in_specs=[blk] * 5 + [pc_blk, blk, blk, vec, vec],
        out_specs=blk,
        out_shape=jax.ShapeDtypeStruct(shp, BF16),
        scratch_shapes=[pltpu.VMEM((A_HEADS // 2, A_HEAD_DIM, 2 * A_HEAD_DIM), F32)],
        compiler_params=_cparams(("parallel", "arbitrary")),
        name="rwkv_scan",
    )(*args, ln_g, ln_b)
    return y.reshape(batch * seq_len, A_WIDTH)


def _lru_body(first, gate_ref, x_ref, px_ref, cw_ref, cb_ref, wa_ref, ba_ref, wx_ref, bx_ref,
              lam_ref, o_ref, h_ref):
    x = x_ref[...]
    ts = x.shape[0]
    prev = jnp.where(first, 0.0, px_ref[...])
    row = _iota2(x.shape, 0)
    row8 = _iota2(prev.shape, 0)

    def delayed(d):
        rolled = pltpu.roll(x, d, 0)
        top = jnp.where(row8 < d, pltpu.roll(prev, d, 0), rolled[:SUBLANES])
        return jnp.concatenate([top, rolled[SUBLANES:]], axis=0)

    xc = cw_ref[CONV_WIDTH - 1:CONV_WIDTH, :] * x + cb_ref[...]
    for d in range(1, CONV_WIDTH):
        xc = xc + cw_ref[CONV_WIDTH - 1 - d:CONV_WIDTH - d, :] * delayed(d)

    r = jax.nn.sigmoid(_dot(xc, wa_ref[...]) + ba_ref[...])
    i = jax.nn.sigmoid(_dot(xc, wx_ref[...]) + bx_ref[...])
    log_a = -LRU_C * r * jax.nn.softplus(-lam_ref[...])
    a = jnp.exp(log_a)
    th = jnp.tanh(log_a)
    u = jnp.sqrt(-2.0 * th / (1.0 - th)) * (i * xc)

    grouped = (ts // SUBLANES, SUBLANES, B_WIDTH)
    in_group = _iota2(grouped, 1)

    def group_delay(z, d, fill):
        return jnp.where(in_group >= d, pltpu.roll(z.reshape(grouped), d, 1), fill).reshape(z.shape)

    d = 1
    while d < SUBLANES:
        u = a * group_delay(u, d, 0.0) + u
        a = a * group_delay(a, d, 1.0)
        d *= 2
    carry = h_ref[...]
    groups = []
    for g0 in range(0, ts, SUBLANES):
        hg = a[g0:g0 + SUBLANES] * carry + u[g0:g0 + SUBLANES]
        carry = hg[SUBLANES - 1:SUBLANES]
        groups.append(hg)
    h_ref[...] = carry
    h = jnp.concatenate(groups, axis=0)
    o_ref[...] = (jax.nn.gelu(gate_ref[...]) * h).astype(o_ref.dtype)


def _log_gammas():
    return np.log1p(-np.exp2(-5.0 - np.arange(C_HEADS, dtype=np.float32))).astype(np.float32)


def _ret_body(cpr, q_ref, k_ref, v_ref, g_ref, pos_ref, invf_ref, gng_ref, o_ref, st_ref):
    half = C_QK_DIM // 2
    lane = _iota2((CHUNK, LANES), 1)
    in_first_half = (lane & (C_QK_DIM - 1)) < half
    lo = lane < C_QK_DIM
    idx_r = _iota2((CHUNK, CHUNK), 0).astype(F32)
    idx_c = _iota2((CHUNK, CHUNK), 1).astype(F32)
    dist = jnp.abs(idx_r - idx_c)
    pos_in = _iota2((CHUNK, 1), 0).astype(F32)
    log_gamma = _log_gammas()

    def rope(t, cos, sin_signed):
        swapped = jnp.where(in_first_half, pltpu.roll(t, LANES - half, 1), pltpu.roll(t, half, 1))
        return t * cos + swapped * sin_signed

    intra_decay = [jnp.exp(float(lg) * dist) for lg in log_gamma]
    q_decay = [jnp.exp(float(lg) * (pos_in + 1.0)) for lg in log_gamma]
    k_decay = [jnp.exp(float(lg) * (CHUNK - 1.0 - pos_in)) for lg in log_gamma]
    chunk_decay = [float(np.exp(lg * np.float32(CHUNK))) for lg in log_gamma]
    items = [(c, h) for c in range(cpr) for h in range(C_HEADS)]

    ang = pos_ref[...] * invf_ref[...]
    cos_packed = jnp.cos(ang)
    sin_packed = jnp.sin(ang)
    lane_group = lane >> (half.bit_length() - 1)

    def spread(x, c):
        out = x
        for g in range(LANES // half):
            if g != c:
                out = jnp.where(lane_group == g, pltpu.roll(x, (half * (g - c)) % LANES, 1), out)
        return out

    q2, k2 = {}, {}
    for c in range(cpr):
        rows = slice(c * CHUNK, (c + 1) * CHUNK)
        cos = spread(cos_packed, c)
        sin = spread(sin_packed, c)
        sin_signed = jnp.where(in_first_half, -sin, sin)
        for p in range(C_HEADS // 2):
            qk_cols = slice(p * LANES, (p + 1) * LANES)
            q2[c, p] = rope(q_ref[rows, qk_cols], cos, sin_signed)
            k2[c, p] = rope(k_ref[rows, qk_cols], cos, sin_signed) * (C_QK_DIM ** -0.5)

    def mine(x, h):
        return jnp.where(lo, x, 0.0) if h % 2 == 0 else jnp.where(lo, 0.0, x)

    qh = [mine(q2[c, h // 2], h) for c, h in items]
    kh = [mine(k2[c, h // 2], h) for c, h in items]
    vh = [v_ref[c * CHUNK:(c + 1) * CHUNK, h * C_V_DIM:(h + 1) * C_V_DIM].astype(BF16)
          for c, h in items]
    scores = [_dot_nt(q, k) * intra_decay[h] for q, k, (c, h) in zip(qh, kh, items)]
    kv = [_dot_tn(k * k_decay[h], v) for k, v, (c, h) in zip(kh, vh, items)]
    intra = [_dot(s, v) for s, v in zip(scores, vh)]

    state = [st_ref[h] for h in range(C_HEADS)]
    starts = []
    for i, (c, h) in enumerate(items):
        starts.append(state[h])
        state[h] = state[h] * chunk_decay[h] + kv[i]
    for h in range(C_HEADS):
        st_ref[h] = state[h]
    cross = [_dot(q * q_decay[h], s) for q, s, (c, h) in zip(qh, starts, items)]

    for i, (c, h) in enumerate(items):
        rows = slice(c * CHUNK, (c + 1) * CHUNK)
        v_cols = slice(h * C_V_DIM, (h + 1) * C_V_DIM)
        o = intra[i] + cross[i]
        mu = jnp.mean(o, axis=-1, keepdims=True)
        d = o - mu
        var = jnp.mean(d * d, axis=-1, keepdims=True)
        on = d * lax.rsqrt(var + C_GN_EPS) * gng_ref[:, v_cols]
        o_ref[rows, v_cols] = (jax.nn.silu(g_ref[rows, v_cols]) * on).astype(o_ref.dtype)


def _mix_bc_kernel(cpr, gate_ref, x_ref, px_ref, cw_ref, cb_ref, wa_ref, ba_ref, wx_ref, bx_ref,
                   lam_ref, q_ref, k_ref, v_ref, g_ref, pos_ref, invf_ref, gng_ref,
                   yb_ref, yc_ref, h_ref, st_ref):
    first = pl.program_id(1) == 0

    @pl.when(first)
    def _():
        h_ref[...] = jnp.zeros_like(h_ref)
        st_ref[...] = jnp.zeros_like(st_ref)

    _ret_body(cpr, q_ref, k_ref, v_ref, g_ref, pos_ref, invf_ref, gng_ref, yc_ref, st_ref)
    _lru_body(first, gate_ref, x_ref, px_ref, cw_ref, cb_ref, wa_ref, ba_ref, wx_ref, bx_ref,
              lam_ref, yb_ref, h_ref)


def _mix_bc(proj, lru, posb, invf, gn_g, batch, seq_len, cpr):
    t = proj.shape[0]
    ts = cpr * CHUNK
    nst = seq_len // ts
    sub = ts // SUBLANES
    gate_blk = (C_COLS + 3 * A_WIDTH) // B_WIDTH
    x_blk = gate_blk + 1

    def tok(width, blk):
        return pl.BlockSpec((ts, width), lambda b, s: (b * nst + s, blk))

    def const(shape):
        return pl.BlockSpec(shape, lambda b, s: (0,) * len(shape))

    return pl.pallas_call(
        functools.partial(_mix_bc_kernel, cpr),
        grid=(batch, nst),
        in_specs=[tok(B_WIDTH, gate_blk), tok(B_WIDTH, x_blk),
                  pl.BlockSpec((SUBLANES, B_WIDTH),
                               lambda b, s: (jnp.maximum((b * nst + s) * sub - 1, 0), x_blk)),
                  const((CONV_WIDTH, B_WIDTH)), const((1, B_WIDTH)),
                  const((B_WIDTH, B_WIDTH)), const((1, B_WIDTH)),
                  const((B_WIDTH, B_WIDTH)), const((1, B_WIDTH)), const((1, B_WIDTH)),
                  tok(C_QK_WIDTH, 0), tok(C_QK_WIDTH, 1), tok(C_WIDTH, 1), tok(C_WIDTH, 2),
                  pl.BlockSpec((CHUNK, LANES), lambda b, s: (b * nst + s, 0)),
                  const((1, LANES)), const((1, C_WIDTH))],
        out_specs=[tok(B_WIDTH, 0), tok(C_WIDTH, 0)],
        out_shape=[jax.ShapeDtypeStruct((t, B_WIDTH), BF16),
                   jax.ShapeDtypeStruct((t, C_WIDTH), BF16)],
        scratch_shapes=[pltpu.VMEM((1, B_WIDTH), F32),
                        pltpu.VMEM((C_HEADS, LANES, C_V_DIM), F32)],
        compiler_params=_cparams(("parallel", "arbitrary")),
        name="mix_bc",
    )(proj, proj, proj, lru["conv_w"], lru["conv_b"], lru["wa"], lru["ba"], lru["wx"],
      lru["bx"], lru["lam"], proj, proj, proj, proj, posb, invf, gn_g)


def _outproj_kernel(x_ref, ya_ref, yb_ref, yc_ref, w_ref, o_ref):
    b0, c0 = A_WIDTH, A_WIDTH + B_WIDTH
    acc = jnp.dot(ya_ref[...], w_ref[0:b0, :].astype(BF16), preferred_element_type=F32)
    acc += jnp.dot(yb_ref[...], w_ref[b0:c0, :].astype(BF16), preferred_element_type=F32)
    acc += jnp.dot(yc_ref[...], w_ref[c0:, :].astype(BF16), preferred_element_type=F32)
    o_ref[...] = x_ref[...] + acc


def _outproj(x, ya, yb, yc, w_stack, layer, tm):
    t, d = x.shape

    def tok(width):
        return pl.BlockSpec((tm, width), lambda i: (i, 0))

    return pl.pallas_call(
        _outproj_kernel,
        grid=(t // tm,),
        in_specs=[tok(d), tok(A_WIDTH), tok(B_WIDTH), tok(C_WIDTH),
                  pl.BlockSpec((None, d, d), lambda i: (layer, 0, 0),
                               pipeline_mode=pl.Buffered(1))],
        out_specs=tok(d),
        out_shape=jax.ShapeDtypeStruct((t, d), F32),
        compiler_params=_cparams(("parallel",)),
        name="outproj",
    )(x, ya, yb, yc, w_stack)


def _ffn_kernel(final, x_ref, g_ref, wg_ref, wu_ref, wd_ref, fg_ref, o_ref, h_ref):
    j = pl.program_id(1)

    @pl.when(j == 0)
    def _():
        x = x_ref[...]
        h_ref[...] = _rms(x, g_ref[...]).astype(BF16)
        o_ref[...] = x

    h = h_ref[...]
    gate = jnp.dot(h, wg_ref[...].astype(BF16), preferred_element_type=F32)
    up = jnp.dot(h, wu_ref[...].astype(BF16), preferred_element_type=F32)
    act = (jax.nn.silu(gate) * up).astype(BF16)
    o_ref[...] += jnp.dot(act, wd_ref[...].astype(BF16), preferred_element_type=F32)

    if final:
        @pl.when(j == pl.num_programs(1) - 1)
        def _():
            o_ref[...] = _rms(o_ref[...], fg_ref[...])


def _ffn(x, g, wg, wu, wd, fg, layer, final, tm, tf):
    t, d = x.shape
    f = wg.shape[2]
    return pl.pallas_call(
        functools.partial(_ffn_kernel, final),
        grid=(t // tm, f // tf),
        in_specs=[pl.BlockSpec((tm, d), lambda i, j: (i, 0)),
                  pl.BlockSpec((1, d), lambda i, j: (0, 0)),
                  pl.BlockSpec((None, d, tf), lambda i, j: (layer, 0, j)),
                  pl.BlockSpec((None, d, tf), lambda i, j: (layer, 0, j)),
                  pl.BlockSpec((None, tf, d), lambda i, j: (layer, j, 0)),
                  pl.BlockSpec((1, d), lambda i, j: (0, 0))],
        out_specs=pl.BlockSpec((tm, d), lambda i, j: (i, 0)),
        out_shape=jax.ShapeDtypeStruct((t, d), F32),
        scratch_shapes=[pltpu.VMEM((tm, d), BF16)],
        compiler_params=_cparams(("parallel", "arbitrary")),
        name="ffn_final" if final else "ffn",
    )(x, g, wg, wu, wd, fg)


def _rows_at(w, offset):
    return jnp.pad(w, ((offset, LORA_PACK - offset - w.shape[0]), (0, 0)))


RELAYOUT_UNIT = 64
RELAYOUT_PARTS = 16


def _source_unit(q):
    c_units = C_COLS // RELAYOUT_UNIT
    a_units = 3 * A_WIDTH // RELAYOUT_UNIT
    b_units = B_COLS // RELAYOUT_UNIT
    last_src = (A_COLS + B_COLS + C_COLS) // RELAYOUT_UNIT - 1
    src_c = (A_COLS + B_COLS) // RELAYOUT_UNIT
    src_b = A_COLS // RELAYOUT_UNIT
    lora = jnp.minimum(a_units + (q - c_units - a_units - b_units), last_src)
    return jnp.where(q < c_units, src_c + q,
                     jnp.where(q < c_units + a_units, q - c_units,
                               jnp.where(q < c_units + a_units + b_units,
                                         src_b + (q - c_units - a_units), lora)))


def _relayout_kernel(has_vres, n_steps, *refs):
    parts, rest = refs[:RELAYOUT_PARTS], refs[RELAYOUT_PARTS:]
    o_ref = rest[-1]
    blocks = [p[...] for p in parts]
    tail = rest[0][...] if has_vres else jnp.zeros_like(blocks[-1])
    blocks[-1] = jnp.where(pl.program_id(0) == n_steps - 1, tail, blocks[-1])
    o_ref[...] = jnp.concatenate(blocks, axis=0).T.astype(BF16)


def _layer_in_weight(w_in_t, w_vres_t, layer):
    _, n, d = w_in_t.shape
    has_vres = layer > 0
    n_steps = P_COLS // (RELAYOUT_UNIT * RELAYOUT_PARTS)

    def part(s):
        return pl.BlockSpec((None, RELAYOUT_UNIT, d),
                            lambda j: (layer, _source_unit(RELAYOUT_PARTS * j + s), 0))

    ins = [w_in_t] * RELAYOUT_PARTS
    specs = [part(s) for s in range(RELAYOUT_PARTS)]
    if has_vres:
        ins.append(w_vres_t)
        specs.append(pl.BlockSpec((None, VRES_LORA, d), lambda j: (layer - 1, 0, 0)))
    return pl.pallas_call(
        functools.partial(_relayout_kernel, has_vres, n_steps),
        grid=(n_steps,),
        in_specs=specs,
        out_specs=pl.BlockSpec((d, RELAYOUT_UNIT * RELAYOUT_PARTS), lambda j: (0, j)),
        out_shape=jax.ShapeDtypeStruct((d, P_COLS), BF16),
        compiler_params=_cparams(("parallel",)),
        name="w_in_relayout",
    )(*ins)


def _block_diag(w):
    g, n, _ = w.shape
    eye = jnp.eye(g, dtype=w.dtype)
    return (eye[:, None, :, None] * w[:, :, None, :]).reshape(g * n, g * n)


def kernel(x, positions, norm1_g, w_in, tshift_mu, rwkv_w0, rwkv_w2, rwkv_a0, rwkv_a2, rwkv_g2, rwkv_k_k, rwkv_k_a, rwkv_r_k, rwkv_ln_g, rwkv_ln_b, w_in_vres, tshift_mu_vres, rwkv_v0, rwkv_v2, lru_conv_w, lru_conv_b, lru_wa, lru_ba, lru_wx, lru_bx, lru_lambda, ret_gn_g, w_out, norm2_g, ffn_w_gate, ffn_w_up, ffn_w_down, final_norm_g):
    batch, seq_len, d = x.shape
    depth = w_in.shape[0]
    t = batch * seq_len
    aw = A_WIDTH
    row = lambda v: v.reshape(1, -1).astype(F32)

    tm_proj = min(512, seq_len)
    tm_in = min(1024, seq_len)
    tm_tok = min(512, seq_len)
    cps = min(16, seq_len // CHUNK)
    cpr = min(4, seq_len // CHUNK)

    inv_freq = ROPE_THETA ** (-jnp.arange(0, C_QK_DIM, 2, dtype=F32) / C_QK_DIM)
    invf = jnp.tile(inv_freq, LANES // inv_freq.shape[0]).reshape(1, LANES)
    n_freq = inv_freq.shape[0]
    assert cpr * n_freq == LANES
    posb = positions.astype(F32).reshape(t // (cpr * CHUNK), cpr, CHUNK)
    posb = jnp.repeat(jnp.swapaxes(posb, 1, 2), n_freq, axis=-1).reshape(t // cpr, LANES)

    w_in_t = jnp.swapaxes(w_in, 1, 2)
    w_vres_t = jnp.swapaxes(w_in_vres, 1, 2)

    xf = x.reshape(t, d)
    v_first = None
    for l in range(depth):
        has_vres = l > 0
        w_l = _layer_in_weight(w_in_t, w_vres_t, l)
        proj = _inproj(xf, row(norm1_g[l]), w_l, tm_in, P_COLS // 3)

        mu = tshift_mu[l]
        o_ad, o_gd = 3 * aw + DECAY_LORA, 3 * aw + DECAY_LORA + ICLR_LORA
        mu_vd = tshift_mu_vres[l - 1] if has_vres else jnp.zeros((VRES_LORA,), F32)
        params = {
            "mu_rkv": row(mu[:3 * aw]),
            "mu_lora": row(jnp.concatenate([mu[3 * aw:A_COLS], mu_vd])),
            "w0": row(rwkv_w0[l]), "a0": row(rwkv_a0[l]), "k_k": row(rwkv_k_k[l]),
            "k_a": row(rwkv_k_a[l]), "r_k": row(rwkv_r_k[l]),
            "w2": _rows_at(rwkv_w2[l], 0), "a2": _rows_at(rwkv_a2[l], o_ad - 3 * aw),
            "g2": _rows_at(rwkv_g2[l], o_gd - 3 * aw),
        }
        if has_vres:
            params["v0"] = row(rwkv_v0[l - 1])
            params["v2"] = _rows_at(rwkv_v2[l - 1], A_COLS - 3 * aw)
        at, rt, kt, bt, v_b, pc, g_a, bonus, *v_f32 = _rwkv_prep(
            proj, params, v_first if has_vres else None, seq_len, tm_tok)
        if l == 0:
            v_first = v_f32[0]
        y_a = _rwkv_scan(at, rt, kt, bt, v_b, pc, g_a, bonus,
                         row(rwkv_ln_g[l]), row(rwkv_ln_b[l]), batch, seq_len, cps)

        lru_params = {
            "conv_w": lru_conv_w[l], "conv_b": row(lru_conv_b[l]),
            "wa": _block_diag(lru_wa[l]).astype(BF16), "ba": row(lru_ba[l]),
            "wx": _block_diag(lru_wx[l]).astype(BF16), "bx": row(lru_bx[l]),
            "lam": row(lru_lambda[l]),
        }
        y_b, y_c = _mix_bc(proj, lru_params, posb, invf, row(ret_gn_g[l]), batch, seq_len, cpr)

        xf = _outproj(xf, y_a, y_b, y_c, w_out, l, tm_proj)
        xf = _ffn(xf, row(norm2_g[l]), ffn_w_gate, ffn_w_up, ffn_w_down, row(final_norm_g),
                  l, l == depth - 1, tm_in, 256)
    return xf.reshape(batch, seq_len, d)
```
